```python
import math
import jax
import jax.numpy as jnp
from jax import lax
import numpy as np

D_MODEL = 1024
BATCH = 8
SEQ = 2048
DEPTH = 2

GRID_W = 64
CTX_LEN = 256

N_EVEN = (DEPTH + 1) // 2
N_ODD = DEPTH // 2

DEEPNORM_ALPHA = (2.0 * DEPTH) ** 0.25
OUT_INIT_SCALE = (8.0 * DEPTH) ** -0.25
LN_EPS = 1e-5
RMS_EPS = 1e-6

DN_HEADS = 4
DN_DK = 128
DN_DV = 128
DN_CHUNK = 64
SHORT_CONV = 4
DN_QK = DN_HEADS * DN_DK
DN_VAL = DN_HEADS * DN_DV
DN_QKV = 2 * DN_QK + DN_VAL
DN_GATES = 2 * 2 * DN_HEADS
DN_IN = DN_QKV + DN_VAL + DN_GATES

HY_WIDTH = D_MODEL // 2
HY_SHORT = 3
HY_EMB = 33
HY_FFN = 64
HY_INNER = 2
HY_TARGET = 1e-2
HY_FAST_PCT = 0.3
HY_SLOW_PCT = 1.5
HY_MIN_DECAY = math.log(HY_TARGET) / HY_SLOW_PCT
HY_MAX_DECAY = math.log(HY_TARGET) / HY_FAST_PCT
HY_IN = 3 * HY_WIDTH

EVEN_IN = DN_IN + HY_IN
EVEN_MIX = DN_VAL + HY_WIDTH

LRU_WIDTH = D_MODEL
LRU_HEADS = 4
LRU_BLOCK = LRU_WIDTH // LRU_HEADS
LRU_C = 8.0
LRU_CONV = 4
LRU_MIN_RAD = 0.9
LRU_MAX_RAD = 0.999

N_EXPERTS = 32
TOP_K = 4
D_FF = D_MODEL
SWIGLU_ALPHA = 1.702
SWIGLU_LIMIT = 7.0

kernel_name = 'hybrid_deltanet_hyena_rglru_moe_prefix'


def layer_norm(x, g, b):
    xf = x.astype(jnp.float32)
    mu = jnp.mean(xf, axis=-1, keepdims=True)
    var = jnp.mean(jnp.square(xf - mu), axis=-1, keepdims=True)
    return ((xf - mu) * lax.rsqrt(var + LN_EPS) * g + b).astype(x.dtype)


def l2norm(t):
    return t * lax.rsqrt(jnp.sum(t * t, axis=-1, keepdims=True) + RMS_EPS)


def depthwise_conv(x, w, pad_left):
    width = w.shape[0]
    length = x.shape[1]
    xp = jnp.pad(x, ((0, 0), (pad_left, width - 1 - pad_left), (0, 0)))
    out = xp[:, 0:length] * w[0]
    for i in range(1, width):
        out = out + xp[:, i:i + length] * w[i]
    return out


def sincos_2d(rows, cols, dim):
    quarter = dim // 4
    omega = 1.0 / (10000.0 ** (jnp.arange(quarter, dtype=jnp.float32) / quarter))
    def emb1d(n):
        ang = jnp.arange(n, dtype=jnp.float32)[:, None] * omega
        return jnp.concatenate([jnp.sin(ang), jnp.cos(ang)], axis=-1)
    er = jnp.broadcast_to(emb1d(rows)[:, None], (rows, cols, dim // 2))
    ec = jnp.broadcast_to(emb1d(cols)[None], (rows, cols, dim // 2))
    return jnp.concatenate([er, ec], axis=-1).reshape(rows * cols, dim)


def gated_delta_chunked(q, k, v, g, beta, s0):
    bsz, heads, length, dk = k.shape
    dv = v.shape[-1]
    csz = DN_CHUNK
    n = length // csz
    q, k, v = (t.reshape(bsz, heads, n, csz, -1) for t in (q, k, v))
    g = jnp.cumsum(g.reshape(bsz, heads, n, csz), axis=-1)
    beta = beta.reshape(bsz, heads, n, csz)
    incl = jnp.tril(jnp.ones((csz, csz), dtype=bool))
    strict = jnp.tril(jnp.ones((csz, csz), dtype=bool), -1)
    gdiff = g[..., :, None] - g[..., None, :]
    decay = jnp.where(incl, jnp.exp(jnp.where(incl, gdiff, 0.0)), 0.0)
    k_beta = k * beta[..., None]
    a_strict = jnp.where(strict, jnp.einsum('bhnid,bhnjd->bhnij', k_beta, k) * decay, 0.0)
    eye = jnp.eye(csz, dtype=jnp.float32)
    t_inv = lax.linalg.triangular_solve(eye + a_strict, jnp.broadcast_to(eye, a_strict.shape),
                                        left_side=True, lower=True, unit_diagonal=True)
    u = jnp.einsum('bhnij,bhnjd->bhnid', t_inv, v * beta[..., None])
    w = jnp.einsum('bhnij,bhnjd->bhnid', t_inv, k_beta * jnp.exp(g)[..., None])
    attn = jnp.where(incl, jnp.einsum('bhnid,bhnjd->bhnij', q, k) * decay, 0.0)
    q_dec = q * jnp.exp(g)[..., None]
    k_dec = k * jnp.exp(g[..., -1:] - g)[..., None]
    g_last = jnp.exp(g[..., -1])

    def step(state, xs):
        q_c, k_c, u_c, w_c, attn_c, gl_c = xs
        v_new = u_c - jnp.einsum('bhcd,bhde->bhce', w_c, state)
        o_c = jnp.einsum('bhcd,bhde->bhce', q_c, state) + jnp.einsum('bhij,bhje->bhie', attn_c, v_new)
        state = state * gl_c[..., None, None] + jnp.einsum('bhcd,bhce->bhde', k_c, v_new)
        return state, o_c

    xs = tuple(jnp.moveaxis(t, 2, 0) for t in (q_dec, k_dec, u, w, attn, g_last))
    s_final, o = lax.scan(step, s0, xs)
    return jnp.moveaxis(o, 0, 2).reshape(bsz, heads, length, dv), s_final


def _dn_prepare(p, conv_w, a_log, dt_bias):
    bsz, length, _ = p.shape
    qkv, z, gates = jnp.split(p, [DN_QKV, DN_QKV + DN_VAL], axis=-1)
    qkv = jax.nn.silu(depthwise_conv(qkv, conv_w, SHORT_CONV // 2).astype(jnp.float32))
    q, k, v = jnp.split(qkv, [DN_QK, 2 * DN_QK], axis=-1)
    to_heads = lambda t: t.reshape(bsz, length, DN_HEADS, -1).transpose(0, 2, 1, 3)
    q = l2norm(to_heads(q)) * (DN_DK ** -0.5)
    k = l2norm(to_heads(k))
    v = to_heads(v)
    gates = jnp.moveaxis(gates.astype(jnp.float32).reshape(bsz, length, 2, 2, DN_HEADS), 1, -1)
    dirs = []
    for d in range(2):
        g = -jnp.exp(a_log[d].astype(jnp.float32))[:, None] * jax.nn.softplus(gates[:, d, 0] + dt_bias[d][:, None])
        beta = jax.nn.sigmoid(gates[:, d, 1])
        dirs.append((g, beta))
    return q, k, v, z, dirs


def _dn_direction(q, k, v, g, beta, s0, reverse):
    if reverse:
        q, k, v = (jnp.flip(t, 2) for t in (q, k, v))
        g, beta = jnp.flip(g, -1), jnp.flip(beta, -1)
    o, s = gated_delta_chunked(q, k, v, g, beta, s0)
    if reverse:
        o = jnp.flip(o, 2)
    return o, s


def _gated_rmsnorm(o, z, gain):
    bsz, heads, length, dv = o.shape
    o = o.transpose(0, 2, 1, 3)
    o = o * lax.rsqrt(jnp.mean(o * o, axis=-1, keepdims=True) + RMS_EPS) * gain
    o = o * jax.nn.silu(z.astype(jnp.float32).reshape(bsz, length, heads, dv))
    return o.reshape(bsz, length, heads * dv)


def deltanet_mixer(p_lat, p_ctx, conv_w, a_log, dt_bias, norm_g, need_ctx):
    ql, kl, vl, zl, dirs_l = _dn_prepare(p_lat, conv_w, a_log, dt_bias)
    qc, kc, vc, zc, dirs_c = _dn_prepare(p_ctx, conv_w, a_log, dt_bias)
    s0 = jnp.zeros((p_ctx.shape[0], DN_HEADS, DN_DK, DN_DV), jnp.float32)
    o_lat = jnp.zeros_like(vl)
    o_ctx = jnp.zeros_like(vc)
    for d, reverse in enumerate((False, True)):
        oc, sc = _dn_direction(qc, kc, vc, dirs_c[d][0], dirs_c[d][1], s0, reverse)
        ol, _ = _dn_direction(ql, kl, vl, dirs_l[d][0], dirs_l[d][1], sc, reverse)
        o_lat = o_lat + ol
        if need_ctx:
            o_ctx = o_ctx + oc
    y_lat = _gated_rmsnorm(o_lat, zl, norm_g).astype(p_lat.dtype)
    y_ctx = _gated_rmsnorm(o_ctx, zc, norm_g).astype(p_ctx.dtype) if need_ctx else None
    return y_lat, y_ctx


def hyena_filter(length, w_in, b_in, w_mid, b_mid, w_out, freq):
    f32 = jnp.float32
    t = jnp.linspace(0.0, 1.0, length, dtype=f32)[:, None]
    bands = (HY_EMB - 1) // 2
    wpos = 2.0 * math.pi * jnp.arange(length, dtype=f32)[:, None] / length
    fb = jnp.linspace(1e-4, bands - 1, bands, dtype=f32)[None]
    z = jnp.concatenate([t, jnp.cos(fb * wpos), -jnp.sin(fb * wpos)], axis=-1)
    freq = freq.astype(f32)
    h = jnp.sin(freq * (z @ w_in.astype(f32) + b_in))
    for i in range(HY_INNER):
        h = jnp.sin(freq * (h @ w_mid[i].astype(f32) + b_mid[i]))
    h = (h @ w_out.astype(f32)).reshape(length, 2, HY_WIDTH)
    deltas = jnp.abs(jnp.linspace(HY_MIN_DECAY, HY_MAX_DECAY, HY_WIDTH, dtype=f32))
    h = h * jnp.exp(-t * deltas)[:, None, :]
    return h[:, 0], h[:, 1]


def centred_long_conv(u, h_fwd, h_bwd):
    length, chans = h_fwd.shape
    kern = jnp.concatenate([h_fwd, jnp.zeros((1, chans), h_fwd.dtype), h_bwd[:0:-1]], axis=0)
    uf = jnp.fft.rfft(u, n=2 * length, axis=1)
    kf = jnp.fft.rfft(kern, n=2 * length, axis=0)
    return jnp.fft.irfft(uf * kf[None], n=2 * length, axis=1)[:, :length]


def hyena_seq(p, conv_w, conv_b, filt, skip):
    length = p.shape[1]
    uc = (depthwise_conv(p, conv_w, HY_SHORT // 2) + conv_b).astype(jnp.float32)
    x0, x1, v = jnp.split(uc, 3, axis=-1)
    h_fwd, h_bwd = hyena_filter(length, *filt)
    v = v * x1
    v = centred_long_conv(v, h_fwd, h_bwd) + v * skip
    return (x0 * v).astype(p.dtype)


def even_mixer(u_lat, u_ctx, w_in, w_out, dn_conv_w, dn_a_log, dn_dt_bias, dn_norm_g,
               hy_conv_w, hy_conv_b, hy_filt, hy_skip, need_ctx):
    p_lat = u_lat @ w_in
    p_ctx = u_ctx @ w_in
    dn_lat, dn_ctx = deltanet_mixer(p_lat[..., :DN_IN], p_ctx[..., :DN_IN], dn_conv_w, dn_a_log,
                                    dn_dt_bias, dn_norm_g, need_ctx)
    hy_lat = hyena_seq(p_lat[..., DN_IN:], hy_conv_w, hy_conv_b, hy_filt, hy_skip)
    y_lat = jnp.concatenate([dn_lat, hy_lat], axis=-1) @ w_out
    y_ctx = None
    if need_ctx:
        hy_ctx = hyena_seq(p_ctx[..., DN_IN:], hy_conv_w, hy_conv_b, hy_filt, hy_skip)
        y_ctx = jnp.concatenate([dn_ctx, hy_ctx], axis=-1) @ w_out
    return y_lat, y_ctx


def rglru_coeffs(xb, wa, ba, wx, bx, a_param):
    bsz, length, width = xb.shape
    xh = xb.reshape(bsz, length, LRU_HEADS, LRU_BLOCK)
    r = jax.nn.sigmoid(jnp.einsum('blhi,hij->blhj', xh, wa.astype(jnp.float32)).reshape(bsz, length, width) + ba)
    i = jax.nn.sigmoid(jnp.einsum('blhi,hij->blhj', xh, wx.astype(jnp.float32)).reshape(bsz, length, width) + bx)
    log_a = -LRU_C * r * jax.nn.softplus(a_param.astype(jnp.float32))
    a = jnp.exp(log_a)
    b = jnp.sqrt(-jnp.expm1(2.0 * log_a)) * (i * xb)
    return a, b


def _scan_combine(left, right):
    a1, b1 = left
    a2, b2 = right
    return a1 * a2, a2 * b1 + b2


def linear_scan(a, b, h0, reverse):
    if reverse:
        a, b = jnp.flip(a, 1), jnp.flip(b, 1)
    a_cum, b_cum = lax.associative_scan(_scan_combine, (a, b), axis=1)
    h = a_cum * h0[:, None] + b_cum
    h_last = h[:, -1]
    if reverse:
        h = jnp.flip(h, 1)
    return h, h_last


def odd_mixer(u_lat, u_ctx, w_in, b_in, conv_w, conv_b, wa, ba, wx, bx, a_param, w_out, b_out, need_ctx):
    def branches(u):
        xb, yb = jnp.split(u @ w_in + b_in, 2, axis=-1)
        xb = (depthwise_conv(xb, conv_w, LRU_CONV // 2) + conv_b).astype(jnp.float32)
        return xb, yb
    x_lat, gate_lat = branches(u_lat)
    x_ctx, gate_ctx = branches(u_ctx)
    zero = jnp.zeros((u_ctx.shape[0], LRU_WIDTH), jnp.float32)
    h_lat = jnp.zeros_like(x_lat)
    h_ctx = jnp.zeros_like(x_ctx)
    for d, reverse in enumerate((False, True)):
        a_c, b_c = rglru_coeffs(x_ctx, wa[d], ba[d], wx[d], bx[d], a_param[d])
        hc, hc_last = linear_scan(a_c, b_c, zero, reverse)
        a_l, b_l = rglru_coeffs(x_lat, wa[d], ba[d], wx[d], bx[d], a_param[d])
        hl, _ = linear_scan(a_l, b_l, hc_last, reverse)
        h_lat = h_lat + hl
        if need_ctx:
            h_ctx = h_ctx + hc
    y_lat = (h_lat * jax.nn.gelu(gate_lat.astype(jnp.float32))).astype(u_lat.dtype) @ w_out + b_out
    y_ctx = None
    if need_ctx:
        y_ctx = (h_ctx * jax.nn.gelu(gate_ctx.astype(jnp.float32))).astype(u_ctx.dtype) @ w_out + b_out
    return y_lat, y_ctx


def moe_ffn(h, router_w, router_b, w1, b1, w2, b2):
    logits = (h @ router_w + router_b).astype(jnp.float32)
    top_v, top_i = lax.top_k(logits, TOP_K)
    wts = jax.nn.softmax(top_v, axis=-1)
    gate = jnp.einsum('tk,tke->te', wts, jax.nn.one_hot(top_i, N_EXPERTS, dtype=jnp.float32))
    out = jnp.zeros(h.shape, jnp.float32)
    for e in range(N_EXPERTS):
        gu = h @ w1[e] + b1[e]
        glu = jnp.minimum(gu[:, 0::2], SWIGLU_LIMIT)
        lin = jnp.clip(gu[:, 1::2], -SWIGLU_LIMIT, SWIGLU_LIMIT)
        act = glu * jax.nn.sigmoid(SWIGLU_ALPHA * glu) * (lin + 1.0)
        out = out + gate[:, e:e + 1] * (act @ w2[e] + b2[e])
    return out.astype(h.dtype)


def setup_inputs(seed: int = 0) -> dict:
    key = jax.random.key(seed)
    keys = list(jax.random.split(key, 48))
    def nrm(shape, scale):
        return jax.random.normal(keys.pop(), shape, jnp.float32) * scale
    def unif(shape, lo, hi):
        return jax.random.uniform(keys.pop(), shape, jnp.float32, lo, hi)
    d = D_MODEL
    ne, no = N_EVEN, N_ODD
    dt = jnp.exp(unif((ne, 2, DN_HEADS), math.log(1e-3), math.log(1e-1)))
    rad2 = unif((no, 2, LRU_WIDTH), LRU_MIN_RAD ** 2, LRU_MAX_RAD ** 2)
    return {
        'x': nrm((BATCH, SEQ, d), 1.0),
        'c': nrm((BATCH, d), 1.0),
        'ctx': nrm((BATCH, CTX_LEN, d), 1.0),
        'c_ctx': nrm((d,), 1.0),
        'ada_w': nrm((DEPTH, d, 6 * d), 0.5 * d ** -0.5),
        'ada_b': nrm((DEPTH, 6 * d), 0.02),
        'ln_g': 1.0 + nrm((DEPTH, 2, d), 0.02),
        'ln_b': nrm((DEPTH, 2, d), 0.02),
        'ev_w_in': nrm((ne, d, EVEN_IN), d ** -0.5),
        'ev_w_out': nrm((ne, EVEN_MIX, d), OUT_INIT_SCALE * EVEN_MIX ** -0.5),
        'dn_conv_w': nrm((ne, SHORT_CONV, DN_QKV), SHORT_CONV ** -0.5),
        'dn_a_log': jnp.log(unif((ne, 2, DN_HEADS), 1.0, 16.0)),
        'dn_dt_bias': dt + jnp.log(-jnp.expm1(-dt)),
        'dn_norm_g': 1.0 + nrm((ne, DN_DV), 0.02),
        'hy_conv_w': nrm((ne, HY_SHORT, HY_IN), HY_SHORT ** -0.5),
        'hy_conv_b': nrm((ne, HY_IN), 0.02),
        'hy_w_in': nrm((ne, HY_EMB, HY_FFN), HY_EMB ** -0.5),
        'hy_b_in': nrm((ne, HY_FFN), 0.02),
        'hy_w_mid': nrm((ne, HY_INNER, HY_FFN, HY_FFN), HY_FFN ** -0.5),
        'hy_b_mid': nrm((ne, HY_INNER, HY_FFN), 0.02),
        'hy_w_out': nrm((ne, HY_FFN, 2 * HY_WIDTH), 0.1 * HY_FFN ** -0.5),
        'hy_freq': 1.0 + nrm((ne, HY_FFN), 0.1),
        'hy_skip': nrm((ne, HY_WIDTH), 0.1),
        'od_w_in': nrm((no, d, 2 * LRU_WIDTH), d ** -0.5),
        'od_b_in': nrm((no, 2 * LRU_WIDTH), 0.02),
        'lru_conv_w': nrm((no, LRU_CONV, LRU_WIDTH), LRU_CONV ** -0.5),
        'lru_conv_b': nrm((no, LRU_WIDTH), 0.02),
        'lru_wa': nrm((no, 2, LRU_HEADS, LRU_BLOCK, LRU_BLOCK), LRU_BLOCK ** -0.5),
        'lru_ba': nrm((no, 2, LRU_WIDTH), 0.02),
        'lru_wx': nrm((no, 2, LRU_HEADS, LRU_BLOCK, LRU_BLOCK), LRU_BLOCK ** -0.5),
        'lru_bx': nrm((no, 2, LRU_WIDTH), 0.02),
        'lru_a_param': jnp.log(jnp.expm1(-0.5 * jnp.log(rad2))),
        'od_w_out': nrm((no, LRU_WIDTH, d), OUT_INIT_SCALE * LRU_WIDTH ** -0.5),
        'od_b_out': nrm((no, d), 0.02),
        'router_w': nrm((DEPTH, d, N_EXPERTS), d ** -0.5),
        'router_b': nrm((DEPTH, N_EXPERTS), 0.01),
        'moe_w1': nrm((DEPTH, N_EXPERTS, d, 2 * D_FF), d ** -0.5),
        'moe_b1': nrm((DEPTH, N_EXPERTS, 2 * D_FF), 0.02),
        'moe_w2': nrm((DEPTH, N_EXPERTS, D_FF, d), OUT_INIT_SCALE * D_FF ** -0.5),
        'moe_b2': nrm((DEPTH, N_EXPERTS, d), 0.02),
    }


def reference(x, c, ctx, c_ctx, ada_w, ada_b, ln_g, ln_b, ev_w_in, ev_w_out, dn_conv_w, dn_a_log,
              dn_dt_bias, dn_norm_g, hy_conv_w, hy_conv_b, hy_w_in, hy_b_in, hy_w_mid, hy_b_mid,
              hy_w_out, hy_freq, hy_skip, od_w_in, od_b_in, lru_conv_w, lru_conv_b, lru_wa, lru_ba,
              lru_wx, lru_bx, lru_a_param, od_w_out, od_b_out, router_w, router_b, moe_w1, moe_b1,
              moe_w2, moe_b2):
    bsz, length, dim = x.shape
    rows = length // GRID_W
    pos = sincos_2d(rows, GRID_W, dim).astype(x.dtype)
    hc = ctx
    for layer in range(DEPTH):
        last = layer == DEPTH - 1
        j = layer // 2
        mod_lat = jax.nn.silu(c) @ ada_w[layer] + ada_b[layer]
        mod_ctx = jax.nn.silu(c_ctx) @ ada_w[layer] + ada_b[layer]
        sh1, sc1, g1, sh2, sc2, g2 = jnp.split(mod_lat[:, None, :], 6, axis=-1)
        csh1, csc1, cg1, csh2, csc2, cg2 = jnp.split(mod_ctx, 6, axis=-1)
        u_lat = x * (1.0 + sc1) + sh1 + pos
        u_ctx = hc * (1.0 + csc1) + csh1
        if layer % 2 == 0:
            y_lat, y_ctx = even_mixer(u_lat, u_ctx, ev_w_in[j], ev_w_out[j], dn_conv_w[j], dn_a_log[j],
                                      dn_dt_bias[j], dn_norm_g[j], hy_conv_w[j], hy_conv_b[j],
                                      (hy_w_in[j], hy_b_in[j], hy_w_mid[j], hy_b_mid[j], hy_w_out[j], hy_freq[j]),
                                      hy_skip[j], not last)
        else:
            y_lat, y_ctx = odd_mixer(u_lat, u_ctx, od_w_in[j], od_b_in[j], lru_conv_w[j], lru_conv_b[j],
                                     lru_wa[j], lru_ba[j], lru_wx[j], lru_bx[j], lru_a_param[j],
                                     od_w_out[j], od_b_out[j], not last)
        x = layer_norm(DEEPNORM_ALPHA * x + g1 * y_lat, ln_g[layer, 0], ln_b[layer, 0])
        v_lat = (x * (1.0 + sc2) + sh2).reshape(bsz * length, dim)
        if last:
            f_lat = moe_ffn(v_lat, router_w[layer], router_b[layer], moe_w1[layer], moe_b1[layer],
                            moe_w2[layer], moe_b2[layer])
        else:
            hc = layer_norm(DEEPNORM_ALPHA * hc + cg1 * y_ctx, ln_g[layer, 0], ln_b[layer, 0])
            v_ctx = (hc * (1.0 + csc2) + csh2).reshape(-1, dim)
            f_all = moe_ffn(jnp.concatenate([v_lat, v_ctx], axis=0), router_w[layer], router_b[layer],
                            moe_w1[layer], moe_b1[layer], moe_w2[layer], moe_b2[layer])
            f_lat = f_all[:bsz * length]
            hc = layer_norm(DEEPNORM_ALPHA * hc + cg2 * f_all[bsz * length:].reshape(hc.shape),
                            ln_g[layer, 1], ln_b[layer, 1])
        x = layer_norm(DEEPNORM_ALPHA * x + g2 * f_lat.reshape(bsz, length, dim), ln_g[layer, 1], ln_b[layer, 1])
    return x
```

```python
import functools
import math

import numpy as np
import jax
import jax.numpy as jnp
from jax import lax
from jax.experimental import pallas as pl
from jax.experimental.pallas import tpu as pltpu

F32 = jnp.float32
BF16 = jnp.bfloat16

VMEM_LIMIT_BYTES = 56 * 1024 * 1024
LANES = 128

DEPTH = 2
GRID_W = 64
DEEPNORM_ALPHA = (2.0 * DEPTH) ** 0.25
LN_EPS = 1e-5
RMS_EPS = 1e-6

DN_HEADS = 4
DN_DK = 128
DN_CHUNK = 64
DN_CONV = 4

HY_EMB = 33
HY_TARGET = 1e-2
HY_MIN_DECAY = math.log(HY_TARGET) / 1.5
HY_MAX_DECAY = math.log(HY_TARGET) / 0.3
HY_CONV = 3

LRU_HEADS = 4
LRU_C = 8.0
LRU_CONV = 4

N_EXPERTS = 32
TOP_K = 4
SWIGLU_ALPHA = 1.702
SWIGLU_LIMIT = 7.0


def _params(*sem):
    return pltpu.CompilerParams(dimension_semantics=sem, vmem_limit_bytes=VMEM_LIMIT_BYTES)


def _dot(a, b):
    return jnp.dot(a, b, preferred_element_type=F32)


def _dot_nt(a, b):
    return lax.dot_general(a, b, (((1,), (1,)), ((), ())), preferred_element_type=F32)


def _split(a):
    hi = a.astype(BF16)
    lo = (a - hi.astype(F32)).astype(BF16)
    return hi, lo


def _dot3(a, b):
    ah, al = _split(a)
    bh, bl = _split(b)
    return _dot(ah, bh) + _dot(ah, bl) + _dot(al, bh)


def _silu(x):
    return x * (1.0 / (1.0 + jnp.exp(-x)))


def _sigmoid(x):
    return 1.0 / (1.0 + jnp.exp(-x))


def _softplus(x):
    return jnp.maximum(x, 0.0) + jnp.log(1.0 + jnp.exp(-jnp.abs(x)))


def _layer_norm(x, g, b):
    mu = jnp.mean(x, axis=-1, keepdims=True)
    xc = x - mu
    var = jnp.mean(xc * xc, axis=-1, keepdims=True)
    return xc * lax.rsqrt(var + LN_EPS) * g + b


def _shift_rows(x, s):
    if s == 0:
        return x
    n = x.shape[0]
    rows = lax.broadcasted_iota(jnp.int32, x.shape, 0)
    valid = (rows >= s) if s > 0 else (rows < n + s)
    return jnp.where(valid, pltpu.roll(x, s % n, axis=0), 0.0)


def _depthwise_conv(x, w, pad_left):
    acc = None
    for i in range(w.shape[0]):
        term = _shift_rows(x, pad_left - i) * w[i:i + 1, :]
        acc = term if acc is None else acc + term
    return acc


def _mod_kernel(c_ref, w_ref, b_ref, o_ref):
    o_ref[0] = _dot3(_silu(c_ref[...]), w_ref[0]) + b_ref[0]


def _modulation(cond, ada_w, ada_b):
    depth, d, n = ada_w.shape
    rows = cond.shape[0]
    tn = 1536
    return pl.pallas_call(
        _mod_kernel,
        grid=(depth, n // tn),
        in_specs=[
            pl.BlockSpec((rows, d), lambda l, j: (0, 0)),
            pl.BlockSpec((1, d, tn), lambda l, j: (l, 0, j)),
            pl.BlockSpec((1, 1, tn), lambda l, j: (l, 0, j)),
        ],
        out_specs=pl.BlockSpec((1, rows, tn), lambda l, j: (l, 0, j)),
        out_shape=jax.ShapeDtypeStruct((depth, rows, n), F32),
        compiler_params=_params("parallel", "parallel"),
        name="modulation",
    )(cond, ada_w, ada_b.reshape(depth, 1, n))


def _inproj_kernel(*refs, splits, has_pos, has_bias):
    x_ref, sc_ref, sh_ref = refs[:3]
    i = 3
    pos_ref = None
    if has_pos:
        pos_ref = refs[i]
        i += 1
    w_ref = refs[i]
    i += 1
    b_ref = None
    if has_bias:
        b_ref = refs[i]
        i += 1
    o_refs = refs[i:]
    u = x_ref[0] * (1.0 + sc_ref[0]) + sh_ref[0]
    if has_pos:
        u = u + pos_ref[...]
    ub = u.astype(BF16)
    for o_ref, (s, e) in zip(o_refs, splits):
        acc = _dot(ub, w_ref[:, s:e])
        if has_bias:
            acc = acc + b_ref[:, s:e]
        o_ref[0] = acc


def _inproj(x, sc, sh, pos, w, b, widths, tm):
    bsz, length, d = x.shape
    n = w.shape[1]
    splits, s = [], 0
    for wd in widths:
        splits.append((s, s + wd))
        s += wd
    assert s == n
    tm = min(tm, length)
    in_specs = [
        pl.BlockSpec((1, tm, d), lambda bi, i: (bi, i, 0)),
        pl.BlockSpec((1, 1, d), lambda bi, i: (bi, 0, 0)),
        pl.BlockSpec((1, 1, d), lambda bi, i: (bi, 0, 0)),
    ]
    args = [x, sc, sh]
    if pos is not None:
        in_specs.append(pl.BlockSpec((tm, d), lambda bi, i: (i, 0)))
        args.append(pos)
    in_specs.append(pl.BlockSpec((d, n), lambda bi, i: (0, 0)))
    args.append(w)
    if b is not None:
        in_specs.append(pl.BlockSpec((1, n), lambda bi, i: (0, 0)))
        args.append(b)
    return pl.pallas_call(
        functools.partial(_inproj_kernel, splits=tuple(splits), has_pos=pos is not None,
                          has_bias=b is not None),
        grid=(bsz, length // tm),
        in_specs=in_specs,
        out_specs=[pl.BlockSpec((1, tm, wd), lambda bi, i: (bi, i, 0)) for wd in widths],
        out_shape=[jax.ShapeDtypeStruct((bsz, length, wd), F32) for wd in widths],
        compiler_params=_params("parallel", "parallel"),
        name="inproj",
    )(*args)


def _unit_tri_inverse(a, eye, lower):
    n = a.shape[0]
    x = jnp.where(eye, 1.0, 0.0)
    for j in (range(n - 1) if lower else range(n - 1, 0, -1)):
        x = x - a[:, j:j + 1] * x[j:j + 1, :]
    return x


def _dn_kernel(alog_ref, dtb_ref,
               ql_ref, kl_ref, vl_ref, zl_ref, qc_ref, kc_ref, vc_ref, zc_ref, gt_ref,
               cwq_ref, cwk_ref, cwv_ref, ng_ref,
               yl_ref, yc_ref,
               qn_s, kn_s, vn_s, gc_s, bt_s, wq_s, u_s, at_s, kdt_s, gl_s, o_s,
               *, ctx_len, lat_len):
    h = pl.program_id(1)
    csz = DN_CHUNK
    nc_ctx = ctx_len // csz
    nc = (ctx_len + lat_len) // csz

    def prep(src_ref, cw_ref, kind):
        t = _silu(_depthwise_conv(src_ref[0], cw_ref[...], DN_CONV // 2))
        if kind == "v":
            return t
        t = t * lax.rsqrt(jnp.sum(t * t, axis=-1, keepdims=True) + RMS_EPS)
        return t * (DN_DK ** -0.5) if kind == "q" else t

    qn_s[0:ctx_len, :] = prep(qc_ref, cwq_ref, "q")
    qn_s[ctx_len:, :] = prep(ql_ref, cwq_ref, "q")
    kn_s[0:ctx_len, :] = prep(kc_ref, cwk_ref, "k")
    kn_s[ctx_len:, :] = prep(kl_ref, cwk_ref, "k")
    vn_s[0:ctx_len, :] = prep(vc_ref, cwv_ref, "v")
    vn_s[ctx_len:, :] = prep(vl_ref, cwv_ref, "v")

    ii = lax.broadcasted_iota(jnp.int32, (csz, csz), 0)
    jj = lax.broadcasted_iota(jnp.int32, (csz, csz), 1)
    eye = ii == jj
    for d in range(2):
        graw = gt_ref[0, d * 2 * DN_HEADS + h]
        braw = gt_ref[0, d * 2 * DN_HEADS + DN_HEADS + h]
        a_neg = -jnp.exp(jnp.zeros_like(graw) + alog_ref[d, h])
        g = a_neg * _softplus(graw + dtb_ref[d, h])
        tri = jnp.where((ii <= jj) if d == 0 else (ii >= jj), 1.0, 0.0).astype(BF16)
        g1 = g.astype(BF16)
        r1 = g - g1.astype(F32)
        g2 = r1.astype(BF16)
        g3 = (r1 - g2.astype(F32)).astype(BF16)
        gc_s[d] = _dot(g1, tri) + _dot(g2, tri) + _dot(g3, tri)
        bt_s[d] = _sigmoid(braw)

    o_s[...] = jnp.zeros_like(o_s)

    def chunk_prep(n, carry):
        r0 = pl.multiple_of(n * csz, csz)
        q = qn_s[pl.ds(r0, csz), :]
        k = kn_s[pl.ds(r0, csz), :]
        v = vn_s[pl.ds(r0, csz), :]
        kbf = k.astype(BF16)
        qk = _dot_nt(q.astype(BF16), kbf)
        for d in range(2):
            incl = (ii >= jj) if d == 0 else (ii <= jj)
            strict = (ii > jj) if d == 0 else (ii < jj)
            gr = gc_s[d, pl.ds(n, 1), :]
            br = bt_s[d, pl.ds(n, 1), :]
            grow = jnp.broadcast_to(gr, (csz, csz))
            gcol = jnp.sum(jnp.where(eye, grow, 0.0), axis=1, keepdims=True)
            bcol = jnp.sum(jnp.where(eye, jnp.broadcast_to(br, (csz, csz)), 0.0), axis=1,
                           keepdims=True)
            decay = jnp.where(incl, jnp.exp(jnp.where(incl, gcol - grow, 0.0)), 0.0)
            kb = k * bcol
            kk = _dot_nt(kb.astype(BF16), kbf)
            a = jnp.where(strict, kk * decay, 0.0)
            tb = _unit_tri_inverse(a, eye, lower=(d == 0)).astype(BF16)
            eg = jnp.exp(gcol)
            u = _dot(tb, (v * bcol).astype(BF16))
            w = _dot(tb, (kb * eg).astype(BF16))
            attn = jnp.where(incl, qk * decay, 0.0)
            glast = gr[:, csz - 1:csz] if d == 0 else gr[:, 0:1]
            kd = k * jnp.exp(glast - gcol)
            wq_s[d, n, 0:csz, :] = w.astype(BF16)
            wq_s[d, n, csz:2 * csz, :] = (q * eg).astype(BF16)
            u_s[d, pl.ds(r0, csz), :] = u
            at_s[d, n] = attn.astype(BF16)
            kdt_s[d, n] = kd.T.astype(BF16)
            gl_s[d, pl.ds(n, 1), :] = jnp.broadcast_to(jnp.exp(glast), (1, LANES))
        return carry

    lax.fori_loop(0, nc, chunk_prep, 0)

    def step(i, states):
        new_states = []
        for d in range(2):
            if d == 0:
                n = i
            else:
                n = jnp.where(i < nc_ctx, nc_ctx - 1 - i, nc + nc_ctx - 1 - i)
            r0 = pl.multiple_of(n * csz, csz)
            s = states[d]
            r = _dot(wq_s[d, n], s.astype(BF16))
            v_new = (u_s[d, pl.ds(r0, csz), :] - r[0:csz]).astype(BF16)
            o = r[csz:2 * csz] + _dot(at_s[d, n], v_new)
            o_s[pl.ds(r0, csz), :] += o
            s = s * gl_s[d, pl.ds(n, 1), :] + _dot(kdt_s[d, n], v_new)
            new_states.append(s)
        return tuple(new_states)

    zero = jnp.zeros((DN_DK, DN_DK), F32)
    lax.fori_loop(0, nc, step, (zero, zero))

    def gated_norm(o, z):
        o = o * lax.rsqrt(jnp.mean(o * o, axis=-1, keepdims=True) + RMS_EPS) * ng_ref[...]
        return (o * _silu(z)).astype(yl_ref.dtype)

    yc_ref[0] = gated_norm(o_s[0:ctx_len, :], zc_ref[0])
    yl_ref[0] = gated_norm(o_s[ctx_len:, :], zl_ref[0])


def _deltanet(qkv_l, z_l, qkv_c, z_c, gates, conv_w, a_log, dt_bias, norm_g):
    bsz, lat_len, _ = qkv_l.shape
    ctx_len = qkv_c.shape[1]
    tot = ctx_len + lat_len
    nc = tot // DN_CHUNK
    ncp = gates.shape[2]
    hd = DN_DK
    nh = DN_HEADS

    def col(off):
        return lambda b, h: (b, 0, off + h)

    def wcol(off):
        return lambda b, h: (0, off + h)

    smem = pl.BlockSpec(memory_space=pltpu.SMEM)
    in_specs = [
        smem, smem,
        pl.BlockSpec((1, lat_len, hd), col(0)),
        pl.BlockSpec((1, lat_len, hd), col(nh)),
        pl.BlockSpec((1, lat_len, hd), col(2 * nh)),
        pl.BlockSpec((1, lat_len, hd), col(0)),
        pl.BlockSpec((1, ctx_len, hd), col(0)),
        pl.BlockSpec((1, ctx_len, hd), col(nh)),
        pl.BlockSpec((1, ctx_len, hd), col(2 * nh)),
        pl.BlockSpec((1, ctx_len, hd), col(0)),
        pl.BlockSpec((1, 4 * nh, ncp, DN_CHUNK), lambda b, h: (b, 0, 0, 0)),
        pl.BlockSpec((DN_CONV, hd), wcol(0)),
        pl.BlockSpec((DN_CONV, hd), wcol(nh)),
        pl.BlockSpec((DN_CONV, hd), wcol(2 * nh)),
        pl.BlockSpec((1, hd), lambda b, h: (0, 0)),
    ]
    scratch = [
        pltpu.VMEM((tot, hd), F32), pltpu.VMEM((tot, hd), F32), pltpu.VMEM((tot, hd), F32),
        pltpu.VMEM((2, ncp, DN_CHUNK), F32), pltpu.VMEM((2, ncp, DN_CHUNK), F32),
        pltpu.VMEM((2, nc, 2 * DN_CHUNK, hd), BF16),
        pltpu.VMEM((2, tot, hd), F32),
        pltpu.VMEM((2, nc, DN_CHUNK, DN_CHUNK), BF16),
        pltpu.VMEM((2, nc, hd, DN_CHUNK), BF16),
        pltpu.VMEM((2, nc, LANES), F32),
        pltpu.VMEM((tot, hd), F32),
    ]
    return pl.pallas_call(
        functools.partial(_dn_kernel, ctx_len=ctx_len, lat_len=lat_len),
        grid=(bsz, nh),
        in_specs=in_specs,
        out_specs=[pl.BlockSpec((1, lat_len, hd), col(0)),
                   pl.BlockSpec((1, ctx_len, hd), col(0))],
        out_shape=[jax.ShapeDtypeStruct((bsz, lat_len, nh * hd), BF16),
                   jax.ShapeDtypeStruct((bsz, ctx_len, nh * hd), BF16)],
        scratch_shapes=scratch,
        compiler_params=_params("parallel", "parallel"),
        name="deltanet",
    )(a_log, dt_bias, qkv_l, qkv_l, qkv_l, z_l, qkv_c, qkv_c, qkv_c, z_c, gates,
      conv_w, conv_w, conv_w, norm_g.reshape(1, hd))


def _hy_filter_kernel(z_ref, win_ref, bin_ref, wmid_ref, bmid_ref, wout_ref, freq_ref, dec_ref,
                      o_ref):
    freq = freq_ref[...]
    hcur = jnp.sin(freq * (_dot3(z_ref[...], win_ref[...]) + bin_ref[...]))
    for i in range(wmid_ref.shape[0]):
        hcur = jnp.sin(freq * (_dot3(hcur, wmid_ref[i]) + bmid_ref[i]))
    o_ref[...] = _dot3(hcur, wout_ref[...]) * dec_ref[...]


def _hy_filter(z, w_in, b_in, w_mid, b_mid, w_out, freq, dec2):
    length = z.shape[0]
    n_out = w_out.shape[1]
    tl = min(256, length)

    def whole(a):
        return pl.BlockSpec(a.shape, lambda i: (0,) * a.ndim)

    return pl.pallas_call(
        _hy_filter_kernel,
        grid=(length // tl,),
        in_specs=[pl.BlockSpec((tl, z.shape[1]), lambda i: (i, 0)),
                  whole(w_in), whole(b_in), whole(w_mid), whole(b_mid), whole(w_out), whole(freq),
                  pl.BlockSpec((tl, n_out), lambda i: (i, 0))],
        out_specs=pl.BlockSpec((tl, n_out), lambda i: (i, 0)),
        out_shape=jax.ShapeDtypeStruct((length, n_out), F32),
        compiler_params=_params("parallel"),
        name="hyena_filter",
    )(z, w_in, b_in, w_mid, b_mid, w_out, freq, dec2)


def _filt_spec_kernel(fc_ref, fs_ref, h_ref, kc_ref, ks_ref, *, width):
    j = pl.program_id(0)
    hmat = h_ref[...]
    rows = lax.broadcasted_iota(jnp.int32, hmat.shape, 0)
    cols = lax.broadcasted_iota(jnp.int32, hmat.shape, 1)
    hmat = jnp.where((rows == 0) & (cols >= width), 0.0, hmat)
    hh, hl = _split(hmat)
    c = _dot(fc_ref[...], hh) + _dot(fc_ref[...], hl)
    s = _dot(fs_ref[...], hh) + _dot(fs_ref[...], hl)
    kc_ref[...] = c[:, :width] + c[:, width:]
    orow = lax.broadcasted_iota(jnp.int32, (c.shape[0], width), 0)
    sign = jnp.where((orow == 0) & (j == 0), 1.0, -1.0)
    ks_ref[...] = s[:, :width] + sign * s[:, width:]


def _filt_spec(fwd, hfilt, tf):
    length, two_w = hfilt.shape
    width = two_w // 2
    nt = length // tf
    return pl.pallas_call(
        functools.partial(_filt_spec_kernel, width=width),
        grid=(nt,),
        in_specs=[pl.BlockSpec((tf, length), lambda j: (j, 0)),
                  pl.BlockSpec((tf, length), lambda j: (nt + j, 0)),
                  pl.BlockSpec((length, two_w), lambda j: (0, 0))],
        out_specs=[pl.BlockSpec((tf, width), lambda j: (j, 0)),
                   pl.BlockSpec((tf, width), lambda j: (j, 0))],
        out_shape=[jax.ShapeDtypeStruct((length, width), F32)] * 2,
        compiler_params=_params("arbitrary"),
        name="hyena_filter_spectrum",
    )(fwd, fwd, hfilt)


def _hy_prep_kernel(x0_ref, x1_ref, v_ref, w0_ref, w1_ref, w2_ref, b0_ref, b1_ref, b2_ref,
                    x0o_ref, vvo_ref):
    x0 = _depthwise_conv(x0_ref[0], w0_ref[...], HY_CONV // 2) + b0_ref[...]
    x1 = _depthwise_conv(x1_ref[0], w1_ref[...], HY_CONV // 2) + b1_ref[...]
    v = _depthwise_conv(v_ref[0], w2_ref[...], HY_CONV // 2) + b2_ref[...]
    x0o_ref[0] = x0
    vvo_ref[0] = v * x1


def _hy_prep(p_hy, conv_w, conv_b):
    bsz, length, three_w = p_hy.shape
    width = three_w // 3
    nb = width // LANES

    def col(off):
        return lambda b, j: (b, 0, off + j)

    def wcol(off):
        return lambda b, j: (0, off + j)

    k = conv_w.shape[0]
    return pl.pallas_call(
        _hy_prep_kernel,
        grid=(bsz, nb),
        in_specs=[pl.BlockSpec((1, length, LANES), col(0)),
                  pl.BlockSpec((1, length, LANES), col(nb)),
                  pl.BlockSpec((1, length, LANES), col(2 * nb)),
                  pl.BlockSpec((k, LANES), wcol(0)),
                  pl.BlockSpec((k, LANES), wcol(nb)),
                  pl.BlockSpec((k, LANES), wcol(2 * nb)),
                  pl.BlockSpec((1, LANES), wcol(0)),
                  pl.BlockSpec((1, LANES), wcol(nb)),
                  pl.BlockSpec((1, LANES), wcol(2 * nb))],
        out_specs=[pl.BlockSpec((1, length, LANES), col(0)),
                   pl.BlockSpec((1, length, LANES), col(0))],
        out_shape=[jax.ShapeDtypeStruct((bsz, length, width), F32)] * 2,
        compiler_params=_params("parallel", "parallel"),
        name="hyena_prep",
    )(p_hy, p_hy, p_hy, conv_w, conv_w, conv_w, conv_b, conv_b, conv_b)


def _dft_fwd_kernel(fc_ref, fs_ref, v_ref, kc_ref, ks_ref, yc_ref, ys_ref, vb_s):
    j = pl.program_id(1)

    @pl.when(j == 0)
    def _():
        vb_s[...] = v_ref[0].astype(BF16)

    uc = _dot(fc_ref[...], vb_s[...])
    us = _dot(fs_ref[...], vb_s[...])
    kc = kc_ref[...]
    ks = ks_ref[...]
    rows = lax.broadcasted_iota(jnp.int32, uc.shape, 0)
    special = (rows == 0) & (j == 0)
    yc_ref[0] = (uc * kc - jnp.where(special, 0.0, us * ks)).astype(BF16)
    ys_ref[0] = jnp.where(special, us * ks, uc * ks + us * kc).astype(BF16)


def _dft_fwd(fwd, vv, kc, ks, tf):
    bsz, length, width = vv.shape
    nt = length // tf
    return pl.pallas_call(
        _dft_fwd_kernel,
        grid=(bsz, nt),
        in_specs=[pl.BlockSpec((tf, length), lambda b, j: (j, 0)),
                  pl.BlockSpec((tf, length), lambda b, j: (nt + j, 0)),
                  pl.BlockSpec((1, length, width), lambda b, j: (b, 0, 0)),
                  pl.BlockSpec((tf, width), lambda b, j: (j, 0)),
                  pl.BlockSpec((tf, width), lambda b, j: (j, 0))],
        out_specs=[pl.BlockSpec((1, tf, width), lambda b, j: (b, j, 0)),
                   pl.BlockSpec((1, tf, width), lambda b, j: (b, j, 0))],
        out_shape=[jax.ShapeDtypeStruct((bsz, length, width), BF16)] * 2,
        scratch_shapes=[pltpu.VMEM((length, width), BF16)],
        compiler_params=_params("parallel", "arbitrary"),
        name="hyena_dft_fwd",
    )(fwd, fwd, vv, kc, ks)


def _dft_inv_kernel(ic_ref, is_ref, yc_ref, ys_ref, vv_ref, x0_ref, skip_ref, o_ref):
    y = _dot(ic_ref[...], yc_ref[0]) + _dot(is_ref[...], ys_ref[0])
    o_ref[0] = (x0_ref[0] * (y + vv_ref[0] * skip_ref[...])).astype(o_ref.dtype)


def _dft_inv(inv, yc, ys, vv, x0, skip, tt):
    bsz, length, width = vv.shape
    nt = length // tt
    return pl.pallas_call(
        _dft_inv_kernel,
        grid=(bsz, nt),
        in_specs=[pl.BlockSpec((tt, length), lambda b, i: (i, 0)),
                  pl.BlockSpec((tt, length), lambda b, i: (i, 1)),
                  pl.BlockSpec((1, length, width), lambda b, i: (b, 0, 0)),
                  pl.BlockSpec((1, length, width), lambda b, i: (b, 0, 0)),
                  pl.BlockSpec((1, tt, width), lambda b, i: (b, i, 0)),
                  pl.BlockSpec((1, tt, width), lambda b, i: (b, i, 0)),
                  pl.BlockSpec((1, width), lambda b, i: (0, 0))],
        out_specs=pl.BlockSpec((1, tt, width), lambda b, i: (b, i, 0)),
        out_shape=jax.ShapeDtypeStruct((bsz, length, width), BF16),
        compiler_params=_params("parallel", "parallel"),
        name="hyena_dft_inv",
    )(inv, inv, yc, ys, vv, x0, skip)


@functools.lru_cache(maxsize=None)
def _hyena_tables(length):
    n2 = 2 * length
    t = np.linspace(0.0, 1.0, length)[:, None]
    bands = (HY_EMB - 1) // 2
    wpos = 2.0 * np.pi * np.arange(length)[:, None] / length
    fb = np.linspace(1e-4, bands - 1, bands)[None]
    z = np.concatenate([t, np.cos(fb * wpos), -np.sin(fb * wpos)], axis=-1)
    zpad = np.zeros((length, LANES))
    zpad[:, :HY_EMB] = z
    f = np.arange(length)[:, None]
    n = np.arange(length)[None, :]
    ang = 2.0 * np.pi * ((f * n) % n2) / n2
    cos_m = np.cos(ang)
    sin_m = np.sin(ang)
    sin_m[0, :] = np.cos(np.pi * np.arange(length))
    fwd = np.concatenate([cos_m, sin_m], axis=0)
    scale = np.full((1, n2), 2.0 / n2)
    scale[0, 0] = 1.0 / n2
    scale[0, length] = 1.0 / n2
    inv = fwd.T * scale
    return (zpad.astype(np.float32), t.astype(np.float32), fwd.astype(np.float32),
            inv.astype(np.float32))


def _hyena(p_hy, conv_w, conv_b, filt, skip):
    bsz, length, three_w = p_hy.shape
    width = three_w // 3
    w_in, b_in, w_mid, b_mid, w_out, freq = filt
    zpad, t, fwd, inv = _hyena_tables(length)
    deltas = np.abs(np.linspace(HY_MIN_DECAY, HY_MAX_DECAY, width))[None, :]
    dec = np.exp(-t.astype(np.float64) * deltas).astype(np.float32)
    dec2 = jnp.asarray(np.concatenate([dec, dec], axis=1))
    ffn = w_in.shape[1]
    w_in_pad = jnp.zeros((LANES, ffn), F32).at[:HY_EMB].set(w_in)
    hfilt = _hy_filter(jnp.asarray(zpad), w_in_pad, b_in.reshape(1, ffn), w_mid,
                       b_mid.reshape(-1, 1, ffn), w_out, freq.reshape(1, ffn), dec2)
    fwd_b = jnp.asarray(fwd).astype(BF16)
    inv_b = jnp.asarray(inv).astype(BF16)
    tf = min(256, length)
    kc, ks = _filt_spec(fwd_b, hfilt, tf)
    x0, vv = _hy_prep(p_hy, conv_w, conv_b.reshape(1, three_w))
    yc, ys = _dft_fwd(fwd_b, vv, kc, ks, tf)
    return _dft_inv(inv_b, yc, ys, vv, x0, skip.reshape(1, width), min(256, length))


def _gelu_tanh(x):
    return 0.5 * x * (1.0 + jnp.tanh(math.sqrt(2.0 / math.pi) * (x + 0.044715 * x * x * x)))


def _lru_kernel(xl_ref, yl_ref, xc_ref, cw_ref, cb_ref, wa_ref, ba_ref, wx_ref, bx_ref, ap_ref,
                o_ref, xs_s, a_s, b_s, h_s, *, ctx_len, lat_len):
    tot = ctx_len + lat_len
    ngrp = tot // 8
    ngrp_ctx = ctx_len // 8
    xs_s[0:ctx_len, :] = _depthwise_conv(xc_ref[0], cw_ref[...], LRU_CONV // 2) + cb_ref[...]
    xs_s[ctx_len:, :] = _depthwise_conv(xl_ref[0], cw_ref[...], LRU_CONV // 2) + cb_ref[...]
    xs = xs_s[...]
    xsb = xs.astype(BF16)
    rows8 = lax.broadcasted_iota(jnp.int32, xs.shape, 0) % 8
    for d in range(2):
        r = _sigmoid(_dot(xsb, wa_ref[d, 0].astype(BF16)) + ba_ref[d])
        gi = _sigmoid(_dot(xsb, wx_ref[d, 0].astype(BF16)) + bx_ref[d])
        log_a = -LRU_C * r * _softplus(ap_ref[d])
        a = jnp.exp(log_a)
        b = jnp.sqrt(1.0 - a * a) * (gi * xs)
        for s in (1, 2, 4):
            if d == 0:
                keep = rows8 >= s
                sa = jnp.where(keep, pltpu.roll(a, s, axis=0), 1.0)
                sb = jnp.where(keep, pltpu.roll(b, s, axis=0), 0.0)
            else:
                keep = rows8 < 8 - s
                sa = jnp.where(keep, pltpu.roll(a, tot - s, axis=0), 1.0)
                sb = jnp.where(keep, pltpu.roll(b, tot - s, axis=0), 0.0)
            b = a * sb + b
            a = a * sa
        a_s[d] = a
        b_s[d] = b

    def group_fwd(i, c):
        r0 = pl.multiple_of(i * 8, 8)
        hg = a_s[0, pl.ds(r0, 8), :] * c + b_s[0, pl.ds(r0, 8), :]
        h_s[pl.ds(r0, 8), :] = hg
        return jnp.broadcast_to(hg[7:8, :], hg.shape)

    def group_bwd(i, c):
        gidx = jnp.where(i < ngrp_ctx, ngrp_ctx - 1 - i, ngrp + ngrp_ctx - 1 - i)
        r0 = pl.multiple_of(gidx * 8, 8)
        hg = a_s[1, pl.ds(r0, 8), :] * c + b_s[1, pl.ds(r0, 8), :]
        h_s[pl.ds(r0, 8), :] += hg
        return jnp.broadcast_to(hg[0:1, :], hg.shape)

    zero = jnp.zeros((8, xs.shape[1]), F32)
    lax.fori_loop(0, ngrp, group_fwd, zero, unroll=4)
    lax.fori_loop(0, ngrp, group_bwd, zero, unroll=4)
    o_ref[0] = (h_s[ctx_len:, :] * _gelu_tanh(yl_ref[0])).astype(o_ref.dtype)


def _rglru(xb_l, yb_l, xb_c, conv_w, conv_b, wa, ba, wx, bx, a_param):
    bsz, lat_len, width = xb_l.shape
    ctx_len = xb_c.shape[1]
    tot = ctx_len + lat_len
    blk = width // LRU_HEADS

    def col(b, h):
        return (b, 0, h)

    def wcol(b, h):
        return (0, h)

    def w3(b, h):
        return (0, 0, h)

    return pl.pallas_call(
        functools.partial(_lru_kernel, ctx_len=ctx_len, lat_len=lat_len),
        grid=(bsz, LRU_HEADS),
        in_specs=[pl.BlockSpec((1, lat_len, blk), col),
                  pl.BlockSpec((1, lat_len, blk), col),
                  pl.BlockSpec((1, ctx_len, blk), col),
                  pl.BlockSpec((LRU_CONV, blk), wcol),
                  pl.BlockSpec((1, blk), wcol),
                  pl.BlockSpec((2, 1, blk, blk), lambda b, h: (0, h, 0, 0)),
                  pl.BlockSpec((2, 1, blk), w3),
                  pl.BlockSpec((2, 1, blk, blk), lambda b, h: (0, h, 0, 0)),
                  pl.BlockSpec((2, 1, blk), w3),
                  pl.BlockSpec((2, 1, blk), w3)],
        out_specs=pl.BlockSpec((1, lat_len, blk), col),
        out_shape=jax.ShapeDtypeStruct((bsz, lat_len, width), BF16),
        scratch_shapes=[pltpu.VMEM((tot, blk), F32),
                        pltpu.VMEM((2, tot, blk), F32),
                        pltpu.VMEM((2, tot, blk), F32),
                        pltpu.VMEM((tot, blk), F32)],
        compiler_params=_params("parallel", "parallel"),
        name="rglru",
    )(xb_l, yb_l, xb_c, conv_w, conv_b.reshape(1, width), wa, ba.reshape(2, 1, width), wx,
      bx.reshape(2, 1, width), a_param.reshape(2, 1, width))


def _post_mixer_kernel(*refs, n_in, has_bias):
    a_refs = refs[:n_in]
    w_refs = refs[n_in:2 * n_in]
    i = 2 * n_in
    b_ref = None
    if has_bias:
        b_ref = refs[i]
        i += 1
    x_ref, g1_ref, sc_ref, sh_ref, lng_ref, lnb_ref, rw_ref, rb_ref = refs[i:i + 8]
    x1_ref, v_ref, gate_ref = refs[i + 8:]
    y = None
    for a_ref, w_ref in zip(a_refs, w_refs):
        t = _dot(a_ref[0], w_ref[...])
        y = t if y is None else y + t
    if has_bias:
        y = y + b_ref[...]
    x1 = _layer_norm(DEEPNORM_ALPHA * x_ref[0] + g1_ref[0] * y, lng_ref[...], lnb_ref[...])
    x1_ref[0] = x1
    v = x1 * (1.0 + sc_ref[0]) + sh_ref[0]
    v_ref[0] = v.astype(BF16)
    logits = _dot3(v, rw_ref[...]) + rb_ref[...]
    lane = lax.broadcasted_iota(jnp.int32, logits.shape, 1).astype(F32)
    work = logits
    sel = jnp.zeros(logits.shape, jnp.bool_)
    m0 = None
    for kk in range(TOP_K):
        m = jnp.max(work, axis=-1, keepdims=True)
        if kk == 0:
            m0 = m
        first = jnp.min(jnp.where(work == m, lane, float(LANES)), axis=-1, keepdims=True)
        pick = lane == first
        sel = jnp.logical_or(sel, pick)
        work = jnp.where(pick, -jnp.inf, work)
    e = jnp.where(sel, jnp.exp(logits - m0), 0.0)
    gate_ref[0] = e / jnp.sum(e, axis=-1, keepdims=True)


def _post_mixer(acts, ws, bias, x, g1, sc2, sh2, ln_g, ln_b, router_w, router_b, tm):
    bsz, length, d = x.shape
    tm = min(tm, length)
    n_in = len(acts)

    def row(bi, i):
        return (bi, i, 0)

    def per_b(bi, i):
        return (bi, 0, 0)

    def const(bi, i):
        return (0, 0)

    in_specs = [pl.BlockSpec((1, tm, a.shape[2]), row) for a in acts]
    in_specs += [pl.BlockSpec(w.shape, const) for w in ws]
    args = list(acts) + list(ws)
    if bias is not None:
        in_specs.append(pl.BlockSpec((1, d), const))
        args.append(bias.reshape(1, d))
    in_specs += [pl.BlockSpec((1, tm, d), row),
                 pl.BlockSpec((1, 1, d), per_b), pl.BlockSpec((1, 1, d), per_b),
                 pl.BlockSpec((1, 1, d), per_b),
                 pl.BlockSpec((1, d), const), pl.BlockSpec((1, d), const),
                 pl.BlockSpec((d, LANES), const), pl.BlockSpec((1, LANES), const)]
    rw = jnp.zeros((d, LANES), F32).at[:, :N_EXPERTS].set(router_w)
    rb = jnp.full((1, LANES), -1e30, F32).at[0, :N_EXPERTS].set(router_b)
    args += [x, g1, sc2, sh2, ln_g.reshape(1, d), ln_b.reshape(1, d), rw, rb]
    return pl.pallas_call(
        functools.partial(_post_mixer_kernel, n_in=n_in, has_bias=bias is not None),
        grid=(bsz, length // tm),
        in_specs=in_specs,
        out_specs=[pl.BlockSpec((1, tm, d), row), pl.BlockSpec((1, tm, d), row),
                   pl.BlockSpec((1, tm, LANES), row)],
        out_shape=[jax.ShapeDtypeStruct((bsz, length, d), F32),
                   jax.ShapeDtypeStruct((bsz, length, d), BF16),
                   jax.ShapeDtypeStruct((bsz, length, LANES), F32)],
        compiler_params=_params("parallel", "parallel"),
        name="post_mixer",
    )(*args)


def _swiglu_interleaved(gu):
    nxt = pltpu.roll(gu, gu.shape[1] - 1, axis=1)
    glu = jnp.minimum(gu, SWIGLU_LIMIT)
    lin = jnp.clip(nxt, -SWIGLU_LIMIT, SWIGLU_LIMIT)
    act = glu * _sigmoid(SWIGLU_ALPHA * glu) * (lin + 1.0)
    lane = lax.broadcasted_iota(jnp.int32, gu.shape, 1)
    return jnp.where(lane % 2 == 0, act, 0.0)


def _moe_kernel(v_ref, gate_ref, w1_ref, b1_ref, w2_ref, b2_ref,
                x1_ref, g2_ref, lng_ref, lnb_ref, o_ref, acc_s):
    e = pl.program_id(2)

    @pl.when(e == 0)
    def _():
        acc_s[...] = jnp.zeros_like(acc_s)

    act = _swiglu_interleaved(_dot(v_ref[0], w1_ref[0]) + b1_ref[0])
    y = _dot(act.astype(BF16), w2_ref[0]) + b2_ref[0]
    gate = gate_ref[0]
    lane = lax.broadcasted_iota(jnp.int32, gate.shape, 1)
    gcol = jnp.sum(jnp.where(lane == e, gate, 0.0), axis=-1, keepdims=True)
    acc_s[...] += gcol * y

    @pl.when(e == pl.num_programs(2) - 1)
    def _():
        o_ref[0] = _layer_norm(DEEPNORM_ALPHA * x1_ref[0] + g2_ref[0] * acc_s[...],
                               lng_ref[...], lnb_ref[...])


def _moe(v, gate, w1, b1, w2z, b2, x1, g2, ln_g, ln_b, tm):
    bsz, length, d = x1.shape
    tm = min(tm, length)
    n_exp, _, dff2 = w1.shape

    def row(bi, i, e):
        return (bi, i, 0)

    def exp3(bi, i, e):
        return (e, 0, 0)

    def const(bi, i, e):
        return (0, 0)

    return pl.pallas_call(
        _moe_kernel,
        grid=(bsz, length // tm, n_exp),
        in_specs=[pl.BlockSpec((1, tm, d), row),
                  pl.BlockSpec((1, tm, LANES), row),
                  pl.BlockSpec((1, d, dff2), exp3), pl.BlockSpec((1, 1, dff2), exp3),
                  pl.BlockSpec((1, dff2, d), exp3), pl.BlockSpec((1, 1, d), exp3),
                  pl.BlockSpec((1, tm, d), row),
                  pl.BlockSpec((1, 1, d), lambda bi, i, e: (bi, 0, 0)),
                  pl.BlockSpec((1, d), const), pl.BlockSpec((1, d), const)],
        out_specs=pl.BlockSpec((1, tm, d), row),
        out_shape=jax.ShapeDtypeStruct((bsz, length, d), F32),
        scratch_shapes=[pltpu.VMEM((tm, d), F32)],
        compiler_params=_params("parallel", "parallel", "arbitrary"),
        name="moe",
    )(v, gate, w1, b1, w2z, b2, x1, g2, ln_g.reshape(1, d), ln_b.reshape(1, d))


@functools.lru_cache(maxsize=None)
def _sincos_2d(rows, cols, dim):
    quarter = dim // 4
    omega = 1.0 / (10000.0 ** (np.arange(quarter, dtype=np.float64) / quarter))

    def emb1d(n):
        ang = np.arange(n, dtype=np.float64)[:, None] * omega
        return np.concatenate([np.sin(ang), np.cos(ang)], axis=-1)

    er = np.broadcast_to(emb1d(rows)[:, None], (rows, cols, dim // 2))
    ec = np.broadcast_to(emb1d(cols)[None], (rows, cols, dim // 2))
    return np.concatenate([er, ec], axis=-1).reshape(rows * cols, dim).astype(np.float32)


def _gate_layout(gates, n_ch):
    bsz, length, _ = gates.shape
    g = gates[..., :n_ch].reshape(bsz, length // DN_CHUNK, DN_CHUNK, n_ch)
    return g.transpose(0, 3, 1, 2)


def kernel(x, c, ctx, c_ctx, ada_w, ada_b, ln_g, ln_b, ev_w_in, ev_w_out, dn_conv_w, dn_a_log, dn_dt_bias, dn_norm_g, hy_conv_w, hy_conv_b, hy_w_in, hy_b_in, hy_w_mid, hy_b_mid, hy_w_out, hy_freq, hy_skip, od_w_in, od_b_in, lru_conv_w, lru_conv_b, lru_wa, lru_ba, lru_wx, lru_bx, lru_a_param, od_w_out, od_b_out, router_w, router_b, moe_w1, moe_b1, moe_w2, moe_b2):
    bsz, length, d = x.shape
    ctx_len = ctx.shape[1]
    pos = jnp.asarray(_sincos_2d(length // GRID_W, GRID_W, d))

    cond = jnp.zeros((16, d), F32).at[:bsz].set(c).at[bsz].set(c_ctx)
    mod = _modulation(cond, ada_w, ada_b).reshape(DEPTH, 16, 6, d)

    def lat_mod(layer, k):
        return mod[layer, :bsz, k][:, None, :]

    def ctx_mod(layer, k):
        return jnp.broadcast_to(mod[layer, bsz, k][None, None, :], (bsz, 1, d))

    hc = ctx
    for layer in range(DEPTH):
        last = layer == DEPTH - 1
        j = layer // 2
        if layer % 2 == 0:
            dn_qk = DN_HEADS * DN_DK
            dn_qkv = 3 * dn_qk
            dn_in = dn_qkv + dn_qk + 4 * DN_HEADS
            w_in = ev_w_in[j]
            gate_w = jnp.zeros((d, LANES), F32).at[:, :4 * DN_HEADS].set(w_in[:, dn_qkv + dn_qk:dn_in])
            w_cat = jnp.concatenate([w_in[:, :dn_qkv + dn_qk], gate_w, w_in[:, dn_in:]],
                                    axis=1).astype(BF16)
            hy_in = w_in.shape[1] - dn_in
            widths = (dn_qkv, dn_qk, LANES, hy_in)
            qkv_l, z_l, gt_l, phy_l = _inproj(x, lat_mod(layer, 1), lat_mod(layer, 0), pos,
                                               w_cat, None, widths, 512)
            qkv_c, z_c, gt_c, phy_c = _inproj(hc, ctx_mod(layer, 1), ctx_mod(layer, 0), None,
                                               w_cat, None, widths, 512)
            gates = jnp.concatenate([_gate_layout(gt_c, 4 * DN_HEADS),
                                     _gate_layout(gt_l, 4 * DN_HEADS)], axis=2)
            n_chunks = gates.shape[2]
            gates = jnp.pad(gates, ((0, 0), (0, 0), (0, -n_chunks % 16), (0, 0)))
            dn_l, dn_c = _deltanet(qkv_l, z_l, qkv_c, z_c, gates, dn_conv_w[j], dn_a_log[j],
                                   dn_dt_bias[j], dn_norm_g[j])
            filt = (hy_w_in[j], hy_b_in[j], hy_w_mid[j], hy_b_mid[j], hy_w_out[j], hy_freq[j])
            hy_l = _hyena(phy_l, hy_conv_w[j], hy_conv_b[j], filt, hy_skip[j])
            w_out = ev_w_out[j].astype(BF16)
            half = dn_l.shape[2]
            acts_l, ws, bias = (dn_l, hy_l), (w_out[:half], w_out[half:]), None
            acts_c = None
            if not last:
                hy_c = _hyena(phy_c, hy_conv_w[j], hy_conv_b[j], filt, hy_skip[j])
                acts_c = (dn_c, hy_c)
        else:
            w_in = od_w_in[j].astype(BF16)
            width = w_in.shape[1] // 2
            b_in = od_b_in[j].reshape(1, 2 * width)
            xb_l, yb_l = _inproj(x, lat_mod(layer, 1), lat_mod(layer, 0), pos, w_in, b_in,
                                 (width, width), 512)
            xb_c, _ = _inproj(hc, ctx_mod(layer, 1), ctx_mod(layer, 0), None, w_in, b_in,
                              (width, width), 512)
            act_l = _rglru(xb_l, yb_l, xb_c, lru_conv_w[j], lru_conv_b[j], lru_wa[j], lru_ba[j],
                           lru_wx[j], lru_bx[j], lru_a_param[j])
            acts_l, ws, bias = (act_l,), (od_w_out[j].astype(BF16),), od_b_out[j]
            acts_c = None
            assert last, "context outputs of the RG-LRU layer are only needed before the last layer"

        w1 = moe_w1[layer].astype(BF16)
        b1 = moe_b1[layer][:, None, :]
        w2 = moe_w2[layer].astype(BF16)
        n_exp, dff, _ = w2.shape
        w2z = jnp.stack([w2, jnp.zeros_like(w2)], axis=2).reshape(n_exp, 2 * dff, d)
        b2 = moe_b2[layer][:, None, :]

        x1, v, gate = _post_mixer(acts_l, ws, bias, x, lat_mod(layer, 2), lat_mod(layer, 4),
                                  lat_mod(layer, 3), ln_g[layer, 0], ln_b[layer, 0],
                                  router_w[layer], router_b[layer], 256)
        x = _moe(v, gate, w1, b1, w2z, b2, x1, lat_mod(layer, 5),
                 ln_g[layer, 1], ln_b[layer, 1], 512)
        if not last:
            hc1, vc, gate_c = _post_mixer(acts_c, ws, bias, hc, ctx_mod(layer, 2),
                                          ctx_mod(layer, 4), ctx_mod(layer, 3), ln_g[layer, 0],
                                          ln_b[layer, 0], router_w[layer], router_b[layer], 256)
            hc = _moe(vc, gate_c, w1, b1, w2z, b2, hc1, ctx_mod(layer, 5),
                      ln_g[layer, 1], ln_b[layer, 1], 512)
    return x
```

```python
import functools
import math

import numpy as np
import jax
import jax.numpy as jnp
from jax import lax
from jax.experimental import pallas as pl
from jax.experimental.pallas import tpu as pltpu
from jax.experimental.pallas import tpu_sc as plsc

F32 = jnp.float32
BF16 = jnp.bfloat16

VMEM_LIMIT_BYTES = 56 * 1024 * 1024
LANES = 128

DEPTH = 2
GRID_W = 64
DEEPNORM_ALPHA = (2.0 * DEPTH) ** 0.25
LN_EPS = 1e-5
RMS_EPS = 1e-6

DN_HEADS = 4
DN_DK = 128
DN_CHUNK = 64
DN_CONV = 4

HY_EMB = 33
HY_TARGET = 1e-2
HY_MIN_DECAY = math.log(HY_TARGET) / 1.5
HY_MAX_DECAY = math.log(HY_TARGET) / 0.3
HY_CONV = 3

LRU_HEADS = 4
LRU_C = 8.0
LRU_CONV = 4

N_EXPERTS = 32
TOP_K = 4
SWIGLU_ALPHA = 1.702
SWIGLU_LIMIT = 7.0


def _params(*sem):
    return pltpu.CompilerParams(dimension_semantics=sem, vmem_limit_bytes=VMEM_LIMIT_BYTES)


def _dot(a, b):
    return jnp.dot(a, b, preferred_element_type=F32)


def _dot_nt(a, b):
    return lax.dot_general(a, b, (((1,), (1,)), ((), ())), preferred_element_type=F32)


def _split(a):
    hi = a.astype(BF16)
    lo = (a - hi.astype(F32)).astype(BF16)
    return hi, lo


def _dot3(a, b):
    ah, al = _split(a)
    bh, bl = _split(b)
    return _dot(ah, bh) + _dot(ah, bl) + _dot(al, bh)


def _silu(x):
    return x * (1.0 / (1.0 + jnp.exp(-x)))


def _sigmoid(x):
    return 1.0 / (1.0 + jnp.exp(-x))


def _softplus(x):
    return jnp.maximum(x, 0.0) + jnp.log(1.0 + jnp.exp(-jnp.abs(x)))


def _layer_norm(x, g, b):
    mu = jnp.mean(x, axis=-1, keepdims=True)
    xc = x - mu
    var = jnp.mean(xc * xc, axis=-1, keepdims=True)
    return xc * lax.rsqrt(var + LN_EPS) * g + b


def _shift_rows(x, s):
    if s == 0:
        return x
    n = x.shape[0]
    rows = lax.broadcasted_iota(jnp.int32, x.shape, 0)
    valid = (rows >= s) if s > 0 else (rows < n + s)
    return jnp.where(valid, pltpu.roll(x, s % n, axis=0), 0.0)


def _depthwise_conv(x, w, pad_left):
    acc = None
    for i in range(w.shape[0]):
        term = _shift_rows(x, pad_left - i) * w[i:i + 1, :]
        acc = term if acc is None else acc + term
    return acc


def _mod_kernel(c_ref, w_ref, b_ref, o_ref):
    o_ref[0] = _dot3(_silu(c_ref[...]), w_ref[0]) + b_ref[0]


def _modulation(cond, ada_w, ada_b):
    depth, d, n = ada_w.shape
    rows = cond.shape[0]
    tn = 1536
    return pl.pallas_call(
        _mod_kernel,
        grid=(depth, n // tn),
        in_specs=[
            pl.BlockSpec((rows, d), lambda l, j: (0, 0)),
            pl.BlockSpec((1, d, tn), lambda l, j: (l, 0, j)),
            pl.BlockSpec((1, 1, tn), lambda l, j: (l, 0, j)),
        ],
        out_specs=pl.BlockSpec((1, rows, tn), lambda l, j: (l, 0, j)),
        out_shape=jax.ShapeDtypeStruct((depth, rows, n), F32),
        compiler_params=_params("parallel", "parallel"),
        name="modulation",
    )(cond, ada_w, ada_b.reshape(depth, 1, n))


def _inproj_kernel(*refs, splits, has_pos, has_bias):
    x_ref, sc_ref, sh_ref = refs[:3]
    i = 3
    pos_ref = None
    if has_pos:
        pos_ref = refs[i]
        i += 1
    w_ref = refs[i]
    i += 1
    b_ref = None
    if has_bias:
        b_ref = refs[i]
        i += 1
    o_refs = refs[i:]
    u = x_ref[0] * (1.0 + sc_ref[0]) + sh_ref[0]
    if has_pos:
        u = u + pos_ref[...]
    ub = u.astype(BF16)
    for o_ref, (s, e) in zip(o_refs, splits):
        acc = _dot(ub, w_ref[:, s:e])
        if has_bias:
            acc = acc + b_ref[:, s:e]
        o_ref[0] = acc


def _inproj(x, sc, sh, pos, w, b, widths, tm):
    bsz, length, d = x.shape
    n = w.shape[1]
    splits, s = [], 0
    for wd in widths:
        splits.append((s, s + wd))
        s += wd
    assert s == n
    tm = min(tm, length)
    in_specs = [
        pl.BlockSpec((1, tm, d), lambda bi, i: (bi, i, 0)),
        pl.BlockSpec((1, 1, d), lambda bi, i: (bi, 0, 0)),
        pl.BlockSpec((1, 1, d), lambda bi, i: (bi, 0, 0)),
    ]
    args = [x, sc, sh]
    if pos is not None:
        in_specs.append(pl.BlockSpec((tm, d), lambda bi, i: (i, 0)))
        args.append(pos)
    in_specs.append(pl.BlockSpec((d, n), lambda bi, i: (0, 0)))
    args.append(w)
    if b is not None:
        in_specs.append(pl.BlockSpec((1, n), lambda bi, i: (0, 0)))
        args.append(b)
    return pl.pallas_call(
        functools.partial(_inproj_kernel, splits=tuple(splits), has_pos=pos is not None,
                          has_bias=b is not None),
        grid=(bsz, length // tm),
        in_specs=in_specs,
        out_specs=[pl.BlockSpec((1, tm, wd), lambda bi, i: (bi, i, 0)) for wd in widths],
        out_shape=[jax.ShapeDtypeStruct((bsz, length, wd), F32) for wd in widths],
        compiler_params=_params("parallel", "parallel"),
        name="inproj",
    )(*args)


def _unit_tri_inverse(a, eye, lower):
    n = a.shape[0]
    x = jnp.where(eye, 1.0, 0.0)
    for j in (range(n - 1) if lower else range(n - 1, 0, -1)):
        x = x - a[:, j:j + 1] * x[j:j + 1, :]
    return x


def _dn_kernel(alog_ref, dtb_ref,
               ql_ref, kl_ref, vl_ref, zl_ref, qc_ref, kc_ref, vc_ref, zc_ref, gt_ref,
               cwq_ref, cwk_ref, cwv_ref, ng_ref,
               yl_ref, yc_ref,
               qn_s, kn_s, vn_s, gc_s, bt_s, wq_s, u_s, at_s, kdt_s, gl_s, o_s,
               *, ctx_len, lat_len):
    h = pl.program_id(1)
    csz = DN_CHUNK
    nc_ctx = ctx_len // csz
    nc = (ctx_len + lat_len) // csz

    def prep(src_ref, cw_ref, kind):
        t = _silu(_depthwise_conv(src_ref[0], cw_ref[...], DN_CONV // 2))
        if kind == "v":
            return t
        t = t * lax.rsqrt(jnp.sum(t * t, axis=-1, keepdims=True) + RMS_EPS)
        return t * (DN_DK ** -0.5) if kind == "q" else t

    qn_s[0:ctx_len, :] = prep(qc_ref, cwq_ref, "q")
    qn_s[ctx_len:, :] = prep(ql_ref, cwq_ref, "q")
    kn_s[0:ctx_len, :] = prep(kc_ref, cwk_ref, "k")
    kn_s[ctx_len:, :] = prep(kl_ref, cwk_ref, "k")
    vn_s[0:ctx_len, :] = prep(vc_ref, cwv_ref, "v")
    vn_s[ctx_len:, :] = prep(vl_ref, cwv_ref, "v")

    ii = lax.broadcasted_iota(jnp.int32, (csz, csz), 0)
    jj = lax.broadcasted_iota(jnp.int32, (csz, csz), 1)
    eye = ii == jj
    for d in range(2):
        graw = gt_ref[0, d * 2 * DN_HEADS + h]
        braw = gt_ref[0, d * 2 * DN_HEADS + DN_HEADS + h]
        a_neg = -jnp.exp(jnp.zeros_like(graw) + alog_ref[d, h])
        g = a_neg * _softplus(graw + dtb_ref[d, h])
        tri = jnp.where((ii <= jj) if d == 0 else (ii >= jj), 1.0, 0.0).astype(BF16)
        g1 = g.astype(BF16)
        r1 = g - g1.astype(F32)
        g2 = r1.astype(BF16)
        g3 = (r1 - g2.astype(F32)).astype(BF16)
        gc_s[d] = _dot(g1, tri) + _dot(g2, tri) + _dot(g3, tri)
        bt_s[d] = _sigmoid(braw)

    o_s[...] = jnp.zeros_like(o_s)

    def chunk_prep(n, carry):
        r0 = pl.multiple_of(n * csz, csz)
        q = qn_s[pl.ds(r0, csz), :]
        k = kn_s[pl.ds(r0, csz), :]
        v = vn_s[pl.ds(r0, csz), :]
        kbf = k.astype(BF16)
        qk = _dot_nt(q.astype(BF16), kbf)
        for d in range(2):
            incl = (ii >= jj) if d == 0 else (ii <= jj)
            strict = (ii > jj) if d == 0 else (ii < jj)
            gr = gc_s[d, pl.ds(n, 1), :]
            br = bt_s[d, pl.ds(n, 1), :]
            grow = jnp.broadcast_to(gr, (csz, csz))
            gcol = jnp.sum(jnp.where(eye, grow, 0.0), axis=1, keepdims=True)
            bcol = jnp.sum(jnp.where(eye, jnp.broadcast_to(br, (csz, csz)), 0.0), axis=1,
                           keepdims=True)
            decay = jnp.where(incl, jnp.exp(jnp.where(incl, gcol - grow, 0.0)), 0.0)
            kb = k * bcol
            kk = _dot_nt(kb.astype(BF16), kbf)
            a = jnp.where(strict, kk * decay, 0.0)
            tb = _unit_tri_inverse(a, eye, lower=(d == 0)).astype(BF16)
            eg = jnp.exp(gcol)
            u = _dot(tb, (v * bcol).astype(BF16))
            w = _dot(tb, (kb * eg).astype(BF16))
            attn = jnp.where(incl, qk * decay, 0.0)
            glast = gr[:, csz - 1:csz] if d == 0 else gr[:, 0:1]
            kd = k * jnp.exp(glast - gcol)
            wq_s[d, n, 0:csz, :] = w.astype(BF16)
            wq_s[d, n, csz:2 * csz, :] = (q * eg).astype(BF16)
            u_s[d, pl.ds(r0, csz), :] = u
            at_s[d, n] = attn.astype(BF16)
            kdt_s[d, n] = kd.T.astype(BF16)
            gl_s[d, pl.ds(n, 1), :] = jnp.broadcast_to(jnp.exp(glast), (1, LANES))
        return carry

    lax.fori_loop(0, nc, chunk_prep, 0)

    def step(i, states):
        new_states = []
        for d in range(2):
            if d == 0:
                n = i
            else:
                n = jnp.where(i < nc_ctx, nc_ctx - 1 - i, nc + nc_ctx - 1 - i)
            r0 = pl.multiple_of(n * csz, csz)
            s = states[d]
            r = _dot(wq_s[d, n], s.astype(BF16))
            v_new = (u_s[d, pl.ds(r0, csz), :] - r[0:csz]).astype(BF16)
            o = r[csz:2 * csz] + _dot(at_s[d, n], v_new)
            o_s[pl.ds(r0, csz), :] += o
            s = s * gl_s[d, pl.ds(n, 1), :] + _dot(kdt_s[d, n], v_new)
            new_states.append(s)
        return tuple(new_states)

    zero = jnp.zeros((DN_DK, DN_DK), F32)
    lax.fori_loop(0, nc, step, (zero, zero))

    def gated_norm(o, z):
        o = o * lax.rsqrt(jnp.mean(o * o, axis=-1, keepdims=True) + RMS_EPS) * ng_ref[...]
        return (o * _silu(z)).astype(yl_ref.dtype)

    yc_ref[0] = gated_norm(o_s[0:ctx_len, :], zc_ref[0])
    yl_ref[0] = gated_norm(o_s[ctx_len:, :], zl_ref[0])


def _deltanet(qkv_l, z_l, qkv_c, z_c, gates, conv_w, a_log, dt_bias, norm_g):
    bsz, lat_len, _ = qkv_l.shape
    ctx_len = qkv_c.shape[1]
    tot = ctx_len + lat_len
    nc = tot // DN_CHUNK
    ncp = gates.shape[2]
    hd = DN_DK
    nh = DN_HEADS

    def col(off):
        return lambda b, h: (b, 0, off + h)

    def wcol(off):
        return lambda b, h: (0, off + h)

    smem = pl.BlockSpec(memory_space=pltpu.SMEM)
    in_specs = [
        smem, smem,
        pl.BlockSpec((1, lat_len, hd), col(0)),
        pl.BlockSpec((1, lat_len, hd), col(nh)),
        pl.BlockSpec((1, lat_len, hd), col(2 * nh)),
        pl.BlockSpec((1, lat_len, hd), col(0)),
        pl.BlockSpec((1, ctx_len, hd), col(0)),
        pl.BlockSpec((1, ctx_len, hd), col(nh)),
        pl.BlockSpec((1, ctx_len, hd), col(2 * nh)),
        pl.BlockSpec((1, ctx_len, hd), col(0)),
        pl.BlockSpec((1, 4 * nh, ncp, DN_CHUNK), lambda b, h: (b, 0, 0, 0)),
        pl.BlockSpec((DN_CONV, hd), wcol(0)),
        pl.BlockSpec((DN_CONV, hd), wcol(nh)),
        pl.BlockSpec((DN_CONV, hd), wcol(2 * nh)),
        pl.BlockSpec((1, hd), lambda b, h: (0, 0)),
    ]
    scratch = [
        pltpu.VMEM((tot, hd), F32), pltpu.VMEM((tot, hd), F32), pltpu.VMEM((tot, hd), F32),
        pltpu.VMEM((2, ncp, DN_CHUNK), F32), pltpu.VMEM((2, ncp, DN_CHUNK), F32),
        pltpu.VMEM((2, nc, 2 * DN_CHUNK, hd), BF16),
        pltpu.VMEM((2, tot, hd), F32),
        pltpu.VMEM((2, nc, DN_CHUNK, DN_CHUNK), BF16),
        pltpu.VMEM((2, nc, hd, DN_CHUNK), BF16),
        pltpu.VMEM((2, nc, LANES), F32),
        pltpu.VMEM((tot, hd), F32),
    ]
    return pl.pallas_call(
        functools.partial(_dn_kernel, ctx_len=ctx_len, lat_len=lat_len),
        grid=(bsz, nh),
        in_specs=in_specs,
        out_specs=[pl.BlockSpec((1, lat_len, hd), col(0)),
                   pl.BlockSpec((1, ctx_len, hd), col(0))],
        out_shape=[jax.ShapeDtypeStruct((bsz, lat_len, nh * hd), BF16),
                   jax.ShapeDtypeStruct((bsz, ctx_len, nh * hd), BF16)],
        scratch_shapes=scratch,
        compiler_params=_params("parallel", "parallel"),
        name="deltanet",
    )(a_log, dt_bias, qkv_l, qkv_l, qkv_l, z_l, qkv_c, qkv_c, qkv_c, z_c, gates,
      conv_w, conv_w, conv_w, norm_g.reshape(1, hd))


def _hy_filter_kernel(z_ref, win_ref, bin_ref, wmid_ref, bmid_ref, wout_ref, freq_ref, dec_ref,
                      o_ref):
    freq = freq_ref[...]
    hcur = jnp.sin(freq * (_dot3(z_ref[...], win_ref[...]) + bin_ref[...]))
    for i in range(wmid_ref.shape[0]):
        hcur = jnp.sin(freq * (_dot3(hcur, wmid_ref[i]) + bmid_ref[i]))
    o_ref[...] = _dot3(hcur, wout_ref[...]) * dec_ref[...]


def _hy_filter(z, w_in, b_in, w_mid, b_mid, w_out, freq, dec2):
    length = z.shape[0]
    n_out = w_out.shape[1]
    tl = min(256, length)

    def whole(a):
        return pl.BlockSpec(a.shape, lambda i: (0,) * a.ndim)

    return pl.pallas_call(
        _hy_filter_kernel,
        grid=(length // tl,),
        in_specs=[pl.BlockSpec((tl, z.shape[1]), lambda i: (i, 0)),
                  whole(w_in), whole(b_in), whole(w_mid), whole(b_mid), whole(w_out), whole(freq),
                  pl.BlockSpec((tl, n_out), lambda i: (i, 0))],
        out_specs=pl.BlockSpec((tl, n_out), lambda i: (i, 0)),
        out_shape=jax.ShapeDtypeStruct((length, n_out), F32),
        compiler_params=_params("parallel"),
        name="hyena_filter",
    )(z, w_in, b_in, w_mid, b_mid, w_out, freq, dec2)


def _filt_spec_kernel(fc_ref, fs_ref, h_ref, kc_ref, ks_ref, *, width):
    j = pl.program_id(0)
    hmat = h_ref[...]
    rows = lax.broadcasted_iota(jnp.int32, hmat.shape, 0)
    cols = lax.broadcasted_iota(jnp.int32, hmat.shape, 1)
    hmat = jnp.where((rows == 0) & (cols >= width), 0.0, hmat)
    hh, hl = _split(hmat)
    c = _dot(fc_ref[...], hh) + _dot(fc_ref[...], hl)
    s = _dot(fs_ref[...], hh) + _dot(fs_ref[...], hl)
    kc_ref[...] = c[:, :width] + c[:, width:]
    orow = lax.broadcasted_iota(jnp.int32, (c.shape[0], width), 0)
    sign = jnp.where((orow == 0) & (j == 0), 1.0, -1.0)
    ks_ref[...] = s[:, :width] + sign * s[:, width:]


def _filt_spec(fwd, hfilt, tf):
    length, two_w = hfilt.shape
    width = two_w // 2
    nt = length // tf
    return pl.pallas_call(
        functools.partial(_filt_spec_kernel, width=width),
        grid=(nt,),
        in_specs=[pl.BlockSpec((tf, length), lambda j: (j, 0)),
                  pl.BlockSpec((tf, length), lambda j: (nt + j, 0)),
                  pl.BlockSpec((length, two_w), lambda j: (0, 0))],
        out_specs=[pl.BlockSpec((tf, width), lambda j: (j, 0)),
                   pl.BlockSpec((tf, width), lambda j: (j, 0))],
        out_shape=[jax.ShapeDtypeStruct((length, width), F32)] * 2,
        compiler_params=_params("arbitrary"),
        name="hyena_filter_spectrum",
    )(fwd, fwd, hfilt)


def _hy_prep_kernel(x0_ref, x1_ref, v_ref, w0_ref, w1_ref, w2_ref, b0_ref, b1_ref, b2_ref,
                    x0o_ref, vvo_ref):
    x0 = _depthwise_conv(x0_ref[0], w0_ref[...], HY_CONV // 2) + b0_ref[...]
    x1 = _depthwise_conv(x1_ref[0], w1_ref[...], HY_CONV // 2) + b1_ref[...]
    v = _depthwise_conv(v_ref[0], w2_ref[...], HY_CONV // 2) + b2_ref[...]
    x0o_ref[0] = x0
    vvo_ref[0] = v * x1


def _hy_prep(p_hy, conv_w, conv_b):
    bsz, length, three_w = p_hy.shape
    width = three_w // 3
    nb = width // LANES

    def col(off):
        return lambda b, j: (b, 0, off + j)

    def wcol(off):
        return lambda b, j: (0, off + j)

    k = conv_w.shape[0]
    return pl.pallas_call(
        _hy_prep_kernel,
        grid=(bsz, nb),
        in_specs=[pl.BlockSpec((1, length, LANES), col(0)),
                  pl.BlockSpec((1, length, LANES), col(nb)),
                  pl.BlockSpec((1, length, LANES), col(2 * nb)),
                  pl.BlockSpec((k, LANES), wcol(0)),
                  pl.BlockSpec((k, LANES), wcol(nb)),
                  pl.BlockSpec((k, LANES), wcol(2 * nb)),
                  pl.BlockSpec((1, LANES), wcol(0)),
                  pl.BlockSpec((1, LANES), wcol(nb)),
                  pl.BlockSpec((1, LANES), wcol(2 * nb))],
        out_specs=[pl.BlockSpec((1, length, LANES), col(0)),
                   pl.BlockSpec((1, length, LANES), col(0))],
        out_shape=[jax.ShapeDtypeStruct((bsz, length, width), F32)] * 2,
        compiler_params=_params("parallel", "parallel"),
        name="hyena_prep",
    )(p_hy, p_hy, p_hy, conv_w, conv_w, conv_w, conv_b, conv_b, conv_b)


def _dft_fwd_kernel(fc_ref, fs_ref, v_ref, kc_ref, ks_ref, yc_ref, ys_ref, vb_s):
    j = pl.program_id(1)

    @pl.when(j == 0)
    def _():
        vb_s[...] = v_ref[0].astype(BF16)

    uc = _dot(fc_ref[...], vb_s[...])
    us = _dot(fs_ref[...], vb_s[...])
    kc = kc_ref[...]
    ks = ks_ref[...]
    rows = lax.broadcasted_iota(jnp.int32, uc.shape, 0)
    special = (rows == 0) & (j == 0)
    yc_ref[0] = (uc * kc - jnp.where(special, 0.0, us * ks)).astype(BF16)
    ys_ref[0] = jnp.where(special, us * ks, uc * ks + us * kc).astype(BF16)


def _dft_fwd(fwd, vv, kc, ks, tf):
    bsz, length, width = vv.shape
    nt = length // tf
    return pl.pallas_call(
        _dft_fwd_kernel,
        grid=(bsz, nt),
        in_specs=[pl.BlockSpec((tf, length), lambda b, j: (j, 0)),
                  pl.BlockSpec((tf, length), lambda b, j: (nt + j, 0)),
                  pl.BlockSpec((1, length, width), lambda b, j: (b, 0, 0)),
                  pl.BlockSpec((tf, width), lambda b, j: (j, 0)),
                  pl.BlockSpec((tf, width), lambda b, j: (j, 0))],
        out_specs=[pl.BlockSpec((1, tf, width), lambda b, j: (b, j, 0)),
                   pl.BlockSpec((1, tf, width), lambda b, j: (b, j, 0))],
        out_shape=[jax.ShapeDtypeStruct((bsz, length, width), BF16)] * 2,
        scratch_shapes=[pltpu.VMEM((length, width), BF16)],
        compiler_params=_params("parallel", "arbitrary"),
        name="hyena_dft_fwd",
    )(fwd, fwd, vv, kc, ks)


def _dft_inv_kernel(ic_ref, is_ref, yc_ref, ys_ref, vv_ref, x0_ref, skip_ref, o_ref):
    y = _dot(ic_ref[...], yc_ref[0]) + _dot(is_ref[...], ys_ref[0])
    o_ref[0] = (x0_ref[0] * (y + vv_ref[0] * skip_ref[...])).astype(o_ref.dtype)


def _dft_inv(inv, yc, ys, vv, x0, skip, tt):
    bsz, length, width = vv.shape
    nt = length // tt
    return pl.pallas_call(
        _dft_inv_kernel,
        grid=(bsz, nt),
        in_specs=[pl.BlockSpec((tt, length), lambda b, i: (i, 0)),
                  pl.BlockSpec((tt, length), lambda b, i: (i, 1)),
                  pl.BlockSpec((1, length, width), lambda b, i: (b, 0, 0)),
                  pl.BlockSpec((1, length, width), lambda b, i: (b, 0, 0)),
                  pl.BlockSpec((1, tt, width), lambda b, i: (b, i, 0)),
                  pl.BlockSpec((1, tt, width), lambda b, i: (b, i, 0)),
                  pl.BlockSpec((1, width), lambda b, i: (0, 0))],
        out_specs=pl.BlockSpec((1, tt, width), lambda b, i: (b, i, 0)),
        out_shape=jax.ShapeDtypeStruct((bsz, length, width), BF16),
        compiler_params=_params("parallel", "parallel"),
        name="hyena_dft_inv",
    )(inv, inv, yc, ys, vv, x0, skip)


@functools.lru_cache(maxsize=None)
def _hyena_tables(length):
    n2 = 2 * length
    t = np.linspace(0.0, 1.0, length)[:, None]
    bands = (HY_EMB - 1) // 2
    wpos = 2.0 * np.pi * np.arange(length)[:, None] / length
    fb = np.linspace(1e-4, bands - 1, bands)[None]
    z = np.concatenate([t, np.cos(fb * wpos), -np.sin(fb * wpos)], axis=-1)
    zpad = np.zeros((length, LANES))
    zpad[:, :HY_EMB] = z
    f = np.arange(length)[:, None]
    n = np.arange(length)[None, :]
    ang = 2.0 * np.pi * ((f * n) % n2) / n2
    cos_m = np.cos(ang)
    sin_m = np.sin(ang)
    sin_m[0, :] = np.cos(np.pi * np.arange(length))
    fwd = np.concatenate([cos_m, sin_m], axis=0)
    scale = np.full((1, n2), 2.0 / n2)
    scale[0, 0] = 1.0 / n2
    scale[0, length] = 1.0 / n2
    inv = fwd.T * scale
    return (zpad.astype(np.float32), t.astype(np.float32), fwd.astype(np.float32),
            inv.astype(np.float32))


def _hyena(p_hy, conv_w, conv_b, filt, skip):
    bsz, length, three_w = p_hy.shape
    width = three_w // 3
    w_in, b_in, w_mid, b_mid, w_out, freq = filt
    zpad, t, fwd, inv = _hyena_tables(length)
    deltas = np.abs(np.linspace(HY_MIN_DECAY, HY_MAX_DECAY, width))[None, :]
    dec = np.exp(-t.astype(np.float64) * deltas).astype(np.float32)
    dec2 = jnp.asarray(np.concatenate([dec, dec], axis=1))
    ffn = w_in.shape[1]
    w_in_pad = jnp.zeros((LANES, ffn), F32).at[:HY_EMB].set(w_in)
    hfilt = _hy_filter(jnp.asarray(zpad), w_in_pad, b_in.reshape(1, ffn), w_mid,
                       b_mid.reshape(-1, 1, ffn), w_out, freq.reshape(1, ffn), dec2)
    fwd_b = jnp.asarray(fwd).astype(BF16)
    inv_b = jnp.asarray(inv).astype(BF16)
    tf = min(256, length)
    kc, ks = _filt_spec(fwd_b, hfilt, tf)
    x0, vv = _hy_prep(p_hy, conv_w, conv_b.reshape(1, three_w))
    yc, ys = _dft_fwd(fwd_b, vv, kc, ks, tf)
    return _dft_inv(inv_b, yc, ys, vv, x0, skip.reshape(1, width), min(256, length))


def _gelu_tanh(x):
    return 0.5 * x * (1.0 + jnp.tanh(math.sqrt(2.0 / math.pi) * (x + 0.044715 * x * x * x)))


def _lru_kernel(xl_ref, yl_ref, xc_ref, cw_ref, cb_ref, wa_ref, ba_ref, wx_ref, bx_ref, ap_ref,
                o_ref, xs_s, a_s, b_s, h_s, *, ctx_len, lat_len):
    tot = ctx_len + lat_len
    ngrp = tot // 8
    ngrp_ctx = ctx_len // 8
    xs_s[0:ctx_len, :] = _depthwise_conv(xc_ref[0], cw_ref[...], LRU_CONV // 2) + cb_ref[...]
    xs_s[ctx_len:, :] = _depthwise_conv(xl_ref[0], cw_ref[...], LRU_CONV // 2) + cb_ref[...]
    xs = xs_s[...]
    xsb = xs.astype(BF16)
    rows8 = lax.broadcasted_iota(jnp.int32, xs.shape, 0) % 8
    for d in range(2):
        r = _sigmoid(_dot(xsb, wa_ref[d, 0].astype(BF16)) + ba_ref[d])
        gi = _sigmoid(_dot(xsb, wx_ref[d, 0].astype(BF16)) + bx_ref[d])
        log_a = -LRU_C * r * _softplus(ap_ref[d])
        a = jnp.exp(log_a)
        b = jnp.sqrt(1.0 - a * a) * (gi * xs)
        for s in (1, 2, 4):
            if d == 0:
                keep = rows8 >= s
                sa = jnp.where(keep, pltpu.roll(a, s, axis=0), 1.0)
                sb = jnp.where(keep, pltpu.roll(b, s, axis=0), 0.0)
            else:
                keep = rows8 < 8 - s
                sa = jnp.where(keep, pltpu.roll(a, tot - s, axis=0), 1.0)
                sb = jnp.where(keep, pltpu.roll(b, tot - s, axis=0), 0.0)
            b = a * sb + b
            a = a * sa
        a_s[d] = a
        b_s[d] = b

    def group_fwd(i, c):
        r0 = pl.multiple_of(i * 8, 8)
        hg = a_s[0, pl.ds(r0, 8), :] * c + b_s[0, pl.ds(r0, 8), :]
        h_s[pl.ds(r0, 8), :] = hg
        return jnp.broadcast_to(hg[7:8, :], hg.shape)

    def group_bwd(i, c):
        gidx = jnp.where(i < ngrp_ctx, ngrp_ctx - 1 - i, ngrp + ngrp_ctx - 1 - i)
        r0 = pl.multiple_of(gidx * 8, 8)
        hg = a_s[1, pl.ds(r0, 8), :] * c + b_s[1, pl.ds(r0, 8), :]
        h_s[pl.ds(r0, 8), :] += hg
        return jnp.broadcast_to(hg[0:1, :], hg.shape)

    zero = jnp.zeros((8, xs.shape[1]), F32)
    lax.fori_loop(0, ngrp, group_fwd, zero, unroll=4)
    lax.fori_loop(0, ngrp, group_bwd, zero, unroll=4)
    o_ref[0] = (h_s[ctx_len:, :] * _gelu_tanh(yl_ref[0])).astype(o_ref.dtype)


def _rglru(xb_l, yb_l, xb_c, conv_w, conv_b, wa, ba, wx, bx, a_param):
    bsz, lat_len, width = xb_l.shape
    ctx_len = xb_c.shape[1]
    tot = ctx_len + lat_len
    blk = width // LRU_HEADS

    def col(b, h):
        return (b, 0, h)

    def wcol(b, h):
        return (0, h)

    def w3(b, h):
        return (0, 0, h)

    return pl.pallas_call(
        functools.partial(_lru_kernel, ctx_len=ctx_len, lat_len=lat_len),
        grid=(bsz, LRU_HEADS),
        in_specs=[pl.BlockSpec((1, lat_len, blk), col),
                  pl.BlockSpec((1, lat_len, blk), col),
                  pl.BlockSpec((1, ctx_len, blk), col),
                  pl.BlockSpec((LRU_CONV, blk), wcol),
                  pl.BlockSpec((1, blk), wcol),
                  pl.BlockSpec((2, 1, blk, blk), lambda b, h: (0, h, 0, 0)),
                  pl.BlockSpec((2, 1, blk), w3),
                  pl.BlockSpec((2, 1, blk, blk), lambda b, h: (0, h, 0, 0)),
                  pl.BlockSpec((2, 1, blk), w3),
                  pl.BlockSpec((2, 1, blk), w3)],
        out_specs=pl.BlockSpec((1, lat_len, blk), col),
        out_shape=jax.ShapeDtypeStruct((bsz, lat_len, width), BF16),
        scratch_shapes=[pltpu.VMEM((tot, blk), F32),
                        pltpu.VMEM((2, tot, blk), F32),
                        pltpu.VMEM((2, tot, blk), F32),
                        pltpu.VMEM((tot, blk), F32)],
        compiler_params=_params("parallel", "parallel"),
        name="rglru",
    )(xb_l, yb_l, xb_c, conv_w, conv_b.reshape(1, width), wa, ba.reshape(2, 1, width), wx,
      bx.reshape(2, 1, width), a_param.reshape(2, 1, width))


def _post_mixer_kernel(*refs, n_in, has_bias):
    a_refs = refs[:n_in]
    w_refs = refs[n_in:2 * n_in]
    i = 2 * n_in
    b_ref = None
    if has_bias:
        b_ref = refs[i]
        i += 1
    x_ref, g1_ref, sc_ref, sh_ref, lng_ref, lnb_ref, rw_ref, rb_ref, base_ref = refs[i:i + 9]
    x1_ref, v_ref, route_ref, cnt_ref = refs[i + 9:]

    @pl.when((pl.program_id(0) == 0) & (pl.program_id(1) == 0))
    def _():
        cnt_ref[...] = base_ref[...]

    y = None
    for a_ref, w_ref in zip(a_refs, w_refs):
        t = _dot(a_ref[0], w_ref[...])
        y = t if y is None else y + t
    if has_bias:
        y = y + b_ref[...]
    x1 = _layer_norm(DEEPNORM_ALPHA * x_ref[0] + g1_ref[0] * y, lng_ref[...], lnb_ref[...])
    x1_ref[0] = x1
    v = x1 * (1.0 + sc_ref[0]) + sh_ref[0]
    v_ref[0] = v
    logits = _dot3(v, rw_ref[...]) + rb_ref[...]
    tm = logits.shape[0]
    lane = lax.broadcasted_iota(jnp.int32, logits.shape, 1).astype(F32)
    work = logits
    picks, firsts = [], []
    m0 = None
    for kk in range(TOP_K):
        m = jnp.max(work, axis=-1, keepdims=True)
        if kk == 0:
            m0 = m
        first = jnp.min(jnp.where(work == m, lane, float(LANES)), axis=-1, keepdims=True)
        pick = lane == first
        picks.append(pick)
        firsts.append(first)
        work = jnp.where(pick, -jnp.inf, work)
    sel = jnp.where(picks[0] | picks[1] | picks[2] | picks[3], 1.0, 0.0)
    e = sel * jnp.exp(logits - m0)
    gate = e / jnp.sum(e, axis=-1, keepdims=True)
    ti = lax.broadcasted_iota(jnp.int32, (tm, tm), 0)
    tj = lax.broadcasted_iota(jnp.int32, (tm, tm), 1)
    before = jnp.where(ti > tj, 1.0, 0.0).astype(BF16)
    slot = _dot(before, sel.astype(BF16)) + cnt_ref[...]
    route = jnp.zeros(logits.shape, F32)
    for kk in range(TOP_K):
        rank = jnp.sum(jnp.where(picks[kk], slot, 0.0), axis=-1, keepdims=True)
        wgt = jnp.sum(jnp.where(picks[kk], gate, 0.0), axis=-1, keepdims=True)
        route = jnp.where(lane == float(kk), firsts[kk], route)
        route = jnp.where(lane == float(TOP_K + kk), rank, route)
        route = jnp.where(lane == float(2 * TOP_K + kk), wgt, route)
    route_ref[0] = route
    cnt_ref[...] += jnp.sum(sel, axis=0, keepdims=True)


def _post_mixer(acts, ws, bias, x, g1, sc2, sh2, ln_g, ln_b, router_w, router_b, base, tm):
    bsz, length, d = x.shape
    tm = min(tm, length)
    n_in = len(acts)

    def row(bi, i):
        return (bi, i, 0)

    def per_b(bi, i):
        return (bi, 0, 0)

    def const(bi, i):
        return (0, 0)

    in_specs = [pl.BlockSpec((1, tm, a.shape[2]), row) for a in acts]
    in_specs += [pl.BlockSpec(w.shape, const) for w in ws]
    args = list(acts) + list(ws)
    if bias is not None:
        in_specs.append(pl.BlockSpec((1, d), const))
        args.append(bias.reshape(1, d))
    in_specs += [pl.BlockSpec((1, tm, d), row),
                 pl.BlockSpec((1, 1, d), per_b), pl.BlockSpec((1, 1, d), per_b),
                 pl.BlockSpec((1, 1, d), per_b),
                 pl.BlockSpec((1, d), const), pl.BlockSpec((1, d), const),
                 pl.BlockSpec((d, LANES), const), pl.BlockSpec((1, LANES), const),
                 pl.BlockSpec((1, LANES), const)]
    rw = jnp.zeros((d, LANES), F32).at[:, :N_EXPERTS].set(router_w)
    rb = jnp.full((1, LANES), -1e30, F32).at[0, :N_EXPERTS].set(router_b)
    args += [x, g1, sc2, sh2, ln_g.reshape(1, d), ln_b.reshape(1, d), rw, rb, base]
    return pl.pallas_call(
        functools.partial(_post_mixer_kernel, n_in=n_in, has_bias=bias is not None),
        grid=(bsz, length // tm),
        in_specs=in_specs,
        out_specs=[pl.BlockSpec((1, tm, d), row), pl.BlockSpec((1, tm, d), row),
                   pl.BlockSpec((1, tm, LANES), row), pl.BlockSpec((1, LANES), const)],
        out_shape=[jax.ShapeDtypeStruct((bsz, length, d), F32),
                   jax.ShapeDtypeStruct((bsz, length, d), F32),
                   jax.ShapeDtypeStruct((bsz, length, LANES), F32),
                   jax.ShapeDtypeStruct((1, LANES), F32)],
        compiler_params=_params("arbitrary", "arbitrary"),
        name="post_mixer",
    )(*args)


def _swiglu_interleaved(gu):
    nxt = pltpu.roll(gu, gu.shape[1] - 1, axis=1)
    glu = jnp.minimum(gu, SWIGLU_LIMIT)
    lin = jnp.clip(nxt, -SWIGLU_LIMIT, SWIGLU_LIMIT)
    act = glu * _sigmoid(SWIGLU_ALPHA * glu) * (lin + 1.0)
    lane = lax.broadcasted_iota(jnp.int32, gu.shape, 1)
    return jnp.where(lane % 2 == 0, act, 0.0)


def _moe_rows_kernel(te_ref, nv_ref, xs_ref, w1_ref, b1_ref, w2_ref, b2_ref, ys_ref):
    del te_ref
    nv = nv_ref[pl.program_id(0)]

    @pl.when(nv > 0)
    def _():
        rows = lax.broadcasted_iota(jnp.int32, xs_ref.shape, 0)
        x = jnp.where(rows < nv, xs_ref[...], 0.0).astype(BF16)
        act = _swiglu_interleaved(_dot(x, w1_ref[0]) + b1_ref[0])
        ys_ref[...] = _dot(act.astype(BF16), w2_ref[0]) + b2_ref[0]

    @pl.when(nv == 0)
    def _():
        ys_ref[...] = jnp.zeros_like(ys_ref)


def _moe_rows(xs, tile_expert, tile_rows, w1, b1, w2z, b2, tm):
    n_rows, d = xs.shape
    _, _, dff2 = w1.shape

    def row(i, te, nv):
        return (i, 0)

    def exp3(i, te, nv):
        return (te[i], 0, 0)

    return pl.pallas_call(
        _moe_rows_kernel,
        grid_spec=pltpu.PrefetchScalarGridSpec(
            num_scalar_prefetch=2,
            grid=(n_rows // tm,),
            in_specs=[pl.BlockSpec((tm, d), row),
                      pl.BlockSpec((1, d, dff2), exp3), pl.BlockSpec((1, 1, dff2), exp3),
                      pl.BlockSpec((1, dff2, d), exp3), pl.BlockSpec((1, 1, d), exp3)],
            out_specs=pl.BlockSpec((tm, d), row)),
        out_shape=jax.ShapeDtypeStruct((n_rows, d), F32),
        compiler_params=_params("arbitrary"),
        name="moe_rows",
    )(tile_expert, tile_rows, xs, w1, b1, w2z, b2)


def _moe_combine_kernel(y0_ref, y1_ref, y2_ref, y3_ref, route_ref, x1_ref, g2_ref, lng_ref, lnb_ref,
                        o_ref):
    route = route_ref[...]
    lane = lax.broadcasted_iota(jnp.int32, route.shape, 1)
    f = None
    for kk, y_ref in enumerate((y0_ref, y1_ref, y2_ref, y3_ref)):
        wgt = jnp.sum(jnp.where(lane == 2 * TOP_K + kk, route, 0.0), axis=-1, keepdims=True)
        term = wgt * y_ref[0]
        f = term if f is None else f + term
    o_ref[0] = _layer_norm(DEEPNORM_ALPHA * x1_ref[0] + g2_ref[0] * f, lng_ref[...], lnb_ref[...])


def _moe_combine(yg, route, row_offset, x1, g2, ln_g, ln_b, tm):
    bsz, length, d = x1.shape
    tm = min(tm, length)
    nt = length // tm
    off = row_offset // tm

    def pick(kk):
        return lambda bi, i: (kk, off + bi * nt + i, 0)

    def const(bi, i):
        return (0, 0)

    return pl.pallas_call(
        _moe_combine_kernel,
        grid=(bsz, nt),
        in_specs=[pl.BlockSpec((1, tm, d), pick(kk)) for kk in range(TOP_K)] + [
            pl.BlockSpec((tm, LANES), lambda bi, i: (off + bi * nt + i, 0)),
            pl.BlockSpec((1, tm, d), lambda bi, i: (bi, i, 0)),
            pl.BlockSpec((1, 1, d), lambda bi, i: (bi, 0, 0)),
            pl.BlockSpec((1, d), const), pl.BlockSpec((1, d), const)],
        out_specs=pl.BlockSpec((1, tm, d), lambda bi, i: (bi, i, 0)),
        out_shape=jax.ShapeDtypeStruct((bsz, length, d), F32),
        compiler_params=_params("parallel", "parallel"),
        name="moe_combine",
    )(yg, yg, yg, yg, route, x1, g2, ln_g.reshape(1, d), ln_b.reshape(1, d))


SC_CORES = 2
SC_SUBCORES = 16
SC_WORKERS = SC_CORES * SC_SUBCORES
SC_WINDOW = 32


def _sc_row_pipeline(nwin, read, write):
    read(0, 0).start()

    @pl.loop(0, nwin, step=2)
    def _(w0):
        for b in range(2):
            w = w0 + b

            @pl.when(w + 1 < nwin)
            def _():
                @pl.when(w >= 1)
                def _():
                    write(w - 1, 1 - b).wait()

                read(w + 1, 1 - b).start()

            read(w, b).wait()
            write(w, b).start()

    write(nwin - 2, 0).wait()
    write(nwin - 1, 1).wait()


def _sc_scatter_rows(src, pos, n_out):
    t_rows, d = src.shape
    nw, nwin, win = pos.shape
    assert nw == SC_WORKERS and nwin % 2 == 0 and t_rows % (nwin * win) == 0
    mesh = plsc.VectorSubcoreMesh(core_axis_name="c", subcore_axis_name="s")

    @functools.partial(
        pl.kernel, mesh=mesh, out_type=jax.ShapeDtypeStruct((n_out, d), src.dtype),
        scratch_types=[pltpu.VMEM((nwin, win), jnp.int32), pltpu.VMEM((2, win, d), src.dtype),
                       pltpu.SemaphoreType.DMA((2,)), pltpu.SemaphoreType.DMA((2,))])
    def scatter(src_hbm, pos_hbm, out_hbm, idx_v, rows_v, rsem, wsem):
        wid = lax.axis_index("s") * SC_CORES + lax.axis_index("c")
        t0 = lax.rem(wid * (nwin * win), t_rows)
        pltpu.sync_copy(pos_hbm.at[wid], idx_v)

        def read(w, slot):
            return pltpu.make_async_copy(src_hbm.at[pl.ds(t0 + w * win, win)], rows_v.at[slot],
                                         rsem.at[slot])

        def write(w, slot):
            return pltpu.make_async_copy(rows_v.at[slot], out_hbm.at[idx_v.at[w]], wsem.at[slot])

        _sc_row_pipeline(nwin, read, write)

    return scatter(src, pos)


def _sc_gather_rows(table, pos):
    _, d = table.shape
    nw, nwin, win = pos.shape
    assert nw == SC_WORKERS and nwin % 2 == 0
    per = nwin * win
    mesh = plsc.VectorSubcoreMesh(core_axis_name="c", subcore_axis_name="s")

    @functools.partial(
        pl.kernel, mesh=mesh, out_type=jax.ShapeDtypeStruct((nw * per, d), table.dtype),
        scratch_types=[pltpu.VMEM((nwin, win), jnp.int32), pltpu.VMEM((2, win, d), table.dtype),
                       pltpu.SemaphoreType.DMA((2,)), pltpu.SemaphoreType.DMA((2,))])
    def gather(table_hbm, pos_hbm, out_hbm, idx_v, rows_v, rsem, wsem):
        wid = lax.axis_index("s") * SC_CORES + lax.axis_index("c")
        base = wid * per
        pltpu.sync_copy(pos_hbm.at[wid], idx_v)

        def read(w, slot):
            return pltpu.make_async_copy(table_hbm.at[idx_v.at[w]], rows_v.at[slot], rsem.at[slot])

        def write(w, slot):
            return pltpu.make_async_copy(rows_v.at[slot], out_hbm.at[pl.ds(base + w * win, win)],
                                         wsem.at[slot])

        _sc_row_pipeline(nwin, read, write)

    return gather(table, pos)


MOE_TILE = 256


def _moe_sparse(v_all, route, counts, w1, b1, w2z, b2):
    t_rows, d = v_all.shape
    n_exp = w1.shape[0]
    pairs = TOP_K * t_rows
    n_tiles = pairs // MOE_TILE + n_exp
    expert = route[:, 0:TOP_K].astype(jnp.int32)
    slot = route[:, TOP_K:2 * TOP_K].astype(jnp.int32)
    cnt = counts[0, :n_exp].astype(jnp.int32)
    tiles_per = (cnt + MOE_TILE - 1) // MOE_TILE
    tile_end = jnp.cumsum(tiles_per)
    tile_start = tile_end - tiles_per
    pos = (tile_start * MOE_TILE)[expert] + slot
    nwin = pairs // (SC_WORKERS * SC_WINDOW)
    pos_km = pos.T.reshape(SC_WORKERS, nwin, SC_WINDOW)
    tile_ids = jnp.arange(n_tiles, dtype=jnp.int32)
    used = tile_ids < tile_end[-1]
    te = jnp.searchsorted(tile_end, jnp.minimum(tile_ids, tile_end[-1] - 1), side="right")
    te = jnp.minimum(te, n_exp - 1).astype(jnp.int32)
    rows_left = cnt[te] - (tile_ids - tile_start[te]) * MOE_TILE
    tile_rows = jnp.where(used, jnp.clip(rows_left, 0, MOE_TILE), 0).astype(jnp.int32)
    xs = _sc_scatter_rows(v_all, pos_km, n_tiles * MOE_TILE)
    ys = _moe_rows(xs, te, tile_rows, w1, b1, w2z, b2, MOE_TILE)
    return _sc_gather_rows(ys, pos_km).reshape(TOP_K, t_rows, d)


@functools.lru_cache(maxsize=None)
def _sincos_2d(rows, cols, dim):
    quarter = dim // 4
    omega = 1.0 / (10000.0 ** (np.arange(quarter, dtype=np.float64) / quarter))

    def emb1d(n):
        ang = np.arange(n, dtype=np.float64)[:, None] * omega
        return np.concatenate([np.sin(ang), np.cos(ang)], axis=-1)

    er = np.broadcast_to(emb1d(rows)[:, None], (rows, cols, dim // 2))
    ec = np.broadcast_to(emb1d(cols)[None], (rows, cols, dim // 2))
    return np.concatenate([er, ec], axis=-1).reshape(rows * cols, dim).astype(np.float32)


def _gate_layout(gates, n_ch):
    bsz, length, _ = gates.shape
    g = gates[..., :n_ch].reshape(bsz, length // DN_CHUNK, DN_CHUNK, n_ch)
    return g.transpose(0, 3, 1, 2)


def kernel(x, c, ctx, c_ctx, ada_w, ada_b, ln_g, ln_b, ev_w_in, ev_w_out, dn_conv_w, dn_a_log, dn_dt_bias, dn_norm_g, hy_conv_w, hy_conv_b, hy_w_in, hy_b_in, hy_w_mid, hy_b_mid, hy_w_out, hy_freq, hy_skip, od_w_in, od_b_in, lru_conv_w, lru_conv_b, lru_wa, lru_ba, lru_wx, lru_bx, lru_a_param, od_w_out, od_b_out, router_w, router_b, moe_w1, moe_b1, moe_w2, moe_b2):
    bsz, length, d = x.shape
    ctx_len = ctx.shape[1]
    pos = jnp.asarray(_sincos_2d(length // GRID_W, GRID_W, d))

    cond = jnp.zeros((16, d), F32).at[:bsz].set(c).at[bsz].set(c_ctx)
    mod = _modulation(cond, ada_w, ada_b).reshape(DEPTH, 16, 6, d)

    def lat_mod(layer, k):
        return mod[layer, :bsz, k][:, None, :]

    def ctx_mod(layer, k):
        return jnp.broadcast_to(mod[layer, bsz, k][None, None, :], (bsz, 1, d))

    hc = ctx
    for layer in range(DEPTH):
        last = layer == DEPTH - 1
        j = layer // 2
        if layer % 2 == 0:
            dn_qk = DN_HEADS * DN_DK
            dn_qkv = 3 * dn_qk
            dn_in = dn_qkv + dn_qk + 4 * DN_HEADS
            w_in = ev_w_in[j]
            gate_w = jnp.zeros((d, LANES), F32).at[:, :4 * DN_HEADS].set(w_in[:, dn_qkv + dn_qk:dn_in])
            w_cat = jnp.concatenate([w_in[:, :dn_qkv + dn_qk], gate_w, w_in[:, dn_in:]],
                                    axis=1).astype(BF16)
            hy_in = w_in.shape[1] - dn_in
            widths = (dn_qkv, dn_qk, LANES, hy_in)
            qkv_l, z_l, gt_l, phy_l = _inproj(x, lat_mod(layer, 1), lat_mod(layer, 0), pos,
                                               w_cat, None, widths, 512)
            qkv_c, z_c, gt_c, phy_c = _inproj(hc, ctx_mod(layer, 1), ctx_mod(layer, 0), None,
                                               w_cat, None, widths, 512)
            gates = jnp.concatenate([_gate_layout(gt_c, 4 * DN_HEADS),
                                     _gate_layout(gt_l, 4 * DN_HEADS)], axis=2)
            n_chunks = gates.shape[2]
            gates = jnp.pad(gates, ((0, 0), (0, 0), (0, -n_chunks % 16), (0, 0)))
            dn_l, dn_c = _deltanet(qkv_l, z_l, qkv_c, z_c, gates, dn_conv_w[j], dn_a_log[j],
                                   dn_dt_bias[j], dn_norm_g[j])
            filt = (hy_w_in[j], hy_b_in[j], hy_w_mid[j], hy_b_mid[j], hy_w_out[j], hy_freq[j])
            hy_l = _hyena(phy_l, hy_conv_w[j], hy_conv_b[j], filt, hy_skip[j])
            w_out = ev_w_out[j].astype(BF16)
            half = dn_l.shape[2]
            acts_l, ws, bias = (dn_l, hy_l), (w_out[:half], w_out[half:]), None
            acts_c = None
            if not last:
                hy_c = _hyena(phy_c, hy_conv_w[j], hy_conv_b[j], filt, hy_skip[j])
                acts_c = (dn_c, hy_c)
        else:
            w_in = od_w_in[j].astype(BF16)
            width = w_in.shape[1] // 2
            b_in = od_b_in[j].reshape(1, 2 * width)
            xb_l, yb_l = _inproj(x, lat_mod(layer, 1), lat_mod(layer, 0), pos, w_in, b_in,
                                 (width, width), 512)
            xb_c, _ = _inproj(hc, ctx_mod(layer, 1), ctx_mod(layer, 0), None, w_in, b_in,
                              (width, width), 512)
            act_l = _rglru(xb_l, yb_l, xb_c, lru_conv_w[j], lru_conv_b[j], lru_wa[j], lru_ba[j],
                           lru_wx[j], lru_bx[j], lru_a_param[j])
            acts_l, ws, bias = (act_l,), (od_w_out[j].astype(BF16),), od_b_out[j]
            acts_c = None
            assert last, "context outputs of the RG-LRU layer are only needed before the last layer"

        w1 = moe_w1[layer].astype(BF16)
        b1 = moe_b1[layer][:, None, :]
        w2 = moe_w2[layer].astype(BF16)
        n_exp, dff, _ = w2.shape
        w2z = jnp.stack([w2, jnp.zeros_like(w2)], axis=2).reshape(n_exp, 2 * dff, d)
        b2 = moe_b2[layer][:, None, :]

        x1, v, route, counts = _post_mixer(acts_l, ws, bias, x, lat_mod(layer, 2),
                                           lat_mod(layer, 4), lat_mod(layer, 3), ln_g[layer, 0],
                                           ln_b[layer, 0], router_w[layer], router_b[layer],
                                           jnp.zeros((1, LANES), F32), 256)
        v_all = v.reshape(bsz * length, d)
        route = route.reshape(bsz * length, LANES)
        if not last:
            hc1, vc, route_c, counts = _post_mixer(acts_c, ws, bias, hc, ctx_mod(layer, 2),
                                                   ctx_mod(layer, 4), ctx_mod(layer, 3),
                                                   ln_g[layer, 0], ln_b[layer, 0], router_w[layer],
                                                   router_b[layer], counts, 256)
            v_all = jnp.concatenate([v_all, vc.reshape(bsz * ctx_len, d)], axis=0)
            route = jnp.concatenate([route, route_c.reshape(bsz * ctx_len, LANES)], axis=0)
        yg = _moe_sparse(v_all, route, counts, w1, b1, w2z, b2)
        x = _moe_combine(yg, route, 0, x1, lat_mod(layer, 5), ln_g[layer, 1], ln_b[layer, 1], 256)
        if not last:
            hc = _moe_combine(yg, route, bsz * length, hc1, ctx_mod(layer, 5), ln_g[layer, 1],
                              ln_b[layer, 1], 256)
    return x
```

```python
import functools
import math

import numpy as np
import jax
import jax.numpy as jnp
from jax import lax
from jax.experimental import pallas as pl
from jax.experimental.pallas import tpu as pltpu
from jax.experimental.pallas import tpu_sc as plsc

F32 = jnp.float32
BF16 = jnp.bfloat16

VMEM_LIMIT_BYTES = 56 * 1024 * 1024
LANES = 128

DEPTH = 2
GRID_W = 64
DEEPNORM_ALPHA = (2.0 * DEPTH) ** 0.25
LN_EPS = 1e-5
RMS_EPS = 1e-6

DN_HEADS = 4
DN_DK = 128
DN_CHUNK = 64
DN_CONV = 4

HY_EMB = 33
HY_TARGET = 1e-2
HY_MIN_DECAY = math.log(HY_TARGET) / 1.5
HY_MAX_DECAY = math.log(HY_TARGET) / 0.3
HY_CONV = 3

LRU_HEADS = 4
LRU_C = 8.0
LRU_CONV = 4

N_EXPERTS = 32
TOP_K = 4
SWIGLU_ALPHA = 1.702
SWIGLU_LIMIT = 7.0


def _params(*sem):
    return pltpu.CompilerParams(dimension_semantics=sem, vmem_limit_bytes=VMEM_LIMIT_BYTES)


def _dot(a, b):
    return jnp.dot(a, b, preferred_element_type=F32)


def _dot_nt(a, b):
    return lax.dot_general(a, b, (((1,), (1,)), ((), ())), preferred_element_type=F32)


def _split(a):
    hi = a.astype(BF16)
    lo = (a - hi.astype(F32)).astype(BF16)
    return hi, lo


def _dot3(a, b):
    ah, al = _split(a)
    bh, bl = _split(b)
    return _dot(ah, bh) + _dot(ah, bl) + _dot(al, bh)


def _silu(x):
    return x * (1.0 / (1.0 + jnp.exp(-x)))


def _sigmoid(x):
    return 1.0 / (1.0 + jnp.exp(-x))


def _softplus(x):
    return jnp.maximum(x, 0.0) + jnp.log(1.0 + jnp.exp(-jnp.abs(x)))


def _layer_norm(x, g, b):
    mu = jnp.mean(x, axis=-1, keepdims=True)
    xc = x - mu
    var = jnp.mean(xc * xc, axis=-1, keepdims=True)
    return xc * lax.rsqrt(var + LN_EPS) * g + b


def _shift_rows(x, s):
    if s == 0:
        return x
    n = x.shape[0]
    rows = lax.broadcasted_iota(jnp.int32, x.shape, 0)
    valid = (rows >= s) if s > 0 else (rows < n + s)
    return jnp.where(valid, pltpu.roll(x, s % n, axis=0), 0.0)


def _depthwise_conv(x, w, pad_left):
    acc = None
    for i in range(w.shape[0]):
        term = _shift_rows(x, pad_left - i) * w[i:i + 1, :]
        acc = term if acc is None else acc + term
    return acc


def _mod_kernel(c_ref, w_ref, b_ref, o_ref):
    o_ref[0] = _dot3(_silu(c_ref[...]), w_ref[0]) + b_ref[0]


def _modulation(cond, ada_w, ada_b):
    depth, d, n = ada_w.shape
    rows = cond.shape[0]
    tn = 1536
    return pl.pallas_call(
        _mod_kernel,
        grid=(depth, n // tn),
        in_specs=[
            pl.BlockSpec((rows, d), lambda l, j: (0, 0)),
            pl.BlockSpec((1, d, tn), lambda l, j: (l, 0, j)),
            pl.BlockSpec((1, 1, tn), lambda l, j: (l, 0, j)),
        ],
        out_specs=pl.BlockSpec((1, rows, tn), lambda l, j: (l, 0, j)),
        out_shape=jax.ShapeDtypeStruct((depth, rows, n), F32),
        compiler_params=_params("parallel", "parallel"),
        name="modulation",
    )(cond, ada_w, ada_b.reshape(depth, 1, n))


def _inproj_kernel(*refs, splits, has_pos, has_bias):
    x_ref, sc_ref, sh_ref = refs[:3]
    i = 3
    pos_ref = None
    if has_pos:
        pos_ref = refs[i]
        i += 1
    w_ref = refs[i]
    i += 1
    b_ref = None
    if has_bias:
        b_ref = refs[i]
        i += 1
    o_refs = refs[i:]
    u = x_ref[0] * (1.0 + sc_ref[0]) + sh_ref[0]
    if has_pos:
        u = u + pos_ref[...]
    ub = u.astype(BF16)
    for o_ref, (s, e) in zip(o_refs, splits):
        acc = _dot(ub, w_ref[:, s:e])
        if has_bias:
            acc = acc + b_ref[:, s:e]
        o_ref[0] = acc


def _inproj(x, sc, sh, pos, w, b, widths, tm):
    bsz, length, d = x.shape
    n = w.shape[1]
    splits, s = [], 0
    for wd in widths:
        splits.append((s, s + wd))
        s += wd
    assert s == n
    tm = min(tm, length)
    in_specs = [
        pl.BlockSpec((1, tm, d), lambda bi, i: (bi, i, 0)),
        pl.BlockSpec((1, 1, d), lambda bi, i: (bi, 0, 0)),
        pl.BlockSpec((1, 1, d), lambda bi, i: (bi, 0, 0)),
    ]
    args = [x, sc, sh]
    if pos is not None:
        in_specs.append(pl.BlockSpec((tm, d), lambda bi, i: (i, 0)))
        args.append(pos)
    in_specs.append(pl.BlockSpec((d, n), lambda bi, i: (0, 0)))
    args.append(w)
    if b is not None:
        in_specs.append(pl.BlockSpec((1, n), lambda bi, i: (0, 0)))
        args.append(b)
    return pl.pallas_call(
        functools.partial(_inproj_kernel, splits=tuple(splits), has_pos=pos is not None,
                          has_bias=b is not None),
        grid=(bsz, length // tm),
        in_specs=in_specs,
        out_specs=[pl.BlockSpec((1, tm, wd), lambda bi, i: (bi, i, 0)) for wd in widths],
        out_shape=[jax.ShapeDtypeStruct((bsz, length, wd), F32) for wd in widths],
        compiler_params=_params("parallel", "parallel"),
        name="inproj",
    )(*args)


def _unit_tri_inverses(mats, lower):
    n = mats[0].shape[0]
    nb = 16
    np_ = len(mats)
    ii = lax.broadcasted_iota(jnp.int32, (n, n), 0)
    jj = lax.broadcasted_iota(jnp.int32, (n, n), 1)
    same16 = (ii // nb) == (jj // nb)
    same32 = (ii // (2 * nb)) == (jj // (2 * nb))
    dgs = []
    for a in mats:
        ad = jnp.where(same16, a, 0.0)
        dgs.append(ad[0:nb] + ad[nb:2 * nb] + ad[2 * nb:3 * nb] + ad[3 * nb:4 * nb])
    dg = jnp.concatenate(dgs, axis=0)
    rr = lax.broadcasted_iota(jnp.int32, dg.shape, 0)
    ll = lax.broadcasted_iota(jnp.int32, dg.shape, 1)
    xd = jnp.where(rr % nb == ll % nb, 1.0, 0.0)
    blk0 = (ll // nb) * nb
    for s in (range(nb - 1) if lower else range(nb - 1, 0, -1)):
        col = jnp.take_along_axis(dg, blk0 + s, axis=1)
        row = jnp.concatenate(
            [jnp.broadcast_to(xd[p * nb + s:p * nb + s + 1, :], (nb, n)) for p in range(np_)], axis=0)
        xd = xd - col * row
    ds = [jnp.where(same16, jnp.concatenate([xd[p * nb:(p + 1) * nb]] * (n // nb), axis=0), 0.0)
          for p in range(np_)]
    lvl1 = same32 & jnp.logical_not(same16)
    t1 = [_dot3(d, jnp.where(lvl1, a, 0.0)) for d, a in zip(ds, mats)]
    x1 = [d - _dot3(t, d) for d, t in zip(ds, t1)]
    t2 = [_dot3(x, jnp.where(same32, 0.0, a)) for x, a in zip(x1, mats)]
    return [x - _dot3(t, x) for x, t in zip(x1, t2)]


def _dn_kernel(alog_ref, dtb_ref,
               ql_ref, kl_ref, vl_ref, zl_ref, qc_ref, kc_ref, vc_ref, zc_ref, gt_ref,
               cwq_ref, cwk_ref, cwv_ref, ng_ref,
               yl_ref, yc_ref,
               qn_s, kn_s, vn_s, gc_s, bt_s, wq_s, u_s, at_s, kdt_s, gl_s, o_s,
               *, ctx_len, lat_len):
    h = pl.program_id(1)
    csz = DN_CHUNK
    nc_ctx = ctx_len // csz
    nc = (ctx_len + lat_len) // csz

    def prep(src_ref, cw_ref, kind):
        t = _silu(_depthwise_conv(src_ref[0], cw_ref[...], DN_CONV // 2))
        if kind == "v":
            return t
        t = t * lax.rsqrt(jnp.sum(t * t, axis=-1, keepdims=True) + RMS_EPS)
        return t * (DN_DK ** -0.5) if kind == "q" else t

    qn_s[0:ctx_len, :] = prep(qc_ref, cwq_ref, "q")
    qn_s[ctx_len:, :] = prep(ql_ref, cwq_ref, "q")
    kn_s[0:ctx_len, :] = prep(kc_ref, cwk_ref, "k")
    kn_s[ctx_len:, :] = prep(kl_ref, cwk_ref, "k")
    vn_s[0:ctx_len, :] = prep(vc_ref, cwv_ref, "v")
    vn_s[ctx_len:, :] = prep(vl_ref, cwv_ref, "v")

    ii = lax.broadcasted_iota(jnp.int32, (csz, csz), 0)
    jj = lax.broadcasted_iota(jnp.int32, (csz, csz), 1)
    eye = ii == jj
    for d in range(2):
        graw = gt_ref[0, d * 2 * DN_HEADS + h]
        braw = gt_ref[0, d * 2 * DN_HEADS + DN_HEADS + h]
        a_neg = -jnp.exp(jnp.zeros_like(graw) + alog_ref[d, h])
        g = a_neg * _softplus(graw + dtb_ref[d, h])
        tri = jnp.where((ii <= jj) if d == 0 else (ii >= jj), 1.0, 0.0).astype(BF16)
        g1 = g.astype(BF16)
        r1 = g - g1.astype(F32)
        g2 = r1.astype(BF16)
        g3 = (r1 - g2.astype(F32)).astype(BF16)
        gc_s[d] = _dot(g1, tri) + _dot(g2, tri) + _dot(g3, tri)
        bt_s[d] = _sigmoid(braw)

    o_s[...] = jnp.zeros_like(o_s)

    group = 4
    assert nc % group == 0

    def chunk_prep(gi, carry):
        ns = [gi * group + c for c in range(group)]
        r0s = [pl.multiple_of(n * csz, csz) for n in ns]
        qs = [qn_s[pl.ds(r0, csz), :] for r0 in r0s]
        ks = [kn_s[pl.ds(r0, csz), :] for r0 in r0s]
        vs = [vn_s[pl.ds(r0, csz), :] for r0 in r0s]
        kbfs = [k.astype(BF16) for k in ks]
        qks = [_dot_nt(q.astype(BF16), kbf) for q, kbf in zip(qs, kbfs)]
        for d in range(2):
            incl = (ii >= jj) if d == 0 else (ii <= jj)
            strict = (ii > jj) if d == 0 else (ii < jj)
            grs = [gc_s[d, pl.ds(n, 1), :] for n in ns]
            grows = [jnp.broadcast_to(gr, (csz, csz)) for gr in grs]
            gcols = [jnp.sum(jnp.where(eye, grow, 0.0), axis=1, keepdims=True) for grow in grows]
            bcols = [jnp.sum(jnp.where(eye, jnp.broadcast_to(bt_s[d, pl.ds(n, 1), :], (csz, csz)),
                                       0.0), axis=1, keepdims=True) for n in ns]
            decays = [jnp.where(incl, jnp.exp(jnp.where(incl, gcol - grow, 0.0)), 0.0)
                      for gcol, grow in zip(gcols, grows)]
            kbs = [k * bcol for k, bcol in zip(ks, bcols)]
            amats = [jnp.where(strict, _dot_nt(kb.astype(BF16), kbf) * decay, 0.0)
                     for kb, kbf, decay in zip(kbs, kbfs, decays)]
            tbs = [t.astype(BF16) for t in _unit_tri_inverses(amats, lower=(d == 0))]
            for c in range(group):
                n, r0, gr, gcol = ns[c], r0s[c], grs[c], gcols[c]
                eg = jnp.exp(gcol)
                u = _dot(tbs[c], (vs[c] * bcols[c]).astype(BF16))
                w = _dot(tbs[c], (kbs[c] * eg).astype(BF16))
                attn = jnp.where(incl, qks[c] * decays[c], 0.0)
                glast = gr[:, csz - 1:csz] if d == 0 else gr[:, 0:1]
                kd = ks[c] * jnp.exp(glast - gcol)
                wq_s[d, n, 0:csz, :] = w.astype(BF16)
                wq_s[d, n, csz:2 * csz, :] = (qs[c] * eg).astype(BF16)
                u_s[d, pl.ds(r0, csz), :] = u
                at_s[d, n] = attn.astype(BF16)
                kdt_s[d, n] = kd.T.astype(BF16)
                gl_s[d, pl.ds(n, 1), :] = jnp.broadcast_to(jnp.exp(glast), (1, LANES))
        return carry

    lax.fori_loop(0, nc // group, chunk_prep, 0)

    def step(i, states):
        new_states = []
        for d in range(2):
            if d == 0:
                n = i
            else:
                n = jnp.where(i < nc_ctx, nc_ctx - 1 - i, nc + nc_ctx - 1 - i)
            r0 = pl.multiple_of(n * csz, csz)
            s = states[d]
            r = _dot(wq_s[d, n], s.astype(BF16))
            v_new = (u_s[d, pl.ds(r0, csz), :] - r[0:csz]).astype(BF16)
            o = r[csz:2 * csz] + _dot(at_s[d, n], v_new)
            o_s[pl.ds(r0, csz), :] += o
            s = s * gl_s[d, pl.ds(n, 1), :] + _dot(kdt_s[d, n], v_new)
            new_states.append(s)
        return tuple(new_states)

    zero = jnp.zeros((DN_DK, DN_DK), F32)
    lax.fori_loop(0, nc, step, (zero, zero))

    def gated_norm(o, z):
        o = o * lax.rsqrt(jnp.mean(o * o, axis=-1, keepdims=True) + RMS_EPS) * ng_ref[...]
        return (o * _silu(z)).astype(yl_ref.dtype)

    yc_ref[0] = gated_norm(o_s[0:ctx_len, :], zc_ref[0])
    yl_ref[0] = gated_norm(o_s[ctx_len:, :], zl_ref[0])


def _deltanet(qkv_l, z_l, qkv_c, z_c, gates, conv_w, a_log, dt_bias, norm_g):
    bsz, lat_len, _ = qkv_l.shape
    ctx_len = qkv_c.shape[1]
    tot = ctx_len + lat_len
    nc = tot // DN_CHUNK
    ncp = gates.shape[2]
    hd = DN_DK
    nh = DN_HEADS

    def col(off):
        return lambda b, h: (b, 0, off + h)

    def wcol(off):
        return lambda b, h: (0, off + h)

    smem = pl.BlockSpec(memory_space=pltpu.SMEM)
    in_specs = [
        smem, smem,
        pl.BlockSpec((1, lat_len, hd), col(0)),
        pl.BlockSpec((1, lat_len, hd), col(nh)),
        pl.BlockSpec((1, lat_len, hd), col(2 * nh)),
        pl.BlockSpec((1, lat_len, hd), col(0)),
        pl.BlockSpec((1, ctx_len, hd), col(0)),
        pl.BlockSpec((1, ctx_len, hd), col(nh)),
        pl.BlockSpec((1, ctx_len, hd), col(2 * nh)),
        pl.BlockSpec((1, ctx_len, hd), col(0)),
        pl.BlockSpec((1, 4 * nh, ncp, DN_CHUNK), lambda b, h: (b, 0, 0, 0)),
        pl.BlockSpec((DN_CONV, hd), wcol(0)),
        pl.BlockSpec((DN_CONV, hd), wcol(nh)),
        pl.BlockSpec((DN_CONV, hd), wcol(2 * nh)),
        pl.BlockSpec((1, hd), lambda b, h: (0, 0)),
    ]
    scratch = [
        pltpu.VMEM((tot, hd), F32), pltpu.VMEM((tot, hd), F32), pltpu.VMEM((tot, hd), F32),
        pltpu.VMEM((2, ncp, DN_CHUNK), F32), pltpu.VMEM((2, ncp, DN_CHUNK), F32),
        pltpu.VMEM((2, nc, 2 * DN_CHUNK, hd), BF16),
        pltpu.VMEM((2, tot, hd), F32),
        pltpu.VMEM((2, nc, DN_CHUNK, DN_CHUNK), BF16),
        pltpu.VMEM((2, nc, hd, DN_CHUNK), BF16),
        pltpu.VMEM((2, nc, LANES), F32),
        pltpu.VMEM((tot, hd), F32),
    ]
    return pl.pallas_call(
        functools.partial(_dn_kernel, ctx_len=ctx_len, lat_len=lat_len),
        grid=(bsz, nh),
        in_specs=in_specs,
        out_specs=[pl.BlockSpec((1, lat_len, hd), col(0)),
                   pl.BlockSpec((1, ctx_len, hd), col(0))],
        out_shape=[jax.ShapeDtypeStruct((bsz, lat_len, nh * hd), BF16),
                   jax.ShapeDtypeStruct((bsz, ctx_len, nh * hd), BF16)],
        scratch_shapes=scratch,
        compiler_params=_params("parallel", "parallel"),
        name="deltanet",
    )(a_log, dt_bias, qkv_l, qkv_l, qkv_l, z_l, qkv_c, qkv_c, qkv_c, z_c, gates,
      conv_w, conv_w, conv_w, norm_g.reshape(1, hd))


def _hy_filter_kernel(z_ref, win_ref, bin_ref, wmid_ref, bmid_ref, wout_ref, freq_ref, dec_ref,
                      o_ref):
    freq = freq_ref[...]
    hcur = jnp.sin(freq * (_dot3(z_ref[...], win_ref[...]) + bin_ref[...]))
    for i in range(wmid_ref.shape[0]):
        hcur = jnp.sin(freq * (_dot3(hcur, wmid_ref[i]) + bmid_ref[i]))
    o_ref[...] = _dot3(hcur, wout_ref[...]) * dec_ref[...]


def _hy_filter(z, w_in, b_in, w_mid, b_mid, w_out, freq, dec2):
    length = z.shape[0]
    n_out = w_out.shape[1]
    tl = min(256, length)

    def whole(a):
        return pl.BlockSpec(a.shape, lambda i: (0,) * a.ndim)

    return pl.pallas_call(
        _hy_filter_kernel,
        grid=(length // tl,),
        in_specs=[pl.BlockSpec((tl, z.shape[1]), lambda i: (i, 0)),
                  whole(w_in), whole(b_in), whole(w_mid), whole(b_mid), whole(w_out), whole(freq),
                  pl.BlockSpec((tl, n_out), lambda i: (i, 0))],
        out_specs=pl.BlockSpec((tl, n_out), lambda i: (i, 0)),
        out_shape=jax.ShapeDtypeStruct((length, n_out), F32),
        compiler_params=_params("parallel"),
        name="hyena_filter",
    )(z, w_in, b_in, w_mid, b_mid, w_out, freq, dec2)


def _filt_spec_kernel(fc_ref, fs_ref, h_ref, kc_ref, ks_ref, *, width):
    j = pl.program_id(0)
    hmat = h_ref[...]
    rows = lax.broadcasted_iota(jnp.int32, hmat.shape, 0)
    cols = lax.broadcasted_iota(jnp.int32, hmat.shape, 1)
    hmat = jnp.where((rows == 0) & (cols >= width), 0.0, hmat)
    hh, hl = _split(hmat)
    c = _dot(fc_ref[...], hh) + _dot(fc_ref[...], hl)
    s = _dot(fs_ref[...], hh) + _dot(fs_ref[...], hl)
    kc_ref[...] = c[:, :width] + c[:, width:]
    orow = lax.broadcasted_iota(jnp.int32, (c.shape[0], width), 0)
    sign = jnp.where((orow == 0) & (j == 0), 1.0, -1.0)
    ks_ref[...] = s[:, :width] + sign * s[:, width:]


def _filt_spec(fwd, hfilt, tf):
    length, two_w = hfilt.shape
    width = two_w // 2
    nt = length // tf
    return pl.pallas_call(
        functools.partial(_filt_spec_kernel, width=width),
        grid=(nt,),
        in_specs=[pl.BlockSpec((tf, length), lambda j: (j, 0)),
                  pl.BlockSpec((tf, length), lambda j: (nt + j, 0)),
                  pl.BlockSpec((length, two_w), lambda j: (0, 0))],
        out_specs=[pl.BlockSpec((tf, width), lambda j: (j, 0)),
                   pl.BlockSpec((tf, width), lambda j: (j, 0))],
        out_shape=[jax.ShapeDtypeStruct((length, width), F32)] * 2,
        compiler_params=_params("arbitrary"),
        name="hyena_filter_spectrum",
    )(fwd, fwd, hfilt)


def _hy_prep_kernel(x0_ref, x1_ref, v_ref, w0_ref, w1_ref, w2_ref, b0_ref, b1_ref, b2_ref,
                    x0o_ref, vvo_ref):
    x0 = _depthwise_conv(x0_ref[0], w0_ref[...], HY_CONV // 2) + b0_ref[...]
    x1 = _depthwise_conv(x1_ref[0], w1_ref[...], HY_CONV // 2) + b1_ref[...]
    v = _depthwise_conv(v_ref[0], w2_ref[...], HY_CONV // 2) + b2_ref[...]
    x0o_ref[0] = x0
    vvo_ref[0] = v * x1


def _hy_prep(p_hy, conv_w, conv_b):
    bsz, length, three_w = p_hy.shape
    width = three_w // 3
    nb = width // LANES

    def col(off):
        return lambda b, j: (b, 0, off + j)

    def wcol(off):
        return lambda b, j: (0, off + j)

    k = conv_w.shape[0]
    return pl.pallas_call(
        _hy_prep_kernel,
        grid=(bsz, nb),
        in_specs=[pl.BlockSpec((1, length, LANES), col(0)),
                  pl.BlockSpec((1, length, LANES), col(nb)),
                  pl.BlockSpec((1, length, LANES), col(2 * nb)),
                  pl.BlockSpec((k, LANES), wcol(0)),
                  pl.BlockSpec((k, LANES), wcol(nb)),
                  pl.BlockSpec((k, LANES), wcol(2 * nb)),
                  pl.BlockSpec((1, LANES), wcol(0)),
                  pl.BlockSpec((1, LANES), wcol(nb)),
                  pl.BlockSpec((1, LANES), wcol(2 * nb))],
        out_specs=[pl.BlockSpec((1, length, LANES), col(0)),
                   pl.BlockSpec((1, length, LANES), col(0))],
        out_shape=[jax.ShapeDtypeStruct((bsz, length, width), F32)] * 2,
        compiler_params=_params("parallel", "parallel"),
        name="hyena_prep",
    )(p_hy, p_hy, p_hy, conv_w, conv_w, conv_w, conv_b, conv_b, conv_b)


def _dft_fwd_kernel(fc_ref, fs_ref, v_ref, kc_ref, ks_ref, yc_ref, ys_ref, vb_s):
    j = pl.program_id(1)

    @pl.when(j == 0)
    def _():
        vb_s[...] = v_ref[0].astype(BF16)

    uc = _dot(fc_ref[...], vb_s[...])
    us = _dot(fs_ref[...], vb_s[...])
    kc = kc_ref[...]
    ks = ks_ref[...]
    rows = lax.broadcasted_iota(jnp.int32, uc.shape, 0)
    special = (rows == 0) & (j == 0)
    yc_ref[0] = (uc * kc - jnp.where(special, 0.0, us * ks)).astype(BF16)
    ys_ref[0] = jnp.where(special, us * ks, uc * ks + us * kc).astype(BF16)


def _dft_fwd(fwd, vv, kc, ks, tf):
    bsz, length, width = vv.shape
    nt = length // tf
    return pl.pallas_call(
        _dft_fwd_kernel,
        grid=(bsz, nt),
        in_specs=[pl.BlockSpec((tf, length), lambda b, j: (j, 0)),
                  pl.BlockSpec((tf, length), lambda b, j: (nt + j, 0)),
                  pl.BlockSpec((1, length, width), lambda b, j: (b, 0, 0)),
                  pl.BlockSpec((tf, width), lambda b, j: (j, 0)),
                  pl.BlockSpec((tf, width), lambda b, j: (j, 0))],
        out_specs=[pl.BlockSpec((1, tf, width), lambda b, j: (b, j, 0)),
                   pl.BlockSpec((1, tf, width), lambda b, j: (b, j, 0))],
        out_shape=[jax.ShapeDtypeStruct((bsz, length, width), BF16)] * 2,
        scratch_shapes=[pltpu.VMEM((length, width), BF16)],
        compiler_params=_params("parallel", "arbitrary"),
        name="hyena_dft_fwd",
    )(fwd, fwd, vv, kc, ks)


def _dft_inv_kernel(ic_ref, is_ref, yc_ref, ys_ref, vv_ref, x0_ref, skip_ref, o_ref):
    y = _dot(ic_ref[...], yc_ref[0]) + _dot(is_ref[...], ys_ref[0])
    o_ref[0] = (x0_ref[0] * (y + vv_ref[0] * skip_ref[...])).astype(o_ref.dtype)


def _dft_inv(inv, yc, ys, vv, x0, skip, tt):
    bsz, length, width = vv.shape
    nt = length // tt
    return pl.pallas_call(
        _dft_inv_kernel,
        grid=(bsz, nt),
        in_specs=[pl.BlockSpec((tt, length), lambda b, i: (i, 0)),
                  pl.BlockSpec((tt, length), lambda b, i: (i, 1)),
                  pl.BlockSpec((1, length, width), lambda b, i: (b, 0, 0)),
                  pl.BlockSpec((1, length, width), lambda b, i: (b, 0, 0)),
                  pl.BlockSpec((1, tt, width), lambda b, i: (b, i, 0)),
                  pl.BlockSpec((1, tt, width), lambda b, i: (b, i, 0)),
                  pl.BlockSpec((1, width), lambda b, i: (0, 0))],
        out_specs=pl.BlockSpec((1, tt, width), lambda b, i: (b, i, 0)),
        out_shape=jax.ShapeDtypeStruct((bsz, length, width), BF16),
        compiler_params=_params("parallel", "parallel"),
        name="hyena_dft_inv",
    )(inv, inv, yc, ys, vv, x0, skip)


@functools.lru_cache(maxsize=None)
def _hyena_tables(length):
    n2 = 2 * length
    t = np.linspace(0.0, 1.0, length)[:, None]
    bands = (HY_EMB - 1) // 2
    wpos = 2.0 * np.pi * np.arange(length)[:, None] / length
    fb = np.linspace(1e-4, bands - 1, bands)[None]
    z = np.concatenate([t, np.cos(fb * wpos), -np.sin(fb * wpos)], axis=-1)
    zpad = np.zeros((length, LANES))
    zpad[:, :HY_EMB] = z
    f = np.arange(length)[:, None]
    n = np.arange(length)[None, :]
    ang = 2.0 * np.pi * ((f * n) % n2) / n2
    cos_m = np.cos(ang)
    sin_m = np.sin(ang)
    sin_m[0, :] = np.cos(np.pi * np.arange(length))
    fwd = np.concatenate([cos_m, sin_m], axis=0)
    scale = np.full((1, n2), 2.0 / n2)
    scale[0, 0] = 1.0 / n2
    scale[0, length] = 1.0 / n2
    inv = fwd.T * scale
    return (zpad.astype(np.float32), t.astype(np.float32), fwd.astype(np.float32),
            inv.astype(np.float32))


def _hyena(p_hy, conv_w, conv_b, filt, skip):
    bsz, length, three_w = p_hy.shape
    width = three_w // 3
    w_in, b_in, w_mid, b_mid, w_out, freq = filt
    zpad, t, fwd, inv = _hyena_tables(length)
    deltas = np.abs(np.linspace(HY_MIN_DECAY, HY_MAX_DECAY, width))[None, :]
    dec = np.exp(-t.astype(np.float64) * deltas).astype(np.float32)
    dec2 = jnp.asarray(np.concatenate([dec, dec], axis=1))
    ffn = w_in.shape[1]
    w_in_pad = jnp.zeros((LANES, ffn), F32).at[:HY_EMB].set(w_in)
    hfilt = _hy_filter(jnp.asarray(zpad), w_in_pad, b_in.reshape(1, ffn), w_mid,
                       b_mid.reshape(-1, 1, ffn), w_out, freq.reshape(1, ffn), dec2)
    fwd_b = jnp.asarray(fwd).astype(BF16)
    inv_b = jnp.asarray(inv).astype(BF16)
    tf = min(256, length)
    kc, ks = _filt_spec(fwd_b, hfilt, tf)
    x0, vv = _hy_prep(p_hy, conv_w, conv_b.reshape(1, three_w))
    yc, ys = _dft_fwd(fwd_b, vv, kc, ks, tf)
    return _dft_inv(inv_b, yc, ys, vv, x0, skip.reshape(1, width), min(256, length))


def _gelu_tanh(x):
    return 0.5 * x * (1.0 + jnp.tanh(math.sqrt(2.0 / math.pi) * (x + 0.044715 * x * x * x)))


def _lru_kernel(xl_ref, yl_ref, xc_ref, cw_ref, cb_ref, wa_ref, ba_ref, wx_ref, bx_ref, ap_ref,
                o_ref, xs_s, a_s, b_s, h_s, *, ctx_len, lat_len):
    tot = ctx_len + lat_len
    ngrp = tot // 8
    ngrp_ctx = ctx_len // 8
    xs_s[0:ctx_len, :] = _depthwise_conv(xc_ref[0], cw_ref[...], LRU_CONV // 2) + cb_ref[...]
    xs_s[ctx_len:, :] = _depthwise_conv(xl_ref[0], cw_ref[...], LRU_CONV // 2) + cb_ref[...]
    xs = xs_s[...]
    xsb = xs.astype(BF16)
    rows8 = lax.broadcasted_iota(jnp.int32, xs.shape, 0) % 8
    for d in range(2):
        r = _sigmoid(_dot(xsb, wa_ref[d, 0].astype(BF16)) + ba_ref[d])
        gi = _sigmoid(_dot(xsb, wx_ref[d, 0].astype(BF16)) + bx_ref[d])
        log_a = -LRU_C * r * _softplus(ap_ref[d])
        a = jnp.exp(log_a)
        b = jnp.sqrt(1.0 - a * a) * (gi * xs)
        for s in (1, 2, 4):
            if d == 0:
                keep = rows8 >= s
                sa = jnp.where(keep, pltpu.roll(a, s, axis=0), 1.0)
                sb = jnp.where(keep, pltpu.roll(b, s, axis=0), 0.0)
            else:
                keep = rows8 < 8 - s
                sa = jnp.where(keep, pltpu.roll(a, tot - s, axis=0), 1.0)
                sb = jnp.where(keep, pltpu.roll(b, tot - s, axis=0), 0.0)
            b = a * sb + b
            a = a * sa
        a_s[d] = a
        b_s[d] = b

    def group_fwd(i, c):
        r0 = pl.multiple_of(i * 8, 8)
        hg = a_s[0, pl.ds(r0, 8), :] * c + b_s[0, pl.ds(r0, 8), :]
        h_s[pl.ds(r0, 8), :] = hg
        return jnp.broadcast_to(hg[7:8, :], hg.shape)

    def group_bwd(i, c):
        gidx = jnp.where(i < ngrp_ctx, ngrp_ctx - 1 - i, ngrp + ngrp_ctx - 1 - i)
        r0 = pl.multiple_of(gidx * 8, 8)
        hg = a_s[1, pl.ds(r0, 8), :] * c + b_s[1, pl.ds(r0, 8), :]
        h_s[pl.ds(r0, 8), :] += hg
        return jnp.broadcast_to(hg[0:1, :], hg.shape)

    zero = jnp.zeros((8, xs.shape[1]), F32)
    lax.fori_loop(0, ngrp, group_fwd, zero, unroll=4)
    lax.fori_loop(0, ngrp, group_bwd, zero, unroll=4)
    o_ref[0] = (h_s[ctx_len:, :] * _gelu_tanh(yl_ref[0])).astype(o_ref.dtype)


def _rglru(xb_l, yb_l, xb_c, conv_w, conv_b, wa, ba, wx, bx, a_param):
    bsz, lat_len, width = xb_l.shape
    ctx_len = xb_c.shape[1]
    tot = ctx_len + lat_len
    blk = width // LRU_HEADS

    def col(b, h):
        return (b, 0, h)

    def wcol(b, h):
        return (0, h)

    def w3(b, h):
        return (0, 0, h)

    return pl.pallas_call(
        functools.partial(_lru_kernel, ctx_len=ctx_len, lat_len=lat_len),
        grid=(bsz, LRU_HEADS),
        in_specs=[pl.BlockSpec((1, lat_len, blk), col),
                  pl.BlockSpec((1, lat_len, blk), col),
                  pl.BlockSpec((1, ctx_len, blk), col),
                  pl.BlockSpec((LRU_CONV, blk), wcol),
                  pl.BlockSpec((1, blk), wcol),
                  pl.BlockSpec((2, 1, blk, blk), lambda b, h: (0, h, 0, 0)),
                  pl.BlockSpec((2, 1, blk), w3),
                  pl.BlockSpec((2, 1, blk, blk), lambda b, h: (0, h, 0, 0)),
                  pl.BlockSpec((2, 1, blk), w3),
                  pl.BlockSpec((2, 1, blk), w3)],
        out_specs=pl.BlockSpec((1, lat_len, blk), col),
        out_shape=jax.ShapeDtypeStruct((bsz, lat_len, width), BF16),
        scratch_shapes=[pltpu.VMEM((tot, blk), F32),
                        pltpu.VMEM((2, tot, blk), F32),
                        pltpu.VMEM((2, tot, blk), F32),
                        pltpu.VMEM((tot, blk), F32)],
        compiler_params=_params("parallel", "parallel"),
        name="rglru",
    )(xb_l, yb_l, xb_c, conv_w, conv_b.reshape(1, width), wa, ba.reshape(2, 1, width), wx,
      bx.reshape(2, 1, width), a_param.reshape(2, 1, width))


def _post_mixer_kernel(*refs, n_in, has_bias):
    a_refs = refs[:n_in]
    w_refs = refs[n_in:2 * n_in]
    i = 2 * n_in
    b_ref = None
    if has_bias:
        b_ref = refs[i]
        i += 1
    x_ref, g1_ref, sc_ref, sh_ref, lng_ref, lnb_ref, rw_ref, rb_ref, base_ref = refs[i:i + 9]
    x1_ref, v_ref, route_ref, cnt_ref = refs[i + 9:]

    @pl.when((pl.program_id(0) == 0) & (pl.program_id(1) == 0))
    def _():
        cnt_ref[...] = base_ref[...]

    y = None
    for a_ref, w_ref in zip(a_refs, w_refs):
        t = _dot(a_ref[0], w_ref[...])
        y = t if y is None else y + t
    if has_bias:
        y = y + b_ref[...]
    x1 = _layer_norm(DEEPNORM_ALPHA * x_ref[0] + g1_ref[0] * y, lng_ref[...], lnb_ref[...])
    x1_ref[0] = x1
    v = x1 * (1.0 + sc_ref[0]) + sh_ref[0]
    v_ref[0] = v
    logits = _dot3(v, rw_ref[...]) + rb_ref[...]
    tm = logits.shape[0]
    lane = lax.broadcasted_iota(jnp.int32, logits.shape, 1).astype(F32)
    work = logits
    picks, firsts = [], []
    m0 = None
    for kk in range(TOP_K):
        m = jnp.max(work, axis=-1, keepdims=True)
        if kk == 0:
            m0 = m
        first = jnp.min(jnp.where(work == m, lane, float(LANES)), axis=-1, keepdims=True)
        pick = lane == first
        picks.append(pick)
        firsts.append(first)
        work = jnp.where(pick, -jnp.inf, work)
    sel = jnp.where(picks[0] | picks[1] | picks[2] | picks[3], 1.0, 0.0)
    e = sel * jnp.exp(logits - m0)
    gate = e / jnp.sum(e, axis=-1, keepdims=True)
    ti = lax.broadcasted_iota(jnp.int32, (tm, tm), 0)
    tj = lax.broadcasted_iota(jnp.int32, (tm, tm), 1)
    before = jnp.where(ti > tj, 1.0, 0.0).astype(BF16)
    slot = _dot(before, sel.astype(BF16)) + cnt_ref[...]
    route = jnp.zeros(logits.shape, F32)
    for kk in range(TOP_K):
        rank = jnp.sum(jnp.where(picks[kk], slot, 0.0), axis=-1, keepdims=True)
        wgt = jnp.sum(jnp.where(picks[kk], gate, 0.0), axis=-1, keepdims=True)
        route = jnp.where(lane == float(kk), firsts[kk], route)
        route = jnp.where(lane == float(TOP_K + kk), rank, route)
        route = jnp.where(lane == float(2 * TOP_K + kk), wgt, route)
    route_ref[0] = route
    cnt_ref[...] += jnp.sum(sel, axis=0, keepdims=True)


def _post_mixer(acts, ws, bias, x, g1, sc2, sh2, ln_g, ln_b, router_w, router_b, base, tm):
    bsz, length, d = x.shape
    tm = min(tm, length)
    n_in = len(acts)

    def row(bi, i):
        return (bi, i, 0)

    def per_b(bi, i):
        return (bi, 0, 0)

    def const(bi, i):
        return (0, 0)

    in_specs = [pl.BlockSpec((1, tm, a.shape[2]), row) for a in acts]
    in_specs += [pl.BlockSpec(w.shape, const) for w in ws]
    args = list(acts) + list(ws)
    if bias is not None:
        in_specs.append(pl.BlockSpec((1, d), const))
        args.append(bias.reshape(1, d))
    in_specs += [pl.BlockSpec((1, tm, d), row),
                 pl.BlockSpec((1, 1, d), per_b), pl.BlockSpec((1, 1, d), per_b),
                 pl.BlockSpec((1, 1, d), per_b),
                 pl.BlockSpec((1, d), const), pl.BlockSpec((1, d), const),
                 pl.BlockSpec((d, LANES), const), pl.BlockSpec((1, LANES), const),
                 pl.BlockSpec((1, LANES), const)]
    rw = jnp.zeros((d, LANES), F32).at[:, :N_EXPERTS].set(router_w)
    rb = jnp.full((1, LANES), -1e30, F32).at[0, :N_EXPERTS].set(router_b)
    args += [x, g1, sc2, sh2, ln_g.reshape(1, d), ln_b.reshape(1, d), rw, rb, base]
    return pl.pallas_call(
        functools.partial(_post_mixer_kernel, n_in=n_in, has_bias=bias is not None),
        grid=(bsz, length // tm),
        in_specs=in_specs,
        out_specs=[pl.BlockSpec((1, tm, d), row), pl.BlockSpec((1, tm, d), row),
                   pl.BlockSpec((1, tm, LANES), row), pl.BlockSpec((1, LANES), const)],
        out_shape=[jax.ShapeDtypeStruct((bsz, length, d), F32),
                   jax.ShapeDtypeStruct((bsz, length, d), F32),
                   jax.ShapeDtypeStruct((bsz, length, LANES), F32),
                   jax.ShapeDtypeStruct((1, LANES), F32)],
        compiler_params=_params("arbitrary", "arbitrary"),
        name="post_mixer",
    )(*args)


def _swiglu_interleaved(gu):
    nxt = pltpu.roll(gu, gu.shape[1] - 1, axis=1)
    glu = jnp.minimum(gu, SWIGLU_LIMIT)
    lin = jnp.clip(nxt, -SWIGLU_LIMIT, SWIGLU_LIMIT)
    act = glu * _sigmoid(SWIGLU_ALPHA * glu) * (lin + 1.0)
    lane = lax.broadcasted_iota(jnp.int32, gu.shape, 1)
    return jnp.where(lane % 2 == 0, act, 0.0)


def _moe_rows_kernel(te_ref, nv_ref, xs_ref, w1_ref, b1_ref, w2_ref, b2_ref, ys_ref):
    del te_ref
    nv = nv_ref[pl.program_id(0)]

    @pl.when(nv > 0)
    def _():
        rows = lax.broadcasted_iota(jnp.int32, xs_ref.shape, 0)
        x = jnp.where(rows < nv, xs_ref[...], 0.0).astype(BF16)
        act = _swiglu_interleaved(_dot(x, w1_ref[0]) + b1_ref[0])
        ys_ref[...] = _dot(act.astype(BF16), w2_ref[0]) + b2_ref[0]

    @pl.when(nv == 0)
    def _():
        ys_ref[...] = jnp.zeros_like(ys_ref)


def _moe_rows(xs, tile_expert, tile_rows, w1, b1, w2z, b2, tm):
    n_rows, d = xs.shape
    _, _, dff2 = w1.shape

    def row(i, te, nv):
        return (i, 0)

    def exp3(i, te, nv):
        return (te[i], 0, 0)

    return pl.pallas_call(
        _moe_rows_kernel,
        grid_spec=pltpu.PrefetchScalarGridSpec(
            num_scalar_prefetch=2,
            grid=(n_rows // tm,),
            in_specs=[pl.BlockSpec((tm, d), row),
                      pl.BlockSpec((1, d, dff2), exp3), pl.BlockSpec((1, 1, dff2), exp3),
                      pl.BlockSpec((1, dff2, d), exp3), pl.BlockSpec((1, 1, d), exp3)],
            out_specs=pl.BlockSpec((tm, d), row)),
        out_shape=jax.ShapeDtypeStruct((n_rows, d), F32),
        compiler_params=_params("arbitrary"),
        name="moe_rows",
    )(tile_expert, tile_rows, xs, w1, b1, w2z, b2)


def _moe_combine_kernel(y0_ref, y1_ref, y2_ref, y3_ref, route_ref, x1_ref, g2_ref, lng_ref, lnb_ref,
                        o_ref):
    route = route_ref[...]
    lane = lax.broadcasted_iota(jnp.int32, route.shape, 1)
    f = None
    for kk, y_ref in enumerate((y0_ref, y1_ref, y2_ref, y3_ref)):
        wgt = jnp.sum(jnp.where(lane == 2 * TOP_K + kk, route, 0.0), axis=-1, keepdims=True)
        term = wgt * y_ref[0]
        f = term if f is None else f + term
    o_ref[0] = _layer_norm(DEEPNORM_ALPHA * x1_ref[0] + g2_ref[0] * f, lng_ref[...], lnb_ref[...])


def _moe_combine(yg, route, row_offset, x1, g2, ln_g, ln_b, tm):
    bsz, length, d = x1.shape
    tm = min(tm, length)
    nt = length // tm
    off = row_offset // tm

    def pick(kk):
        return lambda bi, i: (kk, off + bi * nt + i, 0)

    def const(bi, i):
        return (0, 0)

    return pl.pallas_call(
        _moe_combine_kernel,
        grid=(bsz, nt),
        in_specs=[pl.BlockSpec((1, tm, d), pick(kk)) for kk in range(TOP_K)] + [
            pl.BlockSpec((tm, LANES), lambda bi, i: (off + bi * nt + i, 0)),
            pl.BlockSpec((1, tm, d), lambda bi, i: (bi, i, 0)),
            pl.BlockSpec((1, 1, d), lambda bi, i: (bi, 0, 0)),
            pl.BlockSpec((1, d), const), pl.BlockSpec((1, d), const)],
        out_specs=pl.BlockSpec((1, tm, d), lambda bi, i: (bi, i, 0)),
        out_shape=jax.ShapeDtypeStruct((bsz, length, d), F32),
        compiler_params=_params("parallel", "parallel"),
        name="moe_combine",
    )(yg, yg, yg, yg, route, x1, g2, ln_g.reshape(1, d), ln_b.reshape(1, d))


SC_CORES = 2
SC_SUBCORES = 16
SC_WORKERS = SC_CORES * SC_SUBCORES
SC_WINDOW = 32


def _sc_row_pipeline(nwin, read, write):
    read(0, 0).start()

    @pl.loop(0, nwin, step=2)
    def _(w0):
        for b in range(2):
            w = w0 + b

            @pl.when(w + 1 < nwin)
            def _():
                @pl.when(w >= 1)
                def _():
                    write(w - 1, 1 - b).wait()

                read(w + 1, 1 - b).start()

            read(w, b).wait()
            write(w, b).start()

    write(nwin - 2, 0).wait()
    write(nwin - 1, 1).wait()


def _sc_scatter_rows(src, pos, n_out):
    t_rows, d = src.shape
    nw, nwin, win = pos.shape
    assert nw == SC_WORKERS and nwin % 2 == 0 and t_rows % (nwin * win) == 0
    mesh = plsc.VectorSubcoreMesh(core_axis_name="c", subcore_axis_name="s")

    @functools.partial(
        pl.kernel, mesh=mesh, out_type=jax.ShapeDtypeStruct((n_out, d), src.dtype),
        scratch_types=[pltpu.VMEM((nwin, win), jnp.int32), pltpu.VMEM((2, win, d), src.dtype),
                       pltpu.SemaphoreType.DMA((2,)), pltpu.SemaphoreType.DMA((2,))])
    def scatter(src_hbm, pos_hbm, out_hbm, idx_v, rows_v, rsem, wsem):
        wid = lax.axis_index("s") * SC_CORES + lax.axis_index("c")
        t0 = lax.rem(wid * (nwin * win), t_rows)
        pltpu.sync_copy(pos_hbm.at[wid], idx_v)

        def read(w, slot):
            return pltpu.make_async_copy(src_hbm.at[pl.ds(t0 + w * win, win)], rows_v.at[slot],
                                         rsem.at[slot])

        def write(w, slot):
            return pltpu.make_async_copy(rows_v.at[slot], out_hbm.at[idx_v.at[w]], wsem.at[slot])

        _sc_row_pipeline(nwin, read, write)

    return scatter(src, pos)


def _sc_gather_rows(table, pos):
    _, d = table.shape
    nw, nwin, win = pos.shape
    assert nw == SC_WORKERS and nwin % 2 == 0
    per = nwin * win
    mesh = plsc.VectorSubcoreMesh(core_axis_name="c", subcore_axis_name="s")

    @functools.partial(
        pl.kernel, mesh=mesh, out_type=jax.ShapeDtypeStruct((nw * per, d), table.dtype),
        scratch_types=[pltpu.VMEM((nwin, win), jnp.int32), pltpu.VMEM((2, win, d), table.dtype),
                       pltpu.SemaphoreType.DMA((2,)), pltpu.SemaphoreType.DMA((2,))])
    def gather(table_hbm, pos_hbm, out_hbm, idx_v, rows_v, rsem, wsem):
        wid = lax.axis_index("s") * SC_CORES + lax.axis_index("c")
        base = wid * per
        pltpu.sync_copy(pos_hbm.at[wid], idx_v)

        def read(w, slot):
            return pltpu.make_async_copy(table_hbm.at[idx_v.at[w]], rows_v.at[slot], rsem.at[slot])

        def write(w, slot):
            return pltpu.make_async_copy(rows_v.at[slot], out_hbm.at[pl.ds(base + w * win, win)],
                                         wsem.at[slot])

        _sc_row_pipeline(nwin, read, write)

    return gather(table, pos)


MOE_TILE = 256


def _moe_sparse(v_all, route, counts, w1, b1, w2z, b2):
    t_rows, d = v_all.shape
    n_exp = w1.shape[0]
    pairs = TOP_K * t_rows
    n_tiles = pairs // MOE_TILE + n_exp
    expert = route[:, 0:TOP_K].astype(jnp.int32)
    slot = route[:, TOP_K:2 * TOP_K].astype(jnp.int32)
    cnt = counts[0, :n_exp].astype(jnp.int32)
    tiles_per = (cnt + MOE_TILE - 1) // MOE_TILE
    tile_end = jnp.cumsum(tiles_per)
    tile_start = tile_end - tiles_per
    pos = (tile_start * MOE_TILE)[expert] + slot
    nwin = pairs // (SC_WORKERS * SC_WINDOW)
    pos_km = pos.T.reshape(SC_WORKERS, nwin, SC_WINDOW)
    tile_ids = jnp.arange(n_tiles, dtype=jnp.int32)
    used = tile_ids < tile_end[-1]
    te = jnp.searchsorted(tile_end, jnp.minimum(tile_ids, tile_end[-1] - 1), side="right")
    te = jnp.minimum(te, n_exp - 1).astype(jnp.int32)
    rows_left = cnt[te] - (tile_ids - tile_start[te]) * MOE_TILE
    tile_rows = jnp.where(used, jnp.clip(rows_left, 0, MOE_TILE), 0).astype(jnp.int32)
    xs = _sc_scatter_rows(v_all, pos_km, n_tiles * MOE_TILE)
    ys = _moe_rows(xs, te, tile_rows, w1, b1, w2z, b2, MOE_TILE)
    return _sc_gather_rows(ys, pos_km).reshape(TOP_K, t_rows, d)


@functools.lru_cache(maxsize=None)
def _sincos_2d(rows, cols, dim):
    quarter = dim // 4
    omega = 1.0 / (10000.0 ** (np.arange(quarter, dtype=np.float64) / quarter))

    def emb1d(n):
        ang = np.arange(n, dtype=np.float64)[:, None] * omega
        return np.concatenate([np.sin(ang), np.cos(ang)], axis=-1)

    er = np.broadcast_to(emb1d(rows)[:, None], (rows, cols, dim // 2))
    ec = np.broadcast_to(emb1d(cols)[None], (rows, cols, dim // 2))
    return np.concatenate([er, ec], axis=-1).reshape(rows * cols, dim).astype(np.float32)


def _gate_layout(gates, n_ch):
    bsz, length, _ = gates.shape
    g = gates[..., :n_ch].reshape(bsz, length // DN_CHUNK, DN_CHUNK, n_ch)
    return g.transpose(0, 3, 1, 2)


def kernel(x, c, ctx, c_ctx, ada_w, ada_b, ln_g, ln_b, ev_w_in, ev_w_out, dn_conv_w, dn_a_log, dn_dt_bias, dn_norm_g, hy_conv_w, hy_conv_b, hy_w_in, hy_b_in, hy_w_mid, hy_b_mid, hy_w_out, hy_freq, hy_skip, od_w_in, od_b_in, lru_conv_w, lru_conv_b, lru_wa, lru_ba, lru_wx, lru_bx, lru_a_param, od_w_out, od_b_out, router_w, router_b, moe_w1, moe_b1, moe_w2, moe_b2):
    bsz, length, d = x.shape
    ctx_len = ctx.shape[1]
    pos = jnp.asarray(_sincos_2d(length // GRID_W, GRID_W, d))

    cond = jnp.zeros((16, d), F32).at[:bsz].set(c).at[bsz].set(c_ctx)
    mod = _modulation(cond, ada_w, ada_b).reshape(DEPTH, 16, 6, d)

    def lat_mod(layer, k):
        return mod[layer, :bsz, k][:, None, :]

    def ctx_mod(layer, k):
        return jnp.broadcast_to(mod[layer, bsz, k][None, None, :], (bsz, 1, d))

    hc = ctx
    for layer in range(DEPTH):
        last = layer == DEPTH - 1
        j = layer // 2
        if layer % 2 == 0:
            dn_qk = DN_HEADS * DN_DK
            dn_qkv = 3 * dn_qk
            dn_in = dn_qkv + dn_qk + 4 * DN_HEADS
            w_in = ev_w_in[j]
            gate_w = jnp.zeros((d, LANES), F32).at[:, :4 * DN_HEADS].set(w_in[:, dn_qkv + dn_qk:dn_in])
            w_cat = jnp.concatenate([w_in[:, :dn_qkv + dn_qk], gate_w, w_in[:, dn_in:]],
                                    axis=1).astype(BF16)
            hy_in = w_in.shape[1] - dn_in
            widths = (dn_qkv, dn_qk, LANES, hy_in)
            qkv_l, z_l, gt_l, phy_l = _inproj(x, lat_mod(layer, 1), lat_mod(layer, 0), pos,
                                               w_cat, None, widths, 512)
            qkv_c, z_c, gt_c, phy_c = _inproj(hc, ctx_mod(layer, 1), ctx_mod(layer, 0), None,
                                               w_cat, None, widths, 512)
            gates = jnp.concatenate([_gate_layout(gt_c, 4 * DN_HEADS),
                                     _gate_layout(gt_l, 4 * DN_HEADS)], axis=2)
            n_chunks = gates.shape[2]
            gates = jnp.pad(gates, ((0, 0), (0, 0), (0, -n_chunks % 16), (0, 0)))
            dn_l, dn_c = _deltanet(qkv_l, z_l, qkv_c, z_c, gates, dn_conv_w[j], dn_a_log[j],
                                   dn_dt_bias[j], dn_norm_g[j])
            filt = (hy_w_in[j], hy_b_in[j], hy_w_mid[j], hy_b_mid[j], hy_w_out[j], hy_freq[j])
            hy_l = _hyena(phy_l, hy_conv_w[j], hy_conv_b[j], filt, hy_skip[j])
            w_out = ev_w_out[j].astype(BF16)
            half = dn_l.shape[2]
            acts_l, ws, bias = (dn_l, hy_l), (w_out[:half], w_out[half:]), None
            acts_c = None
            if not last:
                hy_c = _hyena(phy_c, hy_conv_w[j], hy_conv_b[j], filt, hy_skip[j])
                acts_c = (dn_c, hy_c)
        else:
            w_in = od_w_in[j].astype(BF16)
            width = w_in.shape[1] // 2
            b_in = od_b_in[j].reshape(1, 2 * width)
            xb_l, yb_l = _inproj(x, lat_mod(layer, 1), lat_mod(layer, 0), pos, w_in, b_in,
                                 (width, width), 512)
            xb_c, _ = _inproj(hc, ctx_mod(layer, 1), ctx_mod(layer, 0), None, w_in, b_in,
                              (width, width), 512)
            act_l = _rglru(xb_l, yb_l, xb_c, lru_conv_w[j], lru_conv_b[j], lru_wa[j], lru_ba[j],
                           lru_wx[j], lru_bx[j], lru_a_param[j])
            acts_l, ws, bias = (act_l,), (od_w_out[j].astype(BF16),), od_b_out[j]
            acts_c = None
            assert last, "context outputs of the RG-LRU layer are only needed before the last layer"

        w1 = moe_w1[layer].astype(BF16)
        b1 = moe_b1[layer][:, None, :]
        w2 = moe_w2[layer].astype(BF16)
        n_exp, dff, _ = w2.shape
        w2z = jnp.stack([w2, jnp.zeros_like(w2)], axis=2).reshape(n_exp, 2 * dff, d)
        b2 = moe_b2[layer][:, None, :]

        x1, v, route, counts = _post_mixer(acts_l, ws, bias, x, lat_mod(layer, 2),
                                           lat_mod(layer, 4), lat_mod(layer, 3), ln_g[layer, 0],
                                           ln_b[layer, 0], router_w[layer], router_b[layer],
                                           jnp.zeros((1, LANES), F32), 256)
        v_all = v.reshape(bsz * length, d)
        route = route.reshape(bsz * length, LANES)
        if not last:
            hc1, vc, route_c, counts = _post_mixer(acts_c, ws, bias, hc, ctx_mod(layer, 2),
                                                   ctx_mod(layer, 4), ctx_mod(layer, 3),
                                                   ln_g[layer, 0], ln_b[layer, 0], router_w[layer],
                                                   router_b[layer], counts, 256)
            v_all = jnp.concatenate([v_all, vc.reshape(bsz * ctx_len, d)], axis=0)
            route = jnp.concatenate([route, route_c.reshape(bsz * ctx_len, LANES)], axis=0)
        yg = _moe_sparse(v_all, route, counts, w1, b1, w2z, b2)
        x = _moe_combine(yg, route, 0, x1, lat_mod(layer, 5), ln_g[layer, 1], ln_b[layer, 1], 256)
        if not last:
            hc = _moe_combine(yg, route, bsz * length, hc1, ctx_mod(layer, 5), ln_g[layer, 1],
                              ln_b[layer, 1], 256)
    return x
```

```python
import functools
import math

import numpy as np
import jax
import jax.numpy as jnp
from jax import lax
from jax.experimental import pallas as pl
from jax.experimental.pallas import tpu as pltpu
from jax.experimental.pallas import tpu_sc as plsc

F32 = jnp.float32
BF16 = jnp.bfloat16

VMEM_LIMIT_BYTES = 56 * 1024 * 1024
LANES = 128

DEPTH = 2
GRID_W = 64
DEEPNORM_ALPHA = (2.0 * DEPTH) ** 0.25
LN_EPS = 1e-5
RMS_EPS = 1e-6

DN_HEADS = 4
DN_DK = 128
DN_CHUNK = 64
DN_CONV = 4

HY_EMB = 33
HY_TARGET = 1e-2
HY_MIN_DECAY = math.log(HY_TARGET) / 1.5
HY_MAX_DECAY = math.log(HY_TARGET) / 0.3
HY_CONV = 3

LRU_HEADS = 4
LRU_C = 8.0
LRU_CONV = 4

N_EXPERTS = 32
TOP_K = 4
SWIGLU_ALPHA = 1.702
SWIGLU_LIMIT = 7.0


def _params(*sem):
    return pltpu.CompilerParams(dimension_semantics=sem, vmem_limit_bytes=VMEM_LIMIT_BYTES)


def _dot(a, b):
    return jnp.dot(a, b, preferred_element_type=F32)


def _dot_nt(a, b):
    return lax.dot_general(a, b, (((1,), (1,)), ((), ())), preferred_element_type=F32)


def _split(a):
    hi = a.astype(BF16)
    lo = (a - hi.astype(F32)).astype(BF16)
    return hi, lo


def _dot3(a, b):
    ah, al = _split(a)
    bh, bl = _split(b)
    return _dot(ah, bh) + _dot(ah, bl) + _dot(al, bh)


def _silu(x):
    return x * (1.0 / (1.0 + jnp.exp(-x)))


def _sigmoid(x):
    return 1.0 / (1.0 + jnp.exp(-x))


def _softplus(x):
    return jnp.maximum(x, 0.0) + jnp.log(1.0 + jnp.exp(-jnp.abs(x)))


def _layer_norm(x, g, b):
    mu = jnp.mean(x, axis=-1, keepdims=True)
    xc = x - mu
    var = jnp.mean(xc * xc, axis=-1, keepdims=True)
    return xc * lax.rsqrt(var + LN_EPS) * g + b


def _shift_rows(x, s):
    if s == 0:
        return x
    n = x.shape[0]
    rows = lax.broadcasted_iota(jnp.int32, x.shape, 0)
    valid = (rows >= s) if s > 0 else (rows < n + s)
    return jnp.where(valid, pltpu.roll(x, s % n, axis=0), 0.0)


def _depthwise_conv(x, w, pad_left):
    acc = None
    for i in range(w.shape[0]):
        term = _shift_rows(x, pad_left - i) * w[i:i + 1, :]
        acc = term if acc is None else acc + term
    return acc


def _mod_kernel(c_ref, w_ref, b_ref, o_ref):
    o_ref[0] = _dot3(_silu(c_ref[...]), w_ref[0]) + b_ref[0]


def _modulation(cond, ada_w, ada_b):
    depth, d, n = ada_w.shape
    rows = cond.shape[0]
    tn = 1536
    return pl.pallas_call(
        _mod_kernel,
        grid=(depth, n // tn),
        in_specs=[
            pl.BlockSpec((rows, d), lambda l, j: (0, 0)),
            pl.BlockSpec((1, d, tn), lambda l, j: (l, 0, j)),
            pl.BlockSpec((1, 1, tn), lambda l, j: (l, 0, j)),
        ],
        out_specs=pl.BlockSpec((1, rows, tn), lambda l, j: (l, 0, j)),
        out_shape=jax.ShapeDtypeStruct((depth, rows, n), F32),
        compiler_params=_params("parallel", "parallel"),
        name="modulation",
    )(cond, ada_w, ada_b.reshape(depth, 1, n))


def _inproj_kernel(*refs, splits, has_pos, has_bias):
    x_ref, sc_ref, sh_ref = refs[:3]
    i = 3
    pos_ref = None
    if has_pos:
        pos_ref = refs[i]
        i += 1
    w_ref = refs[i]
    i += 1
    b_ref = None
    if has_bias:
        b_ref = refs[i]
        i += 1
    o_refs = refs[i:]
    u = x_ref[0] * (1.0 + sc_ref[0]) + sh_ref[0]
    if has_pos:
        u = u + pos_ref[...]
    ub = u.astype(BF16)
    for o_ref, (s, e) in zip(o_refs, splits):
        acc = _dot(ub, w_ref[:, s:e])
        if has_bias:
            acc = acc + b_ref[:, s:e]
        o_ref[0] = acc


def _inproj(x, sc, sh, pos, w, b, widths, tm):
    bsz, length, d = x.shape
    n = w.shape[1]
    splits, s = [], 0
    for wd in widths:
        splits.append((s, s + wd))
        s += wd
    assert s == n
    tm = min(tm, length)
    in_specs = [
        pl.BlockSpec((1, tm, d), lambda bi, i: (bi, i, 0)),
        pl.BlockSpec((1, 1, d), lambda bi, i: (bi, 0, 0)),
        pl.BlockSpec((1, 1, d), lambda bi, i: (bi, 0, 0)),
    ]
    args = [x, sc, sh]
    if pos is not None:
        in_specs.append(pl.BlockSpec((tm, d), lambda bi, i: (i, 0)))
        args.append(pos)
    in_specs.append(pl.BlockSpec((d, n), lambda bi, i: (0, 0)))
    args.append(w)
    if b is not None:
        in_specs.append(pl.BlockSpec((1, n), lambda bi, i: (0, 0)))
        args.append(b)
    return pl.pallas_call(
        functools.partial(_inproj_kernel, splits=tuple(splits), has_pos=pos is not None,
                          has_bias=b is not None),
        grid=(bsz, length // tm),
        in_specs=in_specs,
        out_specs=[pl.BlockSpec((1, tm, wd), lambda bi, i: (bi, i, 0)) for wd in widths],
        out_shape=[jax.ShapeDtypeStruct((bsz, length, wd), F32) for wd in widths],
        compiler_params=_params("parallel", "parallel"),
        name="inproj",
    )(*args)


def _unit_tri_inverses(mats, lower):
    n = mats[0].shape[0]
    nb = 16
    np_ = len(mats)
    ii = lax.broadcasted_iota(jnp.int32, (n, n), 0)
    jj = lax.broadcasted_iota(jnp.int32, (n, n), 1)
    same16 = (ii // nb) == (jj // nb)
    same32 = (ii // (2 * nb)) == (jj // (2 * nb))
    dgs = []
    for a in mats:
        ad = jnp.where(same16, a, 0.0)
        dgs.append(ad[0:nb] + ad[nb:2 * nb] + ad[2 * nb:3 * nb] + ad[3 * nb:4 * nb])
    dg = jnp.concatenate(dgs, axis=0)
    rr = lax.broadcasted_iota(jnp.int32, dg.shape, 0)
    ll = lax.broadcasted_iota(jnp.int32, dg.shape, 1)
    xd = jnp.where(rr % nb == ll % nb, 1.0, 0.0)
    blk0 = (ll // nb) * nb
    for s in (range(nb - 1) if lower else range(nb - 1, 0, -1)):
        col = jnp.take_along_axis(dg, blk0 + s, axis=1)
        row = jnp.concatenate(
            [jnp.broadcast_to(xd[p * nb + s:p * nb + s + 1, :], (nb, n)) for p in range(np_)], axis=0)
        xd = xd - col * row
    ds = [jnp.where(same16, jnp.concatenate([xd[p * nb:(p + 1) * nb]] * (n // nb), axis=0), 0.0)
          for p in range(np_)]
    lvl1 = same32 & jnp.logical_not(same16)
    t1 = [_dot3(d, jnp.where(lvl1, a, 0.0)) for d, a in zip(ds, mats)]
    x1 = [d - _dot3(t, d) for d, t in zip(ds, t1)]
    t2 = [_dot3(x, jnp.where(same32, 0.0, a)) for x, a in zip(x1, mats)]
    return [x - _dot3(t, x) for x, t in zip(x1, t2)]


def _dn_kernel(alog_ref, dtb_ref,
               ql_ref, kl_ref, vl_ref, zl_ref, qc_ref, kc_ref, vc_ref, zc_ref, gt_ref,
               cwq_ref, cwk_ref, cwv_ref, ng_ref,
               yl_ref, yc_ref,
               qn_s, kn_s, vn_s, gc_s, bt_s, wq_s, u_s, at_s, kdt_s, gl_s, o_s,
               *, ctx_len, lat_len):
    h = pl.program_id(1)
    csz = DN_CHUNK
    nc_ctx = ctx_len // csz
    nc = (ctx_len + lat_len) // csz

    def prep(src_ref, cw_ref, kind):
        t = _silu(_depthwise_conv(src_ref[0], cw_ref[...], DN_CONV // 2))
        if kind == "v":
            return t
        t = t * lax.rsqrt(jnp.sum(t * t, axis=-1, keepdims=True) + RMS_EPS)
        return t * (DN_DK ** -0.5) if kind == "q" else t

    qn_s[0:ctx_len, :] = prep(qc_ref, cwq_ref, "q")
    qn_s[ctx_len:, :] = prep(ql_ref, cwq_ref, "q")
    kn_s[0:ctx_len, :] = prep(kc_ref, cwk_ref, "k")
    kn_s[ctx_len:, :] = prep(kl_ref, cwk_ref, "k")
    vn_s[0:ctx_len, :] = prep(vc_ref, cwv_ref, "v")
    vn_s[ctx_len:, :] = prep(vl_ref, cwv_ref, "v")

    ii = lax.broadcasted_iota(jnp.int32, (csz, csz), 0)
    jj = lax.broadcasted_iota(jnp.int32, (csz, csz), 1)
    eye = ii == jj
    for d in range(2):
        graw = gt_ref[0, d * 2 * DN_HEADS + h]
        braw = gt_ref[0, d * 2 * DN_HEADS + DN_HEADS + h]
        a_neg = -jnp.exp(jnp.zeros_like(graw) + alog_ref[d, h])
        g = a_neg * _softplus(graw + dtb_ref[d, h])
        tri = jnp.where((ii <= jj) if d == 0 else (ii >= jj), 1.0, 0.0).astype(BF16)
        g1 = g.astype(BF16)
        r1 = g - g1.astype(F32)
        g2 = r1.astype(BF16)
        g3 = (r1 - g2.astype(F32)).astype(BF16)
        gc_s[d] = _dot(g1, tri) + _dot(g2, tri) + _dot(g3, tri)
        bt_s[d] = _sigmoid(braw)

    o_s[...] = jnp.zeros_like(o_s)

    group = 6
    assert nc % group == 0

    def chunk_prep(gi, carry):
        ns = [gi * group + c for c in range(group)]
        r0s = [pl.multiple_of(n * csz, csz) for n in ns]
        qs = [qn_s[pl.ds(r0, csz), :] for r0 in r0s]
        ks = [kn_s[pl.ds(r0, csz), :] for r0 in r0s]
        vs = [vn_s[pl.ds(r0, csz), :] for r0 in r0s]
        kbfs = [k.astype(BF16) for k in ks]
        qks = [_dot_nt(q.astype(BF16), kbf) for q, kbf in zip(qs, kbfs)]
        for d in range(2):
            incl = (ii >= jj) if d == 0 else (ii <= jj)
            strict = (ii > jj) if d == 0 else (ii < jj)
            grs = [gc_s[d, pl.ds(n, 1), :] for n in ns]
            grows = [jnp.broadcast_to(gr, (csz, csz)) for gr in grs]
            gcols = [jnp.sum(jnp.where(eye, grow, 0.0), axis=1, keepdims=True) for grow in grows]
            bcols = [jnp.sum(jnp.where(eye, jnp.broadcast_to(bt_s[d, pl.ds(n, 1), :], (csz, csz)),
                                       0.0), axis=1, keepdims=True) for n in ns]
            decays = [jnp.where(incl, jnp.exp(jnp.where(incl, gcol - grow, 0.0)), 0.0)
                      for gcol, grow in zip(gcols, grows)]
            kbs = [k * bcol for k, bcol in zip(ks, bcols)]
            amats = [jnp.where(strict, _dot_nt(kb.astype(BF16), kbf) * decay, 0.0)
                     for kb, kbf, decay in zip(kbs, kbfs, decays)]
            tbs = [t.astype(BF16) for t in _unit_tri_inverses(amats, lower=(d == 0))]
            for c in range(group):
                n, r0, gr, gcol = ns[c], r0s[c], grs[c], gcols[c]
                eg = jnp.exp(gcol)
                u = _dot(tbs[c], (vs[c] * bcols[c]).astype(BF16))
                w = _dot(tbs[c], (kbs[c] * eg).astype(BF16))
                attn = jnp.where(incl, qks[c] * decays[c], 0.0)
                glast = gr[:, csz - 1:csz] if d == 0 else gr[:, 0:1]
                kd = ks[c] * jnp.exp(glast - gcol)
                wq_s[d, n, 0:csz, :] = w.astype(BF16)
                wq_s[d, n, csz:2 * csz, :] = (qs[c] * eg).astype(BF16)
                u_s[d, pl.ds(r0, csz), :] = u
                at_s[d, n] = attn.astype(BF16)
                kdt_s[d, n] = kd.T.astype(BF16)
                gl_s[d, pl.ds(n, 1), :] = jnp.broadcast_to(jnp.exp(glast), (1, LANES))
        return carry

    lax.fori_loop(0, nc // group, chunk_prep, 0)

    def step(i, states):
        new_states = []
        for d in range(2):
            if d == 0:
                n = i
            else:
                n = jnp.where(i < nc_ctx, nc_ctx - 1 - i, nc + nc_ctx - 1 - i)
            r0 = pl.multiple_of(n * csz, csz)
            s = states[d]
            r = _dot(wq_s[d, n], s.astype(BF16))
            v_new = (u_s[d, pl.ds(r0, csz), :] - r[0:csz]).astype(BF16)
            o = r[csz:2 * csz] + _dot(at_s[d, n], v_new)
            o_s[pl.ds(r0, csz), :] += o
            s = s * gl_s[d, pl.ds(n, 1), :] + _dot(kdt_s[d, n], v_new)
            new_states.append(s)
        return tuple(new_states)

    zero = jnp.zeros((DN_DK, DN_DK), F32)
    lax.fori_loop(0, nc, step, (zero, zero))

    def gated_norm(o, z):
        o = o * lax.rsqrt(jnp.mean(o * o, axis=-1, keepdims=True) + RMS_EPS) * ng_ref[...]
        return (o * _silu(z)).astype(yl_ref.dtype)

    yc_ref[0] = gated_norm(o_s[0:ctx_len, :], zc_ref[0])
    yl_ref[0] = gated_norm(o_s[ctx_len:, :], zl_ref[0])


def _deltanet(qkv_l, z_l, qkv_c, z_c, gates, conv_w, a_log, dt_bias, norm_g):
    bsz, lat_len, _ = qkv_l.shape
    ctx_len = qkv_c.shape[1]
    tot = ctx_len + lat_len
    nc = tot // DN_CHUNK
    ncp = gates.shape[2]
    hd = DN_DK
    nh = DN_HEADS

    def col(off):
        return lambda b, h: (b, 0, off + h)

    def wcol(off):
        return lambda b, h: (0, off + h)

    smem = pl.BlockSpec(memory_space=pltpu.SMEM)
    in_specs = [
        smem, smem,
        pl.BlockSpec((1, lat_len, hd), col(0)),
        pl.BlockSpec((1, lat_len, hd), col(nh)),
        pl.BlockSpec((1, lat_len, hd), col(2 * nh)),
        pl.BlockSpec((1, lat_len, hd), col(0)),
        pl.BlockSpec((1, ctx_len, hd), col(0)),
        pl.BlockSpec((1, ctx_len, hd), col(nh)),
        pl.BlockSpec((1, ctx_len, hd), col(2 * nh)),
        pl.BlockSpec((1, ctx_len, hd), col(0)),
        pl.BlockSpec((1, 4 * nh, ncp, DN_CHUNK), lambda b, h: (b, 0, 0, 0)),
        pl.BlockSpec((DN_CONV, hd), wcol(0)),
        pl.BlockSpec((DN_CONV, hd), wcol(nh)),
        pl.BlockSpec((DN_CONV, hd), wcol(2 * nh)),
        pl.BlockSpec((1, hd), lambda b, h: (0, 0)),
    ]
    scratch = [
        pltpu.VMEM((tot, hd), F32), pltpu.VMEM((tot, hd), F32), pltpu.VMEM((tot, hd), F32),
        pltpu.VMEM((2, ncp, DN_CHUNK), F32), pltpu.VMEM((2, ncp, DN_CHUNK), F32),
        pltpu.VMEM((2, nc, 2 * DN_CHUNK, hd), BF16),
        pltpu.VMEM((2, tot, hd), F32),
        pltpu.VMEM((2, nc, DN_CHUNK, DN_CHUNK), BF16),
        pltpu.VMEM((2, nc, hd, DN_CHUNK), BF16),
        pltpu.VMEM((2, nc, LANES), F32),
        pltpu.VMEM((tot, hd), F32),
    ]
    return pl.pallas_call(
        functools.partial(_dn_kernel, ctx_len=ctx_len, lat_len=lat_len),
        grid=(bsz, nh),
        in_specs=in_specs,
        out_specs=[pl.BlockSpec((1, lat_len, hd), col(0)),
                   pl.BlockSpec((1, ctx_len, hd), col(0))],
        out_shape=[jax.ShapeDtypeStruct((bsz, lat_len, nh * hd), BF16),
                   jax.ShapeDtypeStruct((bsz, ctx_len, nh * hd), BF16)],
        scratch_shapes=scratch,
        compiler_params=_params("parallel", "parallel"),
        name="deltanet",
    )(a_log, dt_bias, qkv_l, qkv_l, qkv_l, z_l, qkv_c, qkv_c, qkv_c, z_c, gates,
      conv_w, conv_w, conv_w, norm_g.reshape(1, hd))


def _hy_filter_kernel(z_ref, win_ref, bin_ref, wmid_ref, bmid_ref, wout_ref, freq_ref, dec_ref,
                      o_ref):
    freq = freq_ref[...]
    hcur = jnp.sin(freq * (_dot3(z_ref[...], win_ref[...]) + bin_ref[...]))
    for i in range(wmid_ref.shape[0]):
        hcur = jnp.sin(freq * (_dot3(hcur, wmid_ref[i]) + bmid_ref[i]))
    o_ref[...] = _dot3(hcur, wout_ref[...]) * dec_ref[...]


def _hy_filter(z, w_in, b_in, w_mid, b_mid, w_out, freq, dec2):
    length = z.shape[0]
    n_out = w_out.shape[1]
    tl = min(256, length)

    def whole(a):
        return pl.BlockSpec(a.shape, lambda i: (0,) * a.ndim)

    return pl.pallas_call(
        _hy_filter_kernel,
        grid=(length // tl,),
        in_specs=[pl.BlockSpec((tl, z.shape[1]), lambda i: (i, 0)),
                  whole(w_in), whole(b_in), whole(w_mid), whole(b_mid), whole(w_out), whole(freq),
                  pl.BlockSpec((tl, n_out), lambda i: (i, 0))],
        out_specs=pl.BlockSpec((tl, n_out), lambda i: (i, 0)),
        out_shape=jax.ShapeDtypeStruct((length, n_out), F32),
        compiler_params=_params("parallel"),
        name="hyena_filter",
    )(z, w_in, b_in, w_mid, b_mid, w_out, freq, dec2)


def _filt_spec_kernel(fc_ref, fs_ref, h_ref, kc_ref, ks_ref, *, width):
    j = pl.program_id(0)
    hmat = h_ref[...]
    rows = lax.broadcasted_iota(jnp.int32, hmat.shape, 0)
    cols = lax.broadcasted_iota(jnp.int32, hmat.shape, 1)
    hmat = jnp.where((rows == 0) & (cols >= width), 0.0, hmat)
    hh, hl = _split(hmat)
    c = _dot(fc_ref[...], hh) + _dot(fc_ref[...], hl)
    s = _dot(fs_ref[...], hh) + _dot(fs_ref[...], hl)
    kc_ref[...] = c[:, :width] + c[:, width:]
    orow = lax.broadcasted_iota(jnp.int32, (c.shape[0], width), 0)
    sign = jnp.where((orow == 0) & (j == 0), 1.0, -1.0)
    ks_ref[...] = s[:, :width] + sign * s[:, width:]


def _filt_spec(fwd, hfilt, tf):
    length, two_w = hfilt.shape
    width = two_w // 2
    nt = length // tf
    return pl.pallas_call(
        functools.partial(_filt_spec_kernel, width=width),
        grid=(nt,),
        in_specs=[pl.BlockSpec((tf, length), lambda j: (j, 0)),
                  pl.BlockSpec((tf, length), lambda j: (nt + j, 0)),
                  pl.BlockSpec((length, two_w), lambda j: (0, 0))],
        out_specs=[pl.BlockSpec((tf, width), lambda j: (j, 0)),
                   pl.BlockSpec((tf, width), lambda j: (j, 0))],
        out_shape=[jax.ShapeDtypeStruct((length, width), F32)] * 2,
        compiler_params=_params("arbitrary"),
        name="hyena_filter_spectrum",
    )(fwd, fwd, hfilt)


def _hy_prep_kernel(x0_ref, x1_ref, v_ref, w0_ref, w1_ref, w2_ref, b0_ref, b1_ref, b2_ref,
                    x0o_ref, vvo_ref):
    x0 = _depthwise_conv(x0_ref[0], w0_ref[...], HY_CONV // 2) + b0_ref[...]
    x1 = _depthwise_conv(x1_ref[0], w1_ref[...], HY_CONV // 2) + b1_ref[...]
    v = _depthwise_conv(v_ref[0], w2_ref[...], HY_CONV // 2) + b2_ref[...]
    x0o_ref[0] = x0
    vvo_ref[0] = v * x1


def _hy_prep(p_hy, conv_w, conv_b):
    bsz, length, three_w = p_hy.shape
    width = three_w // 3
    nb = width // LANES

    def col(off):
        return lambda b, j: (b, 0, off + j)

    def wcol(off):
        return lambda b, j: (0, off + j)

    k = conv_w.shape[0]
    return pl.pallas_call(
        _hy_prep_kernel,
        grid=(bsz, nb),
        in_specs=[pl.BlockSpec((1, length, LANES), col(0)),
                  pl.BlockSpec((1, length, LANES), col(nb)),
                  pl.BlockSpec((1, length, LANES), col(2 * nb)),
                  pl.BlockSpec((k, LANES), wcol(0)),
                  pl.BlockSpec((k, LANES), wcol(nb)),
                  pl.BlockSpec((k, LANES), wcol(2 * nb)),
                  pl.BlockSpec((1, LANES), wcol(0)),
                  pl.BlockSpec((1, LANES), wcol(nb)),
                  pl.BlockSpec((1, LANES), wcol(2 * nb))],
        out_specs=[pl.BlockSpec((1, length, LANES), col(0)),
                   pl.BlockSpec((1, length, LANES), col(0))],
        out_shape=[jax.ShapeDtypeStruct((bsz, length, width), F32)] * 2,
        compiler_params=_params("parallel", "parallel"),
        name="hyena_prep",
    )(p_hy, p_hy, p_hy, conv_w, conv_w, conv_w, conv_b, conv_b, conv_b)


def _dft_fwd_kernel(fc_ref, fs_ref, v_ref, kc_ref, ks_ref, yc_ref, ys_ref, vb_s):
    j = pl.program_id(1)

    @pl.when(j == 0)
    def _():
        vb_s[...] = v_ref[0].astype(BF16)

    uc = _dot(fc_ref[...], vb_s[...])
    us = _dot(fs_ref[...], vb_s[...])
    kc = kc_ref[...]
    ks = ks_ref[...]
    rows = lax.broadcasted_iota(jnp.int32, uc.shape, 0)
    special = (rows == 0) & (j == 0)
    yc_ref[0] = (uc * kc - jnp.where(special, 0.0, us * ks)).astype(BF16)
    ys_ref[0] = jnp.where(special, us * ks, uc * ks + us * kc).astype(BF16)


def _dft_fwd(fwd, vv, kc, ks, tf):
    bsz, length, width = vv.shape
    nt = length // tf
    return pl.pallas_call(
        _dft_fwd_kernel,
        grid=(bsz, nt),
        in_specs=[pl.BlockSpec((tf, length), lambda b, j: (j, 0)),
                  pl.BlockSpec((tf, length), lambda b, j: (nt + j, 0)),
                  pl.BlockSpec((1, length, width), lambda b, j: (b, 0, 0)),
                  pl.BlockSpec((tf, width), lambda b, j: (j, 0)),
                  pl.BlockSpec((tf, width), lambda b, j: (j, 0))],
        out_specs=[pl.BlockSpec((1, tf, width), lambda b, j: (b, j, 0)),
                   pl.BlockSpec((1, tf, width), lambda b, j: (b, j, 0))],
        out_shape=[jax.ShapeDtypeStruct((bsz, length, width), BF16)] * 2,
        scratch_shapes=[pltpu.VMEM((length, width), BF16)],
        compiler_params=_params("parallel", "arbitrary"),
        name="hyena_dft_fwd",
    )(fwd, fwd, vv, kc, ks)


def _dft_inv_kernel(ic_ref, is_ref, yc_ref, ys_ref, vv_ref, x0_ref, skip_ref, o_ref):
    y = _dot(ic_ref[...], yc_ref[0]) + _dot(is_ref[...], ys_ref[0])
    o_ref[0] = (x0_ref[0] * (y + vv_ref[0] * skip_ref[...])).astype(o_ref.dtype)


def _dft_inv(inv, yc, ys, vv, x0, skip, tt):
    bsz, length, width = vv.shape
    nt = length // tt
    return pl.pallas_call(
        _dft_inv_kernel,
        grid=(bsz, nt),
        in_specs=[pl.BlockSpec((tt, length), lambda b, i: (i, 0)),
                  pl.BlockSpec((tt, length), lambda b, i: (i, 1)),
                  pl.BlockSpec((1, length, width), lambda b, i: (b, 0, 0)),
                  pl.BlockSpec((1, length, width), lambda b, i: (b, 0, 0)),
                  pl.BlockSpec((1, tt, width), lambda b, i: (b, i, 0)),
                  pl.BlockSpec((1, tt, width), lambda b, i: (b, i, 0)),
                  pl.BlockSpec((1, width), lambda b, i: (0, 0))],
        out_specs=pl.BlockSpec((1, tt, width), lambda b, i: (b, i, 0)),
        out_shape=jax.ShapeDtypeStruct((bsz, length, width), BF16),
        compiler_params=_params("parallel", "parallel"),
        name="hyena_dft_inv",
    )(inv, inv, yc, ys, vv, x0, skip)


@functools.lru_cache(maxsize=None)
def _hyena_tables(length):
    n2 = 2 * length
    t = np.linspace(0.0, 1.0, length)[:, None]
    bands = (HY_EMB - 1) // 2
    wpos = 2.0 * np.pi * np.arange(length)[:, None] / length
    fb = np.linspace(1e-4, bands - 1, bands)[None]
    z = np.concatenate([t, np.cos(fb * wpos), -np.sin(fb * wpos)], axis=-1)
    zpad = np.zeros((length, LANES))
    zpad[:, :HY_EMB] = z
    f = np.arange(length)[:, None]
    n = np.arange(length)[None, :]
    ang = 2.0 * np.pi * ((f * n) % n2) / n2
    cos_m = np.cos(ang)
    sin_m = np.sin(ang)
    sin_m[0, :] = np.cos(np.pi * np.arange(length))
    fwd = np.concatenate([cos_m, sin_m], axis=0)
    scale = np.full((1, n2), 2.0 / n2)
    scale[0, 0] = 1.0 / n2
    scale[0, length] = 1.0 / n2
    inv = fwd.T * scale
    return (zpad.astype(np.float32), t.astype(np.float32), fwd.astype(np.float32),
            inv.astype(np.float32))


def _hyena(p_hy, conv_w, conv_b, filt, skip):
    bsz, length, three_w = p_hy.shape
    width = three_w // 3
    w_in, b_in, w_mid, b_mid, w_out, freq = filt
    zpad, t, fwd, inv = _hyena_tables(length)
    deltas = np.abs(np.linspace(HY_MIN_DECAY, HY_MAX_DECAY, width))[None, :]
    dec = np.exp(-t.astype(np.float64) * deltas).astype(np.float32)
    dec2 = jnp.asarray(np.concatenate([dec, dec], axis=1))
    ffn = w_in.shape[1]
    w_in_pad = jnp.zeros((LANES, ffn), F32).at[:HY_EMB].set(w_in)
    hfilt = _hy_filter(jnp.asarray(zpad), w_in_pad, b_in.reshape(1, ffn), w_mid,
                       b_mid.reshape(-1, 1, ffn), w_out, freq.reshape(1, ffn), dec2)
    fwd_b = jnp.asarray(fwd).astype(BF16)
    inv_b = jnp.asarray(inv).astype(BF16)
    tf = min(256, length)
    kc, ks = _filt_spec(fwd_b, hfilt, tf)
    x0, vv = _hy_prep(p_hy, conv_w, conv_b.reshape(1, three_w))
    yc, ys = _dft_fwd(fwd_b, vv, kc, ks, tf)
    return _dft_inv(inv_b, yc, ys, vv, x0, skip.reshape(1, width), min(256, length))


def _gelu_tanh(x):
    return 0.5 * x * (1.0 + jnp.tanh(math.sqrt(2.0 / math.pi) * (x + 0.044715 * x * x * x)))


def _lru_kernel(xl_ref, yl_ref, xc_ref, cw_ref, cb_ref, wa_ref, ba_ref, wx_ref, bx_ref, ap_ref,
                o_ref, xs_s, a_s, b_s, h_s, *, ctx_len, lat_len):
    tot = ctx_len + lat_len
    ngrp = tot // 8
    ngrp_ctx = ctx_len // 8
    xs_s[0:ctx_len, :] = _depthwise_conv(xc_ref[0], cw_ref[...], LRU_CONV // 2) + cb_ref[...]
    xs_s[ctx_len:, :] = _depthwise_conv(xl_ref[0], cw_ref[...], LRU_CONV // 2) + cb_ref[...]
    xs = xs_s[...]
    xsb = xs.astype(BF16)
    rows8 = lax.broadcasted_iota(jnp.int32, xs.shape, 0) % 8
    for d in range(2):
        r = _sigmoid(_dot(xsb, wa_ref[d, 0].astype(BF16)) + ba_ref[d])
        gi = _sigmoid(_dot(xsb, wx_ref[d, 0].astype(BF16)) + bx_ref[d])
        log_a = -LRU_C * r * _softplus(ap_ref[d])
        a = jnp.exp(log_a)
        b = jnp.sqrt(1.0 - a * a) * (gi * xs)
        for s in (1, 2, 4):
            if d == 0:
                keep = rows8 >= s
                sa = jnp.where(keep, pltpu.roll(a, s, axis=0), 1.0)
                sb = jnp.where(keep, pltpu.roll(b, s, axis=0), 0.0)
            else:
                keep = rows8 < 8 - s
                sa = jnp.where(keep, pltpu.roll(a, tot - s, axis=0), 1.0)
                sb = jnp.where(keep, pltpu.roll(b, tot - s, axis=0), 0.0)
            b = a * sb + b
            a = a * sa
        a_s[d] = a
        b_s[d] = b

    def group_fwd(i, c):
        r0 = pl.multiple_of(i * 8, 8)
        hg = a_s[0, pl.ds(r0, 8), :] * c + b_s[0, pl.ds(r0, 8), :]
        h_s[pl.ds(r0, 8), :] = hg
        return jnp.broadcast_to(hg[7:8, :], hg.shape)

    def group_bwd(i, c):
        gidx = jnp.where(i < ngrp_ctx, ngrp_ctx - 1 - i, ngrp + ngrp_ctx - 1 - i)
        r0 = pl.multiple_of(gidx * 8, 8)
        hg = a_s[1, pl.ds(r0, 8), :] * c + b_s[1, pl.ds(r0, 8), :]
        h_s[pl.ds(r0, 8), :] += hg
        return jnp.broadcast_to(hg[0:1, :], hg.shape)

    zero = jnp.zeros((8, xs.shape[1]), F32)
    lax.fori_loop(0, ngrp, group_fwd, zero, unroll=4)
    lax.fori_loop(0, ngrp, group_bwd, zero, unroll=4)
    o_ref[0] = (h_s[ctx_len:, :] * _gelu_tanh(yl_ref[0])).astype(o_ref.dtype)


def _rglru(xb_l, yb_l, xb_c, conv_w, conv_b, wa, ba, wx, bx, a_param):
    bsz, lat_len, width = xb_l.shape
    ctx_len = xb_c.shape[1]
    tot = ctx_len + lat_len
    blk = width // LRU_HEADS

    def col(b, h):
        return (b, 0, h)

    def wcol(b, h):
        return (0, h)

    def w3(b, h):
        return (0, 0, h)

    return pl.pallas_call(
        functools.partial(_lru_kernel, ctx_len=ctx_len, lat_len=lat_len),
        grid=(bsz, LRU_HEADS),
        in_specs=[pl.BlockSpec((1, lat_len, blk), col),
                  pl.BlockSpec((1, lat_len, blk), col),
                  pl.BlockSpec((1, ctx_len, blk), col),
                  pl.BlockSpec((LRU_CONV, blk), wcol),
                  pl.BlockSpec((1, blk), wcol),
                  pl.BlockSpec((2, 1, blk, blk), lambda b, h: (0, h, 0, 0)),
                  pl.BlockSpec((2, 1, blk), w3),
                  pl.BlockSpec((2, 1, blk, blk), lambda b, h: (0, h, 0, 0)),
                  pl.BlockSpec((2, 1, blk), w3),
                  pl.BlockSpec((2, 1, blk), w3)],
        out_specs=pl.BlockSpec((1, lat_len, blk), col),
        out_shape=jax.ShapeDtypeStruct((bsz, lat_len, width), BF16),
        scratch_shapes=[pltpu.VMEM((tot, blk), F32),
                        pltpu.VMEM((2, tot, blk), F32),
                        pltpu.VMEM((2, tot, blk), F32),
                        pltpu.VMEM((tot, blk), F32)],
        compiler_params=_params("parallel", "parallel"),
        name="rglru",
    )(xb_l, yb_l, xb_c, conv_w, conv_b.reshape(1, width), wa, ba.reshape(2, 1, width), wx,
      bx.reshape(2, 1, width), a_param.reshape(2, 1, width))


def _post_mixer_kernel(*refs, n_in, has_bias):
    a_refs = refs[:n_in]
    w_refs = refs[n_in:2 * n_in]
    i = 2 * n_in
    b_ref = None
    if has_bias:
        b_ref = refs[i]
        i += 1
    x_ref, g1_ref, sc_ref, sh_ref, lng_ref, lnb_ref, rw_ref, rb_ref, base_ref = refs[i:i + 9]
    x1_ref, v_ref, route_ref, cnt_ref = refs[i + 9:]

    @pl.when((pl.program_id(0) == 0) & (pl.program_id(1) == 0))
    def _():
        cnt_ref[...] = base_ref[...]

    y = None
    for a_ref, w_ref in zip(a_refs, w_refs):
        t = _dot(a_ref[0], w_ref[...])
        y = t if y is None else y + t
    if has_bias:
        y = y + b_ref[...]
    x1 = _layer_norm(DEEPNORM_ALPHA * x_ref[0] + g1_ref[0] * y, lng_ref[...], lnb_ref[...])
    x1_ref[0] = x1
    v = x1 * (1.0 + sc_ref[0]) + sh_ref[0]
    v_ref[0] = v
    logits = _dot3(v, rw_ref[...]) + rb_ref[...]
    tm = logits.shape[0]
    lane = lax.broadcasted_iota(jnp.int32, logits.shape, 1).astype(F32)
    work = logits
    picks, firsts = [], []
    m0 = None
    for kk in range(TOP_K):
        m = jnp.max(work, axis=-1, keepdims=True)
        if kk == 0:
            m0 = m
        first = jnp.min(jnp.where(work == m, lane, float(LANES)), axis=-1, keepdims=True)
        pick = lane == first
        picks.append(pick)
        firsts.append(first)
        work = jnp.where(pick, -jnp.inf, work)
    sel = jnp.where(picks[0] | picks[1] | picks[2] | picks[3], 1.0, 0.0)
    e = sel * jnp.exp(logits - m0)
    gate = e / jnp.sum(e, axis=-1, keepdims=True)
    ti = lax.broadcasted_iota(jnp.int32, (tm, tm), 0)
    tj = lax.broadcasted_iota(jnp.int32, (tm, tm), 1)
    before = jnp.where(ti > tj, 1.0, 0.0).astype(BF16)
    slot = _dot(before, sel.astype(BF16)) + cnt_ref[...]
    route = jnp.zeros(logits.shape, F32)
    for kk in range(TOP_K):
        rank = jnp.sum(jnp.where(picks[kk], slot, 0.0), axis=-1, keepdims=True)
        wgt = jnp.sum(jnp.where(picks[kk], gate, 0.0), axis=-1, keepdims=True)
        route = jnp.where(lane == float(kk), firsts[kk], route)
        route = jnp.where(lane == float(TOP_K + kk), rank, route)
        route = jnp.where(lane == float(2 * TOP_K + kk), wgt, route)
    route_ref[0] = route
    cnt_ref[...] += jnp.sum(sel, axis=0, keepdims=True)


def _post_mixer(acts, ws, bias, x, g1, sc2, sh2, ln_g, ln_b, router_w, router_b, base, tm):
    bsz, length, d = x.shape
    tm = min(tm, length)
    n_in = len(acts)

    def row(bi, i):
        return (bi, i, 0)

    def per_b(bi, i):
        return (bi, 0, 0)

    def const(bi, i):
        return (0, 0)

    in_specs = [pl.BlockSpec((1, tm, a.shape[2]), row) for a in acts]
    in_specs += [pl.BlockSpec(w.shape, const) for w in ws]
    args = list(acts) + list(ws)
    if bias is not None:
        in_specs.append(pl.BlockSpec((1, d), const))
        args.append(bias.reshape(1, d))
    in_specs += [pl.BlockSpec((1, tm, d), row),
                 pl.BlockSpec((1, 1, d), per_b), pl.BlockSpec((1, 1, d), per_b),
                 pl.BlockSpec((1, 1, d), per_b),
                 pl.BlockSpec((1, d), const), pl.BlockSpec((1, d), const),
                 pl.BlockSpec((d, LANES), const), pl.BlockSpec((1, LANES), const),
                 pl.BlockSpec((1, LANES), const)]
    rw = jnp.zeros((d, LANES), F32).at[:, :N_EXPERTS].set(router_w)
    rb = jnp.full((1, LANES), -1e30, F32).at[0, :N_EXPERTS].set(router_b)
    args += [x, g1, sc2, sh2, ln_g.reshape(1, d), ln_b.reshape(1, d), rw, rb, base]
    return pl.pallas_call(
        functools.partial(_post_mixer_kernel, n_in=n_in, has_bias=bias is not None),
        grid=(bsz, length // tm),
        in_specs=in_specs,
        out_specs=[pl.BlockSpec((1, tm, d), row), pl.BlockSpec((1, tm, d), row),
                   pl.BlockSpec((1, tm, LANES), row), pl.BlockSpec((1, LANES), const)],
        out_shape=[jax.ShapeDtypeStruct((bsz, length, d), F32),
                   jax.ShapeDtypeStruct((bsz, length, d), F32),
                   jax.ShapeDtypeStruct((bsz, length, LANES), F32),
                   jax.ShapeDtypeStruct((1, LANES), F32)],
        compiler_params=_params("arbitrary", "arbitrary"),
        name="post_mixer",
    )(*args)


def _swiglu_interleaved(gu):
    nxt = pltpu.roll(gu, gu.shape[1] - 1, axis=1)
    glu = jnp.minimum(gu, SWIGLU_LIMIT)
    lin = jnp.clip(nxt, -SWIGLU_LIMIT, SWIGLU_LIMIT)
    act = glu * _sigmoid(SWIGLU_ALPHA * glu) * (lin + 1.0)
    lane = lax.broadcasted_iota(jnp.int32, gu.shape, 1)
    return jnp.where(lane % 2 == 0, act, 0.0)


def _moe_rows_kernel(te_ref, nv_ref, first_ref, xs_ref, w1_ref, b1_ref, w2_ref, b2_ref, ys_ref,
                     w1b_s, w2z_s):
    del te_ref
    i = pl.program_id(0)
    nv = nv_ref[i]

    @pl.when(first_ref[i] == 1)
    def _():
        w1b_s[...] = w1_ref[0].astype(BF16)
        bits = pltpu.bitcast(w2_ref[0].astype(BF16).astype(F32), jnp.uint32) >> 16
        w2z_s[...] = pltpu.bitcast(bits, BF16)

    @pl.when(nv > 0)
    def _():
        rows = lax.broadcasted_iota(jnp.int32, xs_ref.shape, 0)
        x = jnp.where(rows < nv, xs_ref[...], 0.0).astype(BF16)
        act = _swiglu_interleaved(_dot(x, w1b_s[...]) + b1_ref[0])
        ys_ref[...] = _dot(act.astype(BF16), w2z_s[...]) + b2_ref[0]

    @pl.when(nv == 0)
    def _():
        ys_ref[...] = jnp.zeros_like(ys_ref)


def _moe_rows(xs, tile_expert, tile_rows, tile_first, w1, b1, w2, b2, tm):
    n_rows, d = xs.shape
    _, _, dff2 = w1.shape

    def row(i, te, nv, first):
        return (i, 0)

    def exp3(i, te, nv, first):
        return (te[i], 0, 0)

    return pl.pallas_call(
        _moe_rows_kernel,
        grid_spec=pltpu.PrefetchScalarGridSpec(
            num_scalar_prefetch=3,
            grid=(n_rows // tm,),
            in_specs=[pl.BlockSpec((tm, d), row),
                      pl.BlockSpec((1, d, dff2), exp3), pl.BlockSpec((1, 1, dff2), exp3),
                      pl.BlockSpec((1, dff2 // 2, d), exp3), pl.BlockSpec((1, 1, d), exp3)],
            out_specs=pl.BlockSpec((tm, d), row),
            scratch_shapes=[pltpu.VMEM((d, dff2), BF16), pltpu.VMEM((dff2, d), BF16)]),
        out_shape=jax.ShapeDtypeStruct((n_rows, d), F32),
        compiler_params=_params("arbitrary"),
        name="moe_rows",
    )(tile_expert, tile_rows, tile_first, xs, w1, b1, w2, b2)


def _moe_combine_kernel(y0_ref, y1_ref, y2_ref, y3_ref, route_ref, x1_ref, g2_ref, lng_ref, lnb_ref,
                        o_ref):
    route = route_ref[...]
    lane = lax.broadcasted_iota(jnp.int32, route.shape, 1)
    f = None
    for kk, y_ref in enumerate((y0_ref, y1_ref, y2_ref, y3_ref)):
        wgt = jnp.sum(jnp.where(lane == 2 * TOP_K + kk, route, 0.0), axis=-1, keepdims=True)
        term = wgt * y_ref[0]
        f = term if f is None else f + term
    o_ref[0] = _layer_norm(DEEPNORM_ALPHA * x1_ref[0] + g2_ref[0] * f, lng_ref[...], lnb_ref[...])


def _moe_combine(yg, route, row_offset, x1, g2, ln_g, ln_b, tm):
    bsz, length, d = x1.shape
    tm = min(tm, length)
    nt = length // tm
    off = row_offset // tm

    def pick(kk):
        return lambda bi, i: (kk, off + bi * nt + i, 0)

    def const(bi, i):
        return (0, 0)

    return pl.pallas_call(
        _moe_combine_kernel,
        grid=(bsz, nt),
        in_specs=[pl.BlockSpec((1, tm, d), pick(kk)) for kk in range(TOP_K)] + [
            pl.BlockSpec((tm, LANES), lambda bi, i: (off + bi * nt + i, 0)),
            pl.BlockSpec((1, tm, d), lambda bi, i: (bi, i, 0)),
            pl.BlockSpec((1, 1, d), lambda bi, i: (bi, 0, 0)),
            pl.BlockSpec((1, d), const), pl.BlockSpec((1, d), const)],
        out_specs=pl.BlockSpec((1, tm, d), lambda bi, i: (bi, i, 0)),
        out_shape=jax.ShapeDtypeStruct((bsz, length, d), F32),
        compiler_params=_params("parallel", "parallel"),
        name="moe_combine",
    )(yg, yg, yg, yg, route, x1, g2, ln_g.reshape(1, d), ln_b.reshape(1, d))


SC_CORES = 2
SC_SUBCORES = 16
SC_WORKERS = SC_CORES * SC_SUBCORES
SC_WINDOW = 32


def _sc_row_pipeline(nwin, read, write):
    read(0, 0).start()

    @pl.loop(0, nwin, step=2)
    def _(w0):
        for b in range(2):
            w = w0 + b

            @pl.when(w + 1 < nwin)
            def _():
                @pl.when(w >= 1)
                def _():
                    write(w - 1, 1 - b).wait()

                read(w + 1, 1 - b).start()

            read(w, b).wait()
            write(w, b).start()

    write(nwin - 2, 0).wait()
    write(nwin - 1, 1).wait()


def _sc_scatter_rows(src, pos, n_out):
    t_rows, d = src.shape
    nw, nwin, win = pos.shape
    assert nw == SC_WORKERS and nwin % 2 == 0 and t_rows % (nwin * win) == 0
    mesh = plsc.VectorSubcoreMesh(core_axis_name="c", subcore_axis_name="s")

    @functools.partial(
        pl.kernel, mesh=mesh, out_type=jax.ShapeDtypeStruct((n_out, d), src.dtype),
        scratch_types=[pltpu.VMEM((nwin, win), jnp.int32), pltpu.VMEM((2, win, d), src.dtype),
                       pltpu.SemaphoreType.DMA((2,)), pltpu.SemaphoreType.DMA((2,))])
    def scatter(src_hbm, pos_hbm, out_hbm, idx_v, rows_v, rsem, wsem):
        wid = lax.axis_index("s") * SC_CORES + lax.axis_index("c")
        t0 = lax.rem(wid * (nwin * win), t_rows)
        pltpu.sync_copy(pos_hbm.at[wid], idx_v)

        def read(w, slot):
            return pltpu.make_async_copy(src_hbm.at[pl.ds(t0 + w * win, win)], rows_v.at[slot],
                                         rsem.at[slot])

        def write(w, slot):
            return pltpu.make_async_copy(rows_v.at[slot], out_hbm.at[idx_v.at[w]], wsem.at[slot])

        _sc_row_pipeline(nwin, read, write)

    return scatter(src, pos)


def _sc_gather_rows(table, pos):
    _, d = table.shape
    nw, nwin, win = pos.shape
    assert nw == SC_WORKERS and nwin % 2 == 0
    per = nwin * win
    mesh = plsc.VectorSubcoreMesh(core_axis_name="c", subcore_axis_name="s")

    @functools.partial(
        pl.kernel, mesh=mesh, out_type=jax.ShapeDtypeStruct((nw * per, d), table.dtype),
        scratch_types=[pltpu.VMEM((nwin, win), jnp.int32), pltpu.VMEM((2, win, d), table.dtype),
                       pltpu.SemaphoreType.DMA((2,)), pltpu.SemaphoreType.DMA((2,))])
    def gather(table_hbm, pos_hbm, out_hbm, idx_v, rows_v, rsem, wsem):
        wid = lax.axis_index("s") * SC_CORES + lax.axis_index("c")
        base = wid * per
        pltpu.sync_copy(pos_hbm.at[wid], idx_v)

        def read(w, slot):
            return pltpu.make_async_copy(table_hbm.at[idx_v.at[w]], rows_v.at[slot], rsem.at[slot])

        def write(w, slot):
            return pltpu.make_async_copy(rows_v.at[slot], out_hbm.at[pl.ds(base + w * win, win)],
                                         wsem.at[slot])

        _sc_row_pipeline(nwin, read, write)

    return gather(table, pos)


MOE_TILE = 256


def _moe_sparse(v_all, route, counts, w1, b1, w2, b2):
    t_rows, d = v_all.shape
    n_exp = w1.shape[0]
    pairs = TOP_K * t_rows
    n_tiles = pairs // MOE_TILE + n_exp
    expert = route[:, 0:TOP_K].astype(jnp.int32)
    slot = route[:, TOP_K:2 * TOP_K].astype(jnp.int32)
    cnt = counts[0, :n_exp].astype(jnp.int32)
    tiles_per = (cnt + MOE_TILE - 1) // MOE_TILE
    tile_end = jnp.cumsum(tiles_per)
    tile_start = tile_end - tiles_per
    pos = (tile_start * MOE_TILE)[expert] + slot
    nwin = pairs // (SC_WORKERS * SC_WINDOW)
    pos_km = pos.T.reshape(SC_WORKERS, nwin, SC_WINDOW)
    tile_ids = jnp.arange(n_tiles, dtype=jnp.int32)[:, None]
    owns = (tile_ids >= tile_start[None, :]) & (tile_ids < tile_end[None, :])
    experts = jnp.arange(n_exp, dtype=jnp.int32)[None, :]
    last_used = jnp.max(jnp.where(tiles_per > 0, experts[0], 0))
    used = jnp.any(owns, axis=1)
    te = jnp.where(used, jnp.sum(jnp.where(owns, experts, 0), axis=1), last_used).astype(jnp.int32)
    rows_left = cnt[None, :] - (tile_ids - tile_start[None, :]) * MOE_TILE
    tile_rows = jnp.sum(jnp.where(owns, jnp.clip(rows_left, 0, MOE_TILE), 0), axis=1)
    tile_rows = tile_rows.astype(jnp.int32)
    tile_first = jnp.any(owns & (tile_ids == tile_start[None, :]), axis=1).astype(jnp.int32)
    xs = _sc_scatter_rows(v_all, pos_km, n_tiles * MOE_TILE)
    ys = _moe_rows(xs, te, tile_rows, tile_first, w1, b1, w2, b2, MOE_TILE)
    return _sc_gather_rows(ys, pos_km).reshape(TOP_K, t_rows, d)


@functools.lru_cache(maxsize=None)
def _sincos_2d(rows, cols, dim):
    quarter = dim // 4
    omega = 1.0 / (10000.0 ** (np.arange(quarter, dtype=np.float64) / quarter))

    def emb1d(n):
        ang = np.arange(n, dtype=np.float64)[:, None] * omega
        return np.concatenate([np.sin(ang), np.cos(ang)], axis=-1)

    er = np.broadcast_to(emb1d(rows)[:, None], (rows, cols, dim // 2))
    ec = np.broadcast_to(emb1d(cols)[None], (rows, cols, dim // 2))
    return np.concatenate([er, ec], axis=-1).reshape(rows * cols, dim).astype(np.float32)


def _gate_layout(gates, n_ch):
    bsz, length, _ = gates.shape
    g = gates[..., :n_ch].reshape(bsz, length // DN_CHUNK, DN_CHUNK, n_ch)
    return g.transpose(0, 3, 1, 2)


def kernel(x, c, ctx, c_ctx, ada_w, ada_b, ln_g, ln_b, ev_w_in, ev_w_out, dn_conv_w, dn_a_log, dn_dt_bias, dn_norm_g, hy_conv_w, hy_conv_b, hy_w_in, hy_b_in, hy_w_mid, hy_b_mid, hy_w_out, hy_freq, hy_skip, od_w_in, od_b_in, lru_conv_w, lru_conv_b, lru_wa, lru_ba, lru_wx, lru_bx, lru_a_param, od_w_out, od_b_out, router_w, router_b, moe_w1, moe_b1, moe_w2, moe_b2):
    bsz, length, d = x.shape
    ctx_len = ctx.shape[1]
    pos = jnp.asarray(_sincos_2d(length // GRID_W, GRID_W, d))

    cond = jnp.zeros((16, d), F32).at[:bsz].set(c).at[bsz].set(c_ctx)
    mod = _modulation(cond, ada_w, ada_b).reshape(DEPTH, 16, 6, d)

    def lat_mod(layer, k):
        return mod[layer, :bsz, k][:, None, :]

    def ctx_mod(layer, k):
        return jnp.broadcast_to(mod[layer, bsz, k][None, None, :], (bsz, 1, d))

    hc = ctx
    for layer in range(DEPTH):
        last = layer == DEPTH - 1
        j = layer // 2
        if layer % 2 == 0:
            dn_qk = DN_HEADS * DN_DK
            dn_qkv = 3 * dn_qk
            dn_in = dn_qkv + dn_qk + 4 * DN_HEADS
            w_in = ev_w_in[j]
            gate_w = jnp.zeros((d, LANES), F32).at[:, :4 * DN_HEADS].set(w_in[:, dn_qkv + dn_qk:dn_in])
            w_cat = jnp.concatenate([w_in[:, :dn_qkv + dn_qk], gate_w, w_in[:, dn_in:]],
                                    axis=1).astype(BF16)
            hy_in = w_in.shape[1] - dn_in
            widths = (dn_qkv, dn_qk, LANES, hy_in)
            qkv_l, z_l, gt_l, phy_l = _inproj(x, lat_mod(layer, 1), lat_mod(layer, 0), pos,
                                               w_cat, None, widths, 512)
            qkv_c, z_c, gt_c, phy_c = _inproj(hc, ctx_mod(layer, 1), ctx_mod(layer, 0), None,
                                               w_cat, None, widths, 512)
            gates = jnp.concatenate([_gate_layout(gt_c, 4 * DN_HEADS),
                                     _gate_layout(gt_l, 4 * DN_HEADS)], axis=2)
            n_chunks = gates.shape[2]
            gates = jnp.pad(gates, ((0, 0), (0, 0), (0, -n_chunks % 16), (0, 0)))
            dn_l, dn_c = _deltanet(qkv_l, z_l, qkv_c, z_c, gates, dn_conv_w[j], dn_a_log[j],
                                   dn_dt_bias[j], dn_norm_g[j])
            filt = (hy_w_in[j], hy_b_in[j], hy_w_mid[j], hy_b_mid[j], hy_w_out[j], hy_freq[j])
            hy_l = _hyena(phy_l, hy_conv_w[j], hy_conv_b[j], filt, hy_skip[j])
            w_out = ev_w_out[j].astype(BF16)
            half = dn_l.shape[2]
            acts_l, ws, bias = (dn_l, hy_l), (w_out[:half], w_out[half:]), None
            acts_c = None
            if not last:
                hy_c = _hyena(phy_c, hy_conv_w[j], hy_conv_b[j], filt, hy_skip[j])
                acts_c = (dn_c, hy_c)
        else:
            w_in = od_w_in[j].astype(BF16)
            width = w_in.shape[1] // 2
            b_in = od_b_in[j].reshape(1, 2 * width)
            xb_l, yb_l = _inproj(x, lat_mod(layer, 1), lat_mod(layer, 0), pos, w_in, b_in,
                                 (width, width), 512)
            xb_c, _ = _inproj(hc, ctx_mod(layer, 1), ctx_mod(layer, 0), None, w_in, b_in,
                              (width, width), 512)
            act_l = _rglru(xb_l, yb_l, xb_c, lru_conv_w[j], lru_conv_b[j], lru_wa[j], lru_ba[j],
                           lru_wx[j], lru_bx[j], lru_a_param[j])
            acts_l, ws, bias = (act_l,), (od_w_out[j].astype(BF16),), od_b_out[j]
            acts_c = None
            assert last, "context outputs of the RG-LRU layer are only needed before the last layer"

        b1 = moe_b1[layer][:, None, :]
        b2 = moe_b2[layer][:, None, :]

        x1, v, route, counts = _post_mixer(acts_l, ws, bias, x, lat_mod(layer, 2),
                                           lat_mod(layer, 4), lat_mod(layer, 3), ln_g[layer, 0],
                                           ln_b[layer, 0], router_w[layer], router_b[layer],
                                           jnp.zeros((1, LANES), F32), 256)
        v_all = v.reshape(bsz * length, d)
        route = route.reshape(bsz * length, LANES)
        if not last:
            hc1, vc, route_c, counts = _post_mixer(acts_c, ws, bias, hc, ctx_mod(layer, 2),
                                                   ctx_mod(layer, 4), ctx_mod(layer, 3),
                                                   ln_g[layer, 0], ln_b[layer, 0], router_w[layer],
                                                   router_b[layer], counts, 256)
            v_all = jnp.concatenate([v_all, vc.reshape(bsz * ctx_len, d)], axis=0)
            route = jnp.concatenate([route, route_c.reshape(bsz * ctx_len, LANES)], axis=0)
        yg = _moe_sparse(v_all, route, counts, moe_w1[layer], b1, moe_w2[layer], b2)
        x = _moe_combine(yg, route, 0, x1, lat_mod(layer, 5), ln_g[layer, 1], ln_b[layer, 1], 256)
        if not last:
            hc = _moe_combine(yg, route, bsz * length, hc1, ctx_mod(layer, 5), ln_g[layer, 1],
                              ln_b[layer, 1], 256)
    return x
```

```python
import functools
import math

import numpy as np
import jax
import jax.numpy as jnp
from jax import lax
from jax.experimental import pallas as pl
from jax.experimental.pallas import tpu as pltpu
from jax.experimental.pallas import tpu_sc as plsc

F32 = jnp.float32
BF16 = jnp.bfloat16

VMEM_LIMIT_BYTES = 56 * 1024 * 1024
LANES = 128

DEPTH = 2
GRID_W = 64
DEEPNORM_ALPHA = (2.0 * DEPTH) ** 0.25
LN_EPS = 1e-5
RMS_EPS = 1e-6

DN_HEADS = 4
DN_DK = 128
DN_CHUNK = 64
DN_CONV = 4

HY_EMB = 33
HY_TARGET = 1e-2
HY_MIN_DECAY = math.log(HY_TARGET) / 1.5
HY_MAX_DECAY = math.log(HY_TARGET) / 0.3
HY_CONV = 3

LRU_HEADS = 4
LRU_C = 8.0
LRU_CONV = 4

N_EXPERTS = 32
TOP_K = 4
SWIGLU_ALPHA = 1.702
SWIGLU_LIMIT = 7.0


def _params(*sem):
    return pltpu.CompilerParams(dimension_semantics=sem, vmem_limit_bytes=VMEM_LIMIT_BYTES)


def _dot(a, b):
    return jnp.dot(a, b, preferred_element_type=F32)


def _dot_nt(a, b):
    return lax.dot_general(a, b, (((1,), (1,)), ((), ())), preferred_element_type=F32)


def _split(a):
    hi = a.astype(BF16)
    lo = (a - hi.astype(F32)).astype(BF16)
    return hi, lo


def _dot3(a, b):
    ah, al = _split(a)
    bh, bl = _split(b)
    return _dot(ah, bh) + _dot(ah, bl) + _dot(al, bh)


def _silu(x):
    return x * (1.0 / (1.0 + jnp.exp(-x)))


def _sigmoid(x):
    return 1.0 / (1.0 + jnp.exp(-x))


def _softplus(x):
    return jnp.maximum(x, 0.0) + jnp.log(1.0 + jnp.exp(-jnp.abs(x)))


def _layer_norm(x, g, b):
    mu = jnp.mean(x, axis=-1, keepdims=True)
    xc = x - mu
    var = jnp.mean(xc * xc, axis=-1, keepdims=True)
    return xc * lax.rsqrt(var + LN_EPS) * g + b


def _pack_bf16_pairs(x):
    w = x.shape[1] // 2
    lo = pltpu.bitcast(x[:, :w].astype(BF16).astype(F32), jnp.uint32) >> 16
    hi = pltpu.bitcast(x[:, w:].astype(BF16).astype(F32), jnp.uint32) & jnp.uint32(0xFFFF0000)
    return lo | hi


def _unpack_bf16_pairs(p):
    lo = pltpu.bitcast(p << 16, F32)
    hi = pltpu.bitcast(p & jnp.uint32(0xFFFF0000), F32)
    return jnp.concatenate([lo, hi], axis=1)


def _shift_rows(x, s):
    if s == 0:
        return x
    n = x.shape[0]
    rows = lax.broadcasted_iota(jnp.int32, x.shape, 0)
    valid = (rows >= s) if s > 0 else (rows < n + s)
    return jnp.where(valid, pltpu.roll(x, s % n, axis=0), 0.0)


def _depthwise_conv(x, w, pad_left):
    acc = None
    for i in range(w.shape[0]):
        term = _shift_rows(x, pad_left - i) * w[i:i + 1, :]
        acc = term if acc is None else acc + term
    return acc


def _mod_kernel(c_ref, w_ref, b_ref, o_ref):
    o_ref[0] = _dot3(_silu(c_ref[...]), w_ref[0]) + b_ref[0]


def _modulation(cond, ada_w, ada_b):
    depth, d, n = ada_w.shape
    rows = cond.shape[0]
    tn = 1536
    return pl.pallas_call(
        _mod_kernel,
        grid=(depth, n // tn),
        in_specs=[
            pl.BlockSpec((rows, d), lambda l, j: (0, 0)),
            pl.BlockSpec((1, d, tn), lambda l, j: (l, 0, j)),
            pl.BlockSpec((1, 1, tn), lambda l, j: (l, 0, j)),
        ],
        out_specs=pl.BlockSpec((1, rows, tn), lambda l, j: (l, 0, j)),
        out_shape=jax.ShapeDtypeStruct((depth, rows, n), F32),
        compiler_params=_params("parallel", "parallel"),
        name="modulation",
    )(cond, ada_w, ada_b.reshape(depth, 1, n))


def _inproj_kernel(*refs, splits, has_pos, has_bias):
    x_ref, sc_ref, sh_ref = refs[:3]
    i = 3
    pos_ref = None
    if has_pos:
        pos_ref = refs[i]
        i += 1
    w_ref = refs[i]
    i += 1
    b_ref = None
    if has_bias:
        b_ref = refs[i]
        i += 1
    o_refs = refs[i:]
    u = x_ref[0] * (1.0 + sc_ref[0]) + sh_ref[0]
    if has_pos:
        u = u + pos_ref[...]
    ub = u.astype(BF16)
    for o_ref, (s, e) in zip(o_refs, splits):
        acc = _dot(ub, w_ref[:, s:e])
        if has_bias:
            acc = acc + b_ref[:, s:e]
        o_ref[0] = acc


def _inproj(x, sc, sh, pos, w, b, widths, tm):
    bsz, length, d = x.shape
    n = w.shape[1]
    splits, s = [], 0
    for wd in widths:
        splits.append((s, s + wd))
        s += wd
    assert s == n
    tm = min(tm, length)
    in_specs = [
        pl.BlockSpec((1, tm, d), lambda bi, i: (bi, i, 0)),
        pl.BlockSpec((1, 1, d), lambda bi, i: (bi, 0, 0)),
        pl.BlockSpec((1, 1, d), lambda bi, i: (bi, 0, 0)),
    ]
    args = [x, sc, sh]
    if pos is not None:
        in_specs.append(pl.BlockSpec((tm, d), lambda bi, i: (i, 0)))
        args.append(pos)
    in_specs.append(pl.BlockSpec((d, n), lambda bi, i: (0, 0)))
    args.append(w)
    if b is not None:
        in_specs.append(pl.BlockSpec((1, n), lambda bi, i: (0, 0)))
        args.append(b)
    return pl.pallas_call(
        functools.partial(_inproj_kernel, splits=tuple(splits), has_pos=pos is not None,
                          has_bias=b is not None),
        grid=(bsz, length // tm),
        in_specs=in_specs,
        out_specs=[pl.BlockSpec((1, tm, wd), lambda bi, i: (bi, i, 0)) for wd in widths],
        out_shape=[jax.ShapeDtypeStruct((bsz, length, wd), F32) for wd in widths],
        compiler_params=_params("parallel", "parallel"),
        name="inproj",
    )(*args)


def _unit_tri_inverses(mats, lower):
    n = mats[0].shape[0]
    nb = 16
    np_ = len(mats)
    ii = lax.broadcasted_iota(jnp.int32, (n, n), 0)
    jj = lax.broadcasted_iota(jnp.int32, (n, n), 1)
    same16 = (ii // nb) == (jj // nb)
    same32 = (ii // (2 * nb)) == (jj // (2 * nb))
    dgs = []
    for a in mats:
        ad = jnp.where(same16, a, 0.0)
        dgs.append(ad[0:nb] + ad[nb:2 * nb] + ad[2 * nb:3 * nb] + ad[3 * nb:4 * nb])
    dg = jnp.concatenate(dgs, axis=0)
    rr = lax.broadcasted_iota(jnp.int32, dg.shape, 0)
    ll = lax.broadcasted_iota(jnp.int32, dg.shape, 1)
    xd = jnp.where(rr % nb == ll % nb, 1.0, 0.0)
    blk0 = (ll // nb) * nb
    for s in (range(nb - 1) if lower else range(nb - 1, 0, -1)):
        col = jnp.take_along_axis(dg, blk0 + s, axis=1)
        row = jnp.concatenate(
            [jnp.broadcast_to(xd[p * nb + s:p * nb + s + 1, :], (nb, n)) for p in range(np_)], axis=0)
        xd = xd - col * row
    ds = [jnp.where(same16, jnp.concatenate([xd[p * nb:(p + 1) * nb]] * (n // nb), axis=0), 0.0)
          for p in range(np_)]
    lvl1 = same32 & jnp.logical_not(same16)
    t1 = [_dot3(d, jnp.where(lvl1, a, 0.0)) for d, a in zip(ds, mats)]
    x1 = [d - _dot3(t, d) for d, t in zip(ds, t1)]
    t2 = [_dot3(x, jnp.where(same32, 0.0, a)) for x, a in zip(x1, mats)]
    return [x - _dot3(t, x) for x, t in zip(x1, t2)]


def _dn_kernel(alog_ref, dtb_ref,
               ql_ref, kl_ref, vl_ref, zl_ref, qc_ref, kc_ref, vc_ref, zc_ref, gt_ref,
               cwq_ref, cwk_ref, cwv_ref, ng_ref,
               yl_ref, yc_ref,
               qn_s, kn_s, vn_s, gc_s, bt_s, nq_s, c_s, gl_s, o_s,
               *, ctx_len, lat_len):
    h = pl.program_id(1)
    csz = DN_CHUNK
    nc_ctx = ctx_len // csz
    nc = (ctx_len + lat_len) // csz

    def prep(src_ref, cw_ref, kind):
        t = _silu(_depthwise_conv(src_ref[0], cw_ref[...], DN_CONV // 2))
        if kind == "v":
            return t
        t = t * lax.rsqrt(jnp.sum(t * t, axis=-1, keepdims=True) + RMS_EPS)
        return t * (DN_DK ** -0.5) if kind == "q" else t

    qn_s[0:ctx_len, :] = prep(qc_ref, cwq_ref, "q")
    qn_s[ctx_len:, :] = prep(ql_ref, cwq_ref, "q")
    kn_s[0:ctx_len, :] = prep(kc_ref, cwk_ref, "k")
    kn_s[ctx_len:, :] = prep(kl_ref, cwk_ref, "k")
    vn_s[0:ctx_len, :] = prep(vc_ref, cwv_ref, "v")
    vn_s[ctx_len:, :] = prep(vl_ref, cwv_ref, "v")

    ii = lax.broadcasted_iota(jnp.int32, (csz, csz), 0)
    jj = lax.broadcasted_iota(jnp.int32, (csz, csz), 1)
    eye = ii == jj
    for d in range(2):
        graw = gt_ref[0, d * 2 * DN_HEADS + h]
        braw = gt_ref[0, d * 2 * DN_HEADS + DN_HEADS + h]
        a_neg = -jnp.exp(jnp.zeros_like(graw) + alog_ref[d, h])
        g = a_neg * _softplus(graw + dtb_ref[d, h])
        tri = jnp.where((ii <= jj) if d == 0 else (ii >= jj), 1.0, 0.0).astype(BF16)
        g1 = g.astype(BF16)
        r1 = g - g1.astype(F32)
        g2 = r1.astype(BF16)
        g3 = (r1 - g2.astype(F32)).astype(BF16)
        gc_s[d] = _dot(g1, tri) + _dot(g2, tri) + _dot(g3, tri)
        bt_s[d] = _sigmoid(braw)

    o_s[...] = jnp.zeros_like(o_s)

    group = 6
    assert nc % group == 0

    def chunk_prep(gi, carry):
        ns = [gi * group + c for c in range(group)]
        r0s = [pl.multiple_of(n * csz, csz) for n in ns]
        qs = [qn_s[pl.ds(r0, csz), :] for r0 in r0s]
        ks = [kn_s[pl.ds(r0, csz), :] for r0 in r0s]
        vs = [vn_s[pl.ds(r0, csz), :] for r0 in r0s]
        kbfs = [k.astype(BF16) for k in ks]
        qks = [_dot_nt(q.astype(BF16), kbf) for q, kbf in zip(qs, kbfs)]
        for d in range(2):
            incl = (ii >= jj) if d == 0 else (ii <= jj)
            strict = (ii > jj) if d == 0 else (ii < jj)
            grs = [gc_s[d, pl.ds(n, 1), :] for n in ns]
            grows = [jnp.broadcast_to(gr, (csz, csz)) for gr in grs]
            gcols = [jnp.sum(jnp.where(eye, grow, 0.0), axis=1, keepdims=True) for grow in grows]
            bcols = [jnp.sum(jnp.where(eye, jnp.broadcast_to(bt_s[d, pl.ds(n, 1), :], (csz, csz)),
                                       0.0), axis=1, keepdims=True) for n in ns]
            decays = [jnp.where(incl, jnp.exp(jnp.where(incl, gcol - grow, 0.0)), 0.0)
                      for gcol, grow in zip(gcols, grows)]
            kbs = [k * bcol for k, bcol in zip(ks, bcols)]
            amats = [jnp.where(strict, _dot_nt(kb.astype(BF16), kbf) * decay, 0.0)
                     for kb, kbf, decay in zip(kbs, kbfs, decays)]
            tbs = [t.astype(BF16) for t in _unit_tri_inverses(amats, lower=(d == 0))]
            cs = range(group)
            egs = [jnp.exp(gcols[c]) for c in cs]
            ubs = [_dot(tbs[c], (vs[c] * bcols[c]).astype(BF16)).astype(BF16) for c in cs]
            wbs = [_dot(tbs[c], (kbs[c] * egs[c]).astype(BF16)).astype(BF16) for c in cs]
            attns = [jnp.where(incl, qks[c] * decays[c], 0.0).astype(BF16) for c in cs]
            glasts = [grs[c][:, csz - 1:csz] if d == 0 else grs[c][:, 0:1] for c in cs]
            kdts = [(ks[c] * jnp.exp(glasts[c] - gcols[c])).T.astype(BF16) for c in cs]
            nmats = [_dot(kdts[c], wbs[c]).astype(BF16) for c in cs]
            qmats = [(qs[c] * egs[c] - _dot(attns[c], wbs[c])).astype(BF16) for c in cs]
            cmats = [_dot(kdts[c], ubs[c]) for c in cs]
            omats = [_dot(attns[c], ubs[c]) for c in cs]
            for c in cs:
                n = ns[c]
                nq_s[d, n, 0:DN_DK, :] = nmats[c]
                nq_s[d, n, DN_DK:DN_DK + csz, :] = qmats[c]
                c_s[d, n] = cmats[c]
                o_s[pl.ds(r0s[c], csz), :] += omats[c]
                gl_s[d, pl.ds(n, 1), :] = jnp.broadcast_to(jnp.exp(glasts[c]), (1, LANES))
        return carry

    lax.fori_loop(0, nc // group, chunk_prep, 0)

    def step(i, states):
        new_states = []
        for d in range(2):
            if d == 0:
                n = i
            else:
                n = jnp.where(i < nc_ctx, nc_ctx - 1 - i, nc + nc_ctx - 1 - i)
            r0 = pl.multiple_of(n * csz, csz)
            s = states[d]
            r = _dot(nq_s[d, n], s.astype(BF16))
            o_s[pl.ds(r0, csz), :] += r[DN_DK:DN_DK + csz]
            new_states.append(s * gl_s[d, pl.ds(n, 1), :] - r[0:DN_DK] + c_s[d, n])
        return tuple(new_states)

    zero = jnp.zeros((DN_DK, DN_DK), F32)
    lax.fori_loop(0, nc, step, (zero, zero))

    def gated_norm(o, z):
        o = o * lax.rsqrt(jnp.mean(o * o, axis=-1, keepdims=True) + RMS_EPS) * ng_ref[...]
        return (o * _silu(z)).astype(yl_ref.dtype)

    yc_ref[0] = gated_norm(o_s[0:ctx_len, :], zc_ref[0])
    yl_ref[0] = gated_norm(o_s[ctx_len:, :], zl_ref[0])


def _deltanet(qkv_l, z_l, qkv_c, z_c, gates, conv_w, a_log, dt_bias, norm_g):
    bsz, lat_len, _ = qkv_l.shape
    ctx_len = qkv_c.shape[1]
    tot = ctx_len + lat_len
    nc = tot // DN_CHUNK
    ncp = gates.shape[2]
    hd = DN_DK
    nh = DN_HEADS

    def col(off):
        return lambda b, h: (b, 0, off + h)

    def wcol(off):
        return lambda b, h: (0, off + h)

    smem = pl.BlockSpec(memory_space=pltpu.SMEM)
    in_specs = [
        smem, smem,
        pl.BlockSpec((1, lat_len, hd), col(0)),
        pl.BlockSpec((1, lat_len, hd), col(nh)),
        pl.BlockSpec((1, lat_len, hd), col(2 * nh)),
        pl.BlockSpec((1, lat_len, hd), col(0)),
        pl.BlockSpec((1, ctx_len, hd), col(0)),
        pl.BlockSpec((1, ctx_len, hd), col(nh)),
        pl.BlockSpec((1, ctx_len, hd), col(2 * nh)),
        pl.BlockSpec((1, ctx_len, hd), col(0)),
        pl.BlockSpec((1, 4 * nh, ncp, DN_CHUNK), lambda b, h: (b, 0, 0, 0)),
        pl.BlockSpec((DN_CONV, hd), wcol(0)),
        pl.BlockSpec((DN_CONV, hd), wcol(nh)),
        pl.BlockSpec((DN_CONV, hd), wcol(2 * nh)),
        pl.BlockSpec((1, hd), lambda b, h: (0, 0)),
    ]
    scratch = [
        pltpu.VMEM((tot, hd), F32), pltpu.VMEM((tot, hd), F32), pltpu.VMEM((tot, hd), F32),
        pltpu.VMEM((2, ncp, DN_CHUNK), F32), pltpu.VMEM((2, ncp, DN_CHUNK), F32),
        pltpu.VMEM((2, nc, hd + DN_CHUNK, hd), BF16),
        pltpu.VMEM((2, nc, hd, hd), F32),
        pltpu.VMEM((2, nc, LANES), F32),
        pltpu.VMEM((tot, hd), F32),
    ]
    return pl.pallas_call(
        functools.partial(_dn_kernel, ctx_len=ctx_len, lat_len=lat_len),
        grid=(bsz, nh),
        in_specs=in_specs,
        out_specs=[pl.BlockSpec((1, lat_len, hd), col(0)),
                   pl.BlockSpec((1, ctx_len, hd), col(0))],
        out_shape=[jax.ShapeDtypeStruct((bsz, lat_len, nh * hd), BF16),
                   jax.ShapeDtypeStruct((bsz, ctx_len, nh * hd), BF16)],
        scratch_shapes=scratch,
        compiler_params=_params("parallel", "parallel"),
        name="deltanet",
    )(a_log, dt_bias, qkv_l, qkv_l, qkv_l, z_l, qkv_c, qkv_c, qkv_c, z_c, gates,
      conv_w, conv_w, conv_w, norm_g.reshape(1, hd))


def _hy_filter_kernel(z_ref, win_ref, bin_ref, wmid_ref, bmid_ref, wout_ref, freq_ref, dec_ref,
                      o_ref):
    freq = freq_ref[...]
    hcur = jnp.sin(freq * (_dot3(z_ref[...], win_ref[...]) + bin_ref[...]))
    for i in range(wmid_ref.shape[0]):
        hcur = jnp.sin(freq * (_dot3(hcur, wmid_ref[i]) + bmid_ref[i]))
    o_ref[...] = _dot3(hcur, wout_ref[...]) * dec_ref[...]


def _hy_filter(z, w_in, b_in, w_mid, b_mid, w_out, freq, dec2):
    length = z.shape[0]
    n_out = w_out.shape[1]
    tl = min(256, length)

    def whole(a):
        return pl.BlockSpec(a.shape, lambda i: (0,) * a.ndim)

    return pl.pallas_call(
        _hy_filter_kernel,
        grid=(length // tl,),
        in_specs=[pl.BlockSpec((tl, z.shape[1]), lambda i: (i, 0)),
                  whole(w_in), whole(b_in), whole(w_mid), whole(b_mid), whole(w_out), whole(freq),
                  pl.BlockSpec((tl, n_out), lambda i: (i, 0))],
        out_specs=pl.BlockSpec((tl, n_out), lambda i: (i, 0)),
        out_shape=jax.ShapeDtypeStruct((length, n_out), F32),
        compiler_params=_params("parallel"),
        name="hyena_filter",
    )(z, w_in, b_in, w_mid, b_mid, w_out, freq, dec2)


def _filt_spec_kernel(fc_ref, fs_ref, h_ref, kc_ref, ks_ref, *, width):
    j = pl.program_id(0)
    hmat = h_ref[...]
    rows = lax.broadcasted_iota(jnp.int32, hmat.shape, 0)
    cols = lax.broadcasted_iota(jnp.int32, hmat.shape, 1)
    hmat = jnp.where((rows == 0) & (cols >= width), 0.0, hmat)
    hh, hl = _split(hmat)
    c = _dot(fc_ref[...], hh) + _dot(fc_ref[...], hl)
    s = _dot(fs_ref[...], hh) + _dot(fs_ref[...], hl)
    kc_ref[...] = c[:, :width] + c[:, width:]
    orow = lax.broadcasted_iota(jnp.int32, (c.shape[0], width), 0)
    sign = jnp.where((orow == 0) & (j == 0), 1.0, -1.0)
    ks_ref[...] = s[:, :width] + sign * s[:, width:]


def _filt_spec(fwd, hfilt, tf):
    length, two_w = hfilt.shape
    width = two_w // 2
    nt = length // tf
    return pl.pallas_call(
        functools.partial(_filt_spec_kernel, width=width),
        grid=(nt,),
        in_specs=[pl.BlockSpec((tf, length), lambda j: (j, 0)),
                  pl.BlockSpec((tf, length), lambda j: (nt + j, 0)),
                  pl.BlockSpec((length, two_w), lambda j: (0, 0))],
        out_specs=[pl.BlockSpec((tf, width), lambda j: (j, 0)),
                   pl.BlockSpec((tf, width), lambda j: (j, 0))],
        out_shape=[jax.ShapeDtypeStruct((length, width), F32)] * 2,
        compiler_params=_params("arbitrary"),
        name="hyena_filter_spectrum",
    )(fwd, fwd, hfilt)


def _hy_prep_kernel(x0_ref, x1_ref, v_ref, w0_ref, w1_ref, w2_ref, b0_ref, b1_ref, b2_ref,
                    x0o_ref, vvo_ref):
    x0 = _depthwise_conv(x0_ref[0], w0_ref[...], HY_CONV // 2) + b0_ref[...]
    x1 = _depthwise_conv(x1_ref[0], w1_ref[...], HY_CONV // 2) + b1_ref[...]
    v = _depthwise_conv(v_ref[0], w2_ref[...], HY_CONV // 2) + b2_ref[...]
    x0o_ref[0] = x0
    vvo_ref[0] = v * x1


def _hy_prep(p_hy, conv_w, conv_b):
    bsz, length, three_w = p_hy.shape
    width = three_w // 3
    nb = width // LANES

    def col(off):
        return lambda b, j: (b, 0, off + j)

    def wcol(off):
        return lambda b, j: (0, off + j)

    k = conv_w.shape[0]
    return pl.pallas_call(
        _hy_prep_kernel,
        grid=(bsz, nb),
        in_specs=[pl.BlockSpec((1, length, LANES), col(0)),
                  pl.BlockSpec((1, length, LANES), col(nb)),
                  pl.BlockSpec((1, length, LANES), col(2 * nb)),
                  pl.BlockSpec((k, LANES), wcol(0)),
                  pl.BlockSpec((k, LANES), wcol(nb)),
                  pl.BlockSpec((k, LANES), wcol(2 * nb)),
                  pl.BlockSpec((1, LANES), wcol(0)),
                  pl.BlockSpec((1, LANES), wcol(nb)),
                  pl.BlockSpec((1, LANES), wcol(2 * nb))],
        out_specs=[pl.BlockSpec((1, length, LANES), col(0)),
                   pl.BlockSpec((1, length, LANES), col(0))],
        out_shape=[jax.ShapeDtypeStruct((bsz, length, width), F32)] * 2,
        compiler_params=_params("parallel", "parallel"),
        name="hyena_prep",
    )(p_hy, p_hy, p_hy, conv_w, conv_w, conv_w, conv_b, conv_b, conv_b)


def _dft_fwd_kernel(fc_ref, fs_ref, v_ref, kc_ref, ks_ref, yc_ref, ys_ref, vb_s):
    j = pl.program_id(1)

    @pl.when(j == 0)
    def _():
        vb_s[...] = v_ref[0].astype(BF16)

    uc = _dot(fc_ref[...], vb_s[...])
    us = _dot(fs_ref[...], vb_s[...])
    kc = kc_ref[...]
    ks = ks_ref[...]
    rows = lax.broadcasted_iota(jnp.int32, uc.shape, 0)
    special = (rows == 0) & (j == 0)
    yc_ref[0] = (uc * kc - jnp.where(special, 0.0, us * ks)).astype(BF16)
    ys_ref[0] = jnp.where(special, us * ks, uc * ks + us * kc).astype(BF16)


def _dft_fwd(fwd, vv, kc, ks, tf):
    bsz, length, width = vv.shape
    nt = length // tf
    return pl.pallas_call(
        _dft_fwd_kernel,
        grid=(bsz, nt),
        in_specs=[pl.BlockSpec((tf, length), lambda b, j: (j, 0)),
                  pl.BlockSpec((tf, length), lambda b, j: (nt + j, 0)),
                  pl.BlockSpec((1, length, width), lambda b, j: (b, 0, 0)),
                  pl.BlockSpec((tf, width), lambda b, j: (j, 0)),
                  pl.BlockSpec((tf, width), lambda b, j: (j, 0))],
        out_specs=[pl.BlockSpec((1, tf, width), lambda b, j: (b, j, 0)),
                   pl.BlockSpec((1, tf, width), lambda b, j: (b, j, 0))],
        out_shape=[jax.ShapeDtypeStruct((bsz, length, width), BF16)] * 2,
        scratch_shapes=[pltpu.VMEM((length, width), BF16)],
        compiler_params=_params("parallel", "arbitrary"),
        name="hyena_dft_fwd",
    )(fwd, fwd, vv, kc, ks)


def _dft_inv_kernel(ic_ref, is_ref, yc_ref, ys_ref, vv_ref, x0_ref, skip_ref, o_ref):
    y = _dot(ic_ref[...], yc_ref[0]) + _dot(is_ref[...], ys_ref[0])
    o_ref[0] = (x0_ref[0] * (y + vv_ref[0] * skip_ref[...])).astype(o_ref.dtype)


def _dft_inv(inv, yc, ys, vv, x0, skip, tt):
    bsz, length, width = vv.shape
    nt = length // tt
    return pl.pallas_call(
        _dft_inv_kernel,
        grid=(bsz, nt),
        in_specs=[pl.BlockSpec((tt, length), lambda b, i: (i, 0)),
                  pl.BlockSpec((tt, length), lambda b, i: (i, 1)),
                  pl.BlockSpec((1, length, width), lambda b, i: (b, 0, 0)),
                  pl.BlockSpec((1, length, width), lambda b, i: (b, 0, 0)),
                  pl.BlockSpec((1, tt, width), lambda b, i: (b, i, 0)),
                  pl.BlockSpec((1, tt, width), lambda b, i: (b, i, 0)),
                  pl.BlockSpec((1, width), lambda b, i: (0, 0))],
        out_specs=pl.BlockSpec((1, tt, width), lambda b, i: (b, i, 0)),
        out_shape=jax.ShapeDtypeStruct((bsz, length, width), BF16),
        compiler_params=_params("parallel", "parallel"),
        name="hyena_dft_inv",
    )(inv, inv, yc, ys, vv, x0, skip)


@functools.lru_cache(maxsize=None)
def _hyena_tables(length):
    n2 = 2 * length
    t = np.linspace(0.0, 1.0, length)[:, None]
    bands = (HY_EMB - 1) // 2
    wpos = 2.0 * np.pi * np.arange(length)[:, None] / length
    fb = np.linspace(1e-4, bands - 1, bands)[None]
    z = np.concatenate([t, np.cos(fb * wpos), -np.sin(fb * wpos)], axis=-1)
    zpad = np.zeros((length, LANES))
    zpad[:, :HY_EMB] = z
    f = np.arange(length)[:, None]
    n = np.arange(length)[None, :]
    ang = 2.0 * np.pi * ((f * n) % n2) / n2
    cos_m = np.cos(ang)
    sin_m = np.sin(ang)
    sin_m[0, :] = np.cos(np.pi * np.arange(length))
    fwd = np.concatenate([cos_m, sin_m], axis=0)
    scale = np.full((1, n2), 2.0 / n2)
    scale[0, 0] = 1.0 / n2
    scale[0, length] = 1.0 / n2
    inv = fwd.T * scale
    return (zpad.astype(np.float32), t.astype(np.float32), fwd.astype(np.float32),
            inv.astype(np.float32))


def _hyena(p_hy, conv_w, conv_b, filt, skip):
    bsz, length, three_w = p_hy.shape
    width = three_w // 3
    w_in, b_in, w_mid, b_mid, w_out, freq = filt
    zpad, t, fwd, inv = _hyena_tables(length)
    deltas = np.abs(np.linspace(HY_MIN_DECAY, HY_MAX_DECAY, width))[None, :]
    dec = np.exp(-t.astype(np.float64) * deltas).astype(np.float32)
    dec2 = jnp.asarray(np.concatenate([dec, dec], axis=1))
    ffn = w_in.shape[1]
    w_in_pad = jnp.zeros((LANES, ffn), F32).at[:HY_EMB].set(w_in)
    hfilt = _hy_filter(jnp.asarray(zpad), w_in_pad, b_in.reshape(1, ffn), w_mid,
                       b_mid.reshape(-1, 1, ffn), w_out, freq.reshape(1, ffn), dec2)
    fwd_b = jnp.asarray(fwd).astype(BF16)
    inv_b = jnp.asarray(inv).astype(BF16)
    tf = min(256, length)
    kc, ks = _filt_spec(fwd_b, hfilt, tf)
    x0, vv = _hy_prep(p_hy, conv_w, conv_b.reshape(1, three_w))
    yc, ys = _dft_fwd(fwd_b, vv, kc, ks, tf)
    return _dft_inv(inv_b, yc, ys, vv, x0, skip.reshape(1, width), min(256, length))


def _gelu_tanh(x):
    return 0.5 * x * (1.0 + jnp.tanh(math.sqrt(2.0 / math.pi) * (x + 0.044715 * x * x * x)))


def _lru_kernel(xl_ref, yl_ref, xc_ref, cw_ref, cb_ref, wa_ref, ba_ref, wx_ref, bx_ref, ap_ref,
                o_ref, xs_s, a_s, b_s, h_s, *, ctx_len, lat_len):
    tot = ctx_len + lat_len
    ngrp = tot // 8
    ngrp_ctx = ctx_len // 8
    xs_s[0:ctx_len, :] = _depthwise_conv(xc_ref[0], cw_ref[...], LRU_CONV // 2) + cb_ref[...]
    xs_s[ctx_len:, :] = _depthwise_conv(xl_ref[0], cw_ref[...], LRU_CONV // 2) + cb_ref[...]
    xs = xs_s[...]
    xsb = xs.astype(BF16)
    rows8 = lax.broadcasted_iota(jnp.int32, xs.shape, 0) % 8
    for d in range(2):
        r = _sigmoid(_dot(xsb, wa_ref[d, 0].astype(BF16)) + ba_ref[d])
        gi = _sigmoid(_dot(xsb, wx_ref[d, 0].astype(BF16)) + bx_ref[d])
        log_a = -LRU_C * r * _softplus(ap_ref[d])
        a = jnp.exp(log_a)
        b = jnp.sqrt(1.0 - a * a) * (gi * xs)
        for s in (1, 2, 4):
            if d == 0:
                keep = rows8 >= s
                sa = jnp.where(keep, pltpu.roll(a, s, axis=0), 1.0)
                sb = jnp.where(keep, pltpu.roll(b, s, axis=0), 0.0)
            else:
                keep = rows8 < 8 - s
                sa = jnp.where(keep, pltpu.roll(a, tot - s, axis=0), 1.0)
                sb = jnp.where(keep, pltpu.roll(b, tot - s, axis=0), 0.0)
            b = a * sb + b
            a = a * sa
        a_s[d] = a
        b_s[d] = b

    def group_fwd(i, c):
        r0 = pl.multiple_of(i * 8, 8)
        hg = a_s[0, pl.ds(r0, 8), :] * c + b_s[0, pl.ds(r0, 8), :]
        h_s[pl.ds(r0, 8), :] = hg
        return jnp.broadcast_to(hg[7:8, :], hg.shape)

    def group_bwd(i, c):
        gidx = jnp.where(i < ngrp_ctx, ngrp_ctx - 1 - i, ngrp + ngrp_ctx - 1 - i)
        r0 = pl.multiple_of(gidx * 8, 8)
        hg = a_s[1, pl.ds(r0, 8), :] * c + b_s[1, pl.ds(r0, 8), :]
        h_s[pl.ds(r0, 8), :] += hg
        return jnp.broadcast_to(hg[0:1, :], hg.shape)

    zero = jnp.zeros((8, xs.shape[1]), F32)
    lax.fori_loop(0, ngrp, group_fwd, zero, unroll=4)
    lax.fori_loop(0, ngrp, group_bwd, zero, unroll=4)
    o_ref[0] = (h_s[ctx_len:, :] * _gelu_tanh(yl_ref[0])).astype(o_ref.dtype)


def _rglru(xb_l, yb_l, xb_c, conv_w, conv_b, wa, ba, wx, bx, a_param):
    bsz, lat_len, width = xb_l.shape
    ctx_len = xb_c.shape[1]
    tot = ctx_len + lat_len
    blk = width // LRU_HEADS

    def col(b, h):
        return (b, 0, h)

    def wcol(b, h):
        return (0, h)

    def w3(b, h):
        return (0, 0, h)

    return pl.pallas_call(
        functools.partial(_lru_kernel, ctx_len=ctx_len, lat_len=lat_len),
        grid=(bsz, LRU_HEADS),
        in_specs=[pl.BlockSpec((1, lat_len, blk), col),
                  pl.BlockSpec((1, lat_len, blk), col),
                  pl.BlockSpec((1, ctx_len, blk), col),
                  pl.BlockSpec((LRU_CONV, blk), wcol),
                  pl.BlockSpec((1, blk), wcol),
                  pl.BlockSpec((2, 1, blk, blk), lambda b, h: (0, h, 0, 0)),
                  pl.BlockSpec((2, 1, blk), w3),
                  pl.BlockSpec((2, 1, blk, blk), lambda b, h: (0, h, 0, 0)),
                  pl.BlockSpec((2, 1, blk), w3),
                  pl.BlockSpec((2, 1, blk), w3)],
        out_specs=pl.BlockSpec((1, lat_len, blk), col),
        out_shape=jax.ShapeDtypeStruct((bsz, lat_len, width), BF16),
        scratch_shapes=[pltpu.VMEM((tot, blk), F32),
                        pltpu.VMEM((2, tot, blk), F32),
                        pltpu.VMEM((2, tot, blk), F32),
                        pltpu.VMEM((tot, blk), F32)],
        compiler_params=_params("parallel", "parallel"),
        name="rglru",
    )(xb_l, yb_l, xb_c, conv_w, conv_b.reshape(1, width), wa, ba.reshape(2, 1, width), wx,
      bx.reshape(2, 1, width), a_param.reshape(2, 1, width))


def _post_mixer_kernel(*refs, n_in, has_bias):
    a_refs = refs[:n_in]
    w_refs = refs[n_in:2 * n_in]
    i = 2 * n_in
    b_ref = None
    if has_bias:
        b_ref = refs[i]
        i += 1
    x_ref, g1_ref, sc_ref, sh_ref, lng_ref, lnb_ref, rw_ref, rb_ref, base_ref = refs[i:i + 9]
    x1_ref, v_ref, route_ref, cnt_ref = refs[i + 9:]

    @pl.when((pl.program_id(0) == 0) & (pl.program_id(1) == 0))
    def _():
        cnt_ref[...] = base_ref[...]

    y = None
    for a_ref, w_ref in zip(a_refs, w_refs):
        t = _dot(a_ref[0], w_ref[...])
        y = t if y is None else y + t
    if has_bias:
        y = y + b_ref[...]
    x1 = _layer_norm(DEEPNORM_ALPHA * x_ref[0] + g1_ref[0] * y, lng_ref[...], lnb_ref[...])
    x1_ref[0] = x1
    v = x1 * (1.0 + sc_ref[0]) + sh_ref[0]
    v_ref[0] = _pack_bf16_pairs(v)
    logits = _dot3(v, rw_ref[...]) + rb_ref[...]
    tm = logits.shape[0]
    lane = lax.broadcasted_iota(jnp.int32, logits.shape, 1).astype(F32)
    work = logits
    picks, firsts = [], []
    m0 = None
    for kk in range(TOP_K):
        m = jnp.max(work, axis=-1, keepdims=True)
        if kk == 0:
            m0 = m
        first = jnp.min(jnp.where(work == m, lane, float(LANES)), axis=-1, keepdims=True)
        pick = lane == first
        picks.append(pick)
        firsts.append(first)
        work = jnp.where(pick, -jnp.inf, work)
    sel = jnp.where(picks[0] | picks[1] | picks[2] | picks[3], 1.0, 0.0)
    e = sel * jnp.exp(logits - m0)
    gate = e / jnp.sum(e, axis=-1, keepdims=True)
    ti = lax.broadcasted_iota(jnp.int32, (tm, tm), 0)
    tj = lax.broadcasted_iota(jnp.int32, (tm, tm), 1)
    before = jnp.where(ti > tj, 1.0, 0.0).astype(BF16)
    slot = _dot(before, sel.astype(BF16)) + cnt_ref[...]
    route = jnp.zeros(logits.shape, F32)
    for kk in range(TOP_K):
        rank = jnp.sum(jnp.where(picks[kk], slot, 0.0), axis=-1, keepdims=True)
        wgt = jnp.sum(jnp.where(picks[kk], gate, 0.0), axis=-1, keepdims=True)
        route = jnp.where(lane == float(kk), firsts[kk], route)
        route = jnp.where(lane == float(TOP_K + kk), rank, route)
        route = jnp.where(lane == float(2 * TOP_K + kk), wgt, route)
    route_ref[0] = route
    cnt_ref[...] += jnp.sum(sel, axis=0, keepdims=True)


def _post_mixer(acts, ws, bias, x, g1, sc2, sh2, ln_g, ln_b, router_w, router_b, base, tm):
    bsz, length, d = x.shape
    tm = min(tm, length)
    n_in = len(acts)

    def row(bi, i):
        return (bi, i, 0)

    def per_b(bi, i):
        return (bi, 0, 0)

    def const(bi, i):
        return (0, 0)

    in_specs = [pl.BlockSpec((1, tm, a.shape[2]), row) for a in acts]
    in_specs += [pl.BlockSpec(w.shape, const) for w in ws]
    args = list(acts) + list(ws)
    if bias is not None:
        in_specs.append(pl.BlockSpec((1, d), const))
        args.append(bias.reshape(1, d))
    in_specs += [pl.BlockSpec((1, tm, d), row),
                 pl.BlockSpec((1, 1, d), per_b), pl.BlockSpec((1, 1, d), per_b),
                 pl.BlockSpec((1, 1, d), per_b),
                 pl.BlockSpec((1, d), const), pl.BlockSpec((1, d), const),
                 pl.BlockSpec((d, LANES), const), pl.BlockSpec((1, LANES), const),
                 pl.BlockSpec((1, LANES), const)]
    rw = jnp.zeros((d, LANES), F32).at[:, :N_EXPERTS].set(router_w)
    rb = jnp.full((1, LANES), -1e30, F32).at[0, :N_EXPERTS].set(router_b)
    args += [x, g1, sc2, sh2, ln_g.reshape(1, d), ln_b.reshape(1, d), rw, rb, base]
    return pl.pallas_call(
        functools.partial(_post_mixer_kernel, n_in=n_in, has_bias=bias is not None),
        grid=(bsz, length // tm),
        in_specs=in_specs,
        out_specs=[pl.BlockSpec((1, tm, d), row), pl.BlockSpec((1, tm, d // 2), row),
                   pl.BlockSpec((1, tm, LANES), row), pl.BlockSpec((1, LANES), const)],
        out_shape=[jax.ShapeDtypeStruct((bsz, length, d), F32),
                   jax.ShapeDtypeStruct((bsz, length, d // 2), jnp.uint32),
                   jax.ShapeDtypeStruct((bsz, length, LANES), F32),
                   jax.ShapeDtypeStruct((1, LANES), F32)],
        compiler_params=_params("arbitrary", "arbitrary"),
        name="post_mixer",
    )(*args)


def _swiglu_interleaved(gu):
    nxt = pltpu.roll(gu, gu.shape[1] - 1, axis=1)
    glu = jnp.minimum(gu, SWIGLU_LIMIT)
    lin = jnp.clip(nxt, -SWIGLU_LIMIT, SWIGLU_LIMIT)
    act = glu * _sigmoid(SWIGLU_ALPHA * glu) * (lin + 1.0)
    lane = lax.broadcasted_iota(jnp.int32, gu.shape, 1)
    return jnp.where(lane % 2 == 0, act, 0.0)


def _moe_rows_kernel(te_ref, nv_ref, first_ref, xs_ref, w1_ref, b1_ref, w2_ref, b2_ref, ys_ref,
                     w1b_s, w2z_s):
    del te_ref
    i = pl.program_id(0)
    nv = nv_ref[i]

    @pl.when(first_ref[i] == 1)
    def _():
        w1b_s[...] = w1_ref[0, 0].astype(BF16)
        bits = pltpu.bitcast(w2_ref[0, 0].astype(BF16).astype(F32), jnp.uint32) >> 16
        w2z_s[...] = pltpu.bitcast(bits, BF16)

    @pl.when(nv > 0)
    def _():
        rows = lax.broadcasted_iota(jnp.int32, xs_ref.shape, 0)
        x = _unpack_bf16_pairs(jnp.where(rows < nv, xs_ref[...], jnp.uint32(0))).astype(BF16)
        act = _swiglu_interleaved(_dot(x, w1b_s[...]) + b1_ref[0, 0])
        ys_ref[...] = _pack_bf16_pairs(_dot(act.astype(BF16), w2z_s[...]) + b2_ref[0, 0])

    @pl.when(nv == 0)
    def _():
        ys_ref[...] = jnp.zeros_like(ys_ref)


def _moe_rows(xs, tile_expert, tile_rows, tile_first, layer, w1, b1, w2, b2, tm):
    n_rows, dh = xs.shape
    _, _, d, dff2 = w1.shape

    def row(i, te, nv, first):
        return (i, 0)

    def exp4(i, te, nv, first):
        return (layer, te[i], 0, 0)

    return pl.pallas_call(
        _moe_rows_kernel,
        grid_spec=pltpu.PrefetchScalarGridSpec(
            num_scalar_prefetch=3,
            grid=(n_rows // tm,),
            in_specs=[pl.BlockSpec((tm, dh), row),
                      pl.BlockSpec((1, 1, d, dff2), exp4), pl.BlockSpec((1, 1, 1, dff2), exp4),
                      pl.BlockSpec((1, 1, dff2 // 2, d), exp4), pl.BlockSpec((1, 1, 1, d), exp4)],
            out_specs=pl.BlockSpec((tm, dh), row),
            scratch_shapes=[pltpu.VMEM((d, dff2), BF16), pltpu.VMEM((dff2, d), BF16)]),
        out_shape=jax.ShapeDtypeStruct((n_rows, dh), jnp.uint32),
        compiler_params=_params("arbitrary"),
        name="moe_rows",
    )(tile_expert, tile_rows, tile_first, xs, w1, b1, w2, b2)


def _moe_combine_kernel(y0_ref, y1_ref, y2_ref, y3_ref, route_ref, x1_ref, g2_ref, lng_ref, lnb_ref,
                        o_ref):
    route = route_ref[...]
    lane = lax.broadcasted_iota(jnp.int32, route.shape, 1)
    f = None
    for kk, y_ref in enumerate((y0_ref, y1_ref, y2_ref, y3_ref)):
        wgt = jnp.sum(jnp.where(lane == 2 * TOP_K + kk, route, 0.0), axis=-1, keepdims=True)
        term = wgt * _unpack_bf16_pairs(y_ref[0])
        f = term if f is None else f + term
    o_ref[0] = _layer_norm(DEEPNORM_ALPHA * x1_ref[0] + g2_ref[0] * f, lng_ref[...], lnb_ref[...])


def _moe_combine(yg, route, row_offset, x1, g2, ln_g, ln_b, tm):
    bsz, length, d = x1.shape
    tm = min(tm, length)
    nt = length // tm
    off = row_offset // tm

    def pick(kk):
        return lambda bi, i: (kk, off + bi * nt + i, 0)

    def const(bi, i):
        return (0, 0)

    return pl.pallas_call(
        _moe_combine_kernel,
        grid=(bsz, nt),
        in_specs=[pl.BlockSpec((1, tm, d // 2), pick(kk)) for kk in range(TOP_K)] + [
            pl.BlockSpec((tm, LANES), lambda bi, i: (off + bi * nt + i, 0)),
            pl.BlockSpec((1, tm, d), lambda bi, i: (bi, i, 0)),
            pl.BlockSpec((1, 1, d), lambda bi, i: (bi, 0, 0)),
            pl.BlockSpec((1, d), const), pl.BlockSpec((1, d), const)],
        out_specs=pl.BlockSpec((1, tm, d), lambda bi, i: (bi, i, 0)),
        out_shape=jax.ShapeDtypeStruct((bsz, length, d), F32),
        compiler_params=_params("parallel", "parallel"),
        name="moe_combine",
    )(yg, yg, yg, yg, route, x1, g2, ln_g.reshape(1, d), ln_b.reshape(1, d))


SC_CORES = 2
SC_SUBCORES = 16
SC_WORKERS = SC_CORES * SC_SUBCORES
SC_WINDOW = 64


def _sc_row_pipeline(nwin, read, write):
    read(0, 0).start()

    @pl.loop(0, nwin, step=2)
    def _(w0):
        for b in range(2):
            w = w0 + b

            @pl.when(w + 1 < nwin)
            def _():
                @pl.when(w >= 1)
                def _():
                    write(w - 1, 1 - b).wait()

                read(w + 1, 1 - b).start()

            read(w, b).wait()
            write(w, b).start()

    write(nwin - 2, 0).wait()
    write(nwin - 1, 1).wait()


def _sc_scatter_rows(src, pos, n_out):
    t_rows, d = src.shape
    nw, nwin, win = pos.shape
    assert nw == SC_WORKERS and nwin % 2 == 0 and t_rows % (nwin * win) == 0
    mesh = plsc.VectorSubcoreMesh(core_axis_name="c", subcore_axis_name="s")

    @functools.partial(
        pl.kernel, mesh=mesh, out_type=jax.ShapeDtypeStruct((n_out, d), src.dtype),
        scratch_types=[pltpu.VMEM((nwin, win), jnp.int32), pltpu.VMEM((2, win, d), src.dtype),
                       pltpu.SemaphoreType.DMA((2,)), pltpu.SemaphoreType.DMA((2,))])
    def scatter(src_hbm, pos_hbm, out_hbm, idx_v, rows_v, rsem, wsem):
        wid = lax.axis_index("s") * SC_CORES + lax.axis_index("c")
        t0 = lax.rem(wid * (nwin * win), t_rows)
        pltpu.sync_copy(pos_hbm.at[wid], idx_v)

        def read(w, slot):
            return pltpu.make_async_copy(src_hbm.at[pl.ds(t0 + w * win, win)], rows_v.at[slot],
                                         rsem.at[slot])

        def write(w, slot):
            return pltpu.make_async_copy(rows_v.at[slot], out_hbm.at[idx_v.at[w]], wsem.at[slot])

        _sc_row_pipeline(nwin, read, write)

    return scatter(src, pos)


def _sc_gather_rows(table, pos):
    _, d = table.shape
    nw, nwin, win = pos.shape
    assert nw == SC_WORKERS and nwin % 2 == 0
    per = nwin * win
    mesh = plsc.VectorSubcoreMesh(core_axis_name="c", subcore_axis_name="s")

    @functools.partial(
        pl.kernel, mesh=mesh, out_type=jax.ShapeDtypeStruct((nw * per, d), table.dtype),
        scratch_types=[pltpu.VMEM((nwin, win), jnp.int32), pltpu.VMEM((2, win, d), table.dtype),
                       pltpu.SemaphoreType.DMA((2,)), pltpu.SemaphoreType.DMA((2,))])
    def gather(table_hbm, pos_hbm, out_hbm, idx_v, rows_v, rsem, wsem):
        wid = lax.axis_index("s") * SC_CORES + lax.axis_index("c")
        base = wid * per
        pltpu.sync_copy(pos_hbm.at[wid], idx_v)

        def read(w, slot):
            return pltpu.make_async_copy(table_hbm.at[idx_v.at[w]], rows_v.at[slot], rsem.at[slot])

        def write(w, slot):
            return pltpu.make_async_copy(rows_v.at[slot], out_hbm.at[pl.ds(base + w * win, win)],
                                         wsem.at[slot])

        _sc_row_pipeline(nwin, read, write)

    return gather(table, pos)


MOE_TILE = 256


def _moe_sparse(v_all, route, counts, layer, w1, b1, w2, b2):
    t_rows, dh = v_all.shape
    n_exp = w1.shape[1]
    pairs = TOP_K * t_rows
    n_tiles = pairs // MOE_TILE + n_exp
    expert = route[:, 0:TOP_K].astype(jnp.int32)
    slot = route[:, TOP_K:2 * TOP_K].astype(jnp.int32)
    cnt = counts[0, :n_exp].astype(jnp.int32)
    tiles_per = (cnt + MOE_TILE - 1) // MOE_TILE
    tile_end = jnp.cumsum(tiles_per)
    tile_start = tile_end - tiles_per
    pos = (tile_start * MOE_TILE)[expert] + slot
    nwin = pairs // (SC_WORKERS * SC_WINDOW)
    pos_km = pos.T.reshape(SC_WORKERS, nwin, SC_WINDOW)
    tile_ids = jnp.arange(n_tiles, dtype=jnp.int32)[:, None]
    owns = (tile_ids >= tile_start[None, :]) & (tile_ids < tile_end[None, :])
    experts = jnp.arange(n_exp, dtype=jnp.int32)[None, :]
    last_used = jnp.max(jnp.where(tiles_per > 0, experts[0], 0))
    used = jnp.any(owns, axis=1)
    te = jnp.where(used, jnp.sum(jnp.where(owns, experts, 0), axis=1), last_used).astype(jnp.int32)
    rows_left = cnt[None, :] - (tile_ids - tile_start[None, :]) * MOE_TILE
    tile_rows = jnp.sum(jnp.where(owns, jnp.clip(rows_left, 0, MOE_TILE), 0), axis=1)
    tile_rows = tile_rows.astype(jnp.int32)
    tile_first = jnp.any(owns & (tile_ids == tile_start[None, :]), axis=1).astype(jnp.int32)
    xs = _sc_scatter_rows(v_all, pos_km, n_tiles * MOE_TILE)
    ys = _moe_rows(xs, te, tile_rows, tile_first, layer, w1, b1, w2, b2, MOE_TILE)
    return _sc_gather_rows(ys, pos_km).reshape(TOP_K, t_rows, dh)


@functools.lru_cache(maxsize=None)
def _sincos_2d(rows, cols, dim):
    quarter = dim // 4
    omega = 1.0 / (10000.0 ** (np.arange(quarter, dtype=np.float64) / quarter))

    def emb1d(n):
        ang = np.arange(n, dtype=np.float64)[:, None] * omega
        return np.concatenate([np.sin(ang), np.cos(ang)], axis=-1)

    er = np.broadcast_to(emb1d(rows)[:, None], (rows, cols, dim // 2))
    ec = np.broadcast_to(emb1d(cols)[None], (rows, cols, dim // 2))
    return np.concatenate([er, ec], axis=-1).reshape(rows * cols, dim).astype(np.float32)


def _gate_layout(gates, n_ch):
    bsz, length, _ = gates.shape
    g = gates[..., :n_ch].reshape(bsz, length // DN_CHUNK, DN_CHUNK, n_ch)
    return g.transpose(0, 3, 1, 2)


def kernel(x, c, ctx, c_ctx, ada_w, ada_b, ln_g, ln_b, ev_w_in, ev_w_out, dn_conv_w, dn_a_log, dn_dt_bias, dn_norm_g, hy_conv_w, hy_conv_b, hy_w_in, hy_b_in, hy_w_mid, hy_b_mid, hy_w_out, hy_freq, hy_skip, od_w_in, od_b_in, lru_conv_w, lru_conv_b, lru_wa, lru_ba, lru_wx, lru_bx, lru_a_param, od_w_out, od_b_out, router_w, router_b, moe_w1, moe_b1, moe_w2, moe_b2):
    bsz, length, d = x.shape
    ctx_len = ctx.shape[1]
    pos = jnp.asarray(_sincos_2d(length // GRID_W, GRID_W, d))

    cond = jnp.zeros((16, d), F32).at[:bsz].set(c).at[bsz].set(c_ctx)
    mod = _modulation(cond, ada_w, ada_b).reshape(DEPTH, 16, 6, d)

    def lat_mod(layer, k):
        return mod[layer, :bsz, k][:, None, :]

    def ctx_mod(layer, k):
        return jnp.broadcast_to(mod[layer, bsz, k][None, None, :], (bsz, 1, d))

    hc = ctx
    for layer in range(DEPTH):
        last = layer == DEPTH - 1
        j = layer // 2
        if layer % 2 == 0:
            dn_qk = DN_HEADS * DN_DK
            dn_qkv = 3 * dn_qk
            dn_in = dn_qkv + dn_qk + 4 * DN_HEADS
            w_in = ev_w_in[j]
            gate_w = jnp.zeros((d, LANES), F32).at[:, :4 * DN_HEADS].set(w_in[:, dn_qkv + dn_qk:dn_in])
            w_cat = jnp.concatenate([w_in[:, :dn_qkv + dn_qk], gate_w, w_in[:, dn_in:]],
                                    axis=1).astype(BF16)
            hy_in = w_in.shape[1] - dn_in
            widths = (dn_qkv, dn_qk, LANES, hy_in)
            qkv_l, z_l, gt_l, phy_l = _inproj(x, lat_mod(layer, 1), lat_mod(layer, 0), pos,
                                               w_cat, None, widths, 512)
            qkv_c, z_c, gt_c, phy_c = _inproj(hc, ctx_mod(layer, 1), ctx_mod(layer, 0), None,
                                               w_cat, None, widths, 512)
            gates = jnp.concatenate([_gate_layout(gt_c, 4 * DN_HEADS),
                                     _gate_layout(gt_l, 4 * DN_HEADS)], axis=2)
            n_chunks = gates.shape[2]
            gates = jnp.pad(gates, ((0, 0), (0, 0), (0, -n_chunks % 16), (0, 0)))
            dn_l, dn_c = _deltanet(qkv_l, z_l, qkv_c, z_c, gates, dn_conv_w[j], dn_a_log[j],
                                   dn_dt_bias[j], dn_norm_g[j])
            filt = (hy_w_in[j], hy_b_in[j], hy_w_mid[j], hy_b_mid[j], hy_w_out[j], hy_freq[j])
            hy_l = _hyena(phy_l, hy_conv_w[j], hy_conv_b[j], filt, hy_skip[j])
            w_out = ev_w_out[j].astype(BF16)
            half = dn_l.shape[2]
            acts_l, ws, bias = (dn_l, hy_l), (w_out[:half], w_out[half:]), None
            acts_c = None
            if not last:
                hy_c = _hyena(phy_c, hy_conv_w[j], hy_conv_b[j], filt, hy_skip[j])
                acts_c = (dn_c, hy_c)
        else:
            w_in = od_w_in[j].astype(BF16)
            width = w_in.shape[1] // 2
            b_in = od_b_in[j].reshape(1, 2 * width)
            xb_l, yb_l = _inproj(x, lat_mod(layer, 1), lat_mod(layer, 0), pos, w_in, b_in,
                                 (width, width), 512)
            xb_c, _ = _inproj(hc, ctx_mod(layer, 1), ctx_mod(layer, 0), None, w_in, b_in,
                              (width, width), 512)
            act_l = _rglru(xb_l, yb_l, xb_c, lru_conv_w[j], lru_conv_b[j], lru_wa[j], lru_ba[j],
                           lru_wx[j], lru_bx[j], lru_a_param[j])
            acts_l, ws, bias = (act_l,), (od_w_out[j].astype(BF16),), od_b_out[j]
            acts_c = None
            assert last, "context outputs of the RG-LRU layer are only needed before the last layer"


        x1, v, route, counts = _post_mixer(acts_l, ws, bias, x, lat_mod(layer, 2),
                                           lat_mod(layer, 4), lat_mod(layer, 3), ln_g[layer, 0],
                                           ln_b[layer, 0], router_w[layer], router_b[layer],
                                           jnp.zeros((1, LANES), F32), 256)
        v_all = v.reshape(bsz * length, d // 2)
        route = route.reshape(bsz * length, LANES)
        if not last:
            hc1, vc, route_c, counts = _post_mixer(acts_c, ws, bias, hc, ctx_mod(layer, 2),
                                                   ctx_mod(layer, 4), ctx_mod(layer, 3),
                                                   ln_g[layer, 0], ln_b[layer, 0], router_w[layer],
                                                   router_b[layer], counts, 256)
            v_all = jnp.concatenate([v_all, vc.reshape(bsz * ctx_len, d // 2)], axis=0)
            route = jnp.concatenate([route, route_c.reshape(bsz * ctx_len, LANES)], axis=0)
        yg = _moe_sparse(v_all, route, counts, layer, moe_w1, moe_b1[:, :, None, :], moe_w2,
                         moe_b2[:, :, None, :])
        x = _moe_combine(yg, route, 0, x1, lat_mod(layer, 5), ln_g[layer, 1], ln_b[layer, 1], 256)
        if not last:
            hc = _moe_combine(yg, route, bsz * length, hc1, ctx_mod(layer, 5), ln_g[layer, 1],
                              ln_b[layer, 1], 256)
    return x
```

```python
import functools
import math

import numpy as np
import jax
import jax.numpy as jnp
from jax import lax
from jax.experimental import pallas as pl
from jax.experimental.pallas import tpu as pltpu
from jax.experimental.pallas import tpu_sc as plsc

F32 = jnp.float32
BF16 = jnp.bfloat16

VMEM_LIMIT_BYTES = 56 * 1024 * 1024
LANES = 128

DEPTH = 2
GRID_W = 64
DEEPNORM_ALPHA = (2.0 * DEPTH) ** 0.25
LN_EPS = 1e-5
RMS_EPS = 1e-6

DN_HEADS = 4
DN_DK = 128
DN_CHUNK = 64
DN_CONV = 4

HY_EMB = 33
HY_TARGET = 1e-2
HY_MIN_DECAY = math.log(HY_TARGET) / 1.5
HY_MAX_DECAY = math.log(HY_TARGET) / 0.3
HY_CONV = 3

LRU_HEADS = 4
LRU_C = 8.0
LRU_CONV = 4

N_EXPERTS = 32
TOP_K = 4
SWIGLU_ALPHA = 1.702
SWIGLU_LIMIT = 7.0


def _params(*sem):
    return pltpu.CompilerParams(dimension_semantics=sem, vmem_limit_bytes=VMEM_LIMIT_BYTES)


def _dot(a, b):
    return jnp.dot(a, b, preferred_element_type=F32)


def _dot_nt(a, b):
    return lax.dot_general(a, b, (((1,), (1,)), ((), ())), preferred_element_type=F32)


def _split(a):
    hi = a.astype(BF16)
    lo = (a - hi.astype(F32)).astype(BF16)
    return hi, lo


def _dot3(a, b):
    ah, al = _split(a)
    bh, bl = _split(b)
    return _dot(ah, bh) + _dot(ah, bl) + _dot(al, bh)


def _silu(x):
    return x * (1.0 / (1.0 + jnp.exp(-x)))


def _sigmoid(x):
    return 1.0 / (1.0 + jnp.exp(-x))


def _softplus(x):
    return jnp.maximum(x, 0.0) + jnp.log(1.0 + jnp.exp(-jnp.abs(x)))


def _layer_norm(x, g, b):
    mu = jnp.mean(x, axis=-1, keepdims=True)
    xc = x - mu
    var = jnp.mean(xc * xc, axis=-1, keepdims=True)
    return xc * lax.rsqrt(var + LN_EPS) * g + b


def _pack_bf16_pairs(x):
    w = x.shape[1] // 2
    lo = pltpu.bitcast(x[:, :w].astype(BF16).astype(F32), jnp.uint32) >> 16
    hi = pltpu.bitcast(x[:, w:].astype(BF16).astype(F32), jnp.uint32) & jnp.uint32(0xFFFF0000)
    return lo | hi


def _unpack_bf16_pairs(p):
    lo = pltpu.bitcast(p << 16, F32)
    hi = pltpu.bitcast(p & jnp.uint32(0xFFFF0000), F32)
    return jnp.concatenate([lo, hi], axis=1)


def _shift_rows(x, s):
    if s == 0:
        return x
    n = x.shape[0]
    rows = lax.broadcasted_iota(jnp.int32, x.shape, 0)
    valid = (rows >= s) if s > 0 else (rows < n + s)
    return jnp.where(valid, pltpu.roll(x, s % n, axis=0), 0.0)


def _depthwise_conv(x, w, pad_left):
    acc = None
    for i in range(w.shape[0]):
        term = _shift_rows(x, pad_left - i) * w[i:i + 1, :]
        acc = term if acc is None else acc + term
    return acc


def _mod_kernel(c_ref, w_ref, b_ref, o_ref):
    o_ref[0] = _dot3(_silu(c_ref[...]), w_ref[0]) + b_ref[0]


def _modulation(cond, ada_w, ada_b):
    depth, d, n = ada_w.shape
    rows = cond.shape[0]
    tn = 1536
    return pl.pallas_call(
        _mod_kernel,
        grid=(depth, n // tn),
        in_specs=[
            pl.BlockSpec((rows, d), lambda l, j: (0, 0)),
            pl.BlockSpec((1, d, tn), lambda l, j: (l, 0, j)),
            pl.BlockSpec((1, 1, tn), lambda l, j: (l, 0, j)),
        ],
        out_specs=pl.BlockSpec((1, rows, tn), lambda l, j: (l, 0, j)),
        out_shape=jax.ShapeDtypeStruct((depth, rows, n), F32),
        compiler_params=_params("parallel", "parallel"),
        name="modulation",
    )(cond, ada_w, ada_b.reshape(depth, 1, n))


def _inproj_kernel(*refs, splits, has_pos, has_bias):
    x_ref, sc_ref, sh_ref = refs[:3]
    i = 3
    pos_ref = None
    if has_pos:
        pos_ref = refs[i]
        i += 1
    w_ref = refs[i]
    i += 1
    b_ref = None
    if has_bias:
        b_ref = refs[i]
        i += 1
    o_refs = refs[i:]
    u = x_ref[0] * (1.0 + sc_ref[0]) + sh_ref[0]
    if has_pos:
        u = u + pos_ref[...]
    ub = u.astype(BF16)
    for o_ref, (s, e) in zip(o_refs, splits):
        acc = _dot(ub, w_ref[:, s:e])
        if has_bias:
            acc = acc + b_ref[:, s:e]
        o_ref[0] = acc


def _inproj(x, sc, sh, pos, w, b, widths, tm):
    bsz, length, d = x.shape
    n = w.shape[1]
    splits, s = [], 0
    for wd in widths:
        splits.append((s, s + wd))
        s += wd
    assert s == n
    tm = min(tm, length)
    in_specs = [
        pl.BlockSpec((1, tm, d), lambda bi, i: (bi, i, 0)),
        pl.BlockSpec((1, 1, d), lambda bi, i: (bi, 0, 0)),
        pl.BlockSpec((1, 1, d), lambda bi, i: (bi, 0, 0)),
    ]
    args = [x, sc, sh]
    if pos is not None:
        in_specs.append(pl.BlockSpec((tm, d), lambda bi, i: (i, 0)))
        args.append(pos)
    in_specs.append(pl.BlockSpec((d, n), lambda bi, i: (0, 0)))
    args.append(w)
    if b is not None:
        in_specs.append(pl.BlockSpec((1, n), lambda bi, i: (0, 0)))
        args.append(b)
    return pl.pallas_call(
        functools.partial(_inproj_kernel, splits=tuple(splits), has_pos=pos is not None,
                          has_bias=b is not None),
        grid=(bsz, length // tm),
        in_specs=in_specs,
        out_specs=[pl.BlockSpec((1, tm, wd), lambda bi, i: (bi, i, 0)) for wd in widths],
        out_shape=[jax.ShapeDtypeStruct((bsz, length, wd), F32) for wd in widths],
        compiler_params=_params("parallel", "parallel"),
        name="inproj",
    )(*args)


def _unit_tri_inverses(mats, lower):
    n = mats[0].shape[0]
    nb = 16
    np_ = len(mats)
    ii = lax.broadcasted_iota(jnp.int32, (n, n), 0)
    jj = lax.broadcasted_iota(jnp.int32, (n, n), 1)
    same16 = (ii // nb) == (jj // nb)
    same32 = (ii // (2 * nb)) == (jj // (2 * nb))
    dgs = []
    for a in mats:
        ad = jnp.where(same16, a, 0.0)
        dgs.append(ad[0:nb] + ad[nb:2 * nb] + ad[2 * nb:3 * nb] + ad[3 * nb:4 * nb])
    dg = jnp.concatenate(dgs, axis=0)
    rr = lax.broadcasted_iota(jnp.int32, dg.shape, 0)
    ll = lax.broadcasted_iota(jnp.int32, dg.shape, 1)
    xd = jnp.where(rr % nb == ll % nb, 1.0, 0.0)
    blk0 = (ll // nb) * nb
    for s in (range(nb - 1) if lower else range(nb - 1, 0, -1)):
        col = jnp.take_along_axis(dg, blk0 + s, axis=1)
        row = jnp.concatenate(
            [jnp.broadcast_to(xd[p * nb + s:p * nb + s + 1, :], (nb, n)) for p in range(np_)], axis=0)
        xd = xd - col * row
    ds = [jnp.where(same16, jnp.concatenate([xd[p * nb:(p + 1) * nb]] * (n // nb), axis=0), 0.0)
          for p in range(np_)]
    lvl1 = same32 & jnp.logical_not(same16)
    t1 = [_dot3(d, jnp.where(lvl1, a, 0.0)) for d, a in zip(ds, mats)]
    x1 = [d - _dot3(t, d) for d, t in zip(ds, t1)]
    t2 = [_dot3(x, jnp.where(same32, 0.0, a)) for x, a in zip(x1, mats)]
    return [x - _dot3(t, x) for x, t in zip(x1, t2)]


def _dn_kernel(alog_ref, dtb_ref,
               ql_ref, kl_ref, vl_ref, zl_ref, qc_ref, kc_ref, vc_ref, zc_ref, gt_ref,
               cwq_ref, cwk_ref, cwv_ref, ng_ref,
               yl_ref, yc_ref,
               qn_s, kn_s, vn_s, gc_s, bt_s, nq_s, c_s, gl_s, o_s,
               *, ctx_len, lat_len):
    h = pl.program_id(1)
    csz = DN_CHUNK
    nc_ctx = ctx_len // csz
    nc = (ctx_len + lat_len) // csz

    def prep(src_ref, cw_ref, kind):
        t = _silu(_depthwise_conv(src_ref[0], cw_ref[...], DN_CONV // 2))
        if kind == "v":
            return t
        t = t * lax.rsqrt(jnp.sum(t * t, axis=-1, keepdims=True) + RMS_EPS)
        return t * (DN_DK ** -0.5) if kind == "q" else t

    qn_s[0:ctx_len, :] = prep(qc_ref, cwq_ref, "q")
    qn_s[ctx_len:, :] = prep(ql_ref, cwq_ref, "q")
    kn_s[0:ctx_len, :] = prep(kc_ref, cwk_ref, "k")
    kn_s[ctx_len:, :] = prep(kl_ref, cwk_ref, "k")
    vn_s[0:ctx_len, :] = prep(vc_ref, cwv_ref, "v")
    vn_s[ctx_len:, :] = prep(vl_ref, cwv_ref, "v")

    ii = lax.broadcasted_iota(jnp.int32, (csz, csz), 0)
    jj = lax.broadcasted_iota(jnp.int32, (csz, csz), 1)
    eye = ii == jj
    for d in range(2):
        graw = gt_ref[0, d * 2 * DN_HEADS + h]
        braw = gt_ref[0, d * 2 * DN_HEADS + DN_HEADS + h]
        a_neg = -jnp.exp(jnp.zeros_like(graw) + alog_ref[d, h])
        g = a_neg * _softplus(graw + dtb_ref[d, h])
        tri = jnp.where((ii <= jj) if d == 0 else (ii >= jj), 1.0, 0.0).astype(BF16)
        g1 = g.astype(BF16)
        r1 = g - g1.astype(F32)
        g2 = r1.astype(BF16)
        g3 = (r1 - g2.astype(F32)).astype(BF16)
        gc_s[d] = _dot(g1, tri) + _dot(g2, tri) + _dot(g3, tri)
        bt_s[d] = _sigmoid(braw)

    o_s[...] = jnp.zeros_like(o_s)

    group = 6
    assert nc % group == 0

    def chunk_prep(gi, carry):
        ns = [gi * group + c for c in range(group)]
        r0s = [pl.multiple_of(n * csz, csz) for n in ns]
        qs = [qn_s[pl.ds(r0, csz), :] for r0 in r0s]
        ks = [kn_s[pl.ds(r0, csz), :] for r0 in r0s]
        vs = [vn_s[pl.ds(r0, csz), :] for r0 in r0s]
        kbfs = [k.astype(BF16) for k in ks]
        qks = [_dot_nt(q.astype(BF16), kbf) for q, kbf in zip(qs, kbfs)]
        for d in range(2):
            incl = (ii >= jj) if d == 0 else (ii <= jj)
            strict = (ii > jj) if d == 0 else (ii < jj)
            grs = [gc_s[d, pl.ds(n, 1), :] for n in ns]
            grows = [jnp.broadcast_to(gr, (csz, csz)) for gr in grs]
            gcols = [jnp.sum(jnp.where(eye, grow, 0.0), axis=1, keepdims=True) for grow in grows]
            bcols = [jnp.sum(jnp.where(eye, jnp.broadcast_to(bt_s[d, pl.ds(n, 1), :], (csz, csz)),
                                       0.0), axis=1, keepdims=True) for n in ns]
            decays = [jnp.where(incl, jnp.exp(jnp.where(incl, gcol - grow, 0.0)), 0.0)
                      for gcol, grow in zip(gcols, grows)]
            kbs = [k * bcol for k, bcol in zip(ks, bcols)]
            amats = [jnp.where(strict, _dot_nt(kb.astype(BF16), kbf) * decay, 0.0)
                     for kb, kbf, decay in zip(kbs, kbfs, decays)]
            tbs = [t.astype(BF16) for t in _unit_tri_inverses(amats, lower=(d == 0))]
            cs = range(group)
            egs = [jnp.exp(gcols[c]) for c in cs]
            ubs = [_dot(tbs[c], (vs[c] * bcols[c]).astype(BF16)).astype(BF16) for c in cs]
            wbs = [_dot(tbs[c], (kbs[c] * egs[c]).astype(BF16)).astype(BF16) for c in cs]
            attns = [jnp.where(incl, qks[c] * decays[c], 0.0).astype(BF16) for c in cs]
            glasts = [grs[c][:, csz - 1:csz] if d == 0 else grs[c][:, 0:1] for c in cs]
            kdts = [(ks[c] * jnp.exp(glasts[c] - gcols[c])).T.astype(BF16) for c in cs]
            nmats = [_dot(kdts[c], wbs[c]).astype(BF16) for c in cs]
            qmats = [(qs[c] * egs[c] - _dot(attns[c], wbs[c])).astype(BF16) for c in cs]
            cmats = [_dot(kdts[c], ubs[c]) for c in cs]
            omats = [_dot(attns[c], ubs[c]) for c in cs]
            for c in cs:
                n = ns[c]
                nq_s[d, n, 0:DN_DK, :] = nmats[c]
                nq_s[d, n, DN_DK:DN_DK + csz, :] = qmats[c]
                c_s[d, n] = cmats[c]
                o_s[pl.ds(r0s[c], csz), :] += omats[c]
                gl_s[d, pl.ds(n, 1), :] = jnp.broadcast_to(jnp.exp(glasts[c]), (1, LANES))
        return carry

    lax.fori_loop(0, nc // group, chunk_prep, 0)

    def step(i, states):
        new_states = []
        for d in range(2):
            if d == 0:
                n = i
            else:
                n = jnp.where(i < nc_ctx, nc_ctx - 1 - i, nc + nc_ctx - 1 - i)
            r0 = pl.multiple_of(n * csz, csz)
            s = states[d]
            r = _dot(nq_s[d, n], s.astype(BF16))
            o_s[pl.ds(r0, csz), :] += r[DN_DK:DN_DK + csz]
            new_states.append(s * gl_s[d, pl.ds(n, 1), :] - r[0:DN_DK] + c_s[d, n])
        return tuple(new_states)

    zero = jnp.zeros((DN_DK, DN_DK), F32)
    lax.fori_loop(0, nc, step, (zero, zero))

    def gated_norm(o, z):
        o = o * lax.rsqrt(jnp.mean(o * o, axis=-1, keepdims=True) + RMS_EPS) * ng_ref[...]
        return (o * _silu(z)).astype(yl_ref.dtype)

    yc_ref[0] = gated_norm(o_s[0:ctx_len, :], zc_ref[0])
    yl_ref[0] = gated_norm(o_s[ctx_len:, :], zl_ref[0])


def _deltanet(qkv_l, z_l, qkv_c, z_c, gates, conv_w, a_log, dt_bias, norm_g):
    bsz, lat_len, _ = qkv_l.shape
    ctx_len = qkv_c.shape[1]
    tot = ctx_len + lat_len
    nc = tot // DN_CHUNK
    ncp = gates.shape[2]
    hd = DN_DK
    nh = DN_HEADS

    def col(off):
        return lambda b, h: (b, 0, off + h)

    def wcol(off):
        return lambda b, h: (0, off + h)

    smem = pl.BlockSpec(memory_space=pltpu.SMEM)
    in_specs = [
        smem, smem,
        pl.BlockSpec((1, lat_len, hd), col(0)),
        pl.BlockSpec((1, lat_len, hd), col(nh)),
        pl.BlockSpec((1, lat_len, hd), col(2 * nh)),
        pl.BlockSpec((1, lat_len, hd), col(0)),
        pl.BlockSpec((1, ctx_len, hd), col(0)),
        pl.BlockSpec((1, ctx_len, hd), col(nh)),
        pl.BlockSpec((1, ctx_len, hd), col(2 * nh)),
        pl.BlockSpec((1, ctx_len, hd), col(0)),
        pl.BlockSpec((1, 4 * nh, ncp, DN_CHUNK), lambda b, h: (b, 0, 0, 0)),
        pl.BlockSpec((DN_CONV, hd), wcol(0)),
        pl.BlockSpec((DN_CONV, hd), wcol(nh)),
        pl.BlockSpec((DN_CONV, hd), wcol(2 * nh)),
        pl.BlockSpec((1, hd), lambda b, h: (0, 0)),
    ]
    scratch = [
        pltpu.VMEM((tot, hd), F32), pltpu.VMEM((tot, hd), F32), pltpu.VMEM((tot, hd), F32),
        pltpu.VMEM((2, ncp, DN_CHUNK), F32), pltpu.VMEM((2, ncp, DN_CHUNK), F32),
        pltpu.VMEM((2, nc, hd + DN_CHUNK, hd), BF16),
        pltpu.VMEM((2, nc, hd, hd), F32),
        pltpu.VMEM((2, nc, LANES), F32),
        pltpu.VMEM((tot, hd), F32),
    ]
    return pl.pallas_call(
        functools.partial(_dn_kernel, ctx_len=ctx_len, lat_len=lat_len),
        grid=(bsz, nh),
        in_specs=in_specs,
        out_specs=[pl.BlockSpec((1, lat_len, hd), col(0)),
                   pl.BlockSpec((1, ctx_len, hd), col(0))],
        out_shape=[jax.ShapeDtypeStruct((bsz, lat_len, nh * hd), BF16),
                   jax.ShapeDtypeStruct((bsz, ctx_len, nh * hd), BF16)],
        scratch_shapes=scratch,
        compiler_params=_params("parallel", "parallel"),
        name="deltanet",
    )(a_log, dt_bias, qkv_l, qkv_l, qkv_l, z_l, qkv_c, qkv_c, qkv_c, z_c, gates,
      conv_w, conv_w, conv_w, norm_g.reshape(1, hd))


def _hy_filter_kernel(z_ref, win_ref, bin_ref, wmid_ref, bmid_ref, wout_ref, freq_ref, dec_ref,
                      o_ref):
    freq = freq_ref[...]
    hcur = jnp.sin(freq * (_dot3(z_ref[...], win_ref[...]) + bin_ref[...]))
    for i in range(wmid_ref.shape[0]):
        hcur = jnp.sin(freq * (_dot3(hcur, wmid_ref[i]) + bmid_ref[i]))
    o_ref[...] = _dot3(hcur, wout_ref[...]) * dec_ref[...]


def _hy_filter(z, w_in, b_in, w_mid, b_mid, w_out, freq, dec2):
    length = z.shape[0]
    n_out = w_out.shape[1]
    tl = min(256, length)

    def whole(a):
        return pl.BlockSpec(a.shape, lambda i: (0,) * a.ndim)

    return pl.pallas_call(
        _hy_filter_kernel,
        grid=(length // tl,),
        in_specs=[pl.BlockSpec((tl, z.shape[1]), lambda i: (i, 0)),
                  whole(w_in), whole(b_in), whole(w_mid), whole(b_mid), whole(w_out), whole(freq),
                  pl.BlockSpec((tl, n_out), lambda i: (i, 0))],
        out_specs=pl.BlockSpec((tl, n_out), lambda i: (i, 0)),
        out_shape=jax.ShapeDtypeStruct((length, n_out), F32),
        compiler_params=_params("parallel"),
        name="hyena_filter",
    )(z, w_in, b_in, w_mid, b_mid, w_out, freq, dec2)


def _filt_spec_kernel(fc_ref, fs_ref, h_ref, kc_ref, ks_ref, *, width):
    j = pl.program_id(0)
    hmat = h_ref[...]
    rows = lax.broadcasted_iota(jnp.int32, hmat.shape, 0)
    cols = lax.broadcasted_iota(jnp.int32, hmat.shape, 1)
    hmat = jnp.where((rows == 0) & (cols >= width), 0.0, hmat)
    hh, hl = _split(hmat)
    c = _dot(fc_ref[...], hh) + _dot(fc_ref[...], hl)
    s = _dot(fs_ref[...], hh) + _dot(fs_ref[...], hl)
    kc_ref[...] = c[:, :width] + c[:, width:]
    orow = lax.broadcasted_iota(jnp.int32, (c.shape[0], width), 0)
    sign = jnp.where((orow == 0) & (j == 0), 1.0, -1.0)
    ks_ref[...] = s[:, :width] + sign * s[:, width:]


def _filt_spec(fwd, hfilt, tf):
    length, two_w = hfilt.shape
    width = two_w // 2
    nt = length // tf
    return pl.pallas_call(
        functools.partial(_filt_spec_kernel, width=width),
        grid=(nt,),
        in_specs=[pl.BlockSpec((tf, length), lambda j: (j, 0)),
                  pl.BlockSpec((tf, length), lambda j: (nt + j, 0)),
                  pl.BlockSpec((length, two_w), lambda j: (0, 0))],
        out_specs=[pl.BlockSpec((tf, width), lambda j: (j, 0)),
                   pl.BlockSpec((tf, width), lambda j: (j, 0))],
        out_shape=[jax.ShapeDtypeStruct((length, width), F32)] * 2,
        compiler_params=_params("arbitrary"),
        name="hyena_filter_spectrum",
    )(fwd, fwd, hfilt)


def _hy_prep_kernel(x0_ref, x1_ref, v_ref, w0_ref, w1_ref, w2_ref, b0_ref, b1_ref, b2_ref,
                    x0o_ref, vvo_ref):
    x0 = _depthwise_conv(x0_ref[0], w0_ref[...], HY_CONV // 2) + b0_ref[...]
    x1 = _depthwise_conv(x1_ref[0], w1_ref[...], HY_CONV // 2) + b1_ref[...]
    v = _depthwise_conv(v_ref[0], w2_ref[...], HY_CONV // 2) + b2_ref[...]
    x0o_ref[0] = x0
    vvo_ref[0] = v * x1


def _hy_prep(p_hy, conv_w, conv_b):
    bsz, length, three_w = p_hy.shape
    width = three_w // 3
    nb = width // LANES

    def col(off):
        return lambda b, j: (b, 0, off + j)

    def wcol(off):
        return lambda b, j: (0, off + j)

    k = conv_w.shape[0]
    return pl.pallas_call(
        _hy_prep_kernel,
        grid=(bsz, nb),
        in_specs=[pl.BlockSpec((1, length, LANES), col(0)),
                  pl.BlockSpec((1, length, LANES), col(nb)),
                  pl.BlockSpec((1, length, LANES), col(2 * nb)),
                  pl.BlockSpec((k, LANES), wcol(0)),
                  pl.BlockSpec((k, LANES), wcol(nb)),
                  pl.BlockSpec((k, LANES), wcol(2 * nb)),
                  pl.BlockSpec((1, LANES), wcol(0)),
                  pl.BlockSpec((1, LANES), wcol(nb)),
                  pl.BlockSpec((1, LANES), wcol(2 * nb))],
        out_specs=[pl.BlockSpec((1, length, LANES), col(0)),
                   pl.BlockSpec((1, length, LANES), col(0))],
        out_shape=[jax.ShapeDtypeStruct((bsz, length, width), F32)] * 2,
        compiler_params=_params("parallel", "parallel"),
        name="hyena_prep",
    )(p_hy, p_hy, p_hy, conv_w, conv_w, conv_w, conv_b, conv_b, conv_b)


def _dft_fwd_kernel(fc_ref, fs_ref, v_ref, kc_ref, ks_ref, yc_ref, ys_ref, vb_s):
    j = pl.program_id(1)

    @pl.when(j == 0)
    def _():
        vb_s[...] = v_ref[0].astype(BF16)

    uc = _dot(fc_ref[...], vb_s[...])
    us = _dot(fs_ref[...], vb_s[...])
    kc = kc_ref[...]
    ks = ks_ref[...]
    rows = lax.broadcasted_iota(jnp.int32, uc.shape, 0)
    special = (rows == 0) & (j == 0)
    yc_ref[0] = (uc * kc - jnp.where(special, 0.0, us * ks)).astype(BF16)
    ys_ref[0] = jnp.where(special, us * ks, uc * ks + us * kc).astype(BF16)


def _dft_fwd(fwd, vv, kc, ks, tf):
    bsz, length, width = vv.shape
    nt = length // tf
    return pl.pallas_call(
        _dft_fwd_kernel,
        grid=(bsz, nt),
        in_specs=[pl.BlockSpec((tf, length), lambda b, j: (j, 0)),
                  pl.BlockSpec((tf, length), lambda b, j: (nt + j, 0)),
                  pl.BlockSpec((1, length, width), lambda b, j: (b, 0, 0)),
                  pl.BlockSpec((tf, width), lambda b, j: (j, 0)),
                  pl.BlockSpec((tf, width), lambda b, j: (j, 0))],
        out_specs=[pl.BlockSpec((1, tf, width), lambda b, j: (b, j, 0)),
                   pl.BlockSpec((1, tf, width), lambda b, j: (b, j, 0))],
        out_shape=[jax.ShapeDtypeStruct((bsz, length, width), BF16)] * 2,
        scratch_shapes=[pltpu.VMEM((length, width), BF16)],
        compiler_params=_params("parallel", "arbitrary"),
        name="hyena_dft_fwd",
    )(fwd, fwd, vv, kc, ks)


def _dft_inv_kernel(ic_ref, is_ref, yc_ref, ys_ref, vv_ref, x0_ref, skip_ref, o_ref):
    y = _dot(ic_ref[...], yc_ref[0]) + _dot(is_ref[...], ys_ref[0])
    o_ref[0] = (x0_ref[0] * (y + vv_ref[0] * skip_ref[...])).astype(o_ref.dtype)


def _dft_inv(inv, yc, ys, vv, x0, skip, tt):
    bsz, length, width = vv.shape
    nt = length // tt
    return pl.pallas_call(
        _dft_inv_kernel,
        grid=(bsz, nt),
        in_specs=[pl.BlockSpec((tt, length), lambda b, i: (i, 0)),
                  pl.BlockSpec((tt, length), lambda b, i: (i, 1)),
                  pl.BlockSpec((1, length, width), lambda b, i: (b, 0, 0)),
                  pl.BlockSpec((1, length, width), lambda b, i: (b, 0, 0)),
                  pl.BlockSpec((1, tt, width), lambda b, i: (b, i, 0)),
                  pl.BlockSpec((1, tt, width), lambda b, i: (b, i, 0)),
                  pl.BlockSpec((1, width), lambda b, i: (0, 0))],
        out_specs=pl.BlockSpec((1, tt, width), lambda b, i: (b, i, 0)),
        out_shape=jax.ShapeDtypeStruct((bsz, length, width), BF16),
        compiler_params=_params("parallel", "parallel"),
        name="hyena_dft_inv",
    )(inv, inv, yc, ys, vv, x0, skip)


@functools.lru_cache(maxsize=None)
def _hyena_tables(length):
    n2 = 2 * length
    t = np.linspace(0.0, 1.0, length)[:, None]
    bands = (HY_EMB - 1) // 2
    wpos = 2.0 * np.pi * np.arange(length)[:, None] / length
    fb = np.linspace(1e-4, bands - 1, bands)[None]
    z = np.concatenate([t, np.cos(fb * wpos), -np.sin(fb * wpos)], axis=-1)
    zpad = np.zeros((length, LANES))
    zpad[:, :HY_EMB] = z
    f = np.arange(length)[:, None]
    n = np.arange(length)[None, :]
    ang = 2.0 * np.pi * ((f * n) % n2) / n2
    cos_m = np.cos(ang)
    sin_m = np.sin(ang)
    sin_m[0, :] = np.cos(np.pi * np.arange(length))
    fwd = np.concatenate([cos_m, sin_m], axis=0)
    scale = np.full((1, n2), 2.0 / n2)
    scale[0, 0] = 1.0 / n2
    scale[0, length] = 1.0 / n2
    inv = fwd.T * scale
    return (zpad.astype(np.float32), t.astype(np.float32), fwd.astype(np.float32),
            inv.astype(np.float32))


def _hyena(p_hy, conv_w, conv_b, filt, skip):
    bsz, length, three_w = p_hy.shape
    width = three_w // 3
    w_in, b_in, w_mid, b_mid, w_out, freq = filt
    zpad, t, fwd, inv = _hyena_tables(length)
    deltas = np.abs(np.linspace(HY_MIN_DECAY, HY_MAX_DECAY, width))[None, :]
    dec = np.exp(-t.astype(np.float64) * deltas).astype(np.float32)
    dec2 = jnp.asarray(np.concatenate([dec, dec], axis=1))
    ffn = w_in.shape[1]
    w_in_pad = jnp.zeros((LANES, ffn), F32).at[:HY_EMB].set(w_in)
    hfilt = _hy_filter(jnp.asarray(zpad), w_in_pad, b_in.reshape(1, ffn), w_mid,
                       b_mid.reshape(-1, 1, ffn), w_out, freq.reshape(1, ffn), dec2)
    fwd_b = jnp.asarray(fwd).astype(BF16)
    inv_b = jnp.asarray(inv).astype(BF16)
    tf = min(256, length)
    kc, ks = _filt_spec(fwd_b, hfilt, tf)
    x0, vv = _hy_prep(p_hy, conv_w, conv_b.reshape(1, three_w))
    yc, ys = _dft_fwd(fwd_b, vv, kc, ks, tf)
    return _dft_inv(inv_b, yc, ys, vv, x0, skip.reshape(1, width), min(256, length))


def _gelu_tanh(x):
    return 0.5 * x * (1.0 + jnp.tanh(math.sqrt(2.0 / math.pi) * (x + 0.044715 * x * x * x)))


def _lru_kernel(xl_ref, yl_ref, xc_ref, cw_ref, cb_ref, wa_ref, ba_ref, wx_ref, bx_ref, ap_ref,
                o_ref, xs_s, a_s, b_s, h_s, *, ctx_len, lat_len):
    tot = ctx_len + lat_len
    ngrp = tot // 8
    ngrp_ctx = ctx_len // 8
    xs_s[0:ctx_len, :] = _depthwise_conv(xc_ref[0], cw_ref[...], LRU_CONV // 2) + cb_ref[...]
    xs_s[ctx_len:, :] = _depthwise_conv(xl_ref[0], cw_ref[...], LRU_CONV // 2) + cb_ref[...]
    xs = xs_s[...]
    xsb = xs.astype(BF16)
    rows8 = lax.broadcasted_iota(jnp.int32, xs.shape, 0) % 8
    for d in range(2):
        r = _sigmoid(_dot(xsb, wa_ref[d, 0].astype(BF16)) + ba_ref[d])
        gi = _sigmoid(_dot(xsb, wx_ref[d, 0].astype(BF16)) + bx_ref[d])
        log_a = -LRU_C * r * _softplus(ap_ref[d])
        a = jnp.exp(log_a)
        b = jnp.sqrt(1.0 - a * a) * (gi * xs)
        for s in (1, 2, 4):
            if d == 0:
                keep = rows8 >= s
                sa = jnp.where(keep, pltpu.roll(a, s, axis=0), 1.0)
                sb = jnp.where(keep, pltpu.roll(b, s, axis=0), 0.0)
            else:
                keep = rows8 < 8 - s
                sa = jnp.where(keep, pltpu.roll(a, tot - s, axis=0), 1.0)
                sb = jnp.where(keep, pltpu.roll(b, tot - s, axis=0), 0.0)
            b = a * sb + b
            a = a * sa
        a_s[d] = a
        b_s[d] = b

    def group_fwd(i, c):
        r0 = pl.multiple_of(i * 8, 8)
        hg = a_s[0, pl.ds(r0, 8), :] * c + b_s[0, pl.ds(r0, 8), :]
        h_s[pl.ds(r0, 8), :] = hg
        return jnp.broadcast_to(hg[7:8, :], hg.shape)

    def group_bwd(i, c):
        gidx = jnp.where(i < ngrp_ctx, ngrp_ctx - 1 - i, ngrp + ngrp_ctx - 1 - i)
        r0 = pl.multiple_of(gidx * 8, 8)
        hg = a_s[1, pl.ds(r0, 8), :] * c + b_s[1, pl.ds(r0, 8), :]
        h_s[pl.ds(r0, 8), :] += hg
        return jnp.broadcast_to(hg[0:1, :], hg.shape)

    zero = jnp.zeros((8, xs.shape[1]), F32)
    lax.fori_loop(0, ngrp, group_fwd, zero, unroll=4)
    lax.fori_loop(0, ngrp, group_bwd, zero, unroll=4)
    o_ref[0] = (h_s[ctx_len:, :] * _gelu_tanh(yl_ref[0])).astype(o_ref.dtype)


def _rglru(xb_l, yb_l, xb_c, conv_w, conv_b, wa, ba, wx, bx, a_param):
    bsz, lat_len, width = xb_l.shape
    ctx_len = xb_c.shape[1]
    tot = ctx_len + lat_len
    blk = width // LRU_HEADS

    def col(b, h):
        return (b, 0, h)

    def wcol(b, h):
        return (0, h)

    def w3(b, h):
        return (0, 0, h)

    return pl.pallas_call(
        functools.partial(_lru_kernel, ctx_len=ctx_len, lat_len=lat_len),
        grid=(bsz, LRU_HEADS),
        in_specs=[pl.BlockSpec((1, lat_len, blk), col),
                  pl.BlockSpec((1, lat_len, blk), col),
                  pl.BlockSpec((1, ctx_len, blk), col),
                  pl.BlockSpec((LRU_CONV, blk), wcol),
                  pl.BlockSpec((1, blk), wcol),
                  pl.BlockSpec((2, 1, blk, blk), lambda b, h: (0, h, 0, 0)),
                  pl.BlockSpec((2, 1, blk), w3),
                  pl.BlockSpec((2, 1, blk, blk), lambda b, h: (0, h, 0, 0)),
                  pl.BlockSpec((2, 1, blk), w3),
                  pl.BlockSpec((2, 1, blk), w3)],
        out_specs=pl.BlockSpec((1, lat_len, blk), col),
        out_shape=jax.ShapeDtypeStruct((bsz, lat_len, width), BF16),
        scratch_shapes=[pltpu.VMEM((tot, blk), F32),
                        pltpu.VMEM((2, tot, blk), F32),
                        pltpu.VMEM((2, tot, blk), F32),
                        pltpu.VMEM((tot, blk), F32)],
        compiler_params=_params("parallel", "parallel"),
        name="rglru",
    )(xb_l, yb_l, xb_c, conv_w, conv_b.reshape(1, width), wa, ba.reshape(2, 1, width), wx,
      bx.reshape(2, 1, width), a_param.reshape(2, 1, width))


def _post_mixer_kernel(*refs, n_in, has_bias):
    a_refs = refs[:n_in]
    w_refs = refs[n_in:2 * n_in]
    i = 2 * n_in
    b_ref = None
    if has_bias:
        b_ref = refs[i]
        i += 1
    x_ref, g1_ref, sc_ref, sh_ref, lng_ref, lnb_ref, rw_ref, rb_ref, base_ref = refs[i:i + 9]
    x1_ref, v_ref, route_ref, cnt_ref = refs[i + 9:]

    @pl.when((pl.program_id(0) == 0) & (pl.program_id(1) == 0))
    def _():
        cnt_ref[...] = base_ref[...]

    y = None
    for a_ref, w_ref in zip(a_refs, w_refs):
        t = _dot(a_ref[0], w_ref[...])
        y = t if y is None else y + t
    if has_bias:
        y = y + b_ref[...]
    x1 = _layer_norm(DEEPNORM_ALPHA * x_ref[0] + g1_ref[0] * y, lng_ref[...], lnb_ref[...])
    x1_ref[0] = x1
    v = x1 * (1.0 + sc_ref[0]) + sh_ref[0]
    v_ref[0] = _pack_bf16_pairs(v)
    logits = _dot3(v, rw_ref[...]) + rb_ref[...]
    tm = logits.shape[0]
    lane = lax.broadcasted_iota(jnp.int32, logits.shape, 1).astype(F32)
    work = logits
    picks, firsts = [], []
    m0 = None
    for kk in range(TOP_K):
        m = jnp.max(work, axis=-1, keepdims=True)
        if kk == 0:
            m0 = m
        first = jnp.min(jnp.where(work == m, lane, float(LANES)), axis=-1, keepdims=True)
        pick = lane == first
        picks.append(pick)
        firsts.append(first)
        work = jnp.where(pick, -jnp.inf, work)
    sel = jnp.where(picks[0] | picks[1] | picks[2] | picks[3], 1.0, 0.0)
    e = sel * jnp.exp(logits - m0)
    gate = e / jnp.sum(e, axis=-1, keepdims=True)
    ti = lax.broadcasted_iota(jnp.int32, (tm, tm), 0)
    tj = lax.broadcasted_iota(jnp.int32, (tm, tm), 1)
    before = jnp.where(ti > tj, 1.0, 0.0).astype(BF16)
    slot = _dot(before, sel.astype(BF16)) + cnt_ref[...]
    route = jnp.zeros(logits.shape, F32)
    for kk in range(TOP_K):
        rank = jnp.sum(jnp.where(picks[kk], slot, 0.0), axis=-1, keepdims=True)
        wgt = jnp.sum(jnp.where(picks[kk], gate, 0.0), axis=-1, keepdims=True)
        route = jnp.where(lane == float(kk), firsts[kk], route)
        route = jnp.where(lane == float(TOP_K + kk), rank, route)
        route = jnp.where(lane == float(2 * TOP_K + kk), wgt, route)
    route_ref[0] = route
    cnt_ref[...] += jnp.sum(sel, axis=0, keepdims=True)


def _post_mixer(acts, ws, bias, x, g1, sc2, sh2, ln_g, ln_b, router_w, router_b, base, tm):
    bsz, length, d = x.shape
    tm = min(tm, length)
    n_in = len(acts)

    def row(bi, i):
        return (bi, i, 0)

    def per_b(bi, i):
        return (bi, 0, 0)

    def const(bi, i):
        return (0, 0)

    in_specs = [pl.BlockSpec((1, tm, a.shape[2]), row) for a in acts]
    in_specs += [pl.BlockSpec(w.shape, const) for w in ws]
    args = list(acts) + list(ws)
    if bias is not None:
        in_specs.append(pl.BlockSpec((1, d), const))
        args.append(bias.reshape(1, d))
    in_specs += [pl.BlockSpec((1, tm, d), row),
                 pl.BlockSpec((1, 1, d), per_b), pl.BlockSpec((1, 1, d), per_b),
                 pl.BlockSpec((1, 1, d), per_b),
                 pl.BlockSpec((1, d), const), pl.BlockSpec((1, d), const),
                 pl.BlockSpec((d, LANES), const), pl.BlockSpec((1, LANES), const),
                 pl.BlockSpec((1, LANES), const)]
    rw = jnp.zeros((d, LANES), F32).at[:, :N_EXPERTS].set(router_w)
    rb = jnp.full((1, LANES), -1e30, F32).at[0, :N_EXPERTS].set(router_b)
    args += [x, g1, sc2, sh2, ln_g.reshape(1, d), ln_b.reshape(1, d), rw, rb, base]
    return pl.pallas_call(
        functools.partial(_post_mixer_kernel, n_in=n_in, has_bias=bias is not None),
        grid=(bsz, length // tm),
        in_specs=in_specs,
        out_specs=[pl.BlockSpec((1, tm, d), row), pl.BlockSpec((1, tm, d // 2), row),
                   pl.BlockSpec((1, tm, LANES), row), pl.BlockSpec((1, LANES), const)],
        out_shape=[jax.ShapeDtypeStruct((bsz, length, d), F32),
                   jax.ShapeDtypeStruct((bsz, length, d // 2), jnp.uint32),
                   jax.ShapeDtypeStruct((bsz, length, LANES), F32),
                   jax.ShapeDtypeStruct((1, LANES), F32)],
        compiler_params=_params("arbitrary", "arbitrary"),
        name="post_mixer",
    )(*args)


MOE_W1_CHUNK = 512


def _moe_rows_kernel(te_ref, nv_ref, first_ref, xs_ref, w1_ref, b1g_ref, b1l_ref, w2_ref, b2_ref,
                     ys_ref, wt_s, w1g_s, w1l_s, w2b_s):
    del te_ref
    i = pl.program_id(0)
    nv = nv_ref[i]

    @pl.when(first_ref[i] == 1)
    def _():
        half = MOE_W1_CHUNK // 2
        for c in range(w1_ref.shape[3] // MOE_W1_CHUNK):
            t = w1_ref[0, 0][:, c * MOE_W1_CHUNK:(c + 1) * MOE_W1_CHUNK].T
            for j in range(wt_s.shape[0]):
                cols = slice(j * LANES, (j + 1) * LANES)
                wt_s[j] = t[:, cols]
                w1g_s[c * half:(c + 1) * half, cols] = wt_s[j, pl.ds(0, half, stride=2), :].astype(BF16)
                w1l_s[c * half:(c + 1) * half, cols] = wt_s[j, pl.ds(1, half, stride=2), :].astype(BF16)
        w2b_s[...] = w2_ref[0, 0].astype(BF16)

    @pl.when(nv > 0)
    def _():
        rows = lax.broadcasted_iota(jnp.int32, xs_ref.shape, 0)
        x = _unpack_bf16_pairs(jnp.where(rows < nv, xs_ref[...], jnp.uint32(0))).astype(BF16)
        glu = jnp.minimum(_dot_nt(x, w1g_s[...]) + b1g_ref[0, 0], SWIGLU_LIMIT)
        lin = jnp.clip(_dot_nt(x, w1l_s[...]) + b1l_ref[0, 0], -SWIGLU_LIMIT, SWIGLU_LIMIT)
        act = glu * _sigmoid(SWIGLU_ALPHA * glu) * (lin + 1.0)
        ys_ref[...] = _pack_bf16_pairs(_dot(act.astype(BF16), w2b_s[...]) + b2_ref[0, 0])

    @pl.when(nv == 0)
    def _():
        ys_ref[...] = jnp.zeros_like(ys_ref)


def _moe_rows(xs, tile_expert, tile_rows, tile_first, layer, w1, b1g, b1l, w2, b2, tm):
    n_rows, dh = xs.shape
    _, _, d, dff2 = w1.shape
    dff = dff2 // 2

    def row(i, te, nv, first):
        return (i, 0)

    def exp4(i, te, nv, first):
        return (layer, te[i], 0, 0)

    return pl.pallas_call(
        _moe_rows_kernel,
        grid_spec=pltpu.PrefetchScalarGridSpec(
            num_scalar_prefetch=3,
            grid=(n_rows // tm,),
            in_specs=[pl.BlockSpec((tm, dh), row),
                      pl.BlockSpec((1, 1, d, dff2), exp4),
                      pl.BlockSpec((1, 1, 1, dff), exp4), pl.BlockSpec((1, 1, 1, dff), exp4),
                      pl.BlockSpec((1, 1, dff, d), exp4), pl.BlockSpec((1, 1, 1, d), exp4)],
            out_specs=pl.BlockSpec((tm, dh), row),
            scratch_shapes=[pltpu.VMEM((d // LANES, MOE_W1_CHUNK, LANES), F32),
                            pltpu.VMEM((dff, d), BF16),
                            pltpu.VMEM((dff, d), BF16), pltpu.VMEM((dff, d), BF16)]),
        out_shape=jax.ShapeDtypeStruct((n_rows, dh), jnp.uint32),
        compiler_params=_params("arbitrary"),
        name="moe_rows",
    )(tile_expert, tile_rows, tile_first, xs, w1, b1g, b1l, w2, b2)


def _moe_combine_kernel(y0_ref, y1_ref, y2_ref, y3_ref, route_ref, x1_ref, g2_ref, lng_ref, lnb_ref,
                        o_ref):
    route = route_ref[...]
    lane = lax.broadcasted_iota(jnp.int32, route.shape, 1)
    f = None
    for kk, y_ref in enumerate((y0_ref, y1_ref, y2_ref, y3_ref)):
        wgt = jnp.sum(jnp.where(lane == 2 * TOP_K + kk, route, 0.0), axis=-1, keepdims=True)
        term = wgt * _unpack_bf16_pairs(y_ref[0])
        f = term if f is None else f + term
    o_ref[0] = _layer_norm(DEEPNORM_ALPHA * x1_ref[0] + g2_ref[0] * f, lng_ref[...], lnb_ref[...])


def _moe_combine(yg, route, row_offset, x1, g2, ln_g, ln_b, tm):
    bsz, length, d = x1.shape
    tm = min(tm, length)
    nt = length // tm
    off = row_offset // tm

    def pick(kk):
        return lambda bi, i: (kk, off + bi * nt + i, 0)

    def const(bi, i):
        return (0, 0)

    return pl.pallas_call(
        _moe_combine_kernel,
        grid=(bsz, nt),
        in_specs=[pl.BlockSpec((1, tm, d // 2), pick(kk)) for kk in range(TOP_K)] + [
            pl.BlockSpec((tm, LANES), lambda bi, i: (off + bi * nt + i, 0)),
            pl.BlockSpec((1, tm, d), lambda bi, i: (bi, i, 0)),
            pl.BlockSpec((1, 1, d), lambda bi, i: (bi, 0, 0)),
            pl.BlockSpec((1, d), const), pl.BlockSpec((1, d), const)],
        out_specs=pl.BlockSpec((1, tm, d), lambda bi, i: (bi, i, 0)),
        out_shape=jax.ShapeDtypeStruct((bsz, length, d), F32),
        compiler_params=_params("parallel", "parallel"),
        name="moe_combine",
    )(yg, yg, yg, yg, route, x1, g2, ln_g.reshape(1, d), ln_b.reshape(1, d))


SC_CORES = 2
SC_SUBCORES = 16
SC_WORKERS = SC_CORES * SC_SUBCORES
SC_WINDOW = 64


def _sc_row_pipeline(nwin, read, write):
    read(0, 0).start()

    @pl.loop(0, nwin, step=2)
    def _(w0):
        for b in range(2):
            w = w0 + b

            @pl.when(w + 1 < nwin)
            def _():
                @pl.when(w >= 1)
                def _():
                    write(w - 1, 1 - b).wait()

                read(w + 1, 1 - b).start()

            read(w, b).wait()
            write(w, b).start()

    write(nwin - 2, 0).wait()
    write(nwin - 1, 1).wait()


def _sc_scatter_rows(src, pos, n_out):
    t_rows, d = src.shape
    nw, nwin, win = pos.shape
    assert nw == SC_WORKERS and nwin % 2 == 0 and t_rows % (nwin * win) == 0
    mesh = plsc.VectorSubcoreMesh(core_axis_name="c", subcore_axis_name="s")

    @functools.partial(
        pl.kernel, mesh=mesh, out_type=jax.ShapeDtypeStruct((n_out, d), src.dtype),
        scratch_types=[pltpu.VMEM((nwin, win), jnp.int32), pltpu.VMEM((2, win, d), src.dtype),
                       pltpu.SemaphoreType.DMA((2,)), pltpu.SemaphoreType.DMA((2,))])
    def scatter(src_hbm, pos_hbm, out_hbm, idx_v, rows_v, rsem, wsem):
        wid = lax.axis_index("s") * SC_CORES + lax.axis_index("c")
        t0 = lax.rem(wid * (nwin * win), t_rows)
        pltpu.sync_copy(pos_hbm.at[wid], idx_v)

        def read(w, slot):
            return pltpu.make_async_copy(src_hbm.at[pl.ds(t0 + w * win, win)], rows_v.at[slot],
                                         rsem.at[slot])

        def write(w, slot):
            return pltpu.make_async_copy(rows_v.at[slot], out_hbm.at[idx_v.at[w]], wsem.at[slot])

        _sc_row_pipeline(nwin, read, write)

    return scatter(src, pos)


def _sc_gather_rows(table, pos):
    _, d = table.shape
    nw, nwin, win = pos.shape
    assert nw == SC_WORKERS and nwin % 2 == 0
    per = nwin * win
    mesh = plsc.VectorSubcoreMesh(core_axis_name="c", subcore_axis_name="s")

    @functools.partial(
        pl.kernel, mesh=mesh, out_type=jax.ShapeDtypeStruct((nw * per, d), table.dtype),
        scratch_types=[pltpu.VMEM((nwin, win), jnp.int32), pltpu.VMEM((2, win, d), table.dtype),
                       pltpu.SemaphoreType.DMA((2,)), pltpu.SemaphoreType.DMA((2,))])
    def gather(table_hbm, pos_hbm, out_hbm, idx_v, rows_v, rsem, wsem):
        wid = lax.axis_index("s") * SC_CORES + lax.axis_index("c")
        base = wid * per
        pltpu.sync_copy(pos_hbm.at[wid], idx_v)

        def read(w, slot):
            return pltpu.make_async_copy(table_hbm.at[idx_v.at[w]], rows_v.at[slot], rsem.at[slot])

        def write(w, slot):
            return pltpu.make_async_copy(rows_v.at[slot], out_hbm.at[pl.ds(base + w * win, win)],
                                         wsem.at[slot])

        _sc_row_pipeline(nwin, read, write)

    return gather(table, pos)


MOE_TILE = 256


def _moe_sparse(v_all, route, counts, layer, w1, b1g, b1l, w2, b2):
    t_rows, dh = v_all.shape
    n_exp = w1.shape[1]
    pairs = TOP_K * t_rows
    n_tiles = pairs // MOE_TILE + n_exp
    expert = route[:, 0:TOP_K].astype(jnp.int32)
    slot = route[:, TOP_K:2 * TOP_K].astype(jnp.int32)
    cnt = counts[0, :n_exp].astype(jnp.int32)
    tiles_per = (cnt + MOE_TILE - 1) // MOE_TILE
    tile_end = jnp.cumsum(tiles_per)
    tile_start = tile_end - tiles_per
    pos = (tile_start * MOE_TILE)[expert] + slot
    nwin = pairs // (SC_WORKERS * SC_WINDOW)
    pos_km = pos.T.reshape(SC_WORKERS, nwin, SC_WINDOW)
    tile_ids = jnp.arange(n_tiles, dtype=jnp.int32)[:, None]
    owns = (tile_ids >= tile_start[None, :]) & (tile_ids < tile_end[None, :])
    experts = jnp.arange(n_exp, dtype=jnp.int32)[None, :]
    last_used = jnp.max(jnp.where(tiles_per > 0, experts[0], 0))
    used = jnp.any(owns, axis=1)
    te = jnp.where(used, jnp.sum(jnp.where(owns, experts, 0), axis=1), last_used).astype(jnp.int32)
    rows_left = cnt[None, :] - (tile_ids - tile_start[None, :]) * MOE_TILE
    tile_rows = jnp.sum(jnp.where(owns, jnp.clip(rows_left, 0, MOE_TILE), 0), axis=1)
    tile_rows = tile_rows.astype(jnp.int32)
    tile_first = jnp.any(owns & (tile_ids == tile_start[None, :]), axis=1).astype(jnp.int32)
    xs = _sc_scatter_rows(v_all, pos_km, n_tiles * MOE_TILE)
    ys = _moe_rows(xs, te, tile_rows, tile_first, layer, w1, b1g, b1l, w2, b2, MOE_TILE)
    return _sc_gather_rows(ys, pos_km).reshape(TOP_K, t_rows, dh)


@functools.lru_cache(maxsize=None)
def _sincos_2d(rows, cols, dim):
    quarter = dim // 4
    omega = 1.0 / (10000.0 ** (np.arange(quarter, dtype=np.float64) / quarter))

    def emb1d(n):
        ang = np.arange(n, dtype=np.float64)[:, None] * omega
        return np.concatenate([np.sin(ang), np.cos(ang)], axis=-1)

    er = np.broadcast_to(emb1d(rows)[:, None], (rows, cols, dim // 2))
    ec = np.broadcast_to(emb1d(cols)[None], (rows, cols, dim // 2))
    return np.concatenate([er, ec], axis=-1).reshape(rows * cols, dim).astype(np.float32)


def _gate_layout(gates, n_ch):
    bsz, length, _ = gates.shape
    g = gates[..., :n_ch].reshape(bsz, length // DN_CHUNK, DN_CHUNK, n_ch)
    return g.transpose(0, 3, 1, 2)


def kernel(x, c, ctx, c_ctx, ada_w, ada_b, ln_g, ln_b, ev_w_in, ev_w_out, dn_conv_w, dn_a_log, dn_dt_bias, dn_norm_g, hy_conv_w, hy_conv_b, hy_w_in, hy_b_in, hy_w_mid, hy_b_mid, hy_w_out, hy_freq, hy_skip, od_w_in, od_b_in, lru_conv_w, lru_conv_b, lru_wa, lru_ba, lru_wx, lru_bx, lru_a_param, od_w_out, od_b_out, router_w, router_b, moe_w1, moe_b1, moe_w2, moe_b2):
    bsz, length, d = x.shape
    ctx_len = ctx.shape[1]
    pos = jnp.asarray(_sincos_2d(length // GRID_W, GRID_W, d))

    cond = jnp.zeros((16, d), F32).at[:bsz].set(c).at[bsz].set(c_ctx)
    mod = _modulation(cond, ada_w, ada_b).reshape(DEPTH, 16, 6, d)

    def lat_mod(layer, k):
        return mod[layer, :bsz, k][:, None, :]

    def ctx_mod(layer, k):
        return jnp.broadcast_to(mod[layer, bsz, k][None, None, :], (bsz, 1, d))

    hc = ctx
    for layer in range(DEPTH):
        last = layer == DEPTH - 1
        j = layer // 2
        if layer % 2 == 0:
            dn_qk = DN_HEADS * DN_DK
            dn_qkv = 3 * dn_qk
            dn_in = dn_qkv + dn_qk + 4 * DN_HEADS
            w_in = ev_w_in[j]
            gate_w = jnp.zeros((d, LANES), F32).at[:, :4 * DN_HEADS].set(w_in[:, dn_qkv + dn_qk:dn_in])
            w_cat = jnp.concatenate([w_in[:, :dn_qkv + dn_qk], gate_w, w_in[:, dn_in:]],
                                    axis=1).astype(BF16)
            hy_in = w_in.shape[1] - dn_in
            widths = (dn_qkv, dn_qk, LANES, hy_in)
            qkv_l, z_l, gt_l, phy_l = _inproj(x, lat_mod(layer, 1), lat_mod(layer, 0), pos,
                                               w_cat, None, widths, 512)
            qkv_c, z_c, gt_c, phy_c = _inproj(hc, ctx_mod(layer, 1), ctx_mod(layer, 0), None,
                                               w_cat, None, widths, 512)
            gates = jnp.concatenate([_gate_layout(gt_c, 4 * DN_HEADS),
                                     _gate_layout(gt_l, 4 * DN_HEADS)], axis=2)
            n_chunks = gates.shape[2]
            gates = jnp.pad(gates, ((0, 0), (0, 0), (0, -n_chunks % 16), (0, 0)))
            dn_l, dn_c = _deltanet(qkv_l, z_l, qkv_c, z_c, gates, dn_conv_w[j], dn_a_log[j],
                                   dn_dt_bias[j], dn_norm_g[j])
            filt = (hy_w_in[j], hy_b_in[j], hy_w_mid[j], hy_b_mid[j], hy_w_out[j], hy_freq[j])
            hy_l = _hyena(phy_l, hy_conv_w[j], hy_conv_b[j], filt, hy_skip[j])
            w_out = ev_w_out[j].astype(BF16)
            half = dn_l.shape[2]
            acts_l, ws, bias = (dn_l, hy_l), (w_out[:half], w_out[half:]), None
            acts_c = None
            if not last:
                hy_c = _hyena(phy_c, hy_conv_w[j], hy_conv_b[j], filt, hy_skip[j])
                acts_c = (dn_c, hy_c)
        else:
            w_in = od_w_in[j].astype(BF16)
            width = w_in.shape[1] // 2
            b_in = od_b_in[j].reshape(1, 2 * width)
            xb_l, yb_l = _inproj(x, lat_mod(layer, 1), lat_mod(layer, 0), pos, w_in, b_in,
                                 (width, width), 512)
            xb_c, _ = _inproj(hc, ctx_mod(layer, 1), ctx_mod(layer, 0), None, w_in, b_in,
                              (width, width), 512)
            act_l = _rglru(xb_l, yb_l, xb_c, lru_conv_w[j], lru_conv_b[j], lru_wa[j], lru_ba[j],
                           lru_wx[j], lru_bx[j], lru_a_param[j])
            acts_l, ws, bias = (act_l,), (od_w_out[j].astype(BF16),), od_b_out[j]
            acts_c = None
            assert last, "context outputs of the RG-LRU layer are only needed before the last layer"


        x1, v, route, counts = _post_mixer(acts_l, ws, bias, x, lat_mod(layer, 2),
                                           lat_mod(layer, 4), lat_mod(layer, 3), ln_g[layer, 0],
                                           ln_b[layer, 0], router_w[layer], router_b[layer],
                                           jnp.zeros((1, LANES), F32), 512)
        v_all = v.reshape(bsz * length, d // 2)
        route = route.reshape(bsz * length, LANES)
        if not last:
            hc1, vc, route_c, counts = _post_mixer(acts_c, ws, bias, hc, ctx_mod(layer, 2),
                                                   ctx_mod(layer, 4), ctx_mod(layer, 3),
                                                   ln_g[layer, 0], ln_b[layer, 0], router_w[layer],
                                                   router_b[layer], counts, 256)
            v_all = jnp.concatenate([v_all, vc.reshape(bsz * ctx_len, d // 2)], axis=0)
            route = jnp.concatenate([route, route_c.reshape(bsz * ctx_len, LANES)], axis=0)
        yg = _moe_sparse(v_all, route, counts, layer, moe_w1, moe_b1[:, :, None, 0::2],
                         moe_b1[:, :, None, 1::2], moe_w2, moe_b2[:, :, None, :])
        x = _moe_combine(yg, route, 0, x1, lat_mod(layer, 5), ln_g[layer, 1], ln_b[layer, 1], 256)
        if not last:
            hc = _moe_combine(yg, route, bsz * length, hc1, ctx_mod(layer, 5), ln_g[layer, 1],
                              ln_b[layer, 1], 256)
    return x
```

```python
import functools
import math

import numpy as np
import jax
import jax.numpy as jnp
from jax import lax
from jax.experimental import pallas as pl
from jax.experimental.pallas import tpu as pltpu
from jax.experimental.pallas import tpu_sc as plsc

F32 = jnp.float32
BF16 = jnp.bfloat16

VMEM_LIMIT_BYTES = 56 * 1024 * 1024
LANES = 128

DEPTH = 2
GRID_W = 64
DEEPNORM_ALPHA = (2.0 * DEPTH) ** 0.25
LN_EPS = 1e-5
RMS_EPS = 1e-6

DN_HEADS = 4
DN_DK = 128
DN_CHUNK = 64
DN_CONV = 4

HY_EMB = 33
HY_TARGET = 1e-2
HY_MIN_DECAY = math.log(HY_TARGET) / 1.5
HY_MAX_DECAY = math.log(HY_TARGET) / 0.3
HY_CONV = 3

LRU_HEADS = 4
LRU_C = 8.0
LRU_CONV = 4

N_EXPERTS = 32
TOP_K = 4
SWIGLU_ALPHA = 1.702
SWIGLU_LIMIT = 7.0


def _params(*sem):
    return pltpu.CompilerParams(dimension_semantics=sem, vmem_limit_bytes=VMEM_LIMIT_BYTES)


def _dot(a, b):
    return jnp.dot(a, b, preferred_element_type=F32)


def _dot_nt(a, b):
    return lax.dot_general(a, b, (((1,), (1,)), ((), ())), preferred_element_type=F32)


def _split(a):
    hi = a.astype(BF16)
    lo = (a - hi.astype(F32)).astype(BF16)
    return hi, lo


def _dot3(a, b):
    ah, al = _split(a)
    bh, bl = _split(b)
    return _dot(ah, bh) + _dot(ah, bl) + _dot(al, bh)


def _silu(x):
    return x * (1.0 / (1.0 + jnp.exp(-x)))


def _sigmoid(x):
    return 1.0 / (1.0 + jnp.exp(-x))


def _softplus(x):
    return jnp.maximum(x, 0.0) + jnp.log(1.0 + jnp.exp(-jnp.abs(x)))


def _layer_norm(x, g, b):
    mu = jnp.mean(x, axis=-1, keepdims=True)
    xc = x - mu
    var = jnp.mean(xc * xc, axis=-1, keepdims=True)
    return xc * lax.rsqrt(var + LN_EPS) * g + b


def _pack_bf16_pairs(x):
    w = x.shape[1] // 2
    lo = pltpu.bitcast(x[:, :w].astype(BF16).astype(F32), jnp.uint32) >> 16
    hi = pltpu.bitcast(x[:, w:].astype(BF16).astype(F32), jnp.uint32) & jnp.uint32(0xFFFF0000)
    return lo | hi


def _unpack_bf16_pairs(p):
    lo = pltpu.bitcast(p << 16, F32)
    hi = pltpu.bitcast(p & jnp.uint32(0xFFFF0000), F32)
    return jnp.concatenate([lo, hi], axis=1)


def _shift_rows(x, s):
    if s == 0:
        return x
    n = x.shape[0]
    rows = lax.broadcasted_iota(jnp.int32, x.shape, 0)
    valid = (rows >= s) if s > 0 else (rows < n + s)
    return jnp.where(valid, pltpu.roll(x, s % n, axis=0), 0.0)


def _depthwise_conv(x, w, pad_left):
    acc = None
    for i in range(w.shape[0]):
        term = _shift_rows(x, pad_left - i) * w[i:i + 1, :]
        acc = term if acc is None else acc + term
    return acc


def _mod_kernel(c_ref, w_ref, b_ref, o_ref):
    o_ref[0] = _dot3(_silu(c_ref[...]), w_ref[0]) + b_ref[0]


def _modulation(cond, ada_w, ada_b):
    depth, d, n = ada_w.shape
    rows = cond.shape[0]
    tn = 1536
    return pl.pallas_call(
        _mod_kernel,
        grid=(depth, n // tn),
        in_specs=[
            pl.BlockSpec((rows, d), lambda l, j: (0, 0)),
            pl.BlockSpec((1, d, tn), lambda l, j: (l, 0, j)),
            pl.BlockSpec((1, 1, tn), lambda l, j: (l, 0, j)),
        ],
        out_specs=pl.BlockSpec((1, rows, tn), lambda l, j: (l, 0, j)),
        out_shape=jax.ShapeDtypeStruct((depth, rows, n), F32),
        compiler_params=_params("parallel", "parallel"),
        name="modulation",
    )(cond, ada_w, ada_b.reshape(depth, 1, n))


def _inproj_kernel(*refs, splits, has_pos, has_bias):
    x_ref, sc_ref, sh_ref = refs[:3]
    i = 3
    pos_ref = None
    if has_pos:
        pos_ref = refs[i]
        i += 1
    w_ref = refs[i]
    i += 1
    b_ref = None
    if has_bias:
        b_ref = refs[i]
        i += 1
    o_refs = refs[i:]
    u = x_ref[0] * (1.0 + sc_ref[0]) + sh_ref[0]
    if has_pos:
        u = u + pos_ref[...]
    ub = u.astype(BF16)
    for o_ref, (s, e) in zip(o_refs, splits):
        acc = _dot(ub, w_ref[:, s:e])
        if has_bias:
            acc = acc + b_ref[:, s:e]
        o_ref[0] = acc


def _inproj(x, sc, sh, pos, w, b, widths, tm):
    bsz, length, d = x.shape
    n = w.shape[1]
    splits, s = [], 0
    for wd in widths:
        splits.append((s, s + wd))
        s += wd
    assert s == n
    tm = min(tm, length)
    in_specs = [
        pl.BlockSpec((1, tm, d), lambda bi, i: (bi, i, 0)),
        pl.BlockSpec((1, 1, d), lambda bi, i: (bi, 0, 0)),
        pl.BlockSpec((1, 1, d), lambda bi, i: (bi, 0, 0)),
    ]
    args = [x, sc, sh]
    if pos is not None:
        in_specs.append(pl.BlockSpec((tm, d), lambda bi, i: (i, 0)))
        args.append(pos)
    in_specs.append(pl.BlockSpec((d, n), lambda bi, i: (0, 0)))
    args.append(w)
    if b is not None:
        in_specs.append(pl.BlockSpec((1, n), lambda bi, i: (0, 0)))
        args.append(b)
    return pl.pallas_call(
        functools.partial(_inproj_kernel, splits=tuple(splits), has_pos=pos is not None,
                          has_bias=b is not None),
        grid=(bsz, length // tm),
        in_specs=in_specs,
        out_specs=[pl.BlockSpec((1, tm, wd), lambda bi, i: (bi, i, 0)) for wd in widths],
        out_shape=[jax.ShapeDtypeStruct((bsz, length, wd), F32) for wd in widths],
        compiler_params=_params("parallel", "parallel"),
        name="inproj",
    )(*args)


def _unit_tri_inverses(mats, lower):
    n = mats[0].shape[0]
    nb = 16
    np_ = len(mats)
    ii = lax.broadcasted_iota(jnp.int32, (n, n), 0)
    jj = lax.broadcasted_iota(jnp.int32, (n, n), 1)
    same16 = (ii // nb) == (jj // nb)
    same32 = (ii // (2 * nb)) == (jj // (2 * nb))
    dgs = []
    for a in mats:
        ad = jnp.where(same16, a, 0.0)
        dgs.append(ad[0:nb] + ad[nb:2 * nb] + ad[2 * nb:3 * nb] + ad[3 * nb:4 * nb])
    dg = jnp.concatenate(dgs, axis=0)
    rr = lax.broadcasted_iota(jnp.int32, dg.shape, 0)
    ll = lax.broadcasted_iota(jnp.int32, dg.shape, 1)
    xd = jnp.where(rr % nb == ll % nb, 1.0, 0.0)
    blk0 = (ll // nb) * nb
    for s in (range(nb - 1) if lower else range(nb - 1, 0, -1)):
        col = jnp.take_along_axis(dg, blk0 + s, axis=1)
        row = jnp.concatenate(
            [jnp.broadcast_to(xd[p * nb + s:p * nb + s + 1, :], (nb, n)) for p in range(np_)], axis=0)
        xd = xd - col * row
    ds = [jnp.where(same16, jnp.concatenate([xd[p * nb:(p + 1) * nb]] * (n // nb), axis=0), 0.0)
          for p in range(np_)]
    lvl1 = same32 & jnp.logical_not(same16)
    t1 = [_dot3(d, jnp.where(lvl1, a, 0.0)) for d, a in zip(ds, mats)]
    x1 = [d - _dot3(t, d) for d, t in zip(ds, t1)]
    t2 = [_dot3(x, jnp.where(same32, 0.0, a)) for x, a in zip(x1, mats)]
    return [x - _dot3(t, x) for x, t in zip(x1, t2)]


def _dn_kernel(alog_ref, dtb_ref,
               ql_ref, kl_ref, vl_ref, zl_ref, qc_ref, kc_ref, vc_ref, zc_ref, gt_ref,
               cwq_ref, cwk_ref, cwv_ref, ng_ref,
               yl_ref, yc_ref,
               qn_s, kn_s, vn_s, gc_s, bt_s, nq_s, c_s, gl_s, o_s,
               *, ctx_len, lat_len):
    h = pl.program_id(1)
    csz = DN_CHUNK
    nc_ctx = ctx_len // csz
    nc = (ctx_len + lat_len) // csz

    def prep(src_ref, cw_ref, kind):
        t = _silu(_depthwise_conv(src_ref[0], cw_ref[...], DN_CONV // 2))
        if kind == "v":
            return t
        t = t * lax.rsqrt(jnp.sum(t * t, axis=-1, keepdims=True) + RMS_EPS)
        return t * (DN_DK ** -0.5) if kind == "q" else t

    qn_s[0:ctx_len, :] = prep(qc_ref, cwq_ref, "q")
    qn_s[ctx_len:, :] = prep(ql_ref, cwq_ref, "q")
    kn_s[0:ctx_len, :] = prep(kc_ref, cwk_ref, "k")
    kn_s[ctx_len:, :] = prep(kl_ref, cwk_ref, "k")
    vn_s[0:ctx_len, :] = prep(vc_ref, cwv_ref, "v")
    vn_s[ctx_len:, :] = prep(vl_ref, cwv_ref, "v")

    ii = lax.broadcasted_iota(jnp.int32, (csz, csz), 0)
    jj = lax.broadcasted_iota(jnp.int32, (csz, csz), 1)
    eye = ii == jj
    for d in range(2):
        graw = gt_ref[0, d * 2 * DN_HEADS + h]
        braw = gt_ref[0, d * 2 * DN_HEADS + DN_HEADS + h]
        a_neg = -jnp.exp(jnp.zeros_like(graw) + alog_ref[d, h])
        g = a_neg * _softplus(graw + dtb_ref[d, h])
        tri = jnp.where((ii <= jj) if d == 0 else (ii >= jj), 1.0, 0.0).astype(BF16)
        g1 = g.astype(BF16)
        r1 = g - g1.astype(F32)
        g2 = r1.astype(BF16)
        g3 = (r1 - g2.astype(F32)).astype(BF16)
        gc_s[d] = _dot(g1, tri) + _dot(g2, tri) + _dot(g3, tri)
        bt_s[d] = _sigmoid(braw)

    o_s[...] = jnp.zeros_like(o_s)

    group = 12
    assert nc % group == 0

    def chunk_prep(gi, carry):
        ns = [gi * group + c for c in range(group)]
        r0s = [pl.multiple_of(n * csz, csz) for n in ns]
        qs = [qn_s[pl.ds(r0, csz), :] for r0 in r0s]
        ks = [kn_s[pl.ds(r0, csz), :] for r0 in r0s]
        vs = [vn_s[pl.ds(r0, csz), :] for r0 in r0s]
        kbfs = [k.astype(BF16) for k in ks]
        qks = [_dot_nt(q.astype(BF16), kbf) for q, kbf in zip(qs, kbfs)]
        for d in range(2):
            incl = (ii >= jj) if d == 0 else (ii <= jj)
            strict = (ii > jj) if d == 0 else (ii < jj)
            grs = [gc_s[d, pl.ds(n, 1), :] for n in ns]
            grows = [jnp.broadcast_to(gr, (csz, csz)) for gr in grs]
            gcols = [jnp.sum(jnp.where(eye, grow, 0.0), axis=1, keepdims=True) for grow in grows]
            bcols = [jnp.sum(jnp.where(eye, jnp.broadcast_to(bt_s[d, pl.ds(n, 1), :], (csz, csz)),
                                       0.0), axis=1, keepdims=True) for n in ns]
            decays = [jnp.where(incl, jnp.exp(jnp.where(incl, gcol - grow, 0.0)), 0.0)
                      for gcol, grow in zip(gcols, grows)]
            kbs = [k * bcol for k, bcol in zip(ks, bcols)]
            amats = [jnp.where(strict, _dot_nt(kb.astype(BF16), kbf) * decay, 0.0)
                     for kb, kbf, decay in zip(kbs, kbfs, decays)]
            tbs = [t.astype(BF16) for t in _unit_tri_inverses(amats, lower=(d == 0))]
            cs = range(group)
            egs = [jnp.exp(gcols[c]) for c in cs]
            ubs = [_dot(tbs[c], (vs[c] * bcols[c]).astype(BF16)).astype(BF16) for c in cs]
            wbs = [_dot(tbs[c], (kbs[c] * egs[c]).astype(BF16)).astype(BF16) for c in cs]
            attns = [jnp.where(incl, qks[c] * decays[c], 0.0).astype(BF16) for c in cs]
            glasts = [grs[c][:, csz - 1:csz] if d == 0 else grs[c][:, 0:1] for c in cs]
            kdts = [(ks[c] * jnp.exp(glasts[c] - gcols[c])).T.astype(BF16) for c in cs]
            nmats = [_dot(kdts[c], wbs[c]).astype(BF16) for c in cs]
            qmats = [(qs[c] * egs[c] - _dot(attns[c], wbs[c])).astype(BF16) for c in cs]
            cmats = [_dot(kdts[c], ubs[c]) for c in cs]
            omats = [_dot(attns[c], ubs[c]) for c in cs]
            for c in cs:
                n = ns[c]
                nq_s[d, n, 0:DN_DK, :] = nmats[c]
                nq_s[d, n, DN_DK:DN_DK + csz, :] = qmats[c]
                c_s[d, n] = cmats[c]
                o_s[pl.ds(r0s[c], csz), :] += omats[c]
                gl_s[d, pl.ds(n, 1), :] = jnp.broadcast_to(jnp.exp(glasts[c]), (1, LANES))
        return carry

    lax.fori_loop(0, nc // group, chunk_prep, 0)

    def step(i, states):
        new_states = []
        for d in range(2):
            if d == 0:
                n = i
            else:
                n = jnp.where(i < nc_ctx, nc_ctx - 1 - i, nc + nc_ctx - 1 - i)
            r0 = pl.multiple_of(n * csz, csz)
            s = states[d]
            r = _dot(nq_s[d, n], s.astype(BF16))
            o_s[pl.ds(r0, csz), :] += r[DN_DK:DN_DK + csz]
            new_states.append(s * gl_s[d, pl.ds(n, 1), :] - r[0:DN_DK] + c_s[d, n])
        return tuple(new_states)

    zero = jnp.zeros((DN_DK, DN_DK), F32)
    lax.fori_loop(0, nc, step, (zero, zero))

    def gated_norm(o, z):
        o = o * lax.rsqrt(jnp.mean(o * o, axis=-1, keepdims=True) + RMS_EPS) * ng_ref[...]
        return (o * _silu(z)).astype(yl_ref.dtype)

    yc_ref[0] = gated_norm(o_s[0:ctx_len, :], zc_ref[0])
    yl_ref[0] = gated_norm(o_s[ctx_len:, :], zl_ref[0])


def _deltanet(qkv_l, z_l, qkv_c, z_c, gates, conv_w, a_log, dt_bias, norm_g):
    bsz, lat_len, _ = qkv_l.shape
    ctx_len = qkv_c.shape[1]
    tot = ctx_len + lat_len
    nc = tot // DN_CHUNK
    ncp = gates.shape[2]
    hd = DN_DK
    nh = DN_HEADS

    def col(off):
        return lambda b, h: (b, 0, off + h)

    def wcol(off):
        return lambda b, h: (0, off + h)

    smem = pl.BlockSpec(memory_space=pltpu.SMEM)
    in_specs = [
        smem, smem,
        pl.BlockSpec((1, lat_len, hd), col(0)),
        pl.BlockSpec((1, lat_len, hd), col(nh)),
        pl.BlockSpec((1, lat_len, hd), col(2 * nh)),
        pl.BlockSpec((1, lat_len, hd), col(0)),
        pl.BlockSpec((1, ctx_len, hd), col(0)),
        pl.BlockSpec((1, ctx_len, hd), col(nh)),
        pl.BlockSpec((1, ctx_len, hd), col(2 * nh)),
        pl.BlockSpec((1, ctx_len, hd), col(0)),
        pl.BlockSpec((1, 4 * nh, ncp, DN_CHUNK), lambda b, h: (b, 0, 0, 0)),
        pl.BlockSpec((DN_CONV, hd), wcol(0)),
        pl.BlockSpec((DN_CONV, hd), wcol(nh)),
        pl.BlockSpec((DN_CONV, hd), wcol(2 * nh)),
        pl.BlockSpec((1, hd), lambda b, h: (0, 0)),
    ]
    scratch = [
        pltpu.VMEM((tot, hd), F32), pltpu.VMEM((tot, hd), F32), pltpu.VMEM((tot, hd), F32),
        pltpu.VMEM((2, ncp, DN_CHUNK), F32), pltpu.VMEM((2, ncp, DN_CHUNK), F32),
        pltpu.VMEM((2, nc, hd + DN_CHUNK, hd), BF16),
        pltpu.VMEM((2, nc, hd, hd), F32),
        pltpu.VMEM((2, nc, LANES), F32),
        pltpu.VMEM((tot, hd), F32),
    ]
    return pl.pallas_call(
        functools.partial(_dn_kernel, ctx_len=ctx_len, lat_len=lat_len),
        grid=(bsz, nh),
        in_specs=in_specs,
        out_specs=[pl.BlockSpec((1, lat_len, hd), col(0)),
                   pl.BlockSpec((1, ctx_len, hd), col(0))],
        out_shape=[jax.ShapeDtypeStruct((bsz, lat_len, nh * hd), BF16),
                   jax.ShapeDtypeStruct((bsz, ctx_len, nh * hd), BF16)],
        scratch_shapes=scratch,
        compiler_params=_params("parallel", "parallel"),
        name="deltanet",
    )(a_log, dt_bias, qkv_l, qkv_l, qkv_l, z_l, qkv_c, qkv_c, qkv_c, z_c, gates,
      conv_w, conv_w, conv_w, norm_g.reshape(1, hd))


def _hy_filter_kernel(z_ref, win_ref, bin_ref, wmid_ref, bmid_ref, wout_ref, freq_ref, dec_ref,
                      o_ref):
    freq = freq_ref[...]
    hcur = jnp.sin(freq * (_dot3(z_ref[...], win_ref[...]) + bin_ref[...]))
    for i in range(wmid_ref.shape[0]):
        hcur = jnp.sin(freq * (_dot3(hcur, wmid_ref[i]) + bmid_ref[i]))
    o_ref[...] = _dot3(hcur, wout_ref[...]) * dec_ref[...]


def _hy_filter(z, w_in, b_in, w_mid, b_mid, w_out, freq, dec2):
    length = z.shape[0]
    n_out = w_out.shape[1]
    tl = min(256, length)

    def whole(a):
        return pl.BlockSpec(a.shape, lambda i: (0,) * a.ndim)

    return pl.pallas_call(
        _hy_filter_kernel,
        grid=(length // tl,),
        in_specs=[pl.BlockSpec((tl, z.shape[1]), lambda i: (i, 0)),
                  whole(w_in), whole(b_in), whole(w_mid), whole(b_mid), whole(w_out), whole(freq),
                  pl.BlockSpec((tl, n_out), lambda i: (i, 0))],
        out_specs=pl.BlockSpec((tl, n_out), lambda i: (i, 0)),
        out_shape=jax.ShapeDtypeStruct((length, n_out), F32),
        compiler_params=_params("parallel"),
        name="hyena_filter",
    )(z, w_in, b_in, w_mid, b_mid, w_out, freq, dec2)


def _filt_spec_kernel(fc_ref, fs_ref, h_ref, kc_ref, ks_ref, *, width):
    j = pl.program_id(0)
    hmat = h_ref[...]
    rows = lax.broadcasted_iota(jnp.int32, hmat.shape, 0)
    cols = lax.broadcasted_iota(jnp.int32, hmat.shape, 1)
    hmat = jnp.where((rows == 0) & (cols >= width), 0.0, hmat)
    hh, hl = _split(hmat)
    c = _dot(fc_ref[...], hh) + _dot(fc_ref[...], hl)
    s = _dot(fs_ref[...], hh) + _dot(fs_ref[...], hl)
    kc_ref[...] = c[:, :width] + c[:, width:]
    orow = lax.broadcasted_iota(jnp.int32, (c.shape[0], width), 0)
    sign = jnp.where((orow == 0) & (j == 0), 1.0, -1.0)
    ks_ref[...] = s[:, :width] + sign * s[:, width:]


def _filt_spec(fwd, hfilt, tf):
    length, two_w = hfilt.shape
    width = two_w // 2
    nt = length // tf
    return pl.pallas_call(
        functools.partial(_filt_spec_kernel, width=width),
        grid=(nt,),
        in_specs=[pl.BlockSpec((tf, length), lambda j: (j, 0)),
                  pl.BlockSpec((tf, length), lambda j: (nt + j, 0)),
                  pl.BlockSpec((length, two_w), lambda j: (0, 0))],
        out_specs=[pl.BlockSpec((tf, width), lambda j: (j, 0)),
                   pl.BlockSpec((tf, width), lambda j: (j, 0))],
        out_shape=[jax.ShapeDtypeStruct((length, width), F32)] * 2,
        compiler_params=_params("arbitrary"),
        name="hyena_filter_spectrum",
    )(fwd, fwd, hfilt)


def _hy_prep_kernel(x0_ref, x1_ref, v_ref, w0_ref, w1_ref, w2_ref, b0_ref, b1_ref, b2_ref,
                    x0o_ref, vvo_ref):
    x0 = _depthwise_conv(x0_ref[0], w0_ref[...], HY_CONV // 2) + b0_ref[...]
    x1 = _depthwise_conv(x1_ref[0], w1_ref[...], HY_CONV // 2) + b1_ref[...]
    v = _depthwise_conv(v_ref[0], w2_ref[...], HY_CONV // 2) + b2_ref[...]
    x0o_ref[0] = x0
    vvo_ref[0] = v * x1


def _hy_prep(p_hy, conv_w, conv_b):
    bsz, length, three_w = p_hy.shape
    width = three_w // 3
    nb = width // LANES

    def col(off):
        return lambda b, j: (b, 0, off + j)

    def wcol(off):
        return lambda b, j: (0, off + j)

    k = conv_w.shape[0]
    return pl.pallas_call(
        _hy_prep_kernel,
        grid=(bsz, nb),
        in_specs=[pl.BlockSpec((1, length, LANES), col(0)),
                  pl.BlockSpec((1, length, LANES), col(nb)),
                  pl.BlockSpec((1, length, LANES), col(2 * nb)),
                  pl.BlockSpec((k, LANES), wcol(0)),
                  pl.BlockSpec((k, LANES), wcol(nb)),
                  pl.BlockSpec((k, LANES), wcol(2 * nb)),
                  pl.BlockSpec((1, LANES), wcol(0)),
                  pl.BlockSpec((1, LANES), wcol(nb)),
                  pl.BlockSpec((1, LANES), wcol(2 * nb))],
        out_specs=[pl.BlockSpec((1, length, LANES), col(0)),
                   pl.BlockSpec((1, length, LANES), col(0))],
        out_shape=[jax.ShapeDtypeStruct((bsz, length, width), F32)] * 2,
        compiler_params=_params("parallel", "parallel"),
        name="hyena_prep",
    )(p_hy, p_hy, p_hy, conv_w, conv_w, conv_w, conv_b, conv_b, conv_b)


def _dft_fwd_kernel(fc_ref, fs_ref, v_ref, kc_ref, ks_ref, yc_ref, ys_ref, vb_s):
    j = pl.program_id(1)

    @pl.when(j == 0)
    def _():
        vb_s[...] = v_ref[0].astype(BF16)

    uc = _dot(fc_ref[...], vb_s[...])
    us = _dot(fs_ref[...], vb_s[...])
    kc = kc_ref[...]
    ks = ks_ref[...]
    rows = lax.broadcasted_iota(jnp.int32, uc.shape, 0)
    special = (rows == 0) & (j == 0)
    yc_ref[0] = (uc * kc - jnp.where(special, 0.0, us * ks)).astype(BF16)
    ys_ref[0] = jnp.where(special, us * ks, uc * ks + us * kc).astype(BF16)


def _dft_fwd(fwd, vv, kc, ks, tf):
    bsz, length, width = vv.shape
    nt = length // tf
    return pl.pallas_call(
        _dft_fwd_kernel,
        grid=(bsz, nt),
        in_specs=[pl.BlockSpec((tf, length), lambda b, j: (j, 0)),
                  pl.BlockSpec((tf, length), lambda b, j: (nt + j, 0)),
                  pl.BlockSpec((1, length, width), lambda b, j: (b, 0, 0)),
                  pl.BlockSpec((tf, width), lambda b, j: (j, 0)),
                  pl.BlockSpec((tf, width), lambda b, j: (j, 0))],
        out_specs=[pl.BlockSpec((1, tf, width), lambda b, j: (b, j, 0)),
                   pl.BlockSpec((1, tf, width), lambda b, j: (b, j, 0))],
        out_shape=[jax.ShapeDtypeStruct((bsz, length, width), BF16)] * 2,
        scratch_shapes=[pltpu.VMEM((length, width), BF16)],
        compiler_params=_params("parallel", "arbitrary"),
        name="hyena_dft_fwd",
    )(fwd, fwd, vv, kc, ks)


def _dft_inv_kernel(ic_ref, is_ref, yc_ref, ys_ref, vv_ref, x0_ref, skip_ref, o_ref):
    y = _dot(ic_ref[...], yc_ref[0]) + _dot(is_ref[...], ys_ref[0])
    o_ref[0] = (x0_ref[0] * (y + vv_ref[0] * skip_ref[...])).astype(o_ref.dtype)


def _dft_inv(inv, yc, ys, vv, x0, skip, tt):
    bsz, length, width = vv.shape
    nt = length // tt
    return pl.pallas_call(
        _dft_inv_kernel,
        grid=(bsz, nt),
        in_specs=[pl.BlockSpec((tt, length), lambda b, i: (i, 0)),
                  pl.BlockSpec((tt, length), lambda b, i: (i, 1)),
                  pl.BlockSpec((1, length, width), lambda b, i: (b, 0, 0)),
                  pl.BlockSpec((1, length, width), lambda b, i: (b, 0, 0)),
                  pl.BlockSpec((1, tt, width), lambda b, i: (b, i, 0)),
                  pl.BlockSpec((1, tt, width), lambda b, i: (b, i, 0)),
                  pl.BlockSpec((1, width), lambda b, i: (0, 0))],
        out_specs=pl.BlockSpec((1, tt, width), lambda b, i: (b, i, 0)),
        out_shape=jax.ShapeDtypeStruct((bsz, length, width), BF16),
        compiler_params=_params("parallel", "parallel"),
        name="hyena_dft_inv",
    )(inv, inv, yc, ys, vv, x0, skip)


@functools.lru_cache(maxsize=None)
def _hyena_tables(length):
    n2 = 2 * length
    t = np.linspace(0.0, 1.0, length)[:, None]
    bands = (HY_EMB - 1) // 2
    wpos = 2.0 * np.pi * np.arange(length)[:, None] / length
    fb = np.linspace(1e-4, bands - 1, bands)[None]
    z = np.concatenate([t, np.cos(fb * wpos), -np.sin(fb * wpos)], axis=-1)
    zpad = np.zeros((length, LANES))
    zpad[:, :HY_EMB] = z
    f = np.arange(length)[:, None]
    n = np.arange(length)[None, :]
    ang = 2.0 * np.pi * ((f * n) % n2) / n2
    cos_m = np.cos(ang)
    sin_m = np.sin(ang)
    sin_m[0, :] = np.cos(np.pi * np.arange(length))
    fwd = np.concatenate([cos_m, sin_m], axis=0)
    scale = np.full((1, n2), 2.0 / n2)
    scale[0, 0] = 1.0 / n2
    scale[0, length] = 1.0 / n2
    inv = fwd.T * scale
    return (zpad.astype(np.float32), t.astype(np.float32), fwd.astype(np.float32),
            inv.astype(np.float32))


def _hyena(p_hy, conv_w, conv_b, filt, skip):
    bsz, length, three_w = p_hy.shape
    width = three_w // 3
    w_in, b_in, w_mid, b_mid, w_out, freq = filt
    zpad, t, fwd, inv = _hyena_tables(length)
    deltas = np.abs(np.linspace(HY_MIN_DECAY, HY_MAX_DECAY, width))[None, :]
    dec = np.exp(-t.astype(np.float64) * deltas).astype(np.float32)
    dec2 = jnp.asarray(np.concatenate([dec, dec], axis=1))
    ffn = w_in.shape[1]
    w_in_pad = jnp.zeros((LANES, ffn), F32).at[:HY_EMB].set(w_in)
    hfilt = _hy_filter(jnp.asarray(zpad), w_in_pad, b_in.reshape(1, ffn), w_mid,
                       b_mid.reshape(-1, 1, ffn), w_out, freq.reshape(1, ffn), dec2)
    fwd_b = jnp.asarray(fwd).astype(BF16)
    inv_b = jnp.asarray(inv).astype(BF16)
    tf = min(512, length)
    kc, ks = _filt_spec(fwd_b, hfilt, tf)
    x0, vv = _hy_prep(p_hy, conv_w, conv_b.reshape(1, three_w))
    yc, ys = _dft_fwd(fwd_b, vv, kc, ks, tf)
    return _dft_inv(inv_b, yc, ys, vv, x0, skip.reshape(1, width), tf)


def _gelu_tanh(x):
    return 0.5 * x * (1.0 + jnp.tanh(math.sqrt(2.0 / math.pi) * (x + 0.044715 * x * x * x)))


def _lru_kernel(xl_ref, yl_ref, xc_ref, cw_ref, cb_ref, wa_ref, ba_ref, wx_ref, bx_ref, ap_ref,
                o_ref, xs_s, a_s, b_s, h_s, *, ctx_len, lat_len):
    tot = ctx_len + lat_len
    ngrp = tot // 8
    ngrp_ctx = ctx_len // 8
    xs_s[0:ctx_len, :] = _depthwise_conv(xc_ref[0], cw_ref[...], LRU_CONV // 2) + cb_ref[...]
    xs_s[ctx_len:, :] = _depthwise_conv(xl_ref[0], cw_ref[...], LRU_CONV // 2) + cb_ref[...]
    xs = xs_s[...]
    xsb = xs.astype(BF16)
    rows8 = lax.broadcasted_iota(jnp.int32, (ngrp, 8, xs.shape[1]), 1)
    for d in range(2):
        r = _sigmoid(_dot(xsb, wa_ref[d, 0].astype(BF16)) + ba_ref[d])
        gi = _sigmoid(_dot(xsb, wx_ref[d, 0].astype(BF16)) + bx_ref[d])
        log_a = -LRU_C * r * _softplus(ap_ref[d])
        a = jnp.exp(log_a)
        b = jnp.sqrt(1.0 - a * a) * (gi * xs)
        a = a.reshape(ngrp, 8, a.shape[1])
        b = b.reshape(ngrp, 8, b.shape[1])
        for s in (1, 2, 4):
            keep = (rows8 >= s) if d == 0 else (rows8 < 8 - s)
            shift = s if d == 0 else 8 - s
            sa = jnp.where(keep, pltpu.roll(a, shift, axis=1), 1.0)
            sb = jnp.where(keep, pltpu.roll(b, shift, axis=1), 0.0)
            b = a * sb + b
            a = a * sa
        a_s[d] = a.reshape(tot, a.shape[2])
        b_s[d] = b.reshape(tot, b.shape[2])

    def group_fwd(i, c):
        r0 = pl.multiple_of(i * 8, 8)
        hg = a_s[0, pl.ds(r0, 8), :] * c + b_s[0, pl.ds(r0, 8), :]
        h_s[pl.ds(r0, 8), :] = hg
        return jnp.broadcast_to(hg[7:8, :], hg.shape)

    def group_bwd(i, c):
        gidx = jnp.where(i < ngrp_ctx, ngrp_ctx - 1 - i, ngrp + ngrp_ctx - 1 - i)
        r0 = pl.multiple_of(gidx * 8, 8)
        hg = a_s[1, pl.ds(r0, 8), :] * c + b_s[1, pl.ds(r0, 8), :]
        h_s[pl.ds(r0, 8), :] += hg
        return jnp.broadcast_to(hg[0:1, :], hg.shape)

    zero = jnp.zeros((8, xs.shape[1]), F32)
    lax.fori_loop(0, ngrp, group_fwd, zero, unroll=4)
    lax.fori_loop(0, ngrp, group_bwd, zero, unroll=4)
    o_ref[0] = (h_s[ctx_len:, :] * _gelu_tanh(yl_ref[0])).astype(o_ref.dtype)


def _rglru(xb_l, yb_l, xb_c, conv_w, conv_b, wa, ba, wx, bx, a_param):
    bsz, lat_len, width = xb_l.shape
    ctx_len = xb_c.shape[1]
    tot = ctx_len + lat_len
    blk = width // LRU_HEADS

    def col(b, h):
        return (b, 0, h)

    def wcol(b, h):
        return (0, h)

    def w3(b, h):
        return (0, 0, h)

    return pl.pallas_call(
        functools.partial(_lru_kernel, ctx_len=ctx_len, lat_len=lat_len),
        grid=(bsz, LRU_HEADS),
        in_specs=[pl.BlockSpec((1, lat_len, blk), col),
                  pl.BlockSpec((1, lat_len, blk), col),
                  pl.BlockSpec((1, ctx_len, blk), col),
                  pl.BlockSpec((LRU_CONV, blk), wcol),
                  pl.BlockSpec((1, blk), wcol),
                  pl.BlockSpec((2, 1, blk, blk), lambda b, h: (0, h, 0, 0)),
                  pl.BlockSpec((2, 1, blk), w3),
                  pl.BlockSpec((2, 1, blk, blk), lambda b, h: (0, h, 0, 0)),
                  pl.BlockSpec((2, 1, blk), w3),
                  pl.BlockSpec((2, 1, blk), w3)],
        out_specs=pl.BlockSpec((1, lat_len, blk), col),
        out_shape=jax.ShapeDtypeStruct((bsz, lat_len, width), BF16),
        scratch_shapes=[pltpu.VMEM((tot, blk), F32),
                        pltpu.VMEM((2, tot, blk), F32),
                        pltpu.VMEM((2, tot, blk), F32),
                        pltpu.VMEM((tot, blk), F32)],
        compiler_params=_params("parallel", "parallel"),
        name="rglru",
    )(xb_l, yb_l, xb_c, conv_w, conv_b.reshape(1, width), wa, ba.reshape(2, 1, width), wx,
      bx.reshape(2, 1, width), a_param.reshape(2, 1, width))


def _post_mixer_kernel(*refs, n_in, has_bias):
    a_refs = refs[:n_in]
    w_refs = refs[n_in:2 * n_in]
    i = 2 * n_in
    b_ref = None
    if has_bias:
        b_ref = refs[i]
        i += 1
    x_ref, g1_ref, sc_ref, sh_ref, lng_ref, lnb_ref, rw_ref, rb_ref, base_ref = refs[i:i + 9]
    x1_ref, v_ref, route_ref, cnt_ref = refs[i + 9:]

    @pl.when((pl.program_id(0) == 0) & (pl.program_id(1) == 0))
    def _():
        cnt_ref[...] = base_ref[...]

    y = None
    for a_ref, w_ref in zip(a_refs, w_refs):
        t = _dot(a_ref[0], w_ref[...])
        y = t if y is None else y + t
    if has_bias:
        y = y + b_ref[...]
    x1 = _layer_norm(DEEPNORM_ALPHA * x_ref[0] + g1_ref[0] * y, lng_ref[...], lnb_ref[...])
    x1_ref[0] = x1
    v = x1 * (1.0 + sc_ref[0]) + sh_ref[0]
    v_ref[0] = _pack_bf16_pairs(v)
    logits = _dot3(v, rw_ref[...]) + rb_ref[...]
    tm = logits.shape[0]
    lane = lax.broadcasted_iota(jnp.int32, logits.shape, 1).astype(F32)
    work = logits
    picks, firsts = [], []
    m0 = None
    for kk in range(TOP_K):
        m = jnp.max(work, axis=-1, keepdims=True)
        if kk == 0:
            m0 = m
        first = jnp.min(jnp.where(work == m, lane, float(LANES)), axis=-1, keepdims=True)
        pick = lane == first
        picks.append(pick)
        firsts.append(first)
        work = jnp.where(pick, -jnp.inf, work)
    sel = jnp.where(picks[0] | picks[1] | picks[2] | picks[3], 1.0, 0.0)
    e = sel * jnp.exp(logits - m0)
    gate = e / jnp.sum(e, axis=-1, keepdims=True)
    ti = lax.broadcasted_iota(jnp.int32, (tm, tm), 0)
    tj = lax.broadcasted_iota(jnp.int32, (tm, tm), 1)
    before = jnp.where(ti > tj, 1.0, 0.0).astype(BF16)
    slot = _dot(before, sel.astype(BF16)) + cnt_ref[...]
    route = jnp.zeros(logits.shape, F32)
    for kk in range(TOP_K):
        rank = jnp.sum(jnp.where(picks[kk], slot, 0.0), axis=-1, keepdims=True)
        wgt = jnp.sum(jnp.where(picks[kk], gate, 0.0), axis=-1, keepdims=True)
        route = jnp.where(lane == float(kk), firsts[kk], route)
        route = jnp.where(lane == float(TOP_K + kk), rank, route)
        route = jnp.where(lane == float(2 * TOP_K + kk), wgt, route)
    route_ref[0] = route
    cnt_ref[...] += jnp.sum(sel, axis=0, keepdims=True)


def _post_mixer(acts, ws, bias, x, g1, sc2, sh2, ln_g, ln_b, router_w, router_b, base, tm):
    bsz, length, d = x.shape
    tm = min(tm, length)
    n_in = len(acts)

    def row(bi, i):
        return (bi, i, 0)

    def per_b(bi, i):
        return (bi, 0, 0)

    def const(bi, i):
        return (0, 0)

    in_specs = [pl.BlockSpec((1, tm, a.shape[2]), row) for a in acts]
    in_specs += [pl.BlockSpec(w.shape, const) for w in ws]
    args = list(acts) + list(ws)
    if bias is not None:
        in_specs.append(pl.BlockSpec((1, d), const))
        args.append(bias.reshape(1, d))
    in_specs += [pl.BlockSpec((1, tm, d), row),
                 pl.BlockSpec((1, 1, d), per_b), pl.BlockSpec((1, 1, d), per_b),
                 pl.BlockSpec((1, 1, d), per_b),
                 pl.BlockSpec((1, d), const), pl.BlockSpec((1, d), const),
                 pl.BlockSpec((d, LANES), const), pl.BlockSpec((1, LANES), const),
                 pl.BlockSpec((1, LANES), const)]
    rw = jnp.zeros((d, LANES), F32).at[:, :N_EXPERTS].set(router_w)
    rb = jnp.full((1, LANES), -1e30, F32).at[0, :N_EXPERTS].set(router_b)
    args += [x, g1, sc2, sh2, ln_g.reshape(1, d), ln_b.reshape(1, d), rw, rb, base]
    return pl.pallas_call(
        functools.partial(_post_mixer_kernel, n_in=n_in, has_bias=bias is not None),
        grid=(bsz, length // tm),
        in_specs=in_specs,
        out_specs=[pl.BlockSpec((1, tm, d), row), pl.BlockSpec((1, tm, d // 2), row),
                   pl.BlockSpec((1, tm, LANES), row), pl.BlockSpec((1, LANES), const)],
        out_shape=[jax.ShapeDtypeStruct((bsz, length, d), F32),
                   jax.ShapeDtypeStruct((bsz, length, d // 2), jnp.uint32),
                   jax.ShapeDtypeStruct((bsz, length, LANES), F32),
                   jax.ShapeDtypeStruct((1, LANES), F32)],
        compiler_params=_params("arbitrary", "arbitrary"),
        name="post_mixer",
    )(*args)


MOE_W1_CHUNK = 512


def _moe_rows_kernel(te_ref, nv_ref, first_ref, nxt_ref, xs_ref, w1_hbm, b1g_ref, b1l_ref, w2_hbm,
                     b2_ref, ys_ref, w1f_s, w2f_s, wt_s, w1g_s, w1l_s, w2b_s, sem, *, layer):
    i = pl.program_id(0)
    nv = nv_ref[i]

    def fetch(e):
        return (pltpu.make_async_copy(w1_hbm.at[layer, e], w1f_s, sem.at[0]),
                pltpu.make_async_copy(w2_hbm.at[layer, e], w2f_s, sem.at[1]))

    @pl.when(i == 0)
    def _():
        for cp in fetch(te_ref[0]):
            cp.start()

    @pl.when(first_ref[i] == 1)
    def _():
        for cp in fetch(te_ref[i]):
            cp.wait()
        half = MOE_W1_CHUNK // 2
        for c in range(w1f_s.shape[1] // MOE_W1_CHUNK):
            t = w1f_s[:, c * MOE_W1_CHUNK:(c + 1) * MOE_W1_CHUNK].T
            for j in range(wt_s.shape[0]):
                cols = slice(j * LANES, (j + 1) * LANES)
                wt_s[j] = t[:, cols]
                w1g_s[c * half:(c + 1) * half, cols] = wt_s[j, pl.ds(0, half, stride=2), :].astype(BF16)
                w1l_s[c * half:(c + 1) * half, cols] = wt_s[j, pl.ds(1, half, stride=2), :].astype(BF16)
        w2b_s[...] = w2f_s[...].astype(BF16)

        @pl.when(nxt_ref[i] >= 0)
        def _():
            for cp in fetch(nxt_ref[i]):
                cp.start()

    @pl.when(nv > 0)
    def _():
        rows = lax.broadcasted_iota(jnp.int32, xs_ref.shape, 0)
        x = _unpack_bf16_pairs(jnp.where(rows < nv, xs_ref[...], jnp.uint32(0))).astype(BF16)
        glu = jnp.minimum(_dot_nt(x, w1g_s[...]) + b1g_ref[0, 0], SWIGLU_LIMIT)
        lin = jnp.clip(_dot_nt(x, w1l_s[...]) + b1l_ref[0, 0], -SWIGLU_LIMIT, SWIGLU_LIMIT)
        act = glu * _sigmoid(SWIGLU_ALPHA * glu) * (lin + 1.0)
        ys_ref[...] = _pack_bf16_pairs(_dot(act.astype(BF16), w2b_s[...]) + b2_ref[0, 0])

    @pl.when(nv == 0)
    def _():
        ys_ref[...] = jnp.zeros_like(ys_ref)


def _moe_rows(xs, tile_expert, tile_rows, tile_first, tile_next, layer, w1, b1g, b1l, w2, b2, tm):
    n_rows, dh = xs.shape
    _, _, d, dff2 = w1.shape
    dff = dff2 // 2

    def row(i, te, nv, first, nxt):
        return (i, 0)

    def exp4(i, te, nv, first, nxt):
        return (layer, te[i], 0, 0)

    hbm = pl.BlockSpec(memory_space=pl.ANY)
    return pl.pallas_call(
        functools.partial(_moe_rows_kernel, layer=layer),
        grid_spec=pltpu.PrefetchScalarGridSpec(
            num_scalar_prefetch=4,
            grid=(n_rows // tm,),
            in_specs=[pl.BlockSpec((tm, dh), row), hbm,
                      pl.BlockSpec((1, 1, 1, dff), exp4), pl.BlockSpec((1, 1, 1, dff), exp4),
                      hbm, pl.BlockSpec((1, 1, 1, d), exp4)],
            out_specs=pl.BlockSpec((tm, dh), row),
            scratch_shapes=[pltpu.VMEM((d, dff2), F32), pltpu.VMEM((dff, d), F32),
                            pltpu.VMEM((d // LANES, MOE_W1_CHUNK, LANES), F32),
                            pltpu.VMEM((dff, d), BF16),
                            pltpu.VMEM((dff, d), BF16), pltpu.VMEM((dff, d), BF16),
                            pltpu.SemaphoreType.DMA((2,))]),
        out_shape=jax.ShapeDtypeStruct((n_rows, dh), jnp.uint32),
        compiler_params=_params("arbitrary"),
        name="moe_rows",
    )(tile_expert, tile_rows, tile_first, tile_next, xs, w1, b1g, b1l, w2, b2)


def _moe_combine_kernel(y0_ref, y1_ref, y2_ref, y3_ref, route_ref, x1_ref, g2_ref, lng_ref, lnb_ref,
                        o_ref):
    route = route_ref[...]
    lane = lax.broadcasted_iota(jnp.int32, route.shape, 1)
    f = None
    for kk, y_ref in enumerate((y0_ref, y1_ref, y2_ref, y3_ref)):
        wgt = jnp.sum(jnp.where(lane == 2 * TOP_K + kk, route, 0.0), axis=-1, keepdims=True)
        term = wgt * _unpack_bf16_pairs(y_ref[0])
        f = term if f is None else f + term
    o_ref[0] = _layer_norm(DEEPNORM_ALPHA * x1_ref[0] + g2_ref[0] * f, lng_ref[...], lnb_ref[...])


def _moe_combine(yg, route, row_offset, x1, g2, ln_g, ln_b, tm):
    bsz, length, d = x1.shape
    tm = min(tm, length)
    nt = length // tm
    off = row_offset // tm

    def pick(kk):
        return lambda bi, i: (kk, off + bi * nt + i, 0)

    def const(bi, i):
        return (0, 0)

    return pl.pallas_call(
        _moe_combine_kernel,
        grid=(bsz, nt),
        in_specs=[pl.BlockSpec((1, tm, d // 2), pick(kk)) for kk in range(TOP_K)] + [
            pl.BlockSpec((tm, LANES), lambda bi, i: (off + bi * nt + i, 0)),
            pl.BlockSpec((1, tm, d), lambda bi, i: (bi, i, 0)),
            pl.BlockSpec((1, 1, d), lambda bi, i: (bi, 0, 0)),
            pl.BlockSpec((1, d), const), pl.BlockSpec((1, d), const)],
        out_specs=pl.BlockSpec((1, tm, d), lambda bi, i: (bi, i, 0)),
        out_shape=jax.ShapeDtypeStruct((bsz, length, d), F32),
        compiler_params=_params("parallel", "parallel"),
        name="moe_combine",
    )(yg, yg, yg, yg, route, x1, g2, ln_g.reshape(1, d), ln_b.reshape(1, d))


SC_CORES = 2
SC_SUBCORES = 16
SC_WORKERS = SC_CORES * SC_SUBCORES
SC_WINDOW = 64


def _sc_row_pipeline(nwin, read, write):
    read(0, 0).start()

    @pl.loop(0, nwin, step=2)
    def _(w0):
        for b in range(2):
            w = w0 + b

            @pl.when(w + 1 < nwin)
            def _():
                @pl.when(w >= 1)
                def _():
                    write(w - 1, 1 - b).wait()

                read(w + 1, 1 - b).start()

            read(w, b).wait()
            write(w, b).start()

    write(nwin - 2, 0).wait()
    write(nwin - 1, 1).wait()


def _sc_scatter_rows(src, pos, n_out):
    t_rows, d = src.shape
    nw, nwin, win = pos.shape
    assert nw == SC_WORKERS and nwin % 2 == 0 and t_rows % (nwin * win) == 0
    mesh = plsc.VectorSubcoreMesh(core_axis_name="c", subcore_axis_name="s")

    @functools.partial(
        pl.kernel, mesh=mesh, out_type=jax.ShapeDtypeStruct((n_out, d), src.dtype),
        scratch_types=[pltpu.VMEM((nwin, win), jnp.int32), pltpu.VMEM((2, win, d), src.dtype),
                       pltpu.SemaphoreType.DMA((2,)), pltpu.SemaphoreType.DMA((2,))])
    def scatter(src_hbm, pos_hbm, out_hbm, idx_v, rows_v, rsem, wsem):
        wid = lax.axis_index("s") * SC_CORES + lax.axis_index("c")
        t0 = lax.rem(wid * (nwin * win), t_rows)
        pltpu.sync_copy(pos_hbm.at[wid], idx_v)

        def read(w, slot):
            return pltpu.make_async_copy(src_hbm.at[pl.ds(t0 + w * win, win)], rows_v.at[slot],
                                         rsem.at[slot])

        def write(w, slot):
            return pltpu.make_async_copy(rows_v.at[slot], out_hbm.at[idx_v.at[w]], wsem.at[slot])

        _sc_row_pipeline(nwin, read, write)

    return scatter(src, pos)


def _sc_gather_rows(table, pos):
    _, d = table.shape
    nw, nwin, win = pos.shape
    assert nw == SC_WORKERS and nwin % 2 == 0
    per = nwin * win
    mesh = plsc.VectorSubcoreMesh(core_axis_name="c", subcore_axis_name="s")

    @functools.partial(
        pl.kernel, mesh=mesh, out_type=jax.ShapeDtypeStruct((nw * per, d), table.dtype),
        scratch_types=[pltpu.VMEM((nwin, win), jnp.int32), pltpu.VMEM((2, win, d), table.dtype),
                       pltpu.SemaphoreType.DMA((2,)), pltpu.SemaphoreType.DMA((2,))])
    def gather(table_hbm, pos_hbm, out_hbm, idx_v, rows_v, rsem, wsem):
        wid = lax.axis_index("s") * SC_CORES + lax.axis_index("c")
        base = wid * per
        pltpu.sync_copy(pos_hbm.at[wid], idx_v)

        def read(w, slot):
            return pltpu.make_async_copy(table_hbm.at[idx_v.at[w]], rows_v.at[slot], rsem.at[slot])

        def write(w, slot):
            return pltpu.make_async_copy(rows_v.at[slot], out_hbm.at[pl.ds(base + w * win, win)],
                                         wsem.at[slot])

        _sc_row_pipeline(nwin, read, write)

    return gather(table, pos)


MOE_TILE = 256


def _moe_sparse(v_all, route, counts, layer, w1, b1g, b1l, w2, b2):
    t_rows, dh = v_all.shape
    n_exp = w1.shape[1]
    pairs = TOP_K * t_rows
    n_tiles = pairs // MOE_TILE + n_exp
    expert = route[:, 0:TOP_K].astype(jnp.int32)
    slot = route[:, TOP_K:2 * TOP_K].astype(jnp.int32)
    cnt = counts[0, :n_exp].astype(jnp.int32)
    tiles_per = (cnt + MOE_TILE - 1) // MOE_TILE
    tile_end = jnp.cumsum(tiles_per)
    tile_start = tile_end - tiles_per
    pos = (tile_start * MOE_TILE)[expert] + slot
    nwin = pairs // (SC_WORKERS * SC_WINDOW)
    pos_km = pos.T.reshape(SC_WORKERS, nwin, SC_WINDOW)
    tile_ids = jnp.arange(n_tiles, dtype=jnp.int32)[:, None]
    owns = (tile_ids >= tile_start[None, :]) & (tile_ids < tile_end[None, :])
    experts = jnp.arange(n_exp, dtype=jnp.int32)[None, :]
    last_used = jnp.max(jnp.where(tiles_per > 0, experts[0], 0))
    used = jnp.any(owns, axis=1)
    te = jnp.where(used, jnp.sum(jnp.where(owns, experts, 0), axis=1), last_used).astype(jnp.int32)
    rows_left = cnt[None, :] - (tile_ids - tile_start[None, :]) * MOE_TILE
    tile_rows = jnp.sum(jnp.where(owns, jnp.clip(rows_left, 0, MOE_TILE), 0), axis=1)
    tile_rows = tile_rows.astype(jnp.int32)
    tile_first = jnp.any(owns & (tile_ids == tile_start[None, :]), axis=1).astype(jnp.int32)
    later = (experts > experts.T) & (tiles_per[None, :] > 0)
    next_e = jnp.min(jnp.where(later, experts, n_exp), axis=1)
    next_e = jnp.where(next_e == n_exp, -1, next_e)
    tile_next = jnp.where(used, jnp.sum(jnp.where(owns, next_e[None, :], 0), axis=1), -1)
    tile_next = tile_next.astype(jnp.int32)
    xs = _sc_scatter_rows(v_all, pos_km, n_tiles * MOE_TILE)
    ys = _moe_rows(xs, te, tile_rows, tile_first, tile_next, layer, w1, b1g, b1l, w2, b2, MOE_TILE)
    return _sc_gather_rows(ys, pos_km).reshape(TOP_K, t_rows, dh)


@functools.lru_cache(maxsize=None)
def _sincos_2d(rows, cols, dim):
    quarter = dim // 4
    omega = 1.0 / (10000.0 ** (np.arange(quarter, dtype=np.float64) / quarter))

    def emb1d(n):
        ang = np.arange(n, dtype=np.float64)[:, None] * omega
        return np.concatenate([np.sin(ang), np.cos(ang)], axis=-1)

    er = np.broadcast_to(emb1d(rows)[:, None], (rows, cols, dim // 2))
    ec = np.broadcast_to(emb1d(cols)[None], (rows, cols, dim // 2))
    return np.concatenate([er, ec], axis=-1).reshape(rows * cols, dim).astype(np.float32)


def _gate_layout(gates, n_ch):
    bsz, length, _ = gates.shape
    g = gates[..., :n_ch].reshape(bsz, length // DN_CHUNK, DN_CHUNK, n_ch)
    return g.transpose(0, 3, 1, 2)


def kernel(x, c, ctx, c_ctx, ada_w, ada_b, ln_g, ln_b, ev_w_in, ev_w_out, dn_conv_w, dn_a_log, dn_dt_bias, dn_norm_g, hy_conv_w, hy_conv_b, hy_w_in, hy_b_in, hy_w_mid, hy_b_mid, hy_w_out, hy_freq, hy_skip, od_w_in, od_b_in, lru_conv_w, lru_conv_b, lru_wa, lru_ba, lru_wx, lru_bx, lru_a_param, od_w_out, od_b_out, router_w, router_b, moe_w1, moe_b1, moe_w2, moe_b2):
    bsz, length, d = x.shape
    ctx_len = ctx.shape[1]
    pos = jnp.asarray(_sincos_2d(length // GRID_W, GRID_W, d))

    cond = jnp.zeros((16, d), F32).at[:bsz].set(c).at[bsz].set(c_ctx)
    mod = _modulation(cond, ada_w, ada_b).reshape(DEPTH, 16, 6, d)

    def lat_mod(layer, k):
        return mod[layer, :bsz, k][:, None, :]

    def ctx_mod(layer, k):
        return jnp.broadcast_to(mod[layer, bsz, k][None, None, :], (bsz, 1, d))

    hc = ctx
    for layer in range(DEPTH):
        last = layer == DEPTH - 1
        j = layer // 2
        if layer % 2 == 0:
            dn_qk = DN_HEADS * DN_DK
            dn_qkv = 3 * dn_qk
            dn_in = dn_qkv + dn_qk + 4 * DN_HEADS
            w_in = ev_w_in[j]
            gate_w = jnp.zeros((d, LANES), F32).at[:, :4 * DN_HEADS].set(w_in[:, dn_qkv + dn_qk:dn_in])
            w_cat = jnp.concatenate([w_in[:, :dn_qkv + dn_qk], gate_w, w_in[:, dn_in:]],
                                    axis=1).astype(BF16)
            hy_in = w_in.shape[1] - dn_in
            widths = (dn_qkv, dn_qk, LANES, hy_in)
            qkv_l, z_l, gt_l, phy_l = _inproj(x, lat_mod(layer, 1), lat_mod(layer, 0), pos,
                                               w_cat, None, widths, 512)
            qkv_c, z_c, gt_c, phy_c = _inproj(hc, ctx_mod(layer, 1), ctx_mod(layer, 0), None,
                                               w_cat, None, widths, 512)
            gates = jnp.concatenate([_gate_layout(gt_c, 4 * DN_HEADS),
                                     _gate_layout(gt_l, 4 * DN_HEADS)], axis=2)
            n_chunks = gates.shape[2]
            gates = jnp.pad(gates, ((0, 0), (0, 0), (0, -n_chunks % 16), (0, 0)))
            dn_l, dn_c = _deltanet(qkv_l, z_l, qkv_c, z_c, gates, dn_conv_w[j], dn_a_log[j],
                                   dn_dt_bias[j], dn_norm_g[j])
            filt = (hy_w_in[j], hy_b_in[j], hy_w_mid[j], hy_b_mid[j], hy_w_out[j], hy_freq[j])
            hy_l = _hyena(phy_l, hy_conv_w[j], hy_conv_b[j], filt, hy_skip[j])
            w_out = ev_w_out[j].astype(BF16)
            half = dn_l.shape[2]
            acts_l, ws, bias = (dn_l, hy_l), (w_out[:half], w_out[half:]), None
            acts_c = None
            if not last:
                hy_c = _hyena(phy_c, hy_conv_w[j], hy_conv_b[j], filt, hy_skip[j])
                acts_c = (dn_c, hy_c)
        else:
            w_in = od_w_in[j].astype(BF16)
            width = w_in.shape[1] // 2
            b_in = od_b_in[j].reshape(1, 2 * width)
            xb_l, yb_l = _inproj(x, lat_mod(layer, 1), lat_mod(layer, 0), pos, w_in, b_in,
                                 (width, width), 512)
            xb_c, _ = _inproj(hc, ctx_mod(layer, 1), ctx_mod(layer, 0), None, w_in, b_in,
                              (width, width), 512)
            act_l = _rglru(xb_l, yb_l, xb_c, lru_conv_w[j], lru_conv_b[j], lru_wa[j], lru_ba[j],
                           lru_wx[j], lru_bx[j], lru_a_param[j])
            acts_l, ws, bias = (act_l,), (od_w_out[j].astype(BF16),), od_b_out[j]
            acts_c = None
            assert last, "context outputs of the RG-LRU layer are only needed before the last layer"


        x1, v, route, counts = _post_mixer(acts_l, ws, bias, x, lat_mod(layer, 2),
                                           lat_mod(layer, 4), lat_mod(layer, 3), ln_g[layer, 0],
                                           ln_b[layer, 0], router_w[layer], router_b[layer],
                                           jnp.zeros((1, LANES), F32), 512)
        v_all = v.reshape(bsz * length, d // 2)
        route = route.reshape(bsz * length, LANES)
        if not last:
            hc1, vc, route_c, counts = _post_mixer(acts_c, ws, bias, hc, ctx_mod(layer, 2),
                                                   ctx_mod(layer, 4), ctx_mod(layer, 3),
                                                   ln_g[layer, 0], ln_b[layer, 0], router_w[layer],
                                                   router_b[layer], counts, 256)
            v_all = jnp.concatenate([v_all, vc.reshape(bsz * ctx_len, d // 2)], axis=0)
            route = jnp.concatenate([route, route_c.reshape(bsz * ctx_len, LANES)], axis=0)
        yg = _moe_sparse(v_all, route, counts, layer, moe_w1, moe_b1[:, :, None, 0::2],
                         moe_b1[:, :, None, 1::2], moe_w2, moe_b2[:, :, None, :])
        x = _moe_combine(yg, route, 0, x1, lat_mod(layer, 5), ln_g[layer, 1], ln_b[layer, 1], 256)
        if not last:
            hc = _moe_combine(yg, route, bsz * length, hc1, ctx_mod(layer, 5), ln_g[layer, 1],
                              ln_b[layer, 1], 256)
    return x
```

```python
import functools
import math

import numpy as np
import jax
import jax.numpy as jnp
from jax import lax
from jax.experimental import pallas as pl
from jax.experimental.pallas import tpu as pltpu
from jax.experimental.pallas import tpu_sc as plsc

F32 = jnp.float32
BF16 = jnp.bfloat16

VMEM_LIMIT_BYTES = 56 * 1024 * 1024
LANES = 128

DEPTH = 2
GRID_W = 64
DEEPNORM_ALPHA = (2.0 * DEPTH) ** 0.25
LN_EPS = 1e-5
RMS_EPS = 1e-6

DN_HEADS = 4
DN_DK = 128
DN_CHUNK = 64
DN_CONV = 4

HY_EMB = 33
HY_TARGET = 1e-2
HY_MIN_DECAY = math.log(HY_TARGET) / 1.5
HY_MAX_DECAY = math.log(HY_TARGET) / 0.3
HY_CONV = 3

LRU_HEADS = 4
LRU_C = 8.0
LRU_CONV = 4

N_EXPERTS = 32
TOP_K = 4
SWIGLU_ALPHA = 1.702
SWIGLU_LIMIT = 7.0


def _params(*sem):
    return pltpu.CompilerParams(dimension_semantics=sem, vmem_limit_bytes=VMEM_LIMIT_BYTES)


def _dot(a, b):
    return jnp.dot(a, b, preferred_element_type=F32)


def _dot_nt(a, b):
    return lax.dot_general(a, b, (((1,), (1,)), ((), ())), preferred_element_type=F32)


def _split(a):
    hi = a.astype(BF16)
    lo = (a - hi.astype(F32)).astype(BF16)
    return hi, lo


def _dot3(a, b):
    ah, al = _split(a)
    bh, bl = _split(b)
    return _dot(ah, bh) + _dot(ah, bl) + _dot(al, bh)


def _silu(x):
    return x * (1.0 / (1.0 + jnp.exp(-x)))


def _sigmoid(x):
    return 1.0 / (1.0 + jnp.exp(-x))


def _softplus(x):
    return jnp.maximum(x, 0.0) + jnp.log(1.0 + jnp.exp(-jnp.abs(x)))


def _layer_norm(x, g, b):
    mu = jnp.mean(x, axis=-1, keepdims=True)
    xc = x - mu
    var = jnp.mean(xc * xc, axis=-1, keepdims=True)
    return xc * lax.rsqrt(var + LN_EPS) * g + b


def _pack_bf16_pairs(x):
    w = x.shape[1] // 2
    lo = pltpu.bitcast(x[:, :w].astype(BF16).astype(F32), jnp.uint32) >> 16
    hi = pltpu.bitcast(x[:, w:].astype(BF16).astype(F32), jnp.uint32) & jnp.uint32(0xFFFF0000)
    return lo | hi


def _unpack_bf16_pairs(p):
    lo = pltpu.bitcast(p << 16, F32)
    hi = pltpu.bitcast(p & jnp.uint32(0xFFFF0000), F32)
    return jnp.concatenate([lo, hi], axis=1)


def _shift_rows(x, s):
    if s == 0:
        return x
    n = x.shape[0]
    rows = lax.broadcasted_iota(jnp.int32, x.shape, 0)
    valid = (rows >= s) if s > 0 else (rows < n + s)
    return jnp.where(valid, pltpu.roll(x, s % n, axis=0), 0.0)


def _depthwise_conv(x, w, pad_left):
    acc = None
    for i in range(w.shape[0]):
        term = _shift_rows(x, pad_left - i) * w[i:i + 1, :]
        acc = term if acc is None else acc + term
    return acc


def _mod_kernel(c_ref, w_ref, b_ref, o_ref):
    o_ref[0] = _dot3(_silu(c_ref[...]), w_ref[0]) + b_ref[0]


def _modulation(cond, ada_w, ada_b):
    depth, d, n = ada_w.shape
    rows = cond.shape[0]
    tn = 1536
    return pl.pallas_call(
        _mod_kernel,
        grid=(depth, n // tn),
        in_specs=[
            pl.BlockSpec((rows, d), lambda l, j: (0, 0)),
            pl.BlockSpec((1, d, tn), lambda l, j: (l, 0, j)),
            pl.BlockSpec((1, 1, tn), lambda l, j: (l, 0, j)),
        ],
        out_specs=pl.BlockSpec((1, rows, tn), lambda l, j: (l, 0, j)),
        out_shape=jax.ShapeDtypeStruct((depth, rows, n), F32),
        compiler_params=_params("parallel", "parallel"),
        name="modulation",
    )(cond, ada_w, ada_b.reshape(depth, 1, n))


def _inproj_kernel(*refs, splits, has_pos, has_bias):
    x_ref, sc_ref, sh_ref = refs[:3]
    i = 3
    pos_ref = None
    if has_pos:
        pos_ref = refs[i]
        i += 1
    w_ref = refs[i]
    i += 1
    b_ref = None
    if has_bias:
        b_ref = refs[i]
        i += 1
    o_refs = refs[i:]
    u = x_ref[0] * (1.0 + sc_ref[0]) + sh_ref[0]
    if has_pos:
        u = u + pos_ref[...]
    ub = u.astype(BF16)
    for o_ref, (s, e) in zip(o_refs, splits):
        acc = _dot(ub, w_ref[:, s:e])
        if has_bias:
            acc = acc + b_ref[:, s:e]
        o_ref[0] = acc


def _inproj(x, sc, sh, pos, w, b, widths, tm):
    bsz, length, d = x.shape
    n = w.shape[1]
    splits, s = [], 0
    for wd in widths:
        splits.append((s, s + wd))
        s += wd
    assert s == n
    tm = min(tm, length)
    in_specs = [
        pl.BlockSpec((1, tm, d), lambda bi, i: (bi, i, 0)),
        pl.BlockSpec((1, 1, d), lambda bi, i: (bi, 0, 0)),
        pl.BlockSpec((1, 1, d), lambda bi, i: (bi, 0, 0)),
    ]
    args = [x, sc, sh]
    if pos is not None:
        in_specs.append(pl.BlockSpec((tm, d), lambda bi, i: (i, 0)))
        args.append(pos)
    in_specs.append(pl.BlockSpec((d, n), lambda bi, i: (0, 0)))
    args.append(w)
    if b is not None:
        in_specs.append(pl.BlockSpec((1, n), lambda bi, i: (0, 0)))
        args.append(b)
    return pl.pallas_call(
        functools.partial(_inproj_kernel, splits=tuple(splits), has_pos=pos is not None,
                          has_bias=b is not None),
        grid=(bsz, length // tm),
        in_specs=in_specs,
        out_specs=[pl.BlockSpec((1, tm, wd), lambda bi, i: (bi, i, 0)) for wd in widths],
        out_shape=[jax.ShapeDtypeStruct((bsz, length, wd), F32) for wd in widths],
        compiler_params=_params("parallel", "parallel"),
        name="inproj",
    )(*args)


def _unit_tri_inverses(mats, lower):
    n = mats[0].shape[0]
    nb = 16
    np_ = len(mats)
    ii = lax.broadcasted_iota(jnp.int32, (n, n), 0)
    jj = lax.broadcasted_iota(jnp.int32, (n, n), 1)
    same16 = (ii // nb) == (jj // nb)
    same32 = (ii // (2 * nb)) == (jj // (2 * nb))
    dgs = []
    for a in mats:
        ad = jnp.where(same16, a, 0.0)
        dgs.append(ad[0:nb] + ad[nb:2 * nb] + ad[2 * nb:3 * nb] + ad[3 * nb:4 * nb])
    dg = jnp.concatenate(dgs, axis=0)
    rr = lax.broadcasted_iota(jnp.int32, dg.shape, 0)
    ll = lax.broadcasted_iota(jnp.int32, dg.shape, 1)
    xd = jnp.where(rr % nb == ll % nb, 1.0, 0.0)
    blk0 = (ll // nb) * nb
    for s in (range(nb - 1) if lower else range(nb - 1, 0, -1)):
        col = jnp.take_along_axis(dg, blk0 + s, axis=1)
        row = jnp.concatenate(
            [jnp.broadcast_to(xd[p * nb + s:p * nb + s + 1, :], (nb, n)) for p in range(np_)], axis=0)
        xd = xd - col * row
    ds = [jnp.where(same16, jnp.concatenate([xd[p * nb:(p + 1) * nb]] * (n // nb), axis=0), 0.0)
          for p in range(np_)]
    lvl1 = same32 & jnp.logical_not(same16)
    t1 = [_dot3(d, jnp.where(lvl1, a, 0.0)) for d, a in zip(ds, mats)]
    x1 = [d - _dot3(t, d) for d, t in zip(ds, t1)]
    x1b = [x.astype(BF16) for x in x1]
    t2 = [_dot(xb, jnp.where(same32, 0.0, a).astype(BF16)) for xb, a in zip(x1b, mats)]
    return [x - _dot(t.astype(BF16), xb) for x, t, xb in zip(x1, t2, x1b)]


def _dn_kernel(alog_ref, dtb_ref,
               ql_ref, kl_ref, vl_ref, zl_ref, qc_ref, kc_ref, vc_ref, zc_ref, gt_ref,
               cwq_ref, cwk_ref, cwv_ref, ng_ref,
               yl_ref, yc_ref,
               qn_s, kn_s, vn_s, gc_s, bt_s, nq_s, c_s, gl_s, o_s,
               *, ctx_len, lat_len):
    h = pl.program_id(1)
    csz = DN_CHUNK
    nc_ctx = ctx_len // csz
    nc = (ctx_len + lat_len) // csz

    def prep(src_ref, cw_ref, kind):
        t = _silu(_depthwise_conv(src_ref[0], cw_ref[...], DN_CONV // 2))
        if kind == "v":
            return t
        t = t * lax.rsqrt(jnp.sum(t * t, axis=-1, keepdims=True) + RMS_EPS)
        return t * (DN_DK ** -0.5) if kind == "q" else t

    qn_s[0:ctx_len, :] = prep(qc_ref, cwq_ref, "q")
    qn_s[ctx_len:, :] = prep(ql_ref, cwq_ref, "q")
    kn_s[0:ctx_len, :] = prep(kc_ref, cwk_ref, "k")
    kn_s[ctx_len:, :] = prep(kl_ref, cwk_ref, "k")
    vn_s[0:ctx_len, :] = prep(vc_ref, cwv_ref, "v")
    vn_s[ctx_len:, :] = prep(vl_ref, cwv_ref, "v")

    ii = lax.broadcasted_iota(jnp.int32, (csz, csz), 0)
    jj = lax.broadcasted_iota(jnp.int32, (csz, csz), 1)
    eye = ii == jj
    for d in range(2):
        graw = gt_ref[0, d * 2 * DN_HEADS + h]
        braw = gt_ref[0, d * 2 * DN_HEADS + DN_HEADS + h]
        a_neg = -jnp.exp(jnp.zeros_like(graw) + alog_ref[d, h])
        g = a_neg * _softplus(graw + dtb_ref[d, h])
        tri = jnp.where((ii <= jj) if d == 0 else (ii >= jj), 1.0, 0.0).astype(BF16)
        g1 = g.astype(BF16)
        r1 = g - g1.astype(F32)
        g2 = r1.astype(BF16)
        g3 = (r1 - g2.astype(F32)).astype(BF16)
        gc_s[d] = _dot(g1, tri) + _dot(g2, tri) + _dot(g3, tri)
        bt_s[d] = _sigmoid(braw)

    o_s[...] = jnp.zeros_like(o_s)

    group = 12
    assert nc % group == 0

    def chunk_prep(gi, carry):
        ns = [gi * group + c for c in range(group)]
        r0s = [pl.multiple_of(n * csz, csz) for n in ns]
        qs = [qn_s[pl.ds(r0, csz), :] for r0 in r0s]
        ks = [kn_s[pl.ds(r0, csz), :] for r0 in r0s]
        vs = [vn_s[pl.ds(r0, csz), :] for r0 in r0s]
        kbfs = [k.astype(BF16) for k in ks]
        qks = [_dot_nt(q.astype(BF16), kbf) for q, kbf in zip(qs, kbfs)]
        for d in range(2):
            incl = (ii >= jj) if d == 0 else (ii <= jj)
            strict = (ii > jj) if d == 0 else (ii < jj)
            grs = [gc_s[d, pl.ds(n, 1), :] for n in ns]
            grows = [jnp.broadcast_to(gr, (csz, csz)) for gr in grs]
            gcols = [jnp.sum(jnp.where(eye, grow, 0.0), axis=1, keepdims=True) for grow in grows]
            bcols = [jnp.sum(jnp.where(eye, jnp.broadcast_to(bt_s[d, pl.ds(n, 1), :], (csz, csz)),
                                       0.0), axis=1, keepdims=True) for n in ns]
            decays = [jnp.where(incl, jnp.exp(jnp.where(incl, gcol - grow, 0.0)), 0.0)
                      for gcol, grow in zip(gcols, grows)]
            kbs = [k * bcol for k, bcol in zip(ks, bcols)]
            amats = [jnp.where(strict, _dot_nt(kb.astype(BF16), kbf) * decay, 0.0)
                     for kb, kbf, decay in zip(kbs, kbfs, decays)]
            tbs = [t.astype(BF16) for t in _unit_tri_inverses(amats, lower=(d == 0))]
            cs = range(group)
            egs = [jnp.exp(gcols[c]) for c in cs]
            ubs = [_dot(tbs[c], (vs[c] * bcols[c]).astype(BF16)).astype(BF16) for c in cs]
            wbs = [_dot(tbs[c], (kbs[c] * egs[c]).astype(BF16)).astype(BF16) for c in cs]
            attns = [jnp.where(incl, qks[c] * decays[c], 0.0).astype(BF16) for c in cs]
            glasts = [grs[c][:, csz - 1:csz] if d == 0 else grs[c][:, 0:1] for c in cs]
            kdts = [(ks[c] * jnp.exp(glasts[c] - gcols[c])).T.astype(BF16) for c in cs]
            nmats = [_dot(kdts[c], wbs[c]).astype(BF16) for c in cs]
            qmats = [(qs[c] * egs[c] - _dot(attns[c], wbs[c])).astype(BF16) for c in cs]
            cmats = [_dot(kdts[c], ubs[c]) for c in cs]
            omats = [_dot(attns[c], ubs[c]) for c in cs]
            for c in cs:
                n = ns[c]
                nq_s[d, n, 0:DN_DK, :] = nmats[c]
                nq_s[d, n, DN_DK:DN_DK + csz, :] = qmats[c]
                c_s[d, n] = cmats[c]
                o_s[pl.ds(r0s[c], csz), :] += omats[c]
                gl_s[d, pl.ds(n, 1), :] = jnp.broadcast_to(jnp.exp(glasts[c]), (1, LANES))
        return carry

    lax.fori_loop(0, nc // group, chunk_prep, 0)

    def step(i, states):
        new_states = []
        for d in range(2):
            if d == 0:
                n = i
            else:
                n = jnp.where(i < nc_ctx, nc_ctx - 1 - i, nc + nc_ctx - 1 - i)
            r0 = pl.multiple_of(n * csz, csz)
            s = states[d]
            r = _dot(nq_s[d, n], s.astype(BF16))
            o_s[pl.ds(r0, csz), :] += r[DN_DK:DN_DK + csz]
            new_states.append(s * gl_s[d, pl.ds(n, 1), :] - r[0:DN_DK] + c_s[d, n])
        return tuple(new_states)

    zero = jnp.zeros((DN_DK, DN_DK), F32)
    lax.fori_loop(0, nc, step, (zero, zero))

    def gated_norm(o, z):
        o = o * lax.rsqrt(jnp.mean(o * o, axis=-1, keepdims=True) + RMS_EPS) * ng_ref[...]
        return (o * _silu(z)).astype(yl_ref.dtype)

    yc_ref[0] = gated_norm(o_s[0:ctx_len, :], zc_ref[0])
    yl_ref[0] = gated_norm(o_s[ctx_len:, :], zl_ref[0])


def _deltanet(qkv_l, z_l, qkv_c, z_c, gates, conv_w, a_log, dt_bias, norm_g):
    bsz, lat_len, _ = qkv_l.shape
    ctx_len = qkv_c.shape[1]
    tot = ctx_len + lat_len
    nc = tot // DN_CHUNK
    ncp = gates.shape[2]
    hd = DN_DK
    nh = DN_HEADS

    def col(off):
        return lambda b, h: (b, 0, off + h)

    def wcol(off):
        return lambda b, h: (0, off + h)

    smem = pl.BlockSpec(memory_space=pltpu.SMEM)
    in_specs = [
        smem, smem,
        pl.BlockSpec((1, lat_len, hd), col(0)),
        pl.BlockSpec((1, lat_len, hd), col(nh)),
        pl.BlockSpec((1, lat_len, hd), col(2 * nh)),
        pl.BlockSpec((1, lat_len, hd), col(0)),
        pl.BlockSpec((1, ctx_len, hd), col(0)),
        pl.BlockSpec((1, ctx_len, hd), col(nh)),
        pl.BlockSpec((1, ctx_len, hd), col(2 * nh)),
        pl.BlockSpec((1, ctx_len, hd), col(0)),
        pl.BlockSpec((1, 4 * nh, ncp, DN_CHUNK), lambda b, h: (b, 0, 0, 0)),
        pl.BlockSpec((DN_CONV, hd), wcol(0)),
        pl.BlockSpec((DN_CONV, hd), wcol(nh)),
        pl.BlockSpec((DN_CONV, hd), wcol(2 * nh)),
        pl.BlockSpec((1, hd), lambda b, h: (0, 0)),
    ]
    scratch = [
        pltpu.VMEM((tot, hd), F32), pltpu.VMEM((tot, hd), F32), pltpu.VMEM((tot, hd), F32),
        pltpu.VMEM((2, ncp, DN_CHUNK), F32), pltpu.VMEM((2, ncp, DN_CHUNK), F32),
        pltpu.VMEM((2, nc, hd + DN_CHUNK, hd), BF16),
        pltpu.VMEM((2, nc, hd, hd), F32),
        pltpu.VMEM((2, nc, LANES), F32),
        pltpu.VMEM((tot, hd), F32),
    ]
    return pl.pallas_call(
        functools.partial(_dn_kernel, ctx_len=ctx_len, lat_len=lat_len),
        grid=(bsz, nh),
        in_specs=in_specs,
        out_specs=[pl.BlockSpec((1, lat_len, hd), col(0)),
                   pl.BlockSpec((1, ctx_len, hd), col(0))],
        out_shape=[jax.ShapeDtypeStruct((bsz, lat_len, nh * hd), BF16),
                   jax.ShapeDtypeStruct((bsz, ctx_len, nh * hd), BF16)],
        scratch_shapes=scratch,
        compiler_params=_params("parallel", "parallel"),
        name="deltanet",
    )(a_log, dt_bias, qkv_l, qkv_l, qkv_l, z_l, qkv_c, qkv_c, qkv_c, z_c, gates,
      conv_w, conv_w, conv_w, norm_g.reshape(1, hd))


def _hy_filter_kernel(z_ref, win_ref, bin_ref, wmid_ref, bmid_ref, wout_ref, freq_ref, dec_ref,
                      o_ref):
    freq = freq_ref[...]
    hcur = jnp.sin(freq * (_dot3(z_ref[...], win_ref[...]) + bin_ref[...]))
    for i in range(wmid_ref.shape[0]):
        hcur = jnp.sin(freq * (_dot3(hcur, wmid_ref[i]) + bmid_ref[i]))
    o_ref[...] = _dot3(hcur, wout_ref[...]) * dec_ref[...]


def _hy_filter(z, w_in, b_in, w_mid, b_mid, w_out, freq, dec2):
    length = z.shape[0]
    n_out = w_out.shape[1]
    tl = min(256, length)

    def whole(a):
        return pl.BlockSpec(a.shape, lambda i: (0,) * a.ndim)

    return pl.pallas_call(
        _hy_filter_kernel,
        grid=(length // tl,),
        in_specs=[pl.BlockSpec((tl, z.shape[1]), lambda i: (i, 0)),
                  whole(w_in), whole(b_in), whole(w_mid), whole(b_mid), whole(w_out), whole(freq),
                  pl.BlockSpec((tl, n_out), lambda i: (i, 0))],
        out_specs=pl.BlockSpec((tl, n_out), lambda i: (i, 0)),
        out_shape=jax.ShapeDtypeStruct((length, n_out), F32),
        compiler_params=_params("parallel"),
        name="hyena_filter",
    )(z, w_in, b_in, w_mid, b_mid, w_out, freq, dec2)


def _filt_spec_kernel(fc_ref, fs_ref, h_ref, kc_ref, ks_ref, *, width):
    j = pl.program_id(0)
    hmat = h_ref[...]
    rows = lax.broadcasted_iota(jnp.int32, hmat.shape, 0)
    cols = lax.broadcasted_iota(jnp.int32, hmat.shape, 1)
    hmat = jnp.where((rows == 0) & (cols >= width), 0.0, hmat)
    hh, hl = _split(hmat)
    c = _dot(fc_ref[...], hh) + _dot(fc_ref[...], hl)
    s = _dot(fs_ref[...], hh) + _dot(fs_ref[...], hl)
    kc_ref[...] = c[:, :width] + c[:, width:]
    orow = lax.broadcasted_iota(jnp.int32, (c.shape[0], width), 0)
    sign = jnp.where((orow == 0) & (j == 0), 1.0, -1.0)
    ks_ref[...] = s[:, :width] + sign * s[:, width:]


def _filt_spec(fwd, hfilt, tf):
    length, two_w = hfilt.shape
    width = two_w // 2
    nt = length // tf
    return pl.pallas_call(
        functools.partial(_filt_spec_kernel, width=width),
        grid=(nt,),
        in_specs=[pl.BlockSpec((tf, length), lambda j: (j, 0)),
                  pl.BlockSpec((tf, length), lambda j: (nt + j, 0)),
                  pl.BlockSpec((length, two_w), lambda j: (0, 0))],
        out_specs=[pl.BlockSpec((tf, width), lambda j: (j, 0)),
                   pl.BlockSpec((tf, width), lambda j: (j, 0))],
        out_shape=[jax.ShapeDtypeStruct((length, width), F32)] * 2,
        compiler_params=_params("arbitrary"),
        name="hyena_filter_spectrum",
    )(fwd, fwd, hfilt)


def _hy_prep_kernel(x0_ref, x1_ref, v_ref, w0_ref, w1_ref, w2_ref, b0_ref, b1_ref, b2_ref,
                    x0o_ref, vvo_ref):
    x0 = _depthwise_conv(x0_ref[0], w0_ref[...], HY_CONV // 2) + b0_ref[...]
    x1 = _depthwise_conv(x1_ref[0], w1_ref[...], HY_CONV // 2) + b1_ref[...]
    v = _depthwise_conv(v_ref[0], w2_ref[...], HY_CONV // 2) + b2_ref[...]
    x0o_ref[0] = x0
    vvo_ref[0] = v * x1


def _hy_prep(p_hy, conv_w, conv_b):
    bsz, length, three_w = p_hy.shape
    width = three_w // 3
    nb = width // LANES

    def col(off):
        return lambda b, j: (b, 0, off + j)

    def wcol(off):
        return lambda b, j: (0, off + j)

    k = conv_w.shape[0]
    return pl.pallas_call(
        _hy_prep_kernel,
        grid=(bsz, nb),
        in_specs=[pl.BlockSpec((1, length, LANES), col(0)),
                  pl.BlockSpec((1, length, LANES), col(nb)),
                  pl.BlockSpec((1, length, LANES), col(2 * nb)),
                  pl.BlockSpec((k, LANES), wcol(0)),
                  pl.BlockSpec((k, LANES), wcol(nb)),
                  pl.BlockSpec((k, LANES), wcol(2 * nb)),
                  pl.BlockSpec((1, LANES), wcol(0)),
                  pl.BlockSpec((1, LANES), wcol(nb)),
                  pl.BlockSpec((1, LANES), wcol(2 * nb))],
        out_specs=[pl.BlockSpec((1, length, LANES), col(0)),
                   pl.BlockSpec((1, length, LANES), col(0))],
        out_shape=[jax.ShapeDtypeStruct((bsz, length, width), F32)] * 2,
        compiler_params=_params("parallel", "parallel"),
        name="hyena_prep",
    )(p_hy, p_hy, p_hy, conv_w, conv_w, conv_w, conv_b, conv_b, conv_b)


def _dft_fwd_kernel(fc_ref, fs_ref, v_ref, kc_ref, ks_ref, yc_ref, ys_ref, vb_s):
    j = pl.program_id(1)

    @pl.when(j == 0)
    def _():
        vb_s[...] = v_ref[0].astype(BF16)

    uc = _dot(fc_ref[...], vb_s[...])
    us = _dot(fs_ref[...], vb_s[...])
    kc = kc_ref[...]
    ks = ks_ref[...]
    rows = lax.broadcasted_iota(jnp.int32, uc.shape, 0)
    special = (rows == 0) & (j == 0)
    yc_ref[0] = (uc * kc - jnp.where(special, 0.0, us * ks)).astype(BF16)
    ys_ref[0] = jnp.where(special, us * ks, uc * ks + us * kc).astype(BF16)


def _dft_fwd(fwd, vv, kc, ks, tf):
    bsz, length, width = vv.shape
    nt = length // tf
    return pl.pallas_call(
        _dft_fwd_kernel,
        grid=(bsz, nt),
        in_specs=[pl.BlockSpec((tf, length), lambda b, j: (j, 0)),
                  pl.BlockSpec((tf, length), lambda b, j: (nt + j, 0)),
                  pl.BlockSpec((1, length, width), lambda b, j: (b, 0, 0)),
                  pl.BlockSpec((tf, width), lambda b, j: (j, 0)),
                  pl.BlockSpec((tf, width), lambda b, j: (j, 0))],
        out_specs=[pl.BlockSpec((1, tf, width), lambda b, j: (b, j, 0)),
                   pl.BlockSpec((1, tf, width), lambda b, j: (b, j, 0))],
        out_shape=[jax.ShapeDtypeStruct((bsz, length, width), BF16)] * 2,
        scratch_shapes=[pltpu.VMEM((length, width), BF16)],
        compiler_params=_params("parallel", "arbitrary"),
        name="hyena_dft_fwd",
    )(fwd, fwd, vv, kc, ks)


def _dft_inv_kernel(ic_ref, is_ref, yc_ref, ys_ref, vv_ref, x0_ref, skip_ref, o_ref):
    y = _dot(ic_ref[...], yc_ref[0]) + _dot(is_ref[...], ys_ref[0])
    o_ref[0] = (x0_ref[0] * (y + vv_ref[0] * skip_ref[...])).astype(o_ref.dtype)


def _dft_inv(inv, yc, ys, vv, x0, skip, tt):
    bsz, length, width = vv.shape
    nt = length // tt
    return pl.pallas_call(
        _dft_inv_kernel,
        grid=(bsz, nt),
        in_specs=[pl.BlockSpec((tt, length), lambda b, i: (i, 0)),
                  pl.BlockSpec((tt, length), lambda b, i: (i, 1)),
                  pl.BlockSpec((1, length, width), lambda b, i: (b, 0, 0)),
                  pl.BlockSpec((1, length, width), lambda b, i: (b, 0, 0)),
                  pl.BlockSpec((1, tt, width), lambda b, i: (b, i, 0)),
                  pl.BlockSpec((1, tt, width), lambda b, i: (b, i, 0)),
                  pl.BlockSpec((1, width), lambda b, i: (0, 0))],
        out_specs=pl.BlockSpec((1, tt, width), lambda b, i: (b, i, 0)),
        out_shape=jax.ShapeDtypeStruct((bsz, length, width), BF16),
        compiler_params=_params("parallel", "parallel"),
        name="hyena_dft_inv",
    )(inv, inv, yc, ys, vv, x0, skip)


@functools.lru_cache(maxsize=None)
def _hyena_tables(length):
    n2 = 2 * length
    t = np.linspace(0.0, 1.0, length)[:, None]
    bands = (HY_EMB - 1) // 2
    wpos = 2.0 * np.pi * np.arange(length)[:, None] / length
    fb = np.linspace(1e-4, bands - 1, bands)[None]
    z = np.concatenate([t, np.cos(fb * wpos), -np.sin(fb * wpos)], axis=-1)
    zpad = np.zeros((length, LANES))
    zpad[:, :HY_EMB] = z
    f = np.arange(length)[:, None]
    n = np.arange(length)[None, :]
    ang = 2.0 * np.pi * ((f * n) % n2) / n2
    cos_m = np.cos(ang)
    sin_m = np.sin(ang)
    sin_m[0, :] = np.cos(np.pi * np.arange(length))
    fwd = np.concatenate([cos_m, sin_m], axis=0)
    scale = np.full((1, n2), 2.0 / n2)
    scale[0, 0] = 1.0 / n2
    scale[0, length] = 1.0 / n2
    inv = fwd.T * scale
    return (zpad.astype(np.float32), t.astype(np.float32), fwd.astype(np.float32),
            inv.astype(np.float32))


def _hyena(p_hy, conv_w, conv_b, filt, skip):
    bsz, length, three_w = p_hy.shape
    width = three_w // 3
    w_in, b_in, w_mid, b_mid, w_out, freq = filt
    zpad, t, fwd, inv = _hyena_tables(length)
    deltas = np.abs(np.linspace(HY_MIN_DECAY, HY_MAX_DECAY, width))[None, :]
    dec = np.exp(-t.astype(np.float64) * deltas).astype(np.float32)
    dec2 = jnp.asarray(np.concatenate([dec, dec], axis=1))
    ffn = w_in.shape[1]
    w_in_pad = jnp.zeros((LANES, ffn), F32).at[:HY_EMB].set(w_in)
    hfilt = _hy_filter(jnp.asarray(zpad), w_in_pad, b_in.reshape(1, ffn), w_mid,
                       b_mid.reshape(-1, 1, ffn), w_out, freq.reshape(1, ffn), dec2)
    fwd_b = jnp.asarray(fwd).astype(BF16)
    inv_b = jnp.asarray(inv).astype(BF16)
    tf = min(512, length)
    kc, ks = _filt_spec(fwd_b, hfilt, tf)
    x0, vv = _hy_prep(p_hy, conv_w, conv_b.reshape(1, three_w))
    yc, ys = _dft_fwd(fwd_b, vv, kc, ks, tf)
    return _dft_inv(inv_b, yc, ys, vv, x0, skip.reshape(1, width), tf)


def _gelu_tanh(x):
    return 0.5 * x * (1.0 + jnp.tanh(math.sqrt(2.0 / math.pi) * (x + 0.044715 * x * x * x)))


def _lru_kernel(xl_ref, yl_ref, xc_ref, cw_ref, cb_ref, wa_ref, ba_ref, wx_ref, bx_ref, ap_ref,
                o_ref, xs_s, a_s, b_s, h_s, *, ctx_len, lat_len):
    tot = ctx_len + lat_len
    ngrp = tot // 8
    ngrp_ctx = ctx_len // 8
    xs_s[0:ctx_len, :] = _depthwise_conv(xc_ref[0], cw_ref[...], LRU_CONV // 2) + cb_ref[...]
    xs_s[ctx_len:, :] = _depthwise_conv(xl_ref[0], cw_ref[...], LRU_CONV // 2) + cb_ref[...]
    xs = xs_s[...]
    xsb = xs.astype(BF16)
    rows8 = lax.broadcasted_iota(jnp.int32, (ngrp, 8, xs.shape[1]), 1)
    for d in range(2):
        r = _sigmoid(_dot(xsb, wa_ref[d, 0].astype(BF16)) + ba_ref[d])
        gi = _sigmoid(_dot(xsb, wx_ref[d, 0].astype(BF16)) + bx_ref[d])
        log_a = -LRU_C * r * _softplus(ap_ref[d])
        a = jnp.exp(log_a)
        b = jnp.sqrt(1.0 - a * a) * (gi * xs)
        a = a.reshape(ngrp, 8, a.shape[1])
        b = b.reshape(ngrp, 8, b.shape[1])
        for s in (1, 2, 4):
            keep = (rows8 >= s) if d == 0 else (rows8 < 8 - s)
            shift = s if d == 0 else 8 - s
            sa = jnp.where(keep, pltpu.roll(a, shift, axis=1), 1.0)
            sb = jnp.where(keep, pltpu.roll(b, shift, axis=1), 0.0)
            b = a * sb + b
            a = a * sa
        a_s[d] = a.reshape(tot, a.shape[2])
        b_s[d] = b.reshape(tot, b.shape[2])

    def group_fwd(i, c):
        r0 = pl.multiple_of(i * 8, 8)
        hg = a_s[0, pl.ds(r0, 8), :] * c + b_s[0, pl.ds(r0, 8), :]
        h_s[pl.ds(r0, 8), :] = hg
        return jnp.broadcast_to(hg[7:8, :], hg.shape)

    def group_bwd(i, c):
        gidx = jnp.where(i < ngrp_ctx, ngrp_ctx - 1 - i, ngrp + ngrp_ctx - 1 - i)
        r0 = pl.multiple_of(gidx * 8, 8)
        hg = a_s[1, pl.ds(r0, 8), :] * c + b_s[1, pl.ds(r0, 8), :]
        h_s[pl.ds(r0, 8), :] += hg
        return jnp.broadcast_to(hg[0:1, :], hg.shape)

    zero = jnp.zeros((8, xs.shape[1]), F32)
    lax.fori_loop(0, ngrp, group_fwd, zero, unroll=4)
    lax.fori_loop(0, ngrp, group_bwd, zero, unroll=4)
    o_ref[0] = (h_s[ctx_len:, :] * _gelu_tanh(yl_ref[0])).astype(o_ref.dtype)


def _rglru(xb_l, yb_l, xb_c, conv_w, conv_b, wa, ba, wx, bx, a_param):
    bsz, lat_len, width = xb_l.shape
    ctx_len = xb_c.shape[1]
    tot = ctx_len + lat_len
    blk = width // LRU_HEADS

    def col(b, h):
        return (b, 0, h)

    def wcol(b, h):
        return (0, h)

    def w3(b, h):
        return (0, 0, h)

    return pl.pallas_call(
        functools.partial(_lru_kernel, ctx_len=ctx_len, lat_len=lat_len),
        grid=(bsz, LRU_HEADS),
        in_specs=[pl.BlockSpec((1, lat_len, blk), col),
                  pl.BlockSpec((1, lat_len, blk), col),
                  pl.BlockSpec((1, ctx_len, blk), col),
                  pl.BlockSpec((LRU_CONV, blk), wcol),
                  pl.BlockSpec((1, blk), wcol),
                  pl.BlockSpec((2, 1, blk, blk), lambda b, h: (0, h, 0, 0)),
                  pl.BlockSpec((2, 1, blk), w3),
                  pl.BlockSpec((2, 1, blk, blk), lambda b, h: (0, h, 0, 0)),
                  pl.BlockSpec((2, 1, blk), w3),
                  pl.BlockSpec((2, 1, blk), w3)],
        out_specs=pl.BlockSpec((1, lat_len, blk), col),
        out_shape=jax.ShapeDtypeStruct((bsz, lat_len, width), BF16),
        scratch_shapes=[pltpu.VMEM((tot, blk), F32),
                        pltpu.VMEM((2, tot, blk), F32),
                        pltpu.VMEM((2, tot, blk), F32),
                        pltpu.VMEM((tot, blk), F32)],
        compiler_params=_params("parallel", "parallel"),
        name="rglru",
    )(xb_l, yb_l, xb_c, conv_w, conv_b.reshape(1, width), wa, ba.reshape(2, 1, width), wx,
      bx.reshape(2, 1, width), a_param.reshape(2, 1, width))


def _post_mixer_kernel(*refs, n_in, has_bias):
    a_refs = refs[:n_in]
    w_refs = refs[n_in:2 * n_in]
    i = 2 * n_in
    b_ref = None
    if has_bias:
        b_ref = refs[i]
        i += 1
    x_ref, g1_ref, sc_ref, sh_ref, lng_ref, lnb_ref, rw_ref, rb_ref, base_ref = refs[i:i + 9]
    x1_ref, v_ref, route_ref, cnt_ref = refs[i + 9:]

    @pl.when((pl.program_id(0) == 0) & (pl.program_id(1) == 0))
    def _():
        cnt_ref[...] = base_ref[...]

    y = None
    for a_ref, w_ref in zip(a_refs, w_refs):
        t = _dot(a_ref[0], w_ref[...])
        y = t if y is None else y + t
    if has_bias:
        y = y + b_ref[...]
    x1 = _layer_norm(DEEPNORM_ALPHA * x_ref[0] + g1_ref[0] * y, lng_ref[...], lnb_ref[...])
    x1_ref[0] = x1
    v = x1 * (1.0 + sc_ref[0]) + sh_ref[0]
    v_ref[0] = _pack_bf16_pairs(v)
    logits = _dot3(v, rw_ref[...]) + rb_ref[...]
    tm = logits.shape[0]
    lane = lax.broadcasted_iota(jnp.int32, logits.shape, 1).astype(F32)
    work = logits
    picks, firsts = [], []
    m0 = None
    for kk in range(TOP_K):
        m = jnp.max(work, axis=-1, keepdims=True)
        if kk == 0:
            m0 = m
        first = jnp.min(jnp.where(work == m, lane, float(LANES)), axis=-1, keepdims=True)
        pick = lane == first
        picks.append(pick)
        firsts.append(first)
        work = jnp.where(pick, -jnp.inf, work)
    sel = jnp.where(picks[0] | picks[1] | picks[2] | picks[3], 1.0, 0.0)
    e = sel * jnp.exp(logits - m0)
    gate = e / jnp.sum(e, axis=-1, keepdims=True)
    ti = lax.broadcasted_iota(jnp.int32, (tm, tm), 0)
    tj = lax.broadcasted_iota(jnp.int32, (tm, tm), 1)
    before = jnp.where(ti > tj, 1.0, 0.0).astype(BF16)
    slot = _dot(before, sel.astype(BF16)) + cnt_ref[...]
    route = jnp.zeros(logits.shape, F32)
    for kk in range(TOP_K):
        rank = jnp.sum(jnp.where(picks[kk], slot, 0.0), axis=-1, keepdims=True)
        wgt = jnp.sum(jnp.where(picks[kk], gate, 0.0), axis=-1, keepdims=True)
        route = jnp.where(lane == float(kk), firsts[kk], route)
        route = jnp.where(lane == float(TOP_K + kk), rank, route)
        route = jnp.where(lane == float(2 * TOP_K + kk), wgt, route)
    route_ref[0] = route
    cnt_ref[...] += jnp.sum(sel, axis=0, keepdims=True)


def _post_mixer(acts, ws, bias, x, g1, sc2, sh2, ln_g, ln_b, router_w, router_b, base, tm):
    bsz, length, d = x.shape
    tm = min(tm, length)
    n_in = len(acts)

    def row(bi, i):
        return (bi, i, 0)

    def per_b(bi, i):
        return (bi, 0, 0)

    def const(bi, i):
        return (0, 0)

    in_specs = [pl.BlockSpec((1, tm, a.shape[2]), row) for a in acts]
    in_specs += [pl.BlockSpec(w.shape, const) for w in ws]
    args = list(acts) + list(ws)
    if bias is not None:
        in_specs.append(pl.BlockSpec((1, d), const))
        args.append(bias.reshape(1, d))
    in_specs += [pl.BlockSpec((1, tm, d), row),
                 pl.BlockSpec((1, 1, d), per_b), pl.BlockSpec((1, 1, d), per_b),
                 pl.BlockSpec((1, 1, d), per_b),
                 pl.BlockSpec((1, d), const), pl.BlockSpec((1, d), const),
                 pl.BlockSpec((d, LANES), const), pl.BlockSpec((1, LANES), const),
                 pl.BlockSpec((1, LANES), const)]
    rw = jnp.zeros((d, LANES), F32).at[:, :N_EXPERTS].set(router_w)
    rb = jnp.full((1, LANES), -1e30, F32).at[0, :N_EXPERTS].set(router_b)
    args += [x, g1, sc2, sh2, ln_g.reshape(1, d), ln_b.reshape(1, d), rw, rb, base]
    return pl.pallas_call(
        functools.partial(_post_mixer_kernel, n_in=n_in, has_bias=bias is not None),
        grid=(bsz, length // tm),
        in_specs=in_specs,
        out_specs=[pl.BlockSpec((1, tm, d), row), pl.BlockSpec((1, tm, d // 2), row),
                   pl.BlockSpec((1, tm, LANES), row), pl.BlockSpec((1, LANES), const)],
        out_shape=[jax.ShapeDtypeStruct((bsz, length, d), F32),
                   jax.ShapeDtypeStruct((bsz, length, d // 2), jnp.uint32),
                   jax.ShapeDtypeStruct((bsz, length, LANES), F32),
                   jax.ShapeDtypeStruct((1, LANES), F32)],
        compiler_params=_params("arbitrary", "arbitrary"),
        name="post_mixer",
    )(*args)


MOE_W1_CHUNK = 512


def _moe_rows_kernel(te_ref, nv_ref, first_ref, nxt_ref, xs_ref, w1_hbm, b1g_ref, b1l_ref, w2_hbm,
                     b2_ref, ys_ref, w1f_s, w2f_s, wt_s, w1g_s, w1l_s, w2b_s, sem, *, layer):
    i = pl.program_id(0)
    nv = nv_ref[i]

    def fetch(e):
        return (pltpu.make_async_copy(w1_hbm.at[layer, e], w1f_s, sem.at[0]),
                pltpu.make_async_copy(w2_hbm.at[layer, e], w2f_s, sem.at[1]))

    @pl.when(i == 0)
    def _():
        for cp in fetch(te_ref[0]):
            cp.start()

    @pl.when(first_ref[i] == 1)
    def _():
        for cp in fetch(te_ref[i]):
            cp.wait()
        half = MOE_W1_CHUNK // 2
        for c in range(w1f_s.shape[1] // MOE_W1_CHUNK):
            t = w1f_s[:, c * MOE_W1_CHUNK:(c + 1) * MOE_W1_CHUNK].T
            for j in range(wt_s.shape[0]):
                cols = slice(j * LANES, (j + 1) * LANES)
                wt_s[j] = t[:, cols]
                w1g_s[c * half:(c + 1) * half, cols] = wt_s[j, pl.ds(0, half, stride=2), :].astype(BF16)
                w1l_s[c * half:(c + 1) * half, cols] = wt_s[j, pl.ds(1, half, stride=2), :].astype(BF16)
        w2b_s[...] = w2f_s[...].astype(BF16)

        @pl.when(nxt_ref[i] >= 0)
        def _():
            for cp in fetch(nxt_ref[i]):
                cp.start()

    @pl.when(nv > 0)
    def _():
        rows = lax.broadcasted_iota(jnp.int32, xs_ref.shape, 0)
        x = _unpack_bf16_pairs(jnp.where(rows < nv, xs_ref[...], jnp.uint32(0))).astype(BF16)
        glu = jnp.minimum(_dot_nt(x, w1g_s[...]) + b1g_ref[0, 0], SWIGLU_LIMIT)
        lin = jnp.clip(_dot_nt(x, w1l_s[...]) + b1l_ref[0, 0], -SWIGLU_LIMIT, SWIGLU_LIMIT)
        act = glu * _sigmoid(SWIGLU_ALPHA * glu) * (lin + 1.0)
        ys_ref[...] = _pack_bf16_pairs(_dot(act.astype(BF16), w2b_s[...]) + b2_ref[0, 0])

    @pl.when(nv == 0)
    def _():
        ys_ref[...] = jnp.zeros_like(ys_ref)


def _moe_rows(xs, tile_expert, tile_rows, tile_first, tile_next, layer, w1, b1g, b1l, w2, b2, tm):
    n_rows, dh = xs.shape
    _, _, d, dff2 = w1.shape
    dff = dff2 // 2

    def row(i, te, nv, first, nxt):
        return (i, 0)

    def exp4(i, te, nv, first, nxt):
        return (layer, te[i], 0, 0)

    hbm = pl.BlockSpec(memory_space=pl.ANY)
    return pl.pallas_call(
        functools.partial(_moe_rows_kernel, layer=layer),
        grid_spec=pltpu.PrefetchScalarGridSpec(
            num_scalar_prefetch=4,
            grid=(n_rows // tm,),
            in_specs=[pl.BlockSpec((tm, dh), row), hbm,
                      pl.BlockSpec((1, 1, 1, dff), exp4), pl.BlockSpec((1, 1, 1, dff), exp4),
                      hbm, pl.BlockSpec((1, 1, 1, d), exp4)],
            out_specs=pl.BlockSpec((tm, dh), row),
            scratch_shapes=[pltpu.VMEM((d, dff2), F32), pltpu.VMEM((dff, d), F32),
                            pltpu.VMEM((d // LANES, MOE_W1_CHUNK, LANES), F32),
                            pltpu.VMEM((dff, d), BF16),
                            pltpu.VMEM((dff, d), BF16), pltpu.VMEM((dff, d), BF16),
                            pltpu.SemaphoreType.DMA((2,))]),
        out_shape=jax.ShapeDtypeStruct((n_rows, dh), jnp.uint32),
        compiler_params=_params("arbitrary"),
        name="moe_rows",
    )(tile_expert, tile_rows, tile_first, tile_next, xs, w1, b1g, b1l, w2, b2)


def _moe_combine_kernel(y0_ref, y1_ref, y2_ref, y3_ref, route_ref, x1_ref, g2_ref, lng_ref, lnb_ref,
                        o_ref):
    route = route_ref[...]
    lane = lax.broadcasted_iota(jnp.int32, route.shape, 1)
    f = None
    for kk, y_ref in enumerate((y0_ref, y1_ref, y2_ref, y3_ref)):
        wgt = jnp.sum(jnp.where(lane == 2 * TOP_K + kk, route, 0.0), axis=-1, keepdims=True)
        term = wgt * _unpack_bf16_pairs(y_ref[0])
        f = term if f is None else f + term
    o_ref[0] = _layer_norm(DEEPNORM_ALPHA * x1_ref[0] + g2_ref[0] * f, lng_ref[...], lnb_ref[...])


def _moe_combine(yg, route, row_offset, x1, g2, ln_g, ln_b, tm):
    bsz, length, d = x1.shape
    tm = min(tm, length)
    nt = length // tm
    off = row_offset // tm

    def pick(kk):
        return lambda bi, i: (kk, off + bi * nt + i, 0)

    def const(bi, i):
        return (0, 0)

    return pl.pallas_call(
        _moe_combine_kernel,
        grid=(bsz, nt),
        in_specs=[pl.BlockSpec((1, tm, d // 2), pick(kk)) for kk in range(TOP_K)] + [
            pl.BlockSpec((tm, LANES), lambda bi, i: (off + bi * nt + i, 0)),
            pl.BlockSpec((1, tm, d), lambda bi, i: (bi, i, 0)),
            pl.BlockSpec((1, 1, d), lambda bi, i: (bi, 0, 0)),
            pl.BlockSpec((1, d), const), pl.BlockSpec((1, d), const)],
        out_specs=pl.BlockSpec((1, tm, d), lambda bi, i: (bi, i, 0)),
        out_shape=jax.ShapeDtypeStruct((bsz, length, d), F32),
        compiler_params=_params("parallel", "parallel"),
        name="moe_combine",
    )(yg, yg, yg, yg, route, x1, g2, ln_g.reshape(1, d), ln_b.reshape(1, d))


SC_CORES = 2
SC_SUBCORES = 16
SC_WORKERS = SC_CORES * SC_SUBCORES
SC_WINDOW = 64


def _sc_row_pipeline(nwin, read, write):
    read(0, 0).start()

    @pl.loop(0, nwin, step=2)
    def _(w0):
        for b in range(2):
            w = w0 + b

            @pl.when(w + 1 < nwin)
            def _():
                @pl.when(w >= 1)
                def _():
                    write(w - 1, 1 - b).wait()

                read(w + 1, 1 - b).start()

            read(w, b).wait()
            write(w, b).start()

    write(nwin - 2, 0).wait()
    write(nwin - 1, 1).wait()


def _sc_scatter_rows(src, pos, n_out):
    t_rows, d = src.shape
    nw, nwin, win = pos.shape
    assert nw == SC_WORKERS and nwin % 2 == 0 and t_rows % (nwin * win) == 0
    mesh = plsc.VectorSubcoreMesh(core_axis_name="c", subcore_axis_name="s")

    @functools.partial(
        pl.kernel, mesh=mesh, out_type=jax.ShapeDtypeStruct((n_out, d), src.dtype),
        scratch_types=[pltpu.VMEM((nwin, win), jnp.int32), pltpu.VMEM((2, win, d), src.dtype),
                       pltpu.SemaphoreType.DMA((2,)), pltpu.SemaphoreType.DMA((2,))])
    def scatter(src_hbm, pos_hbm, out_hbm, idx_v, rows_v, rsem, wsem):
        wid = lax.axis_index("s") * SC_CORES + lax.axis_index("c")
        t0 = lax.rem(wid * (nwin * win), t_rows)
        pltpu.sync_copy(pos_hbm.at[wid], idx_v)

        def read(w, slot):
            return pltpu.make_async_copy(src_hbm.at[pl.ds(t0 + w * win, win)], rows_v.at[slot],
                                         rsem.at[slot])

        def write(w, slot):
            return pltpu.make_async_copy(rows_v.at[slot], out_hbm.at[idx_v.at[w]], wsem.at[slot])

        _sc_row_pipeline(nwin, read, write)

    return scatter(src, pos)


def _sc_gather_rows(table, pos):
    _, d = table.shape
    nw, nwin, win = pos.shape
    assert nw == SC_WORKERS and nwin % 2 == 0
    per = nwin * win
    mesh = plsc.VectorSubcoreMesh(core_axis_name="c", subcore_axis_name="s")

    @functools.partial(
        pl.kernel, mesh=mesh, out_type=jax.ShapeDtypeStruct((nw * per, d), table.dtype),
        scratch_types=[pltpu.VMEM((nwin, win), jnp.int32), pltpu.VMEM((2, win, d), table.dtype),
                       pltpu.SemaphoreType.DMA((2,)), pltpu.SemaphoreType.DMA((2,))])
    def gather(table_hbm, pos_hbm, out_hbm, idx_v, rows_v, rsem, wsem):
        wid = lax.axis_index("s") * SC_CORES + lax.axis_index("c")
        base = wid * per
        pltpu.sync_copy(pos_hbm.at[wid], idx_v)

        def read(w, slot):
            return pltpu.make_async_copy(table_hbm.at[idx_v.at[w]], rows_v.at[slot], rsem.at[slot])

        def write(w, slot):
            return pltpu.make_async_copy(rows_v.at[slot], out_hbm.at[pl.ds(base + w * win, win)],
                                         wsem.at[slot])

        _sc_row_pipeline(nwin, read, write)

    return gather(table, pos)


MOE_TILE = 256


def _moe_sparse(v_all, route, counts, layer, w1, b1g, b1l, w2, b2):
    t_rows, dh = v_all.shape
    n_exp = w1.shape[1]
    pairs = TOP_K * t_rows
    n_tiles = pairs // MOE_TILE + n_exp
    route_t = route.T
    expert = route_t[0:TOP_K].astype(jnp.int32)
    slot = route_t[TOP_K:2 * TOP_K].astype(jnp.int32)
    cnt = counts[0, :n_exp].astype(jnp.int32)
    tiles_per = (cnt + MOE_TILE - 1) // MOE_TILE
    tile_end = jnp.cumsum(tiles_per)
    tile_start = tile_end - tiles_per
    pos = (tile_start * MOE_TILE)[expert] + slot
    nwin = pairs // (SC_WORKERS * SC_WINDOW)
    pos_km = pos.reshape(SC_WORKERS, nwin, SC_WINDOW)
    tile_ids = jnp.arange(n_tiles, dtype=jnp.int32)[:, None]
    owns = (tile_ids >= tile_start[None, :]) & (tile_ids < tile_end[None, :])
    experts = jnp.arange(n_exp, dtype=jnp.int32)[None, :]
    last_used = jnp.max(jnp.where(tiles_per > 0, experts[0], 0))
    used = jnp.any(owns, axis=1)
    te = jnp.where(used, jnp.sum(jnp.where(owns, experts, 0), axis=1), last_used).astype(jnp.int32)
    rows_left = cnt[None, :] - (tile_ids - tile_start[None, :]) * MOE_TILE
    tile_rows = jnp.sum(jnp.where(owns, jnp.clip(rows_left, 0, MOE_TILE), 0), axis=1)
    tile_rows = tile_rows.astype(jnp.int32)
    tile_first = jnp.any(owns & (tile_ids == tile_start[None, :]), axis=1).astype(jnp.int32)
    later = (experts > experts.T) & (tiles_per[None, :] > 0)
    next_e = jnp.min(jnp.where(later, experts, n_exp), axis=1)
    next_e = jnp.where(next_e == n_exp, -1, next_e)
    tile_next = jnp.where(used, jnp.sum(jnp.where(owns, next_e[None, :], 0), axis=1), -1)
    tile_next = tile_next.astype(jnp.int32)
    xs = _sc_scatter_rows(v_all, pos_km, n_tiles * MOE_TILE)
    ys = _moe_rows(xs, te, tile_rows, tile_first, tile_next, layer, w1, b1g, b1l, w2, b2, MOE_TILE)
    return _sc_gather_rows(ys, pos_km).reshape(TOP_K, t_rows, dh)


@functools.lru_cache(maxsize=None)
def _sincos_2d(rows, cols, dim):
    quarter = dim // 4
    omega = 1.0 / (10000.0 ** (np.arange(quarter, dtype=np.float64) / quarter))

    def emb1d(n):
        ang = np.arange(n, dtype=np.float64)[:, None] * omega
        return np.concatenate([np.sin(ang), np.cos(ang)], axis=-1)

    er = np.broadcast_to(emb1d(rows)[:, None], (rows, cols, dim // 2))
    ec = np.broadcast_to(emb1d(cols)[None], (rows, cols, dim // 2))
    return np.concatenate([er, ec], axis=-1).reshape(rows * cols, dim).astype(np.float32)


def _gate_layout(gates, n_ch):
    bsz, length, _ = gates.shape
    g = gates[..., :n_ch].reshape(bsz, length // DN_CHUNK, DN_CHUNK, n_ch)
    return g.transpose(0, 3, 1, 2)


def kernel(x, c, ctx, c_ctx, ada_w, ada_b, ln_g, ln_b, ev_w_in, ev_w_out, dn_conv_w, dn_a_log, dn_dt_bias, dn_norm_g, hy_conv_w, hy_conv_b, hy_w_in, hy_b_in, hy_w_mid, hy_b_mid, hy_w_out, hy_freq, hy_skip, od_w_in, od_b_in, lru_conv_w, lru_conv_b, lru_wa, lru_ba, lru_wx, lru_bx, lru_a_param, od_w_out, od_b_out, router_w, router_b, moe_w1, moe_b1, moe_w2, moe_b2):
    bsz, length, d = x.shape
    ctx_len = ctx.shape[1]
    pos = jnp.asarray(_sincos_2d(length // GRID_W, GRID_W, d))

    cond = jnp.zeros((16, d), F32).at[:bsz].set(c).at[bsz].set(c_ctx)
    mod = _modulation(cond, ada_w, ada_b).reshape(DEPTH, 16, 6, d)

    def lat_mod(layer, k):
        return mod[layer, :bsz, k][:, None, :]

    def ctx_mod(layer, k):
        return jnp.broadcast_to(mod[layer, bsz, k][None, None, :], (bsz, 1, d))

    hc = ctx
    for layer in range(DEPTH):
        last = layer == DEPTH - 1
        j = layer // 2
        if layer % 2 == 0:
            dn_qk = DN_HEADS * DN_DK
            dn_qkv = 3 * dn_qk
            dn_in = dn_qkv + dn_qk + 4 * DN_HEADS
            w_in = ev_w_in[j]
            gate_w = jnp.zeros((d, LANES), F32).at[:, :4 * DN_HEADS].set(w_in[:, dn_qkv + dn_qk:dn_in])
            w_cat = jnp.concatenate([w_in[:, :dn_qkv + dn_qk], gate_w, w_in[:, dn_in:]],
                                    axis=1).astype(BF16)
            hy_in = w_in.shape[1] - dn_in
            widths = (dn_qkv, dn_qk, LANES, hy_in)
            qkv_l, z_l, gt_l, phy_l = _inproj(x, lat_mod(layer, 1), lat_mod(layer, 0), pos,
                                               w_cat, None, widths, 512)
            qkv_c, z_c, gt_c, phy_c = _inproj(hc, ctx_mod(layer, 1), ctx_mod(layer, 0), None,
                                               w_cat, None, widths, 512)
            gates = jnp.concatenate([_gate_layout(gt_c, 4 * DN_HEADS),
                                     _gate_layout(gt_l, 4 * DN_HEADS)], axis=2)
            n_chunks = gates.shape[2]
            gates = jnp.pad(gates, ((0, 0), (0, 0), (0, -n_chunks % 16), (0, 0)))
            dn_l, dn_c = _deltanet(qkv_l, z_l, qkv_c, z_c, gates, dn_conv_w[j], dn_a_log[j],
                                   dn_dt_bias[j], dn_norm_g[j])
            filt = (hy_w_in[j], hy_b_in[j], hy_w_mid[j], hy_b_mid[j], hy_w_out[j], hy_freq[j])
            hy_l = _hyena(phy_l, hy_conv_w[j], hy_conv_b[j], filt, hy_skip[j])
            w_out = ev_w_out[j].astype(BF16)
            half = dn_l.shape[2]
            acts_l, ws, bias = (dn_l, hy_l), (w_out[:half], w_out[half:]), None
            acts_c = None
            if not last:
                hy_c = _hyena(phy_c, hy_conv_w[j], hy_conv_b[j], filt, hy_skip[j])
                acts_c = (dn_c, hy_c)
        else:
            w_in = od_w_in[j].astype(BF16)
            width = w_in.shape[1] // 2
            b_in = od_b_in[j].reshape(1, 2 * width)
            xb_l, yb_l = _inproj(x, lat_mod(layer, 1), lat_mod(layer, 0), pos, w_in, b_in,
                                 (width, width), 512)
            xb_c, _ = _inproj(hc, ctx_mod(layer, 1), ctx_mod(layer, 0), None, w_in, b_in,
                              (width, width), 512)
            act_l = _rglru(xb_l, yb_l, xb_c, lru_conv_w[j], lru_conv_b[j], lru_wa[j], lru_ba[j],
                           lru_wx[j], lru_bx[j], lru_a_param[j])
            acts_l, ws, bias = (act_l,), (od_w_out[j].astype(BF16),), od_b_out[j]
            acts_c = None
            assert last, "context outputs of the RG-LRU layer are only needed before the last layer"


        x1, v, route, counts = _post_mixer(acts_l, ws, bias, x, lat_mod(layer, 2),
                                           lat_mod(layer, 4), lat_mod(layer, 3), ln_g[layer, 0],
                                           ln_b[layer, 0], router_w[layer], router_b[layer],
                                           jnp.zeros((1, LANES), F32), 512)
        v_all = v.reshape(bsz * length, d // 2)
        route = route.reshape(bsz * length, LANES)
        if not last:
            hc1, vc, route_c, counts = _post_mixer(acts_c, ws, bias, hc, ctx_mod(layer, 2),
                                                   ctx_mod(layer, 4), ctx_mod(layer, 3),
                                                   ln_g[layer, 0], ln_b[layer, 0], router_w[layer],
                                                   router_b[layer], counts, 256)
            v_all = jnp.concatenate([v_all, vc.reshape(bsz * ctx_len, d // 2)], axis=0)
            route = jnp.concatenate([route, route_c.reshape(bsz * ctx_len, LANES)], axis=0)
        yg = _moe_sparse(v_all, route, counts, layer, moe_w1, moe_b1[:, :, None, 0::2],
                         moe_b1[:, :, None, 1::2], moe_w2, moe_b2[:, :, None, :])
        x = _moe_combine(yg, route, 0, x1, lat_mod(layer, 5), ln_g[layer, 1], ln_b[layer, 1], 256)
        if not last:
            hc = _moe_combine(yg, route, bsz * length, hc1, ctx_mod(layer, 5), ln_g[layer, 1],
                              ln_b[layer, 1], 256)
    return x
```

```python
import functools
import math

import numpy as np
import jax
import jax.numpy as jnp
from jax import lax
from jax.experimental import pallas as pl
from jax.experimental.pallas import tpu as pltpu
from jax.experimental.pallas import tpu_sc as plsc

F32 = jnp.float32
BF16 = jnp.bfloat16

VMEM_LIMIT_BYTES = 56 * 1024 * 1024
LANES = 128

DEPTH = 2
GRID_W = 64
DEEPNORM_ALPHA = (2.0 * DEPTH) ** 0.25
LN_EPS = 1e-5
RMS_EPS = 1e-6

DN_HEADS = 4
DN_DK = 128
DN_CHUNK = 64
DN_CONV = 4

HY_EMB = 33
HY_TARGET = 1e-2
HY_MIN_DECAY = math.log(HY_TARGET) / 1.5
HY_MAX_DECAY = math.log(HY_TARGET) / 0.3
HY_CONV = 3

LRU_HEADS = 4
LRU_C = 8.0
LRU_CONV = 4

N_EXPERTS = 32
TOP_K = 4
SWIGLU_ALPHA = 1.702
SWIGLU_LIMIT = 7.0


def _params(*sem):
    return pltpu.CompilerParams(dimension_semantics=sem, vmem_limit_bytes=VMEM_LIMIT_BYTES)


def _dot(a, b):
    return jnp.dot(a, b, preferred_element_type=F32)


def _dot_nt(a, b):
    return lax.dot_general(a, b, (((1,), (1,)), ((), ())), preferred_element_type=F32)


def _split(a):
    hi = a.astype(BF16)
    lo = (a - hi.astype(F32)).astype(BF16)
    return hi, lo


def _dot3(a, b):
    ah, al = _split(a)
    bh, bl = _split(b)
    return _dot(ah, bh) + _dot(ah, bl) + _dot(al, bh)


def _silu(x):
    return x * (1.0 / (1.0 + jnp.exp(-x)))


def _sigmoid(x):
    return 1.0 / (1.0 + jnp.exp(-x))


def _softplus(x):
    return jnp.maximum(x, 0.0) + jnp.log(1.0 + jnp.exp(-jnp.abs(x)))


def _layer_norm(x, g, b):
    mu = jnp.mean(x, axis=-1, keepdims=True)
    xc = x - mu
    var = jnp.mean(xc * xc, axis=-1, keepdims=True)
    return xc * lax.rsqrt(var + LN_EPS) * g + b


def _pack_bf16_pairs(x):
    w = x.shape[1] // 2
    lo = pltpu.bitcast(x[:, :w].astype(BF16).astype(F32), jnp.uint32) >> 16
    hi = pltpu.bitcast(x[:, w:].astype(BF16).astype(F32), jnp.uint32) & jnp.uint32(0xFFFF0000)
    return lo | hi


def _unpack_bf16_pairs(p):
    lo = pltpu.bitcast(p << 16, F32)
    hi = pltpu.bitcast(p & jnp.uint32(0xFFFF0000), F32)
    return jnp.concatenate([lo, hi], axis=1)


def _shift_rows(x, s):
    if s == 0:
        return x
    n = x.shape[0]
    rows = lax.broadcasted_iota(jnp.int32, x.shape, 0)
    valid = (rows >= s) if s > 0 else (rows < n + s)
    return jnp.where(valid, pltpu.roll(x, s % n, axis=0), 0.0)


def _depthwise_conv(x, w, pad_left):
    acc = None
    for i in range(w.shape[0]):
        term = _shift_rows(x, pad_left - i) * w[i:i + 1, :]
        acc = term if acc is None else acc + term
    return acc


def _mod_kernel(c_ref, w_ref, b_ref, o_ref):
    o_ref[0] = _dot3(_silu(c_ref[...]), w_ref[0]) + b_ref[0]


def _modulation(cond, ada_w, ada_b):
    depth, d, n = ada_w.shape
    rows = cond.shape[0]
    tn = 1536
    return pl.pallas_call(
        _mod_kernel,
        grid=(depth, n // tn),
        in_specs=[
            pl.BlockSpec((rows, d), lambda l, j: (0, 0)),
            pl.BlockSpec((1, d, tn), lambda l, j: (l, 0, j)),
            pl.BlockSpec((1, 1, tn), lambda l, j: (l, 0, j)),
        ],
        out_specs=pl.BlockSpec((1, rows, tn), lambda l, j: (l, 0, j)),
        out_shape=jax.ShapeDtypeStruct((depth, rows, n), F32),
        compiler_params=_params("parallel", "parallel"),
        name="modulation",
    )(cond, ada_w, ada_b.reshape(depth, 1, n))


def _inproj_kernel(*refs, splits, has_pos, has_bias):
    x_ref, sc_ref, sh_ref = refs[:3]
    i = 3
    pos_ref = None
    if has_pos:
        pos_ref = refs[i]
        i += 1
    w_ref = refs[i]
    i += 1
    b_ref = None
    if has_bias:
        b_ref = refs[i]
        i += 1
    o_refs = refs[i:]
    u = x_ref[0] * (1.0 + sc_ref[0]) + sh_ref[0]
    if has_pos:
        u = u + pos_ref[...]
    ub = u.astype(BF16)
    for o_ref, (s, e) in zip(o_refs, splits):
        acc = _dot(ub, w_ref[:, s:e])
        if has_bias:
            acc = acc + b_ref[:, s:e]
        o_ref[0] = acc


def _inproj(x, sc, sh, pos, w, b, widths, tm):
    bsz, length, d = x.shape
    n = w.shape[1]
    splits, s = [], 0
    for wd in widths:
        splits.append((s, s + wd))
        s += wd
    assert s == n
    tm = min(tm, length)
    in_specs = [
        pl.BlockSpec((1, tm, d), lambda bi, i: (bi, i, 0)),
        pl.BlockSpec((1, 1, d), lambda bi, i: (bi, 0, 0)),
        pl.BlockSpec((1, 1, d), lambda bi, i: (bi, 0, 0)),
    ]
    args = [x, sc, sh]
    if pos is not None:
        in_specs.append(pl.BlockSpec((tm, d), lambda bi, i: (i, 0)))
        args.append(pos)
    in_specs.append(pl.BlockSpec((d, n), lambda bi, i: (0, 0)))
    args.append(w)
    if b is not None:
        in_specs.append(pl.BlockSpec((1, n), lambda bi, i: (0, 0)))
        args.append(b)
    return pl.pallas_call(
        functools.partial(_inproj_kernel, splits=tuple(splits), has_pos=pos is not None,
                          has_bias=b is not None),
        grid=(bsz, length // tm),
        in_specs=in_specs,
        out_specs=[pl.BlockSpec((1, tm, wd), lambda bi, i: (bi, i, 0)) for wd in widths],
        out_shape=[jax.ShapeDtypeStruct((bsz, length, wd), F32) for wd in widths],
        compiler_params=_params("parallel", "parallel"),
        name="inproj",
    )(*args)


def _unit_tri_inverses(mats, lower):
    n = mats[0].shape[0]
    nb = 16
    np_ = len(mats)
    ii = lax.broadcasted_iota(jnp.int32, (n, n), 0)
    jj = lax.broadcasted_iota(jnp.int32, (n, n), 1)
    same16 = (ii // nb) == (jj // nb)
    same32 = (ii // (2 * nb)) == (jj // (2 * nb))
    dgs = []
    for a in mats:
        ad = jnp.where(same16, a, 0.0)
        dgs.append(ad[0:nb] + ad[nb:2 * nb] + ad[2 * nb:3 * nb] + ad[3 * nb:4 * nb])
    dg = jnp.concatenate(dgs, axis=0)
    rr = lax.broadcasted_iota(jnp.int32, dg.shape, 0)
    ll = lax.broadcasted_iota(jnp.int32, dg.shape, 1)
    xd = jnp.where(rr % nb == ll % nb, 1.0, 0.0)
    blk0 = (ll // nb) * nb
    for s in (range(nb - 1) if lower else range(nb - 1, 0, -1)):
        col = jnp.take_along_axis(dg, blk0 + s, axis=1)
        row = jnp.concatenate(
            [jnp.broadcast_to(xd[p * nb + s:p * nb + s + 1, :], (nb, n)) for p in range(np_)], axis=0)
        xd = xd - col * row
    ds = [jnp.where(same16, jnp.concatenate([xd[p * nb:(p + 1) * nb]] * (n // nb), axis=0), 0.0)
          for p in range(np_)]
    lvl1 = same32 & jnp.logical_not(same16)
    t1 = [_dot3(d, jnp.where(lvl1, a, 0.0)) for d, a in zip(ds, mats)]
    x1 = [d - _dot3(t, d) for d, t in zip(ds, t1)]
    x1b = [x.astype(BF16) for x in x1]
    t2 = [_dot(xb, jnp.where(same32, 0.0, a).astype(BF16)) for xb, a in zip(x1b, mats)]
    return [x - _dot(t.astype(BF16), xb) for x, t, xb in zip(x1, t2, x1b)]


def _dn_kernel(alog_ref, dtb_ref,
               ql_ref, kl_ref, vl_ref, zl_ref, qc_ref, kc_ref, vc_ref, zc_ref, gt_ref,
               cwq_ref, cwk_ref, cwv_ref, ng_ref,
               yl_ref, yc_ref,
               qn_s, kn_s, vn_s, gc_s, bt_s, nq_s, c_s, gl_s, o_s,
               *, ctx_len, lat_len):
    h = pl.program_id(1)
    csz = DN_CHUNK
    nc_ctx = ctx_len // csz
    nc = (ctx_len + lat_len) // csz

    def prep(src_ref, cw_ref, kind):
        t = _silu(_depthwise_conv(src_ref[0], cw_ref[...], DN_CONV // 2))
        if kind == "v":
            return t
        t = t * lax.rsqrt(jnp.sum(t * t, axis=-1, keepdims=True) + RMS_EPS)
        return t * (DN_DK ** -0.5) if kind == "q" else t

    qn_s[0:ctx_len, :] = prep(qc_ref, cwq_ref, "q")
    qn_s[ctx_len:, :] = prep(ql_ref, cwq_ref, "q")
    kn_s[0:ctx_len, :] = prep(kc_ref, cwk_ref, "k")
    kn_s[ctx_len:, :] = prep(kl_ref, cwk_ref, "k")
    vn_s[0:ctx_len, :] = prep(vc_ref, cwv_ref, "v")
    vn_s[ctx_len:, :] = prep(vl_ref, cwv_ref, "v")

    ii = lax.broadcasted_iota(jnp.int32, (csz, csz), 0)
    jj = lax.broadcasted_iota(jnp.int32, (csz, csz), 1)
    eye = ii == jj
    for d in range(2):
        graw = gt_ref[0, d * 2 * DN_HEADS + h]
        braw = gt_ref[0, d * 2 * DN_HEADS + DN_HEADS + h]
        a_neg = -jnp.exp(jnp.zeros_like(graw) + alog_ref[d, h])
        g = a_neg * _softplus(graw + dtb_ref[d, h])
        tri = jnp.where((ii <= jj) if d == 0 else (ii >= jj), 1.0, 0.0).astype(BF16)
        g1 = g.astype(BF16)
        r1 = g - g1.astype(F32)
        g2 = r1.astype(BF16)
        g3 = (r1 - g2.astype(F32)).astype(BF16)
        gc_s[d] = _dot(g1, tri) + _dot(g2, tri) + _dot(g3, tri)
        bt_s[d] = _sigmoid(braw)

    o_s[...] = jnp.zeros_like(o_s)

    group = 12
    assert nc % group == 0

    def chunk_prep(gi, carry):
        ns = [gi * group + c for c in range(group)]
        r0s = [pl.multiple_of(n * csz, csz) for n in ns]
        qs = [qn_s[pl.ds(r0, csz), :] for r0 in r0s]
        ks = [kn_s[pl.ds(r0, csz), :] for r0 in r0s]
        vs = [vn_s[pl.ds(r0, csz), :] for r0 in r0s]
        kbfs = [k.astype(BF16) for k in ks]
        qks = [_dot_nt(q.astype(BF16), kbf) for q, kbf in zip(qs, kbfs)]
        for d in range(2):
            incl = (ii >= jj) if d == 0 else (ii <= jj)
            strict = (ii > jj) if d == 0 else (ii < jj)
            grs = [gc_s[d, pl.ds(n, 1), :] for n in ns]
            grows = [jnp.broadcast_to(gr, (csz, csz)) for gr in grs]
            gcols = [jnp.sum(jnp.where(eye, grow, 0.0), axis=1, keepdims=True) for grow in grows]
            bcols = [jnp.sum(jnp.where(eye, jnp.broadcast_to(bt_s[d, pl.ds(n, 1), :], (csz, csz)),
                                       0.0), axis=1, keepdims=True) for n in ns]
            decays = [jnp.where(incl, jnp.exp(jnp.where(incl, gcol - grow, 0.0)), 0.0)
                      for gcol, grow in zip(gcols, grows)]
            kbs = [k * bcol for k, bcol in zip(ks, bcols)]
            amats = [jnp.where(strict, _dot_nt(kb.astype(BF16), kbf) * decay, 0.0)
                     for kb, kbf, decay in zip(kbs, kbfs, decays)]
            tbs = [t.astype(BF16) for t in _unit_tri_inverses(amats, lower=(d == 0))]
            cs = range(group)
            egs = [jnp.exp(gcols[c]) for c in cs]
            ubs = [_dot(tbs[c], (vs[c] * bcols[c]).astype(BF16)).astype(BF16) for c in cs]
            wbs = [_dot(tbs[c], (kbs[c] * egs[c]).astype(BF16)).astype(BF16) for c in cs]
            attns = [jnp.where(incl, qks[c] * decays[c], 0.0).astype(BF16) for c in cs]
            glasts = [grs[c][:, csz - 1:csz] if d == 0 else grs[c][:, 0:1] for c in cs]
            kdts = [(ks[c] * jnp.exp(glasts[c] - gcols[c])).T.astype(BF16) for c in cs]
            nmats = [_dot(kdts[c], wbs[c]).astype(BF16) for c in cs]
            qmats = [(qs[c] * egs[c] - _dot(attns[c], wbs[c])).astype(BF16) for c in cs]
            cmats = [_dot(kdts[c], ubs[c]) for c in cs]
            omats = [_dot(attns[c], ubs[c]) for c in cs]
            for c in cs:
                n = ns[c]
                nq_s[d, n, 0:DN_DK, :] = nmats[c]
                nq_s[d, n, DN_DK:DN_DK + csz, :] = qmats[c]
                c_s[d, n] = cmats[c]
                o_s[pl.ds(r0s[c], csz), :] += omats[c]
                gl_s[d, pl.ds(n, 1), :] = jnp.broadcast_to(jnp.exp(glasts[c]), (1, LANES))
        return carry

    lax.fori_loop(0, nc // group, chunk_prep, 0)

    def step(i, states):
        new_states = []
        for d in range(2):
            if d == 0:
                n = i
            else:
                n = jnp.where(i < nc_ctx, nc_ctx - 1 - i, nc + nc_ctx - 1 - i)
            r0 = pl.multiple_of(n * csz, csz)
            s = states[d]
            r = _dot(nq_s[d, n], s.astype(BF16))
            o_s[pl.ds(r0, csz), :] += r[DN_DK:DN_DK + csz]
            new_states.append(s * gl_s[d, pl.ds(n, 1), :] - r[0:DN_DK] + c_s[d, n])
        return tuple(new_states)

    zero = jnp.zeros((DN_DK, DN_DK), F32)
    lax.fori_loop(0, nc, step, (zero, zero))

    def gated_norm(o, z):
        o = o * lax.rsqrt(jnp.mean(o * o, axis=-1, keepdims=True) + RMS_EPS) * ng_ref[...]
        return (o * _silu(z)).astype(yl_ref.dtype)

    yc_ref[0] = gated_norm(o_s[0:ctx_len, :], zc_ref[0])
    yl_ref[0] = gated_norm(o_s[ctx_len:, :], zl_ref[0])


def _deltanet(qkv_l, z_l, qkv_c, z_c, gates, conv_w, a_log, dt_bias, norm_g):
    bsz, lat_len, _ = qkv_l.shape
    ctx_len = qkv_c.shape[1]
    tot = ctx_len + lat_len
    nc = tot // DN_CHUNK
    ncp = gates.shape[2]
    hd = DN_DK
    nh = DN_HEADS

    def col(off):
        return lambda b, h: (b, 0, off + h)

    def wcol(off):
        return lambda b, h: (0, off + h)

    smem = pl.BlockSpec(memory_space=pltpu.SMEM)
    in_specs = [
        smem, smem,
        pl.BlockSpec((1, lat_len, hd), col(0)),
        pl.BlockSpec((1, lat_len, hd), col(nh)),
        pl.BlockSpec((1, lat_len, hd), col(2 * nh)),
        pl.BlockSpec((1, lat_len, hd), col(0)),
        pl.BlockSpec((1, ctx_len, hd), col(0)),
        pl.BlockSpec((1, ctx_len, hd), col(nh)),
        pl.BlockSpec((1, ctx_len, hd), col(2 * nh)),
        pl.BlockSpec((1, ctx_len, hd), col(0)),
        pl.BlockSpec((1, 4 * nh, ncp, DN_CHUNK), lambda b, h: (b, 0, 0, 0)),
        pl.BlockSpec((DN_CONV, hd), wcol(0)),
        pl.BlockSpec((DN_CONV, hd), wcol(nh)),
        pl.BlockSpec((DN_CONV, hd), wcol(2 * nh)),
        pl.BlockSpec((1, hd), lambda b, h: (0, 0)),
    ]
    scratch = [
        pltpu.VMEM((tot, hd), F32), pltpu.VMEM((tot, hd), F32), pltpu.VMEM((tot, hd), F32),
        pltpu.VMEM((2, ncp, DN_CHUNK), F32), pltpu.VMEM((2, ncp, DN_CHUNK), F32),
        pltpu.VMEM((2, nc, hd + DN_CHUNK, hd), BF16),
        pltpu.VMEM((2, nc, hd, hd), F32),
        pltpu.VMEM((2, nc, LANES), F32),
        pltpu.VMEM((tot, hd), F32),
    ]
    return pl.pallas_call(
        functools.partial(_dn_kernel, ctx_len=ctx_len, lat_len=lat_len),
        grid=(bsz, nh),
        in_specs=in_specs,
        out_specs=[pl.BlockSpec((1, lat_len, hd), col(0)),
                   pl.BlockSpec((1, ctx_len, hd), col(0))],
        out_shape=[jax.ShapeDtypeStruct((bsz, lat_len, nh * hd), BF16),
                   jax.ShapeDtypeStruct((bsz, ctx_len, nh * hd), BF16)],
        scratch_shapes=scratch,
        compiler_params=_params("parallel", "parallel"),
        name="deltanet",
    )(a_log, dt_bias, qkv_l, qkv_l, qkv_l, z_l, qkv_c, qkv_c, qkv_c, z_c, gates,
      conv_w, conv_w, conv_w, norm_g.reshape(1, hd))


def _hy_filter_kernel(z_ref, win_ref, bin_ref, wmid_ref, bmid_ref, wout_ref, freq_ref, dec_ref,
                      o_ref):
    freq = freq_ref[...]
    hcur = jnp.sin(freq * (_dot3(z_ref[...], win_ref[...]) + bin_ref[...]))
    for i in range(wmid_ref.shape[0]):
        hcur = jnp.sin(freq * (_dot3(hcur, wmid_ref[i]) + bmid_ref[i]))
    o_ref[...] = _dot3(hcur, wout_ref[...]) * dec_ref[...]


def _hy_filter(z, w_in, b_in, w_mid, b_mid, w_out, freq, dec2):
    length = z.shape[0]
    n_out = w_out.shape[1]
    tl = min(256, length)

    def whole(a):
        return pl.BlockSpec(a.shape, lambda i: (0,) * a.ndim)

    return pl.pallas_call(
        _hy_filter_kernel,
        grid=(length // tl,),
        in_specs=[pl.BlockSpec((tl, z.shape[1]), lambda i: (i, 0)),
                  whole(w_in), whole(b_in), whole(w_mid), whole(b_mid), whole(w_out), whole(freq),
                  pl.BlockSpec((tl, n_out), lambda i: (i, 0))],
        out_specs=pl.BlockSpec((tl, n_out), lambda i: (i, 0)),
        out_shape=jax.ShapeDtypeStruct((length, n_out), F32),
        compiler_params=_params("parallel"),
        name="hyena_filter",
    )(z, w_in, b_in, w_mid, b_mid, w_out, freq, dec2)


def _filt_spec_kernel(fc_ref, fs_ref, h_ref, kc_ref, ks_ref, *, width):
    j = pl.program_id(0)
    hmat = h_ref[...]
    rows = lax.broadcasted_iota(jnp.int32, hmat.shape, 0)
    cols = lax.broadcasted_iota(jnp.int32, hmat.shape, 1)
    hmat = jnp.where((rows == 0) & (cols >= width), 0.0, hmat)
    hh, hl = _split(hmat)
    c = _dot(fc_ref[...], hh) + _dot(fc_ref[...], hl)
    s = _dot(fs_ref[...], hh) + _dot(fs_ref[...], hl)
    kc_ref[...] = c[:, :width] + c[:, width:]
    orow = lax.broadcasted_iota(jnp.int32, (c.shape[0], width), 0)
    sign = jnp.where((orow == 0) & (j == 0), 1.0, -1.0)
    ks_ref[...] = s[:, :width] + sign * s[:, width:]


def _filt_spec(fwd, hfilt, tf):
    length, two_w = hfilt.shape
    width = two_w // 2
    nt = length // tf
    return pl.pallas_call(
        functools.partial(_filt_spec_kernel, width=width),
        grid=(nt,),
        in_specs=[pl.BlockSpec((tf, length), lambda j: (j, 0)),
                  pl.BlockSpec((tf, length), lambda j: (nt + j, 0)),
                  pl.BlockSpec((length, two_w), lambda j: (0, 0))],
        out_specs=[pl.BlockSpec((tf, width), lambda j: (j, 0)),
                   pl.BlockSpec((tf, width), lambda j: (j, 0))],
        out_shape=[jax.ShapeDtypeStruct((length, width), F32)] * 2,
        compiler_params=_params("arbitrary"),
        name="hyena_filter_spectrum",
    )(fwd, fwd, hfilt)


def _hy_prep_kernel(x0_ref, x1_ref, v_ref, w0_ref, w1_ref, w2_ref, b0_ref, b1_ref, b2_ref,
                    x0o_ref, vvo_ref):
    x0 = _depthwise_conv(x0_ref[0], w0_ref[...], HY_CONV // 2) + b0_ref[...]
    x1 = _depthwise_conv(x1_ref[0], w1_ref[...], HY_CONV // 2) + b1_ref[...]
    v = _depthwise_conv(v_ref[0], w2_ref[...], HY_CONV // 2) + b2_ref[...]
    x0o_ref[0] = x0
    vvo_ref[0] = v * x1


def _hy_prep(p_hy, conv_w, conv_b):
    bsz, length, three_w = p_hy.shape
    width = three_w // 3
    nb = width // LANES

    def col(off):
        return lambda b, j: (b, 0, off + j)

    def wcol(off):
        return lambda b, j: (0, off + j)

    k = conv_w.shape[0]
    return pl.pallas_call(
        _hy_prep_kernel,
        grid=(bsz, nb),
        in_specs=[pl.BlockSpec((1, length, LANES), col(0)),
                  pl.BlockSpec((1, length, LANES), col(nb)),
                  pl.BlockSpec((1, length, LANES), col(2 * nb)),
                  pl.BlockSpec((k, LANES), wcol(0)),
                  pl.BlockSpec((k, LANES), wcol(nb)),
                  pl.BlockSpec((k, LANES), wcol(2 * nb)),
                  pl.BlockSpec((1, LANES), wcol(0)),
                  pl.BlockSpec((1, LANES), wcol(nb)),
                  pl.BlockSpec((1, LANES), wcol(2 * nb))],
        out_specs=[pl.BlockSpec((1, length, LANES), col(0)),
                   pl.BlockSpec((1, length, LANES), col(0))],
        out_shape=[jax.ShapeDtypeStruct((bsz, length, width), F32)] * 2,
        compiler_params=_params("parallel", "parallel"),
        name="hyena_prep",
    )(p_hy, p_hy, p_hy, conv_w, conv_w, conv_w, conv_b, conv_b, conv_b)


def _dft_fwd_kernel(fc_ref, fs_ref, v_ref, kc_ref, ks_ref, yc_ref, ys_ref, vb_s):
    j = pl.program_id(1)

    @pl.when(j == 0)
    def _():
        vb_s[...] = v_ref[0].astype(BF16)

    uc = _dot(fc_ref[...], vb_s[...])
    us = _dot(fs_ref[...], vb_s[...])
    kc = kc_ref[...]
    ks = ks_ref[...]
    rows = lax.broadcasted_iota(jnp.int32, uc.shape, 0)
    special = (rows == 0) & (j == 0)
    yc_ref[0] = (uc * kc - jnp.where(special, 0.0, us * ks)).astype(BF16)
    ys_ref[0] = jnp.where(special, us * ks, uc * ks + us * kc).astype(BF16)


def _dft_fwd(fwd, vv, kc, ks, tf):
    bsz, length, width = vv.shape
    nt = length // tf
    return pl.pallas_call(
        _dft_fwd_kernel,
        grid=(bsz, nt),
        in_specs=[pl.BlockSpec((tf, length), lambda b, j: (j, 0)),
                  pl.BlockSpec((tf, length), lambda b, j: (nt + j, 0)),
                  pl.BlockSpec((1, length, width), lambda b, j: (b, 0, 0)),
                  pl.BlockSpec((tf, width), lambda b, j: (j, 0)),
                  pl.BlockSpec((tf, width), lambda b, j: (j, 0))],
        out_specs=[pl.BlockSpec((1, tf, width), lambda b, j: (b, j, 0)),
                   pl.BlockSpec((1, tf, width), lambda b, j: (b, j, 0))],
        out_shape=[jax.ShapeDtypeStruct((bsz, length, width), BF16)] * 2,
        scratch_shapes=[pltpu.VMEM((length, width), BF16)],
        compiler_params=_params("parallel", "arbitrary"),
        name="hyena_dft_fwd",
    )(fwd, fwd, vv, kc, ks)


def _dft_inv_kernel(ic_ref, is_ref, yc_ref, ys_ref, vv_ref, x0_ref, skip_ref, o_ref):
    y = _dot(ic_ref[...], yc_ref[0]) + _dot(is_ref[...], ys_ref[0])
    o_ref[0] = (x0_ref[0] * (y + vv_ref[0] * skip_ref[...])).astype(o_ref.dtype)


def _dft_inv(inv, yc, ys, vv, x0, skip, tt):
    bsz, length, width = vv.shape
    nt = length // tt
    return pl.pallas_call(
        _dft_inv_kernel,
        grid=(bsz, nt),
        in_specs=[pl.BlockSpec((tt, length), lambda b, i: (i, 0)),
                  pl.BlockSpec((tt, length), lambda b, i: (i, 1)),
                  pl.BlockSpec((1, length, width), lambda b, i: (b, 0, 0)),
                  pl.BlockSpec((1, length, width), lambda b, i: (b, 0, 0)),
                  pl.BlockSpec((1, tt, width), lambda b, i: (b, i, 0)),
                  pl.BlockSpec((1, tt, width), lambda b, i: (b, i, 0)),
                  pl.BlockSpec((1, width), lambda b, i: (0, 0))],
        out_specs=pl.BlockSpec((1, tt, width), lambda b, i: (b, i, 0)),
        out_shape=jax.ShapeDtypeStruct((bsz, length, width), BF16),
        compiler_params=_params("parallel", "parallel"),
        name="hyena_dft_inv",
    )(inv, inv, yc, ys, vv, x0, skip)


@functools.lru_cache(maxsize=None)
def _hyena_tables(length):
    n2 = 2 * length
    t = np.linspace(0.0, 1.0, length)[:, None]
    bands = (HY_EMB - 1) // 2
    wpos = 2.0 * np.pi * np.arange(length)[:, None] / length
    fb = np.linspace(1e-4, bands - 1, bands)[None]
    z = np.concatenate([t, np.cos(fb * wpos), -np.sin(fb * wpos)], axis=-1)
    zpad = np.zeros((length, LANES))
    zpad[:, :HY_EMB] = z
    f = np.arange(length)[:, None]
    n = np.arange(length)[None, :]
    ang = 2.0 * np.pi * ((f * n) % n2) / n2
    cos_m = np.cos(ang)
    sin_m = np.sin(ang)
    sin_m[0, :] = np.cos(np.pi * np.arange(length))
    fwd = np.concatenate([cos_m, sin_m], axis=0)
    scale = np.full((1, n2), 2.0 / n2)
    scale[0, 0] = 1.0 / n2
    scale[0, length] = 1.0 / n2
    inv = fwd.T * scale
    return (zpad.astype(np.float32), t.astype(np.float32), fwd.astype(np.float32),
            inv.astype(np.float32))


def _hyena(p_hy, conv_w, conv_b, filt, skip):
    bsz, length, three_w = p_hy.shape
    width = three_w // 3
    w_in, b_in, w_mid, b_mid, w_out, freq = filt
    zpad, t, fwd, inv = _hyena_tables(length)
    deltas = np.abs(np.linspace(HY_MIN_DECAY, HY_MAX_DECAY, width))[None, :]
    dec = np.exp(-t.astype(np.float64) * deltas).astype(np.float32)
    dec2 = jnp.asarray(np.concatenate([dec, dec], axis=1))
    ffn = w_in.shape[1]
    w_in_pad = jnp.zeros((LANES, ffn), F32).at[:HY_EMB].set(w_in)
    hfilt = _hy_filter(jnp.asarray(zpad), w_in_pad, b_in.reshape(1, ffn), w_mid,
                       b_mid.reshape(-1, 1, ffn), w_out, freq.reshape(1, ffn), dec2)
    fwd_b = jnp.asarray(fwd).astype(BF16)
    inv_b = jnp.asarray(inv).astype(BF16)
    tf = min(512, length)
    kc, ks = _filt_spec(fwd_b, hfilt, tf)
    x0, vv = _hy_prep(p_hy, conv_w, conv_b.reshape(1, three_w))
    yc, ys = _dft_fwd(fwd_b, vv, kc, ks, tf)
    return _dft_inv(inv_b, yc, ys, vv, x0, skip.reshape(1, width), tf)


def _gelu_tanh(x):
    return 0.5 * x * (1.0 + jnp.tanh(math.sqrt(2.0 / math.pi) * (x + 0.044715 * x * x * x)))


def _lru_kernel(xl_ref, yl_ref, xc_ref, cw_ref, cb_ref, wa_ref, ba_ref, wx_ref, bx_ref, ap_ref,
                o_ref, xs_s, a_s, b_s, h_s, *, ctx_len, lat_len):
    tot = ctx_len + lat_len
    ngrp = tot // 8
    ngrp_ctx = ctx_len // 8
    xs_s[0:ctx_len, :] = _depthwise_conv(xc_ref[0], cw_ref[...], LRU_CONV // 2) + cb_ref[...]
    xs_s[ctx_len:, :] = _depthwise_conv(xl_ref[0], cw_ref[...], LRU_CONV // 2) + cb_ref[...]
    xs = xs_s[...]
    xsb = xs.astype(BF16)
    rows8 = lax.broadcasted_iota(jnp.int32, (ngrp, 8, xs.shape[1]), 1)
    for d in range(2):
        r = _sigmoid(_dot(xsb, wa_ref[d, 0].astype(BF16)) + ba_ref[d])
        gi = _sigmoid(_dot(xsb, wx_ref[d, 0].astype(BF16)) + bx_ref[d])
        log_a = -LRU_C * r * _softplus(ap_ref[d])
        a = jnp.exp(log_a)
        b = jnp.sqrt(1.0 - a * a) * (gi * xs)
        a = a.reshape(ngrp, 8, a.shape[1])
        b = b.reshape(ngrp, 8, b.shape[1])
        for s in (1, 2, 4):
            keep = (rows8 >= s) if d == 0 else (rows8 < 8 - s)
            shift = s if d == 0 else 8 - s
            sa = jnp.where(keep, pltpu.roll(a, shift, axis=1), 1.0)
            sb = jnp.where(keep, pltpu.roll(b, shift, axis=1), 0.0)
            b = a * sb + b
            a = a * sa
        a_s[d] = a.reshape(tot, a.shape[2])
        b_s[d] = b.reshape(tot, b.shape[2])

    def group_fwd(i, c):
        r0 = pl.multiple_of(i * 8, 8)
        hg = a_s[0, pl.ds(r0, 8), :] * c + b_s[0, pl.ds(r0, 8), :]
        h_s[pl.ds(r0, 8), :] = hg
        return jnp.broadcast_to(hg[7:8, :], hg.shape)

    def group_bwd(i, c):
        gidx = jnp.where(i < ngrp_ctx, ngrp_ctx - 1 - i, ngrp + ngrp_ctx - 1 - i)
        r0 = pl.multiple_of(gidx * 8, 8)
        hg = a_s[1, pl.ds(r0, 8), :] * c + b_s[1, pl.ds(r0, 8), :]
        h_s[pl.ds(r0, 8), :] += hg
        return jnp.broadcast_to(hg[0:1, :], hg.shape)

    zero = jnp.zeros((8, xs.shape[1]), F32)
    lax.fori_loop(0, ngrp, group_fwd, zero, unroll=4)
    lax.fori_loop(0, ngrp, group_bwd, zero, unroll=4)
    o_ref[0] = (h_s[ctx_len:, :] * _gelu_tanh(yl_ref[0])).astype(o_ref.dtype)


def _rglru(xb_l, yb_l, xb_c, conv_w, conv_b, wa, ba, wx, bx, a_param):
    bsz, lat_len, width = xb_l.shape
    ctx_len = xb_c.shape[1]
    tot = ctx_len + lat_len
    blk = width // LRU_HEADS

    def col(b, h):
        return (b, 0, h)

    def wcol(b, h):
        return (0, h)

    def w3(b, h):
        return (0, 0, h)

    return pl.pallas_call(
        functools.partial(_lru_kernel, ctx_len=ctx_len, lat_len=lat_len),
        grid=(bsz, LRU_HEADS),
        in_specs=[pl.BlockSpec((1, lat_len, blk), col),
                  pl.BlockSpec((1, lat_len, blk), col),
                  pl.BlockSpec((1, ctx_len, blk), col),
                  pl.BlockSpec((LRU_CONV, blk), wcol),
                  pl.BlockSpec((1, blk), wcol),
                  pl.BlockSpec((2, 1, blk, blk), lambda b, h: (0, h, 0, 0)),
                  pl.BlockSpec((2, 1, blk), w3),
                  pl.BlockSpec((2, 1, blk, blk), lambda b, h: (0, h, 0, 0)),
                  pl.BlockSpec((2, 1, blk), w3),
                  pl.BlockSpec((2, 1, blk), w3)],
        out_specs=pl.BlockSpec((1, lat_len, blk), col),
        out_shape=jax.ShapeDtypeStruct((bsz, lat_len, width), BF16),
        scratch_shapes=[pltpu.VMEM((tot, blk), F32),
                        pltpu.VMEM((2, tot, blk), F32),
                        pltpu.VMEM((2, tot, blk), F32),
                        pltpu.VMEM((tot, blk), F32)],
        compiler_params=_params("parallel", "parallel"),
        name="rglru",
    )(xb_l, yb_l, xb_c, conv_w, conv_b.reshape(1, width), wa, ba.reshape(2, 1, width), wx,
      bx.reshape(2, 1, width), a_param.reshape(2, 1, width))


def _post_mixer_kernel(*refs, n_in, has_bias):
    a_refs = refs[:n_in]
    w_refs = refs[n_in:2 * n_in]
    i = 2 * n_in
    b_ref = None
    if has_bias:
        b_ref = refs[i]
        i += 1
    x_ref, g1_ref, sc_ref, sh_ref, lng_ref, lnb_ref, rw_ref, rb_ref, base_ref = refs[i:i + 9]
    x1_ref, v_ref, route_ref, cnt_ref = refs[i + 9:]

    @pl.when((pl.program_id(0) == 0) & (pl.program_id(1) == 0))
    def _():
        cnt_ref[...] = base_ref[...]

    y = None
    for a_ref, w_ref in zip(a_refs, w_refs):
        t = _dot(a_ref[0], w_ref[...])
        y = t if y is None else y + t
    if has_bias:
        y = y + b_ref[...]
    x1 = _layer_norm(DEEPNORM_ALPHA * x_ref[0] + g1_ref[0] * y, lng_ref[...], lnb_ref[...])
    x1_ref[0] = x1
    v = x1 * (1.0 + sc_ref[0]) + sh_ref[0]
    v_ref[0] = _pack_bf16_pairs(v)
    logits = _dot3(v, rw_ref[...]) + rb_ref[...]
    tm = logits.shape[0]
    lane = lax.broadcasted_iota(jnp.int32, logits.shape, 1).astype(F32)
    work = logits
    picks, firsts = [], []
    m0 = None
    for kk in range(TOP_K):
        m = jnp.max(work, axis=-1, keepdims=True)
        if kk == 0:
            m0 = m
        first = jnp.min(jnp.where(work == m, lane, float(LANES)), axis=-1, keepdims=True)
        pick = lane == first
        picks.append(pick)
        firsts.append(first)
        work = jnp.where(pick, -jnp.inf, work)
    sel = jnp.where(picks[0] | picks[1] | picks[2] | picks[3], 1.0, 0.0)
    e = sel * jnp.exp(logits - m0)
    gate = e / jnp.sum(e, axis=-1, keepdims=True)
    ti = lax.broadcasted_iota(jnp.int32, (tm, tm), 0)
    tj = lax.broadcasted_iota(jnp.int32, (tm, tm), 1)
    before = jnp.where(ti > tj, 1.0, 0.0).astype(BF16)
    slot = _dot(before, sel.astype(BF16)) + cnt_ref[...]
    route = jnp.zeros(logits.shape, F32)
    for kk in range(TOP_K):
        rank = jnp.sum(jnp.where(picks[kk], slot, 0.0), axis=-1, keepdims=True)
        wgt = jnp.sum(jnp.where(picks[kk], gate, 0.0), axis=-1, keepdims=True)
        route = jnp.where(lane == float(kk), firsts[kk], route)
        route = jnp.where(lane == float(TOP_K + kk), rank, route)
        route = jnp.where(lane == float(2 * TOP_K + kk), wgt, route)
    route_ref[0] = route
    cnt_ref[...] += jnp.sum(sel, axis=0, keepdims=True)


def _post_mixer(acts, ws, bias, x, g1, sc2, sh2, ln_g, ln_b, router_w, router_b, base, tm):
    bsz, length, d = x.shape
    tm = min(tm, length)
    n_in = len(acts)

    def row(bi, i):
        return (bi, i, 0)

    def per_b(bi, i):
        return (bi, 0, 0)

    def const(bi, i):
        return (0, 0)

    in_specs = [pl.BlockSpec((1, tm, a.shape[2]), row) for a in acts]
    in_specs += [pl.BlockSpec(w.shape, const) for w in ws]
    args = list(acts) + list(ws)
    if bias is not None:
        in_specs.append(pl.BlockSpec((1, d), const))
        args.append(bias.reshape(1, d))
    in_specs += [pl.BlockSpec((1, tm, d), row),
                 pl.BlockSpec((1, 1, d), per_b), pl.BlockSpec((1, 1, d), per_b),
                 pl.BlockSpec((1, 1, d), per_b),
                 pl.BlockSpec((1, d), const), pl.BlockSpec((1, d), const),
                 pl.BlockSpec((d, LANES), const), pl.BlockSpec((1, LANES), const),
                 pl.BlockSpec((1, LANES), const)]
    rw = jnp.zeros((d, LANES), F32).at[:, :N_EXPERTS].set(router_w)
    rb = jnp.full((1, LANES), -jnp.inf, F32).at[0, :N_EXPERTS].set(router_b)
    args += [x, g1, sc2, sh2, ln_g.reshape(1, d), ln_b.reshape(1, d), rw, rb, base]
    return pl.pallas_call(
        functools.partial(_post_mixer_kernel, n_in=n_in, has_bias=bias is not None),
        grid=(bsz, length // tm),
        in_specs=in_specs,
        out_specs=[pl.BlockSpec((1, tm, d), row), pl.BlockSpec((1, tm, d // 2), row),
                   pl.BlockSpec((1, tm, LANES), row), pl.BlockSpec((1, LANES), const)],
        out_shape=[jax.ShapeDtypeStruct((bsz, length, d), F32),
                   jax.ShapeDtypeStruct((bsz, length, d // 2), jnp.uint32),
                   jax.ShapeDtypeStruct((bsz, length, LANES), F32),
                   jax.ShapeDtypeStruct((1, LANES), F32)],
        compiler_params=_params("arbitrary", "arbitrary"),
        name="post_mixer",
    )(*args)


MOE_W1_CHUNK = 512


def _moe_rows_kernel(te_ref, nv_ref, first_ref, nxt_ref, xs_ref, w1_hbm, b1g_ref, b1l_ref, w2_hbm,
                     b2_ref, ys_ref, w1f_s, w2f_s, w1g_s, w1l_s, w2b_s, sem, *, layer):
    i = pl.program_id(0)
    nv = nv_ref[i]

    def fetch(e):
        return (pltpu.make_async_copy(w1_hbm.at[layer, e], w1f_s, sem.at[0]),
                pltpu.make_async_copy(w2_hbm.at[layer, e], w2f_s, sem.at[1]))

    @pl.when(i == 0)
    def _():
        for cp in fetch(te_ref[0]):
            cp.start()

    @pl.when(first_ref[i] == 1)
    def _():
        for cp in fetch(te_ref[i]):
            cp.wait()
        half = MOE_W1_CHUNK // 2
        for c in range(w1f_s.shape[1] // MOE_W1_CHUNK):
            t = w1f_s[:, c * MOE_W1_CHUNK:(c + 1) * MOE_W1_CHUNK].astype(BF16).T
            words = pltpu.bitcast(t, jnp.uint32)
            w1g_s[c * half:(c + 1) * half, :] = pltpu.bitcast(words << 16, F32).astype(BF16)
            w1l_s[c * half:(c + 1) * half, :] = pltpu.bitcast(
                words & jnp.uint32(0xFFFF0000), F32).astype(BF16)
        w2b_s[...] = w2f_s[...].astype(BF16)

        @pl.when(nxt_ref[i] >= 0)
        def _():
            for cp in fetch(nxt_ref[i]):
                cp.start()

    @pl.when(nv > 0)
    def _():
        rows = lax.broadcasted_iota(jnp.int32, xs_ref.shape, 0)
        x = _unpack_bf16_pairs(jnp.where(rows < nv, xs_ref[...], jnp.uint32(0))).astype(BF16)
        glu = jnp.minimum(_dot_nt(x, w1g_s[...]) + b1g_ref[0, 0], SWIGLU_LIMIT)
        lin = jnp.clip(_dot_nt(x, w1l_s[...]) + b1l_ref[0, 0], -SWIGLU_LIMIT, SWIGLU_LIMIT)
        act = glu * _sigmoid(SWIGLU_ALPHA * glu) * (lin + 1.0)
        ys_ref[...] = _pack_bf16_pairs(_dot(act.astype(BF16), w2b_s[...]) + b2_ref[0, 0])

    @pl.when(nv == 0)
    def _():
        ys_ref[...] = jnp.zeros_like(ys_ref)


def _moe_rows(xs, tile_expert, tile_rows, tile_first, tile_next, layer, w1, b1g, b1l, w2, b2, tm):
    n_rows, dh = xs.shape
    _, _, d, dff2 = w1.shape
    dff = dff2 // 2

    def row(i, te, nv, first, nxt):
        return (i, 0)

    def exp4(i, te, nv, first, nxt):
        return (layer, te[i], 0, 0)

    hbm = pl.BlockSpec(memory_space=pl.ANY)
    return pl.pallas_call(
        functools.partial(_moe_rows_kernel, layer=layer),
        grid_spec=pltpu.PrefetchScalarGridSpec(
            num_scalar_prefetch=4,
            grid=(n_rows // tm,),
            in_specs=[pl.BlockSpec((tm, dh), row), hbm,
                      pl.BlockSpec((1, 1, 1, dff), exp4), pl.BlockSpec((1, 1, 1, dff), exp4),
                      hbm, pl.BlockSpec((1, 1, 1, d), exp4)],
            out_specs=pl.BlockSpec((tm, dh), row),
            scratch_shapes=[pltpu.VMEM((d, dff2), F32), pltpu.VMEM((dff, d), F32),
                            pltpu.VMEM((dff, d), BF16),
                            pltpu.VMEM((dff, d), BF16), pltpu.VMEM((dff, d), BF16),
                            pltpu.SemaphoreType.DMA((2,))]),
        out_shape=jax.ShapeDtypeStruct((n_rows, dh), jnp.uint32),
        compiler_params=_params("arbitrary"),
        name="moe_rows",
    )(tile_expert, tile_rows, tile_first, tile_next, xs, w1, b1g, b1l, w2, b2)


def _moe_combine_kernel(y0_ref, y1_ref, y2_ref, y3_ref, route_ref, x1_ref, g2_ref, lng_ref, lnb_ref,
                        o_ref):
    route = route_ref[...]
    lane = lax.broadcasted_iota(jnp.int32, route.shape, 1)
    f = None
    for kk, y_ref in enumerate((y0_ref, y1_ref, y2_ref, y3_ref)):
        wgt = jnp.sum(jnp.where(lane == 2 * TOP_K + kk, route, 0.0), axis=-1, keepdims=True)
        term = wgt * _unpack_bf16_pairs(y_ref[0])
        f = term if f is None else f + term
    o_ref[0] = _layer_norm(DEEPNORM_ALPHA * x1_ref[0] + g2_ref[0] * f, lng_ref[...], lnb_ref[...])


def _moe_combine(yg, route, row_offset, x1, g2, ln_g, ln_b, tm):
    bsz, length, d = x1.shape
    tm = min(tm, length)
    nt = length // tm
    off = row_offset // tm

    def pick(kk):
        return lambda bi, i: (kk, off + bi * nt + i, 0)

    def const(bi, i):
        return (0, 0)

    return pl.pallas_call(
        _moe_combine_kernel,
        grid=(bsz, nt),
        in_specs=[pl.BlockSpec((1, tm, d // 2), pick(kk)) for kk in range(TOP_K)] + [
            pl.BlockSpec((tm, LANES), lambda bi, i: (off + bi * nt + i, 0)),
            pl.BlockSpec((1, tm, d), lambda bi, i: (bi, i, 0)),
            pl.BlockSpec((1, 1, d), lambda bi, i: (bi, 0, 0)),
            pl.BlockSpec((1, d), const), pl.BlockSpec((1, d), const)],
        out_specs=pl.BlockSpec((1, tm, d), lambda bi, i: (bi, i, 0)),
        out_shape=jax.ShapeDtypeStruct((bsz, length, d), F32),
        compiler_params=_params("parallel", "parallel"),
        name="moe_combine",
    )(yg, yg, yg, yg, route, x1, g2, ln_g.reshape(1, d), ln_b.reshape(1, d))


SC_CORES = 2
SC_SUBCORES = 16
SC_WORKERS = SC_CORES * SC_SUBCORES
SC_WINDOW = 64


def _sc_row_pipeline(nwin, read, write):
    read(0, 0).start()

    @pl.loop(0, nwin, step=2)
    def _(w0):
        for b in range(2):
            w = w0 + b

            @pl.when(w + 1 < nwin)
            def _():
                @pl.when(w >= 1)
                def _():
                    write(w - 1, 1 - b).wait()

                read(w + 1, 1 - b).start()

            read(w, b).wait()
            write(w, b).start()

    write(nwin - 2, 0).wait()
    write(nwin - 1, 1).wait()


def _sc_scatter_rows(src, pos, n_out):
    t_rows, d = src.shape
    nw, nwin, win = pos.shape
    assert nw == SC_WORKERS and nwin % 2 == 0 and t_rows % (nwin * win) == 0
    mesh = plsc.VectorSubcoreMesh(core_axis_name="c", subcore_axis_name="s")

    @functools.partial(
        pl.kernel, mesh=mesh, out_type=jax.ShapeDtypeStruct((n_out, d), src.dtype),
        scratch_types=[pltpu.VMEM((nwin, win), jnp.int32), pltpu.VMEM((2, win, d), src.dtype),
                       pltpu.SemaphoreType.DMA((2,)), pltpu.SemaphoreType.DMA((2,))])
    def scatter(src_hbm, pos_hbm, out_hbm, idx_v, rows_v, rsem, wsem):
        wid = lax.axis_index("s") * SC_CORES + lax.axis_index("c")
        t0 = lax.rem(wid * (nwin * win), t_rows)
        pltpu.sync_copy(pos_hbm.at[wid], idx_v)

        def read(w, slot):
            return pltpu.make_async_copy(src_hbm.at[pl.ds(t0 + w * win, win)], rows_v.at[slot],
                                         rsem.at[slot])

        def write(w, slot):
            return pltpu.make_async_copy(rows_v.at[slot], out_hbm.at[idx_v.at[w]], wsem.at[slot])

        _sc_row_pipeline(nwin, read, write)

    return scatter(src, pos)


def _sc_gather_rows(table, pos):
    _, d = table.shape
    nw, nwin, win = pos.shape
    assert nw == SC_WORKERS and nwin % 2 == 0
    per = nwin * win
    mesh = plsc.VectorSubcoreMesh(core_axis_name="c", subcore_axis_name="s")

    @functools.partial(
        pl.kernel, mesh=mesh, out_type=jax.ShapeDtypeStruct((nw * per, d), table.dtype),
        scratch_types=[pltpu.VMEM((nwin, win), jnp.int32), pltpu.VMEM((2, win, d), table.dtype),
                       pltpu.SemaphoreType.DMA((2,)), pltpu.SemaphoreType.DMA((2,))])
    def gather(table_hbm, pos_hbm, out_hbm, idx_v, rows_v, rsem, wsem):
        wid = lax.axis_index("s") * SC_CORES + lax.axis_index("c")
        base = wid * per
        pltpu.sync_copy(pos_hbm.at[wid], idx_v)

        def read(w, slot):
            return pltpu.make_async_copy(table_hbm.at[idx_v.at[w]], rows_v.at[slot], rsem.at[slot])

        def write(w, slot):
            return pltpu.make_async_copy(rows_v.at[slot], out_hbm.at[pl.ds(base + w * win, win)],
                                         wsem.at[slot])

        _sc_row_pipeline(nwin, read, write)

    return gather(table, pos)


MOE_TILE = 256


def _moe_sparse(v_all, route, counts, layer, w1, b1g, b1l, w2, b2):
    t_rows, dh = v_all.shape
    n_exp = w1.shape[1]
    pairs = TOP_K * t_rows
    n_tiles = pairs // MOE_TILE + n_exp
    route_t = route.T
    expert = route_t[0:TOP_K].astype(jnp.int32)
    slot = route_t[TOP_K:2 * TOP_K].astype(jnp.int32)
    cnt = counts[0, :n_exp].astype(jnp.int32)
    tiles_per = (cnt + MOE_TILE - 1) // MOE_TILE
    tile_end = jnp.cumsum(tiles_per)
    tile_start = tile_end - tiles_per
    pos = slot
    for e in range(n_exp):
        pos = pos + jnp.where(expert == e, tile_start[e] * MOE_TILE, 0)
    nwin = pairs // (SC_WORKERS * SC_WINDOW)
    pos_km = pos.reshape(SC_WORKERS, nwin, SC_WINDOW)
    tile_ids = jnp.arange(n_tiles, dtype=jnp.int32)[:, None]
    owns = (tile_ids >= tile_start[None, :]) & (tile_ids < tile_end[None, :])
    experts = jnp.arange(n_exp, dtype=jnp.int32)[None, :]
    last_used = jnp.max(jnp.where(tiles_per > 0, experts[0], 0))
    used = jnp.any(owns, axis=1)
    te = jnp.where(used, jnp.sum(jnp.where(owns, experts, 0), axis=1), last_used).astype(jnp.int32)
    rows_left = cnt[None, :] - (tile_ids - tile_start[None, :]) * MOE_TILE
    tile_rows = jnp.sum(jnp.where(owns, jnp.clip(rows_left, 0, MOE_TILE), 0), axis=1)
    tile_rows = tile_rows.astype(jnp.int32)
    tile_first = jnp.any(owns & (tile_ids == tile_start[None, :]), axis=1).astype(jnp.int32)
    later = (experts > experts.T) & (tiles_per[None, :] > 0)
    next_e = jnp.min(jnp.where(later, experts, n_exp), axis=1)
    next_e = jnp.where(next_e == n_exp, -1, next_e)
    tile_next = jnp.where(used, jnp.sum(jnp.where(owns, next_e[None, :], 0), axis=1), -1)
    tile_next = tile_next.astype(jnp.int32)
    xs = _sc_scatter_rows(v_all, pos_km, n_tiles * MOE_TILE)
    ys = _moe_rows(xs, te, tile_rows, tile_first, tile_next, layer, w1, b1g, b1l, w2, b2, MOE_TILE)
    return _sc_gather_rows(ys, pos_km).reshape(TOP_K, t_rows, dh)


@functools.lru_cache(maxsize=None)
def _sincos_2d(rows, cols, dim):
    quarter = dim // 4
    omega = 1.0 / (10000.0 ** (np.arange(quarter, dtype=np.float64) / quarter))

    def emb1d(n):
        ang = np.arange(n, dtype=np.float64)[:, None] * omega
        return np.concatenate([np.sin(ang), np.cos(ang)], axis=-1)

    er = np.broadcast_to(emb1d(rows)[:, None], (rows, cols, dim // 2))
    ec = np.broadcast_to(emb1d(cols)[None], (rows, cols, dim // 2))
    return np.concatenate([er, ec], axis=-1).reshape(rows * cols, dim).astype(np.float32)


def _gate_layout(gates, n_ch):
    bsz, length, _ = gates.shape
    g = gates[..., :n_ch].reshape(bsz, length // DN_CHUNK, DN_CHUNK, n_ch)
    return g.transpose(0, 3, 1, 2)


def kernel(x, c, ctx, c_ctx, ada_w, ada_b, ln_g, ln_b, ev_w_in, ev_w_out, dn_conv_w, dn_a_log, dn_dt_bias, dn_norm_g, hy_conv_w, hy_conv_b, hy_w_in, hy_b_in, hy_w_mid, hy_b_mid, hy_w_out, hy_freq, hy_skip, od_w_in, od_b_in, lru_conv_w, lru_conv_b, lru_wa, lru_ba, lru_wx, lru_bx, lru_a_param, od_w_out, od_b_out, router_w, router_b, moe_w1, moe_b1, moe_w2, moe_b2):
    bsz, length, d = x.shape
    ctx_len = ctx.shape[1]
    pos = jnp.asarray(_sincos_2d(length // GRID_W, GRID_W, d))

    cond = jnp.zeros((16, d), F32).at[:bsz].set(c).at[bsz].set(c_ctx)
    mod = _modulation(cond, ada_w, ada_b).reshape(DEPTH, 16, 6, d)

    def lat_mod(layer, k):
        return mod[layer, :bsz, k][:, None, :]

    def ctx_mod(layer, k):
        return jnp.broadcast_to(mod[layer, bsz, k][None, None, :], (bsz, 1, d))

    hc = ctx
    for layer in range(DEPTH):
        last = layer == DEPTH - 1
        j = layer // 2
        if layer % 2 == 0:
            dn_qk = DN_HEADS * DN_DK
            dn_qkv = 3 * dn_qk
            dn_in = dn_qkv + dn_qk + 4 * DN_HEADS
            w_in = ev_w_in[j]
            gate_w = jnp.zeros((d, LANES), F32).at[:, :4 * DN_HEADS].set(w_in[:, dn_qkv + dn_qk:dn_in])
            w_cat = jnp.concatenate([w_in[:, :dn_qkv + dn_qk], gate_w, w_in[:, dn_in:]],
                                    axis=1).astype(BF16)
            hy_in = w_in.shape[1] - dn_in
            widths = (dn_qkv, dn_qk, LANES, hy_in)
            qkv_l, z_l, gt_l, phy_l = _inproj(x, lat_mod(layer, 1), lat_mod(layer, 0), pos,
                                               w_cat, None, widths, 512)
            qkv_c, z_c, gt_c, phy_c = _inproj(hc, ctx_mod(layer, 1), ctx_mod(layer, 0), None,
                                               w_cat, None, widths, 512)
            gates = jnp.concatenate([_gate_layout(gt_c, 4 * DN_HEADS),
                                     _gate_layout(gt_l, 4 * DN_HEADS)], axis=2)
            n_chunks = gates.shape[2]
            gates = jnp.pad(gates, ((0, 0), (0, 0), (0, -n_chunks % 16), (0, 0)))
            dn_l, dn_c = _deltanet(qkv_l, z_l, qkv_c, z_c, gates, dn_conv_w[j], dn_a_log[j],
                                   dn_dt_bias[j], dn_norm_g[j])
            filt = (hy_w_in[j], hy_b_in[j], hy_w_mid[j], hy_b_mid[j], hy_w_out[j], hy_freq[j])
            hy_l = _hyena(phy_l, hy_conv_w[j], hy_conv_b[j], filt, hy_skip[j])
            w_out = ev_w_out[j].astype(BF16)
            half = dn_l.shape[2]
            acts_l, ws, bias = (dn_l, hy_l), (w_out[:half], w_out[half:]), None
            acts_c = None
            if not last:
                hy_c = _hyena(phy_c, hy_conv_w[j], hy_conv_b[j], filt, hy_skip[j])
                acts_c = (dn_c, hy_c)
        else:
            w_in = od_w_in[j].astype(BF16)
            width = w_in.shape[1] // 2
            b_in = od_b_in[j].reshape(1, 2 * width)
            xb_l, yb_l = _inproj(x, lat_mod(layer, 1), lat_mod(layer, 0), pos, w_in, b_in,
                                 (width, width), 512)
            xb_c, _ = _inproj(hc, ctx_mod(layer, 1), ctx_mod(layer, 0), None, w_in, b_in,
                              (width, width), 512)
            act_l = _rglru(xb_l, yb_l, xb_c, lru_conv_w[j], lru_conv_b[j], lru_wa[j], lru_ba[j],
                           lru_wx[j], lru_bx[j], lru_a_param[j])
            acts_l, ws, bias = (act_l,), (od_w_out[j].astype(BF16),), od_b_out[j]
            acts_c = None
            assert last, "context outputs of the RG-LRU layer are only needed before the last layer"


        x1, v, route, counts = _post_mixer(acts_l, ws, bias, x, lat_mod(layer, 2),
                                           lat_mod(layer, 4), lat_mod(layer, 3), ln_g[layer, 0],
                                           ln_b[layer, 0], router_w[layer], router_b[layer],
                                           jnp.zeros((1, LANES), F32), 512)
        v_all = v.reshape(bsz * length, d // 2)
        route = route.reshape(bsz * length, LANES)
        if not last:
            hc1, vc, route_c, counts = _post_mixer(acts_c, ws, bias, hc, ctx_mod(layer, 2),
                                                   ctx_mod(layer, 4), ctx_mod(layer, 3),
                                                   ln_g[layer, 0], ln_b[layer, 0], router_w[layer],
                                                   router_b[layer], counts, 256)
            v_all = jnp.concatenate([v_all, vc.reshape(bsz * ctx_len, d // 2)], axis=0)
            route = jnp.concatenate([route, route_c.reshape(bsz * ctx_len, LANES)], axis=0)
        yg = _moe_sparse(v_all, route, counts, layer, moe_w1, moe_b1[:, :, None, 0::2],
                         moe_b1[:, :, None, 1::2], moe_w2, moe_b2[:, :, None, :])
        x = _moe_combine(yg, route, 0, x1, lat_mod(layer, 5), ln_g[layer, 1], ln_b[layer, 1], 256)
        if not last:
            hc = _moe_combine(yg, route, bsz * length, hc1, ctx_mod(layer, 5), ln_g[layer, 1],
                              ln_b[layer, 1], 256)
    return x
```

```python
import functools
import math

import numpy as np
import jax
import jax.numpy as jnp
from jax import lax
from jax.experimental import pallas as pl
from jax.experimental.pallas import tpu as pltpu
from jax.experimental.pallas import tpu_sc as plsc

F32 = jnp.float32
BF16 = jnp.bfloat16

VMEM_LIMIT_BYTES = 56 * 1024 * 1024
LANES = 128

DEPTH = 2
GRID_W = 64
DEEPNORM_ALPHA = (2.0 * DEPTH) ** 0.25
LN_EPS = 1e-5
RMS_EPS = 1e-6

DN_HEADS = 4
DN_DK = 128
DN_CHUNK = 64
DN_CONV = 4

HY_EMB = 33
HY_TARGET = 1e-2
HY_MIN_DECAY = math.log(HY_TARGET) / 1.5
HY_MAX_DECAY = math.log(HY_TARGET) / 0.3
HY_CONV = 3

LRU_HEADS = 4
LRU_C = 8.0
LRU_CONV = 4

N_EXPERTS = 32
TOP_K = 4
SWIGLU_ALPHA = 1.702
SWIGLU_LIMIT = 7.0


def _params(*sem):
    return pltpu.CompilerParams(dimension_semantics=sem, vmem_limit_bytes=VMEM_LIMIT_BYTES)


def _dot(a, b):
    return jnp.dot(a, b, preferred_element_type=F32)


def _dot_nt(a, b):
    return lax.dot_general(a, b, (((1,), (1,)), ((), ())), preferred_element_type=F32)


def _split(a):
    hi = a.astype(BF16)
    lo = (a - hi.astype(F32)).astype(BF16)
    return hi, lo


def _dot3(a, b):
    ah, al = _split(a)
    bh, bl = _split(b)
    return _dot(ah, bh) + _dot(ah, bl) + _dot(al, bh)


def _silu(x):
    return x * (1.0 / (1.0 + jnp.exp(-x)))


def _sigmoid(x):
    return 1.0 / (1.0 + jnp.exp(-x))


def _softplus(x):
    return jnp.maximum(x, 0.0) + jnp.log(1.0 + jnp.exp(-jnp.abs(x)))


def _layer_norm(x, g, b):
    mu = jnp.mean(x, axis=-1, keepdims=True)
    xc = x - mu
    var = jnp.mean(xc * xc, axis=-1, keepdims=True)
    return xc * lax.rsqrt(var + LN_EPS) * g + b


def _pack_bf16_pairs(x):
    w = x.shape[1] // 2
    lo = pltpu.bitcast(x[:, :w].astype(BF16).astype(F32), jnp.uint32) >> 16
    hi = pltpu.bitcast(x[:, w:].astype(BF16).astype(F32), jnp.uint32) & jnp.uint32(0xFFFF0000)
    return lo | hi


def _unpack_bf16_pairs(p):
    lo = pltpu.bitcast(p << 16, F32)
    hi = pltpu.bitcast(p & jnp.uint32(0xFFFF0000), F32)
    return jnp.concatenate([lo, hi], axis=1)


def _shift_rows(x, s):
    if s == 0:
        return x
    n = x.shape[0]
    rows = lax.broadcasted_iota(jnp.int32, x.shape, 0)
    valid = (rows >= s) if s > 0 else (rows < n + s)
    return jnp.where(valid, pltpu.roll(x, s % n, axis=0), 0.0)


def _depthwise_conv(x, w, pad_left):
    acc = None
    for i in range(w.shape[0]):
        term = _shift_rows(x, pad_left - i) * w[i:i + 1, :]
        acc = term if acc is None else acc + term
    return acc


def _mod_kernel(c_ref, w_ref, b_ref, o_ref):
    o_ref[0] = _dot3(_silu(c_ref[...]), w_ref[0]) + b_ref[0]


def _modulation(cond, ada_w, ada_b):
    depth, d, n = ada_w.shape
    rows = cond.shape[0]
    tn = 1536
    return pl.pallas_call(
        _mod_kernel,
        grid=(depth, n // tn),
        in_specs=[
            pl.BlockSpec((rows, d), lambda l, j: (0, 0)),
            pl.BlockSpec((1, d, tn), lambda l, j: (l, 0, j)),
            pl.BlockSpec((1, 1, tn), lambda l, j: (l, 0, j)),
        ],
        out_specs=pl.BlockSpec((1, rows, tn), lambda l, j: (l, 0, j)),
        out_shape=jax.ShapeDtypeStruct((depth, rows, n), F32),
        compiler_params=_params("parallel", "parallel"),
        name="modulation",
    )(cond, ada_w, ada_b.reshape(depth, 1, n))


def _inproj_kernel(*refs, splits, has_pos, has_bias):
    x_ref, sc_ref, sh_ref = refs[:3]
    i = 3
    pos_ref = None
    if has_pos:
        pos_ref = refs[i]
        i += 1
    w_ref = refs[i]
    i += 1
    b_ref = None
    if has_bias:
        b_ref = refs[i]
        i += 1
    o_refs = refs[i:]
    u = x_ref[0] * (1.0 + sc_ref[0]) + sh_ref[0]
    if has_pos:
        u = u + pos_ref[...]
    ub = u.astype(BF16)
    for o_ref, (s, e) in zip(o_refs, splits):
        acc = _dot(ub, w_ref[:, s:e])
        if has_bias:
            acc = acc + b_ref[:, s:e]
        o_ref[0] = acc


def _inproj(x, sc, sh, pos, w, b, widths, tm):
    bsz, length, d = x.shape
    n = w.shape[1]
    splits, s = [], 0
    for wd in widths:
        splits.append((s, s + wd))
        s += wd
    assert s == n
    tm = min(tm, length)
    in_specs = [
        pl.BlockSpec((1, tm, d), lambda bi, i: (bi, i, 0)),
        pl.BlockSpec((1, 1, d), lambda bi, i: (bi, 0, 0)),
        pl.BlockSpec((1, 1, d), lambda bi, i: (bi, 0, 0)),
    ]
    args = [x, sc, sh]
    if pos is not None:
        in_specs.append(pl.BlockSpec((tm, d), lambda bi, i: (i, 0)))
        args.append(pos)
    in_specs.append(pl.BlockSpec((d, n), lambda bi, i: (0, 0)))
    args.append(w)
    if b is not None:
        in_specs.append(pl.BlockSpec((1, n), lambda bi, i: (0, 0)))
        args.append(b)
    return pl.pallas_call(
        functools.partial(_inproj_kernel, splits=tuple(splits), has_pos=pos is not None,
                          has_bias=b is not None),
        grid=(bsz, length // tm),
        in_specs=in_specs,
        out_specs=[pl.BlockSpec((1, tm, wd), lambda bi, i: (bi, i, 0)) for wd in widths],
        out_shape=[jax.ShapeDtypeStruct((bsz, length, wd), F32) for wd in widths],
        compiler_params=_params("parallel", "parallel"),
        name="inproj",
    )(*args)


def _unit_tri_inverses(mats, lower):
    n = mats[0].shape[0]
    nb = 16
    np_ = len(mats)
    ii = lax.broadcasted_iota(jnp.int32, (n, n), 0)
    jj = lax.broadcasted_iota(jnp.int32, (n, n), 1)
    same16 = (ii // nb) == (jj // nb)
    same32 = (ii // (2 * nb)) == (jj // (2 * nb))
    dgs = []
    for a in mats:
        ad = jnp.where(same16, a, 0.0)
        dgs.append(ad[0:nb] + ad[nb:2 * nb] + ad[2 * nb:3 * nb] + ad[3 * nb:4 * nb])
    dg = jnp.concatenate(dgs, axis=0)
    rr = lax.broadcasted_iota(jnp.int32, dg.shape, 0)
    ll = lax.broadcasted_iota(jnp.int32, dg.shape, 1)
    xd = jnp.where(rr % nb == ll % nb, 1.0, 0.0)
    blk0 = (ll // nb) * nb
    for s in (range(nb - 1) if lower else range(nb - 1, 0, -1)):
        col = jnp.take_along_axis(dg, blk0 + s, axis=1)
        row = jnp.concatenate(
            [jnp.broadcast_to(xd[p * nb + s:p * nb + s + 1, :], (nb, n)) for p in range(np_)], axis=0)
        xd = xd - col * row
    ds = [jnp.where(same16, jnp.concatenate([xd[p * nb:(p + 1) * nb]] * (n // nb), axis=0), 0.0)
          for p in range(np_)]
    lvl1 = same32 & jnp.logical_not(same16)
    t1 = [_dot3(d, jnp.where(lvl1, a, 0.0)) for d, a in zip(ds, mats)]
    x1 = [d - _dot3(t, d) for d, t in zip(ds, t1)]
    x1b = [x.astype(BF16) for x in x1]
    t2 = [_dot(xb, jnp.where(same32, 0.0, a).astype(BF16)) for xb, a in zip(x1b, mats)]
    return [x - _dot(t.astype(BF16), xb) for x, t, xb in zip(x1, t2, x1b)]


def _dn_kernel(alog_ref, dtb_ref,
               ql_ref, kl_ref, vl_ref, zl_ref, qc_ref, kc_ref, vc_ref, zc_ref, gt_ref,
               cwq_ref, cwk_ref, cwv_ref, ng_ref,
               yl_ref, yc_ref,
               qn_s, kn_s, vn_s, gc_s, bt_s, nq_s, c_s, gl_s, o_s,
               *, ctx_len, lat_len):
    h = pl.program_id(1)
    csz = DN_CHUNK
    nc_ctx = ctx_len // csz
    nc = (ctx_len + lat_len) // csz

    def prep(src_ref, cw_ref, kind):
        t = _silu(_depthwise_conv(src_ref[0], cw_ref[...], DN_CONV // 2))
        if kind == "v":
            return t
        t = t * lax.rsqrt(jnp.sum(t * t, axis=-1, keepdims=True) + RMS_EPS)
        return t * (DN_DK ** -0.5) if kind == "q" else t

    qn_s[0:ctx_len, :] = prep(qc_ref, cwq_ref, "q")
    qn_s[ctx_len:, :] = prep(ql_ref, cwq_ref, "q")
    kn_s[0:ctx_len, :] = prep(kc_ref, cwk_ref, "k")
    kn_s[ctx_len:, :] = prep(kl_ref, cwk_ref, "k")
    vn_s[0:ctx_len, :] = prep(vc_ref, cwv_ref, "v")
    vn_s[ctx_len:, :] = prep(vl_ref, cwv_ref, "v")

    ii = lax.broadcasted_iota(jnp.int32, (csz, csz), 0)
    jj = lax.broadcasted_iota(jnp.int32, (csz, csz), 1)
    eye = ii == jj
    for d in range(2):
        graw = gt_ref[0, d * 2 * DN_HEADS + h]
        braw = gt_ref[0, d * 2 * DN_HEADS + DN_HEADS + h]
        a_neg = -jnp.exp(jnp.zeros_like(graw) + alog_ref[d, h])
        g = a_neg * _softplus(graw + dtb_ref[d, h])
        tri = jnp.where((ii <= jj) if d == 0 else (ii >= jj), 1.0, 0.0).astype(BF16)
        g1 = g.astype(BF16)
        r1 = g - g1.astype(F32)
        g2 = r1.astype(BF16)
        g3 = (r1 - g2.astype(F32)).astype(BF16)
        gc_s[d] = _dot(g1, tri) + _dot(g2, tri) + _dot(g3, tri)
        bt_s[d] = _sigmoid(braw)

    o_s[...] = jnp.zeros_like(o_s)

    group = 12
    assert nc % group == 0

    def chunk_prep(gi, carry):
        ns = [gi * group + c for c in range(group)]
        r0s = [pl.multiple_of(n * csz, csz) for n in ns]
        qs = [qn_s[pl.ds(r0, csz), :] for r0 in r0s]
        ks = [kn_s[pl.ds(r0, csz), :] for r0 in r0s]
        vs = [vn_s[pl.ds(r0, csz), :] for r0 in r0s]
        kbfs = [k.astype(BF16) for k in ks]
        qks = [_dot_nt(q.astype(BF16), kbf) for q, kbf in zip(qs, kbfs)]
        for d in range(2):
            incl = (ii >= jj) if d == 0 else (ii <= jj)
            strict = (ii > jj) if d == 0 else (ii < jj)
            grs = [gc_s[d, pl.ds(n, 1), :] for n in ns]
            grows = [jnp.broadcast_to(gr, (csz, csz)) for gr in grs]
            gcols = [jnp.sum(jnp.where(eye, grow, 0.0), axis=1, keepdims=True) for grow in grows]
            bcols = [jnp.sum(jnp.where(eye, jnp.broadcast_to(bt_s[d, pl.ds(n, 1), :], (csz, csz)),
                                       0.0), axis=1, keepdims=True) for n in ns]
            decays = [jnp.where(incl, jnp.exp(jnp.where(incl, gcol - grow, 0.0)), 0.0)
                      for gcol, grow in zip(gcols, grows)]
            kbs = [k * bcol for k, bcol in zip(ks, bcols)]
            amats = [jnp.where(strict, _dot_nt(kb.astype(BF16), kbf) * decay, 0.0)
                     for kb, kbf, decay in zip(kbs, kbfs, decays)]
            tbs = [t.astype(BF16) for t in _unit_tri_inverses(amats, lower=(d == 0))]
            cs = range(group)
            egs = [jnp.exp(gcols[c]) for c in cs]
            ubs = [_dot(tbs[c], (vs[c] * bcols[c]).astype(BF16)).astype(BF16) for c in cs]
            wbs = [_dot(tbs[c], (kbs[c] * egs[c]).astype(BF16)).astype(BF16) for c in cs]
            attns = [jnp.where(incl, qks[c] * decays[c], 0.0).astype(BF16) for c in cs]
            glasts = [grs[c][:, csz - 1:csz] if d == 0 else grs[c][:, 0:1] for c in cs]
            kdts = [(ks[c] * jnp.exp(glasts[c] - gcols[c])).T.astype(BF16) for c in cs]
            nmats = [_dot(kdts[c], wbs[c]).astype(BF16) for c in cs]
            qmats = [(qs[c] * egs[c] - _dot(attns[c], wbs[c])).astype(BF16) for c in cs]
            cmats = [_dot(kdts[c], ubs[c]) for c in cs]
            omats = [_dot(attns[c], ubs[c]) for c in cs]
            for c in cs:
                n = ns[c]
                nq_s[d, n, 0:DN_DK, :] = nmats[c]
                nq_s[d, n, DN_DK:DN_DK + csz, :] = qmats[c]
                c_s[d, n] = cmats[c]
                o_s[pl.ds(r0s[c], csz), :] += omats[c]
                gl_s[d, pl.ds(n, 1), :] = jnp.broadcast_to(jnp.exp(glasts[c]), (1, LANES))
        return carry

    lax.fori_loop(0, nc // group, chunk_prep, 0)

    def step(i, states):
        new_states = []
        for d in range(2):
            if d == 0:
                n = i
            else:
                n = jnp.where(i < nc_ctx, nc_ctx - 1 - i, nc + nc_ctx - 1 - i)
            r0 = pl.multiple_of(n * csz, csz)
            s = states[d]
            r = _dot(nq_s[d, n], s.astype(BF16))
            o_s[pl.ds(r0, csz), :] += r[DN_DK:DN_DK + csz]
            new_states.append(s * gl_s[d, pl.ds(n, 1), :] - r[0:DN_DK] + c_s[d, n])
        return tuple(new_states)

    zero = jnp.zeros((DN_DK, DN_DK), F32)
    lax.fori_loop(0, nc, step, (zero, zero))

    def gated_norm(o, z):
        o = o * lax.rsqrt(jnp.mean(o * o, axis=-1, keepdims=True) + RMS_EPS) * ng_ref[...]
        return (o * _silu(z)).astype(yl_ref.dtype)

    yc_ref[0] = gated_norm(o_s[0:ctx_len, :], zc_ref[0])
    yl_ref[0] = gated_norm(o_s[ctx_len:, :], zl_ref[0])


def _deltanet(qkv_l, z_l, qkv_c, z_c, gates, conv_w, a_log, dt_bias, norm_g):
    bsz, lat_len, _ = qkv_l.shape
    ctx_len = qkv_c.shape[1]
    tot = ctx_len + lat_len
    nc = tot // DN_CHUNK
    ncp = gates.shape[2]
    hd = DN_DK
    nh = DN_HEADS

    def col(off):
        return lambda b, h: (b, 0, off + h)

    def wcol(off):
        return lambda b, h: (0, off + h)

    smem = pl.BlockSpec(memory_space=pltpu.SMEM)
    in_specs = [
        smem, smem,
        pl.BlockSpec((1, lat_len, hd), col(0)),
        pl.BlockSpec((1, lat_len, hd), col(nh)),
        pl.BlockSpec((1, lat_len, hd), col(2 * nh)),
        pl.BlockSpec((1, lat_len, hd), col(0)),
        pl.BlockSpec((1, ctx_len, hd), col(0)),
        pl.BlockSpec((1, ctx_len, hd), col(nh)),
        pl.BlockSpec((1, ctx_len, hd), col(2 * nh)),
        pl.BlockSpec((1, ctx_len, hd), col(0)),
        pl.BlockSpec((1, 4 * nh, ncp, DN_CHUNK), lambda b, h: (b, 0, 0, 0)),
        pl.BlockSpec((DN_CONV, hd), wcol(0)),
        pl.BlockSpec((DN_CONV, hd), wcol(nh)),
        pl.BlockSpec((DN_CONV, hd), wcol(2 * nh)),
        pl.BlockSpec((1, hd), lambda b, h: (0, 0)),
    ]
    scratch = [
        pltpu.VMEM((tot, hd), F32), pltpu.VMEM((tot, hd), F32), pltpu.VMEM((tot, hd), F32),
        pltpu.VMEM((2, ncp, DN_CHUNK), F32), pltpu.VMEM((2, ncp, DN_CHUNK), F32),
        pltpu.VMEM((2, nc, hd + DN_CHUNK, hd), BF16),
        pltpu.VMEM((2, nc, hd, hd), F32),
        pltpu.VMEM((2, nc, LANES), F32),
        pltpu.VMEM((tot, hd), F32),
    ]
    return pl.pallas_call(
        functools.partial(_dn_kernel, ctx_len=ctx_len, lat_len=lat_len),
        grid=(bsz, nh),
        in_specs=in_specs,
        out_specs=[pl.BlockSpec((1, lat_len, hd), col(0)),
                   pl.BlockSpec((1, ctx_len, hd), col(0))],
        out_shape=[jax.ShapeDtypeStruct((bsz, lat_len, nh * hd), BF16),
                   jax.ShapeDtypeStruct((bsz, ctx_len, nh * hd), BF16)],
        scratch_shapes=scratch,
        compiler_params=_params("parallel", "parallel"),
        name="deltanet",
    )(a_log, dt_bias, qkv_l, qkv_l, qkv_l, z_l, qkv_c, qkv_c, qkv_c, z_c, gates,
      conv_w, conv_w, conv_w, norm_g.reshape(1, hd))


def _hy_filter_kernel(z_ref, win_ref, bin_ref, wmid_ref, bmid_ref, wout_ref, freq_ref, dec_ref,
                      o_ref):
    freq = freq_ref[...]
    hcur = jnp.sin(freq * (_dot3(z_ref[...], win_ref[...]) + bin_ref[...]))
    for i in range(wmid_ref.shape[0]):
        hcur = jnp.sin(freq * (_dot3(hcur, wmid_ref[i]) + bmid_ref[i]))
    o_ref[...] = _dot3(hcur, wout_ref[...]) * dec_ref[...]


def _hy_filter(z, w_in, b_in, w_mid, b_mid, w_out, freq, dec2):
    length = z.shape[0]
    n_out = w_out.shape[1]
    tl = min(256, length)

    def whole(a):
        return pl.BlockSpec(a.shape, lambda i: (0,) * a.ndim)

    return pl.pallas_call(
        _hy_filter_kernel,
        grid=(length // tl,),
        in_specs=[pl.BlockSpec((tl, z.shape[1]), lambda i: (i, 0)),
                  whole(w_in), whole(b_in), whole(w_mid), whole(b_mid), whole(w_out), whole(freq),
                  pl.BlockSpec((tl, n_out), lambda i: (i, 0))],
        out_specs=pl.BlockSpec((tl, n_out), lambda i: (i, 0)),
        out_shape=jax.ShapeDtypeStruct((length, n_out), F32),
        compiler_params=_params("parallel"),
        name="hyena_filter",
    )(z, w_in, b_in, w_mid, b_mid, w_out, freq, dec2)


def _filt_spec_kernel(fc_ref, fs_ref, h_ref, kc_ref, ks_ref, *, width):
    j = pl.program_id(0)
    hmat = h_ref[...]
    rows = lax.broadcasted_iota(jnp.int32, hmat.shape, 0)
    cols = lax.broadcasted_iota(jnp.int32, hmat.shape, 1)
    hmat = jnp.where((rows == 0) & (cols >= width), 0.0, hmat)
    hh, hl = _split(hmat)
    c = _dot(fc_ref[...], hh) + _dot(fc_ref[...], hl)
    s = _dot(fs_ref[...], hh) + _dot(fs_ref[...], hl)
    kc_ref[...] = c[:, :width] + c[:, width:]
    orow = lax.broadcasted_iota(jnp.int32, (c.shape[0], width), 0)
    sign = jnp.where((orow == 0) & (j == 0), 1.0, -1.0)
    ks_ref[...] = s[:, :width] + sign * s[:, width:]


def _filt_spec(fwd, hfilt, tf):
    length, two_w = hfilt.shape
    width = two_w // 2
    nt = length // tf
    return pl.pallas_call(
        functools.partial(_filt_spec_kernel, width=width),
        grid=(nt,),
        in_specs=[pl.BlockSpec((tf, length), lambda j: (j, 0)),
                  pl.BlockSpec((tf, length), lambda j: (nt + j, 0)),
                  pl.BlockSpec((length, two_w), lambda j: (0, 0))],
        out_specs=[pl.BlockSpec((tf, width), lambda j: (j, 0)),
                   pl.BlockSpec((tf, width), lambda j: (j, 0))],
        out_shape=[jax.ShapeDtypeStruct((length, width), F32)] * 2,
        compiler_params=_params("arbitrary"),
        name="hyena_filter_spectrum",
    )(fwd, fwd, hfilt)


def _hy_prep_kernel(x0_ref, x1_ref, v_ref, w0_ref, w1_ref, w2_ref, b0_ref, b1_ref, b2_ref,
                    x0o_ref, vvo_ref):
    x0 = _depthwise_conv(x0_ref[0], w0_ref[...], HY_CONV // 2) + b0_ref[...]
    x1 = _depthwise_conv(x1_ref[0], w1_ref[...], HY_CONV // 2) + b1_ref[...]
    v = _depthwise_conv(v_ref[0], w2_ref[...], HY_CONV // 2) + b2_ref[...]
    x0o_ref[0] = x0
    vvo_ref[0] = v * x1


def _hy_prep(p_hy, conv_w, conv_b):
    bsz, length, three_w = p_hy.shape
    width = three_w // 3
    nb = width // LANES

    def col(off):
        return lambda b, j: (b, 0, off + j)

    def wcol(off):
        return lambda b, j: (0, off + j)

    k = conv_w.shape[0]
    return pl.pallas_call(
        _hy_prep_kernel,
        grid=(bsz, nb),
        in_specs=[pl.BlockSpec((1, length, LANES), col(0)),
                  pl.BlockSpec((1, length, LANES), col(nb)),
                  pl.BlockSpec((1, length, LANES), col(2 * nb)),
                  pl.BlockSpec((k, LANES), wcol(0)),
                  pl.BlockSpec((k, LANES), wcol(nb)),
                  pl.BlockSpec((k, LANES), wcol(2 * nb)),
                  pl.BlockSpec((1, LANES), wcol(0)),
                  pl.BlockSpec((1, LANES), wcol(nb)),
                  pl.BlockSpec((1, LANES), wcol(2 * nb))],
        out_specs=[pl.BlockSpec((1, length, LANES), col(0)),
                   pl.BlockSpec((1, length, LANES), col(0))],
        out_shape=[jax.ShapeDtypeStruct((bsz, length, width), F32)] * 2,
        compiler_params=_params("parallel", "parallel"),
        name="hyena_prep",
    )(p_hy, p_hy, p_hy, conv_w, conv_w, conv_w, conv_b, conv_b, conv_b)


def _dft_fwd_kernel(fc_ref, fs_ref, v_ref, kc_ref, ks_ref, yc_ref, ys_ref, vb_s):
    j = pl.program_id(1)

    @pl.when(j == 0)
    def _():
        vb_s[...] = v_ref[0].astype(BF16)

    uc = _dot(fc_ref[...], vb_s[...])
    us = _dot(fs_ref[...], vb_s[...])
    kc = kc_ref[...]
    ks = ks_ref[...]
    rows = lax.broadcasted_iota(jnp.int32, uc.shape, 0)
    special = (rows == 0) & (j == 0)
    yc_ref[0] = (uc * kc - jnp.where(special, 0.0, us * ks)).astype(BF16)
    ys_ref[0] = jnp.where(special, us * ks, uc * ks + us * kc).astype(BF16)


def _dft_fwd(fwd, vv, kc, ks, tf):
    bsz, length, width = vv.shape
    nt = length // tf
    return pl.pallas_call(
        _dft_fwd_kernel,
        grid=(bsz, nt),
        in_specs=[pl.BlockSpec((tf, length), lambda b, j: (j, 0)),
                  pl.BlockSpec((tf, length), lambda b, j: (nt + j, 0)),
                  pl.BlockSpec((1, length, width), lambda b, j: (b, 0, 0)),
                  pl.BlockSpec((tf, width), lambda b, j: (j, 0)),
                  pl.BlockSpec((tf, width), lambda b, j: (j, 0))],
        out_specs=[pl.BlockSpec((1, tf, width), lambda b, j: (b, j, 0)),
                   pl.BlockSpec((1, tf, width), lambda b, j: (b, j, 0))],
        out_shape=[jax.ShapeDtypeStruct((bsz, length, width), BF16)] * 2,
        scratch_shapes=[pltpu.VMEM((length, width), BF16)],
        compiler_params=_params("parallel", "arbitrary"),
        name="hyena_dft_fwd",
    )(fwd, fwd, vv, kc, ks)


def _dft_inv_kernel(ic_ref, is_ref, yc_ref, ys_ref, vv_ref, x0_ref, skip_ref, o_ref):
    y = _dot(ic_ref[...], yc_ref[0]) + _dot(is_ref[...], ys_ref[0])
    o_ref[0] = (x0_ref[0] * (y + vv_ref[0] * skip_ref[...])).astype(o_ref.dtype)


def _dft_inv(inv, yc, ys, vv, x0, skip, tt):
    bsz, length, width = vv.shape
    nt = length // tt
    return pl.pallas_call(
        _dft_inv_kernel,
        grid=(bsz, nt),
        in_specs=[pl.BlockSpec((tt, length), lambda b, i: (i, 0)),
                  pl.BlockSpec((tt, length), lambda b, i: (i, 1)),
                  pl.BlockSpec((1, length, width), lambda b, i: (b, 0, 0)),
                  pl.BlockSpec((1, length, width), lambda b, i: (b, 0, 0)),
                  pl.BlockSpec((1, tt, width), lambda b, i: (b, i, 0)),
                  pl.BlockSpec((1, tt, width), lambda b, i: (b, i, 0)),
                  pl.BlockSpec((1, width), lambda b, i: (0, 0))],
        out_specs=pl.BlockSpec((1, tt, width), lambda b, i: (b, i, 0)),
        out_shape=jax.ShapeDtypeStruct((bsz, length, width), BF16),
        compiler_params=_params("parallel", "parallel"),
        name="hyena_dft_inv",
    )(inv, inv, yc, ys, vv, x0, skip)


@functools.lru_cache(maxsize=None)
def _hyena_tables(length):
    n2 = 2 * length
    t = np.linspace(0.0, 1.0, length)[:, None]
    bands = (HY_EMB - 1) // 2
    wpos = 2.0 * np.pi * np.arange(length)[:, None] / length
    fb = np.linspace(1e-4, bands - 1, bands)[None]
    z = np.concatenate([t, np.cos(fb * wpos), -np.sin(fb * wpos)], axis=-1)
    zpad = np.zeros((length, LANES))
    zpad[:, :HY_EMB] = z
    f = np.arange(length)[:, None]
    n = np.arange(length)[None, :]
    ang = 2.0 * np.pi * ((f * n) % n2) / n2
    cos_m = np.cos(ang)
    sin_m = np.sin(ang)
    sin_m[0, :] = np.cos(np.pi * np.arange(length))
    fwd = np.concatenate([cos_m, sin_m], axis=0)
    scale = np.full((1, n2), 2.0 / n2)
    scale[0, 0] = 1.0 / n2
    scale[0, length] = 1.0 / n2
    inv = fwd.T * scale
    return (zpad.astype(np.float32), t.astype(np.float32), fwd.astype(np.float32),
            inv.astype(np.float32))


def _hyena(p_hy, conv_w, conv_b, filt, skip):
    bsz, length, three_w = p_hy.shape
    width = three_w // 3
    w_in, b_in, w_mid, b_mid, w_out, freq = filt
    zpad, t, fwd, inv = _hyena_tables(length)
    deltas = np.abs(np.linspace(HY_MIN_DECAY, HY_MAX_DECAY, width))[None, :]
    dec = np.exp(-t.astype(np.float64) * deltas).astype(np.float32)
    dec2 = jnp.asarray(np.concatenate([dec, dec], axis=1))
    ffn = w_in.shape[1]
    w_in_pad = jnp.zeros((LANES, ffn), F32).at[:HY_EMB].set(w_in)
    hfilt = _hy_filter(jnp.asarray(zpad), w_in_pad, b_in.reshape(1, ffn), w_mid,
                       b_mid.reshape(-1, 1, ffn), w_out, freq.reshape(1, ffn), dec2)
    fwd_b = jnp.asarray(fwd).astype(BF16)
    inv_b = jnp.asarray(inv).astype(BF16)
    tf = min(512, length)
    kc, ks = _filt_spec(fwd_b, hfilt, tf)
    x0, vv = _hy_prep(p_hy, conv_w, conv_b.reshape(1, three_w))
    yc, ys = _dft_fwd(fwd_b, vv, kc, ks, tf)
    return _dft_inv(inv_b, yc, ys, vv, x0, skip.reshape(1, width), tf)


def _gelu_tanh(x):
    return 0.5 * x * (1.0 + jnp.tanh(math.sqrt(2.0 / math.pi) * (x + 0.044715 * x * x * x)))


def _lru_kernel(xl_ref, yl_ref, xc_ref, cw_ref, cb_ref, wa_ref, ba_ref, wx_ref, bx_ref, ap_ref,
                o_ref, xs_s, a_s, b_s, *, ctx_len, lat_len):
    tot = ctx_len + lat_len
    ngrp = tot // 8
    ngrp_ctx = ctx_len // 8
    xs_s[0:ctx_len, :] = _depthwise_conv(xc_ref[0], cw_ref[...], LRU_CONV // 2) + cb_ref[...]
    xs_s[ctx_len:, :] = _depthwise_conv(xl_ref[0], cw_ref[...], LRU_CONV // 2) + cb_ref[...]
    xs = xs_s[...]
    xsb = xs.astype(BF16)
    rows8 = lax.broadcasted_iota(jnp.int32, (ngrp, 8, xs.shape[1]), 1)
    for d in range(2):
        r = _sigmoid(_dot(xsb, wa_ref[d, 0].astype(BF16)) + ba_ref[d])
        gi = _sigmoid(_dot(xsb, wx_ref[d, 0].astype(BF16)) + bx_ref[d])
        log_a = -LRU_C * r * _softplus(ap_ref[d])
        a = jnp.exp(log_a)
        b = jnp.sqrt(1.0 - a * a) * (gi * xs)
        a = a.reshape(ngrp, 8, a.shape[1])
        b = b.reshape(ngrp, 8, b.shape[1])
        for s in (1, 2, 4):
            keep = (rows8 >= s) if d == 0 else (rows8 < 8 - s)
            shift = s if d == 0 else 8 - s
            sa = jnp.where(keep, pltpu.roll(a, shift, axis=1), 1.0)
            sb = jnp.where(keep, pltpu.roll(b, shift, axis=1), 0.0)
            b = a * sb + b
            a = a * sa
        a_s[d] = a.reshape(tot, a.shape[2])
        b_s[d] = b.reshape(tot, b.shape[2])

    def group(i, carry):
        cf, cb = carry
        rf = pl.multiple_of(i * 8, 8)
        gidx = jnp.where(i < ngrp_ctx, ngrp_ctx - 1 - i, ngrp + ngrp_ctx - 1 - i)
        rb = pl.multiple_of(gidx * 8, 8)
        hf = a_s[0, pl.ds(rf, 8), :] * cf + b_s[0, pl.ds(rf, 8), :]
        hb = a_s[1, pl.ds(rb, 8), :] * cb + b_s[1, pl.ds(rb, 8), :]
        b_s[0, pl.ds(rf, 8), :] = hf
        b_s[1, pl.ds(rb, 8), :] = hb
        return (jnp.broadcast_to(hf[7:8, :], hf.shape), jnp.broadcast_to(hb[0:1, :], hb.shape))

    zero = jnp.zeros((8, xs.shape[1]), F32)
    lax.fori_loop(0, ngrp, group, (zero, zero), unroll=4)
    h = b_s[0, ctx_len:, :] + b_s[1, ctx_len:, :]
    o_ref[0] = (h * _gelu_tanh(yl_ref[0])).astype(o_ref.dtype)


def _rglru(xb_l, yb_l, xb_c, conv_w, conv_b, wa, ba, wx, bx, a_param):
    bsz, lat_len, width = xb_l.shape
    ctx_len = xb_c.shape[1]
    tot = ctx_len + lat_len
    blk = width // LRU_HEADS

    def col(b, h):
        return (b, 0, h)

    def wcol(b, h):
        return (0, h)

    def w3(b, h):
        return (0, 0, h)

    return pl.pallas_call(
        functools.partial(_lru_kernel, ctx_len=ctx_len, lat_len=lat_len),
        grid=(bsz, LRU_HEADS),
        in_specs=[pl.BlockSpec((1, lat_len, blk), col),
                  pl.BlockSpec((1, lat_len, blk), col),
                  pl.BlockSpec((1, ctx_len, blk), col),
                  pl.BlockSpec((LRU_CONV, blk), wcol),
                  pl.BlockSpec((1, blk), wcol),
                  pl.BlockSpec((2, 1, blk, blk), lambda b, h: (0, h, 0, 0)),
                  pl.BlockSpec((2, 1, blk), w3),
                  pl.BlockSpec((2, 1, blk, blk), lambda b, h: (0, h, 0, 0)),
                  pl.BlockSpec((2, 1, blk), w3),
                  pl.BlockSpec((2, 1, blk), w3)],
        out_specs=pl.BlockSpec((1, lat_len, blk), col),
        out_shape=jax.ShapeDtypeStruct((bsz, lat_len, width), BF16),
        scratch_shapes=[pltpu.VMEM((tot, blk), F32),
                        pltpu.VMEM((2, tot, blk), F32),
                        pltpu.VMEM((2, tot, blk), F32)],
        compiler_params=_params("parallel", "parallel"),
        name="rglru",
    )(xb_l, yb_l, xb_c, conv_w, conv_b.reshape(1, width), wa, ba.reshape(2, 1, width), wx,
      bx.reshape(2, 1, width), a_param.reshape(2, 1, width))


def _post_mixer_kernel(*refs, n_in, has_bias):
    a_refs = refs[:n_in]
    w_refs = refs[n_in:2 * n_in]
    i = 2 * n_in
    b_ref = None
    if has_bias:
        b_ref = refs[i]
        i += 1
    x_ref, g1_ref, sc_ref, sh_ref, lng_ref, lnb_ref, rw_ref, rb_ref, base_ref = refs[i:i + 9]
    x1_ref, v_ref, route_ref, cnt_ref = refs[i + 9:]

    @pl.when((pl.program_id(0) == 0) & (pl.program_id(1) == 0))
    def _():
        cnt_ref[...] = base_ref[...]

    y = None
    for a_ref, w_ref in zip(a_refs, w_refs):
        t = _dot(a_ref[0], w_ref[...])
        y = t if y is None else y + t
    if has_bias:
        y = y + b_ref[...]
    x1 = _layer_norm(DEEPNORM_ALPHA * x_ref[0] + g1_ref[0] * y, lng_ref[...], lnb_ref[...])
    x1_ref[0] = x1
    v = x1 * (1.0 + sc_ref[0]) + sh_ref[0]
    v_ref[0] = _pack_bf16_pairs(v)
    logits = _dot3(v, rw_ref[...]) + rb_ref[...]
    tm = logits.shape[0]
    lane = lax.broadcasted_iota(jnp.int32, logits.shape, 1).astype(F32)
    work = logits
    picks, firsts = [], []
    m0 = None
    for kk in range(TOP_K):
        m = jnp.max(work, axis=-1, keepdims=True)
        if kk == 0:
            m0 = m
        first = jnp.min(jnp.where(work == m, lane, float(LANES)), axis=-1, keepdims=True)
        pick = lane == first
        picks.append(pick)
        firsts.append(first)
        work = jnp.where(pick, -jnp.inf, work)
    sel = jnp.where(picks[0] | picks[1] | picks[2] | picks[3], 1.0, 0.0)
    e = sel * jnp.exp(logits - m0)
    gate = e / jnp.sum(e, axis=-1, keepdims=True)
    ti = lax.broadcasted_iota(jnp.int32, (tm, tm), 0)
    tj = lax.broadcasted_iota(jnp.int32, (tm, tm), 1)
    before = jnp.where(ti > tj, 1.0, 0.0).astype(BF16)
    slot = _dot(before, sel.astype(BF16)) + cnt_ref[...]
    route = jnp.zeros(logits.shape, F32)
    for kk in range(TOP_K):
        rank = jnp.sum(jnp.where(picks[kk], slot, 0.0), axis=-1, keepdims=True)
        wgt = jnp.sum(jnp.where(picks[kk], gate, 0.0), axis=-1, keepdims=True)
        route = jnp.where(lane == float(kk), firsts[kk], route)
        route = jnp.where(lane == float(TOP_K + kk), rank, route)
        route = jnp.where(lane == float(2 * TOP_K + kk), wgt, route)
    route_ref[0] = route
    cnt_ref[...] += jnp.sum(sel, axis=0, keepdims=True)


def _post_mixer(acts, ws, bias, x, g1, sc2, sh2, ln_g, ln_b, router_w, router_b, base, tm):
    bsz, length, d = x.shape
    tm = min(tm, length)
    n_in = len(acts)

    def row(bi, i):
        return (bi, i, 0)

    def per_b(bi, i):
        return (bi, 0, 0)

    def const(bi, i):
        return (0, 0)

    in_specs = [pl.BlockSpec((1, tm, a.shape[2]), row) for a in acts]
    in_specs += [pl.BlockSpec(w.shape, const) for w in ws]
    args = list(acts) + list(ws)
    if bias is not None:
        in_specs.append(pl.BlockSpec((1, d), const))
        args.append(bias.reshape(1, d))
    in_specs += [pl.BlockSpec((1, tm, d), row),
                 pl.BlockSpec((1, 1, d), per_b), pl.BlockSpec((1, 1, d), per_b),
                 pl.BlockSpec((1, 1, d), per_b),
                 pl.BlockSpec((1, d), const), pl.BlockSpec((1, d), const),
                 pl.BlockSpec((d, LANES), const), pl.BlockSpec((1, LANES), const),
                 pl.BlockSpec((1, LANES), const)]
    rw = jnp.zeros((d, LANES), F32).at[:, :N_EXPERTS].set(router_w)
    rb = jnp.full((1, LANES), -jnp.inf, F32).at[0, :N_EXPERTS].set(router_b)
    args += [x, g1, sc2, sh2, ln_g.reshape(1, d), ln_b.reshape(1, d), rw, rb, base]
    return pl.pallas_call(
        functools.partial(_post_mixer_kernel, n_in=n_in, has_bias=bias is not None),
        grid=(bsz, length // tm),
        in_specs=in_specs,
        out_specs=[pl.BlockSpec((1, tm, d), row), pl.BlockSpec((1, tm, d // 2), row),
                   pl.BlockSpec((1, tm, LANES), row), pl.BlockSpec((1, LANES), const)],
        out_shape=[jax.ShapeDtypeStruct((bsz, length, d), F32),
                   jax.ShapeDtypeStruct((bsz, length, d // 2), jnp.uint32),
                   jax.ShapeDtypeStruct((bsz, length, LANES), F32),
                   jax.ShapeDtypeStruct((1, LANES), F32)],
        compiler_params=_params("arbitrary", "arbitrary"),
        name="post_mixer",
    )(*args)


MOE_W1_CHUNK = 512


def _moe_rows_kernel(te_ref, nv_ref, first_ref, nxt_ref, xs_ref, w1_hbm, b1g_ref, b1l_ref, w2_hbm,
                     b2_ref, ys_ref, w1f_s, w2f_s, w1g_s, w1l_s, w2b_s, sem, *, layer):
    i = pl.program_id(0)
    nv = nv_ref[i]

    def fetch(e):
        return (pltpu.make_async_copy(w1_hbm.at[layer, e], w1f_s, sem.at[0]),
                pltpu.make_async_copy(w2_hbm.at[layer, e], w2f_s, sem.at[1]))

    @pl.when(i == 0)
    def _():
        for cp in fetch(te_ref[0]):
            cp.start()

    @pl.when(first_ref[i] == 1)
    def _():
        for cp in fetch(te_ref[i]):
            cp.wait()
        half = MOE_W1_CHUNK // 2
        for c in range(w1f_s.shape[1] // MOE_W1_CHUNK):
            t = w1f_s[:, c * MOE_W1_CHUNK:(c + 1) * MOE_W1_CHUNK].astype(BF16).T
            words = pltpu.bitcast(t, jnp.uint32)
            w1g_s[c * half:(c + 1) * half, :] = pltpu.bitcast(words << 16, F32).astype(BF16)
            w1l_s[c * half:(c + 1) * half, :] = pltpu.bitcast(
                words & jnp.uint32(0xFFFF0000), F32).astype(BF16)
        w2b_s[...] = w2f_s[...].astype(BF16)

        @pl.when(nxt_ref[i] >= 0)
        def _():
            for cp in fetch(nxt_ref[i]):
                cp.start()

    @pl.when(nv > 0)
    def _():
        rows = lax.broadcasted_iota(jnp.int32, xs_ref.shape, 0)
        x = _unpack_bf16_pairs(jnp.where(rows < nv, xs_ref[...], jnp.uint32(0))).astype(BF16)
        glu = jnp.minimum(_dot_nt(x, w1g_s[...]) + b1g_ref[0, 0], SWIGLU_LIMIT)
        lin = jnp.clip(_dot_nt(x, w1l_s[...]) + b1l_ref[0, 0], -SWIGLU_LIMIT, SWIGLU_LIMIT)
        act = glu * _sigmoid(SWIGLU_ALPHA * glu) * (lin + 1.0)
        ys_ref[...] = _pack_bf16_pairs(_dot(act.astype(BF16), w2b_s[...]) + b2_ref[0, 0])

    @pl.when(nv == 0)
    def _():
        ys_ref[...] = jnp.zeros_like(ys_ref)


def _moe_rows(xs, tile_expert, tile_rows, tile_first, tile_next, layer, w1, b1g, b1l, w2, b2, tm):
    n_rows, dh = xs.shape
    _, _, d, dff2 = w1.shape
    dff = dff2 // 2

    def row(i, te, nv, first, nxt):
        return (i, 0)

    def exp4(i, te, nv, first, nxt):
        return (layer, te[i], 0, 0)

    hbm = pl.BlockSpec(memory_space=pl.ANY)
    return pl.pallas_call(
        functools.partial(_moe_rows_kernel, layer=layer),
        grid_spec=pltpu.PrefetchScalarGridSpec(
            num_scalar_prefetch=4,
            grid=(n_rows // tm,),
            in_specs=[pl.BlockSpec((tm, dh), row), hbm,
                      pl.BlockSpec((1, 1, 1, dff), exp4), pl.BlockSpec((1, 1, 1, dff), exp4),
                      hbm, pl.BlockSpec((1, 1, 1, d), exp4)],
            out_specs=pl.BlockSpec((tm, dh), row),
            scratch_shapes=[pltpu.VMEM((d, dff2), F32), pltpu.VMEM((dff, d), F32),
                            pltpu.VMEM((dff, d), BF16),
                            pltpu.VMEM((dff, d), BF16), pltpu.VMEM((dff, d), BF16),
                            pltpu.SemaphoreType.DMA((2,))]),
        out_shape=jax.ShapeDtypeStruct((n_rows, dh), jnp.uint32),
        compiler_params=_params("arbitrary"),
        name="moe_rows",
    )(tile_expert, tile_rows, tile_first, tile_next, xs, w1, b1g, b1l, w2, b2)


def _moe_combine_kernel(y0_ref, y1_ref, y2_ref, y3_ref, route_ref, x1_ref, g2_ref, lng_ref, lnb_ref,
                        o_ref):
    route = route_ref[...]
    lane = lax.broadcasted_iota(jnp.int32, route.shape, 1)
    f = None
    for kk, y_ref in enumerate((y0_ref, y1_ref, y2_ref, y3_ref)):
        wgt = jnp.sum(jnp.where(lane == 2 * TOP_K + kk, route, 0.0), axis=-1, keepdims=True)
        term = wgt * _unpack_bf16_pairs(y_ref[0])
        f = term if f is None else f + term
    o_ref[0] = _layer_norm(DEEPNORM_ALPHA * x1_ref[0] + g2_ref[0] * f, lng_ref[...], lnb_ref[...])


def _moe_combine(yg, route, row_offset, x1, g2, ln_g, ln_b, tm):
    bsz, length, d = x1.shape
    tm = min(tm, length)
    nt = length // tm
    off = row_offset // tm

    def pick(kk):
        return lambda bi, i: (kk, off + bi * nt + i, 0)

    def const(bi, i):
        return (0, 0)

    return pl.pallas_call(
        _moe_combine_kernel,
        grid=(bsz, nt),
        in_specs=[pl.BlockSpec((1, tm, d // 2), pick(kk)) for kk in range(TOP_K)] + [
            pl.BlockSpec((tm, LANES), lambda bi, i: (off + bi * nt + i, 0)),
            pl.BlockSpec((1, tm, d), lambda bi, i: (bi, i, 0)),
            pl.BlockSpec((1, 1, d), lambda bi, i: (bi, 0, 0)),
            pl.BlockSpec((1, d), const), pl.BlockSpec((1, d), const)],
        out_specs=pl.BlockSpec((1, tm, d), lambda bi, i: (bi, i, 0)),
        out_shape=jax.ShapeDtypeStruct((bsz, length, d), F32),
        compiler_params=_params("parallel", "parallel"),
        name="moe_combine",
    )(yg, yg, yg, yg, route, x1, g2, ln_g.reshape(1, d), ln_b.reshape(1, d))


SC_CORES = 2
SC_SUBCORES = 16
SC_WORKERS = SC_CORES * SC_SUBCORES
SC_WINDOW = 64
SC_TOKEN_WINDOW = 32


def _sc_row_pipeline(nwin, read, write):
    def start(copies):
        for cp in copies:
            cp.start()

    def wait(copies):
        for cp in copies:
            cp.wait()

    start(read(0, 0))

    @pl.loop(0, nwin, step=2)
    def _(w0):
        for b in range(2):
            w = w0 + b

            @pl.when(w + 1 < nwin)
            def _():
                @pl.when(w >= 1)
                def _():
                    wait(write(w - 1, 1 - b))

                start(read(w + 1, 1 - b))

            wait(read(w, b))
            start(write(w, b))

    wait(write(nwin - 2, 0))
    wait(write(nwin - 1, 1))


def _sc_scatter_rows(src, pos, n_out):
    t_rows, d = src.shape
    nw, nwin, picks, win = pos.shape
    assert nw == SC_WORKERS and nwin % 2 == 0 and t_rows == nw * nwin * win
    mesh = plsc.VectorSubcoreMesh(core_axis_name="c", subcore_axis_name="s")

    @functools.partial(
        pl.kernel, mesh=mesh, out_type=jax.ShapeDtypeStruct((n_out, d), src.dtype),
        scratch_types=[pltpu.VMEM((nwin, picks, win), jnp.int32),
                       pltpu.VMEM((2, win, d), src.dtype),
                       pltpu.SemaphoreType.DMA((2,)), pltpu.SemaphoreType.DMA((2,))])
    def scatter(src_hbm, pos_hbm, out_hbm, idx_v, rows_v, rsem, wsem):
        wid = lax.axis_index("s") * SC_CORES + lax.axis_index("c")
        t0 = wid * (nwin * win)
        pltpu.sync_copy(pos_hbm.at[wid], idx_v)

        def read(w, slot):
            return [pltpu.make_async_copy(src_hbm.at[pl.ds(t0 + w * win, win)], rows_v.at[slot],
                                          rsem.at[slot])]

        def write(w, slot):
            return [pltpu.make_async_copy(rows_v.at[slot], out_hbm.at[idx_v.at[w, k]],
                                          wsem.at[slot]) for k in range(picks)]

        _sc_row_pipeline(nwin, read, write)

    return scatter(src, pos)


def _sc_gather_rows(table, pos):
    _, d = table.shape
    nw, nwin, win = pos.shape
    assert nw == SC_WORKERS and nwin % 2 == 0
    per = nwin * win
    mesh = plsc.VectorSubcoreMesh(core_axis_name="c", subcore_axis_name="s")

    @functools.partial(
        pl.kernel, mesh=mesh, out_type=jax.ShapeDtypeStruct((nw * per, d), table.dtype),
        scratch_types=[pltpu.VMEM((nwin, win), jnp.int32), pltpu.VMEM((2, win, d), table.dtype),
                       pltpu.SemaphoreType.DMA((2,)), pltpu.SemaphoreType.DMA((2,))])
    def gather(table_hbm, pos_hbm, out_hbm, idx_v, rows_v, rsem, wsem):
        wid = lax.axis_index("s") * SC_CORES + lax.axis_index("c")
        base = wid * per
        pltpu.sync_copy(pos_hbm.at[wid], idx_v)

        def read(w, slot):
            return [pltpu.make_async_copy(table_hbm.at[idx_v.at[w]], rows_v.at[slot],
                                          rsem.at[slot])]

        def write(w, slot):
            return [pltpu.make_async_copy(rows_v.at[slot], out_hbm.at[pl.ds(base + w * win, win)],
                                          wsem.at[slot])]

        _sc_row_pipeline(nwin, read, write)

    return gather(table, pos)


MOE_TILE = 256


def _moe_sparse(v_all, route, counts, layer, w1, b1g, b1l, w2, b2):
    t_rows, dh = v_all.shape
    n_exp = w1.shape[1]
    pairs = TOP_K * t_rows
    n_tiles = pairs // MOE_TILE + n_exp
    route_t = route.T
    expert = route_t[0:TOP_K].astype(jnp.int32)
    slot = route_t[TOP_K:2 * TOP_K].astype(jnp.int32)
    cnt = counts[0, :n_exp].astype(jnp.int32)
    tiles_per = (cnt + MOE_TILE - 1) // MOE_TILE
    tile_end = jnp.cumsum(tiles_per)
    tile_start = tile_end - tiles_per
    pos = slot
    for e in range(n_exp):
        pos = pos + jnp.where(expert == e, tile_start[e] * MOE_TILE, 0)
    nwin = pairs // (SC_WORKERS * SC_WINDOW)
    pos_km = pos.reshape(SC_WORKERS, nwin, SC_WINDOW)
    nwin_t = t_rows // (SC_WORKERS * SC_TOKEN_WINDOW)
    pos_tm = pos.reshape(TOP_K, SC_WORKERS, nwin_t, SC_TOKEN_WINDOW).transpose(1, 2, 0, 3)
    tile_ids = jnp.arange(n_tiles, dtype=jnp.int32)[:, None]
    owns = (tile_ids >= tile_start[None, :]) & (tile_ids < tile_end[None, :])
    experts = jnp.arange(n_exp, dtype=jnp.int32)[None, :]
    last_used = jnp.max(jnp.where(tiles_per > 0, experts[0], 0))
    used = jnp.any(owns, axis=1)
    te = jnp.where(used, jnp.sum(jnp.where(owns, experts, 0), axis=1), last_used).astype(jnp.int32)
    rows_left = cnt[None, :] - (tile_ids - tile_start[None, :]) * MOE_TILE
    tile_rows = jnp.sum(jnp.where(owns, jnp.clip(rows_left, 0, MOE_TILE), 0), axis=1)
    tile_rows = tile_rows.astype(jnp.int32)
    tile_first = jnp.any(owns & (tile_ids == tile_start[None, :]), axis=1).astype(jnp.int32)
    later = (experts > experts.T) & (tiles_per[None, :] > 0)
    next_e = jnp.min(jnp.where(later, experts, n_exp), axis=1)
    next_e = jnp.where(next_e == n_exp, -1, next_e)
    tile_next = jnp.where(used, jnp.sum(jnp.where(owns, next_e[None, :], 0), axis=1), -1)
    tile_next = tile_next.astype(jnp.int32)
    xs = _sc_scatter_rows(v_all, pos_tm, n_tiles * MOE_TILE)
    ys = _moe_rows(xs, te, tile_rows, tile_first, tile_next, layer, w1, b1g, b1l, w2, b2, MOE_TILE)
    return _sc_gather_rows(ys, pos_km).reshape(TOP_K, t_rows, dh)


@functools.lru_cache(maxsize=None)
def _sincos_2d(rows, cols, dim):
    quarter = dim // 4
    omega = 1.0 / (10000.0 ** (np.arange(quarter, dtype=np.float64) / quarter))

    def emb1d(n):
        ang = np.arange(n, dtype=np.float64)[:, None] * omega
        return np.concatenate([np.sin(ang), np.cos(ang)], axis=-1)

    er = np.broadcast_to(emb1d(rows)[:, None], (rows, cols, dim // 2))
    ec = np.broadcast_to(emb1d(cols)[None], (rows, cols, dim // 2))
    return np.concatenate([er, ec], axis=-1).reshape(rows * cols, dim).astype(np.float32)


def _gate_layout(gates, n_ch):
    bsz, length, _ = gates.shape
    g = gates[..., :n_ch].reshape(bsz, length // DN_CHUNK, DN_CHUNK, n_ch)
    return g.transpose(0, 3, 1, 2)


def kernel(x, c, ctx, c_ctx, ada_w, ada_b, ln_g, ln_b, ev_w_in, ev_w_out, dn_conv_w, dn_a_log, dn_dt_bias, dn_norm_g, hy_conv_w, hy_conv_b, hy_w_in, hy_b_in, hy_w_mid, hy_b_mid, hy_w_out, hy_freq, hy_skip, od_w_in, od_b_in, lru_conv_w, lru_conv_b, lru_wa, lru_ba, lru_wx, lru_bx, lru_a_param, od_w_out, od_b_out, router_w, router_b, moe_w1, moe_b1, moe_w2, moe_b2):
    bsz, length, d = x.shape
    ctx_len = ctx.shape[1]
    pos = jnp.asarray(_sincos_2d(length // GRID_W, GRID_W, d))

    cond = jnp.zeros((16, d), F32).at[:bsz].set(c).at[bsz].set(c_ctx)
    mod = _modulation(cond, ada_w, ada_b).reshape(DEPTH, 16, 6, d)

    def lat_mod(layer, k):
        return mod[layer, :bsz, k][:, None, :]

    def ctx_mod(layer, k):
        return jnp.broadcast_to(mod[layer, bsz, k][None, None, :], (bsz, 1, d))

    hc = ctx
    for layer in range(DEPTH):
        last = layer == DEPTH - 1
        j = layer // 2
        if layer % 2 == 0:
            dn_qk = DN_HEADS * DN_DK
            dn_qkv = 3 * dn_qk
            dn_in = dn_qkv + dn_qk + 4 * DN_HEADS
            w_in = ev_w_in[j]
            gate_w = jnp.zeros((d, LANES), F32).at[:, :4 * DN_HEADS].set(w_in[:, dn_qkv + dn_qk:dn_in])
            w_cat = jnp.concatenate([w_in[:, :dn_qkv + dn_qk], gate_w, w_in[:, dn_in:]],
                                    axis=1).astype(BF16)
            hy_in = w_in.shape[1] - dn_in
            widths = (dn_qkv, dn_qk, LANES, hy_in)
            qkv_l, z_l, gt_l, phy_l = _inproj(x, lat_mod(layer, 1), lat_mod(layer, 0), pos,
                                               w_cat, None, widths, 512)
            qkv_c, z_c, gt_c, phy_c = _inproj(hc, ctx_mod(layer, 1), ctx_mod(layer, 0), None,
                                               w_cat, None, widths, 512)
            gates = jnp.concatenate([_gate_layout(gt_c, 4 * DN_HEADS),
                                     _gate_layout(gt_l, 4 * DN_HEADS)], axis=2)
            n_chunks = gates.shape[2]
            gates = jnp.pad(gates, ((0, 0), (0, 0), (0, -n_chunks % 16), (0, 0)))
            dn_l, dn_c = _deltanet(qkv_l, z_l, qkv_c, z_c, gates, dn_conv_w[j], dn_a_log[j],
                                   dn_dt_bias[j], dn_norm_g[j])
            filt = (hy_w_in[j], hy_b_in[j], hy_w_mid[j], hy_b_mid[j], hy_w_out[j], hy_freq[j])
            hy_l = _hyena(phy_l, hy_conv_w[j], hy_conv_b[j], filt, hy_skip[j])
            w_out = ev_w_out[j].astype(BF16)
            half = dn_l.shape[2]
            acts_l, ws, bias = (dn_l, hy_l), (w_out[:half], w_out[half:]), None
            acts_c = None
            if not last:
                hy_c = _hyena(phy_c, hy_conv_w[j], hy_conv_b[j], filt, hy_skip[j])
                acts_c = (dn_c, hy_c)
        else:
            w_in = od_w_in[j].astype(BF16)
            width = w_in.shape[1] // 2
            b_in = od_b_in[j].reshape(1, 2 * width)
            xb_l, yb_l = _inproj(x, lat_mod(layer, 1), lat_mod(layer, 0), pos, w_in, b_in,
                                 (width, width), 512)
            xb_c, _ = _inproj(hc, ctx_mod(layer, 1), ctx_mod(layer, 0), None, w_in, b_in,
                              (width, width), 512)
            act_l = _rglru(xb_l, yb_l, xb_c, lru_conv_w[j], lru_conv_b[j], lru_wa[j], lru_ba[j],
                           lru_wx[j], lru_bx[j], lru_a_param[j])
            acts_l, ws, bias = (act_l,), (od_w_out[j].astype(BF16),), od_b_out[j]
            acts_c = None
            assert last, "context outputs of the RG-LRU layer are only needed before the last layer"


        x1, v, route, counts = _post_mixer(acts_l, ws, bias, x, lat_mod(layer, 2),
                                           lat_mod(layer, 4), lat_mod(layer, 3), ln_g[layer, 0],
                                           ln_b[layer, 0], router_w[layer], router_b[layer],
                                           jnp.zeros((1, LANES), F32), 512)
        v_all = v.reshape(bsz * length, d // 2)
        route = route.reshape(bsz * length, LANES)
        if not last:
            hc1, vc, route_c, counts = _post_mixer(acts_c, ws, bias, hc, ctx_mod(layer, 2),
                                                   ctx_mod(layer, 4), ctx_mod(layer, 3),
                                                   ln_g[layer, 0], ln_b[layer, 0], router_w[layer],
                                                   router_b[layer], counts, 256)
            v_all = jnp.concatenate([v_all, vc.reshape(bsz * ctx_len, d // 2)], axis=0)
            route = jnp.concatenate([route, route_c.reshape(bsz * ctx_len, LANES)], axis=0)
        yg = _moe_sparse(v_all, route, counts, layer, moe_w1, moe_b1[:, :, None, 0::2],
                         moe_b1[:, :, None, 1::2], moe_w2, moe_b2[:, :, None, :])
        x = _moe_combine(yg, route, 0, x1, lat_mod(layer, 5), ln_g[layer, 1], ln_b[layer, 1], 512)
        if not last:
            hc = _moe_combine(yg, route, bsz * length, hc1, ctx_mod(layer, 5), ln_g[layer, 1],
                              ln_b[layer, 1], 256)
    return x
```

```python
import functools
import math

import numpy as np
import jax
import jax.numpy as jnp
from jax import lax
from jax.experimental import pallas as pl
from jax.experimental.pallas import tpu as pltpu
from jax.experimental.pallas import tpu_sc as plsc

F32 = jnp.float32
BF16 = jnp.bfloat16

VMEM_LIMIT_BYTES = 56 * 1024 * 1024
LANES = 128

DEPTH = 2
GRID_W = 64
DEEPNORM_ALPHA = (2.0 * DEPTH) ** 0.25
LN_EPS = 1e-5
RMS_EPS = 1e-6

DN_HEADS = 4
DN_DK = 128
DN_CHUNK = 64
DN_CONV = 4

HY_EMB = 33
HY_TARGET = 1e-2
HY_MIN_DECAY = math.log(HY_TARGET) / 1.5
HY_MAX_DECAY = math.log(HY_TARGET) / 0.3
HY_CONV = 3

LRU_HEADS = 4
LRU_C = 8.0
LRU_CONV = 4

N_EXPERTS = 32
TOP_K = 4
SWIGLU_ALPHA = 1.702
SWIGLU_LIMIT = 7.0


def _params(*sem):
    return pltpu.CompilerParams(dimension_semantics=sem, vmem_limit_bytes=VMEM_LIMIT_BYTES)


def _dot(a, b):
    return jnp.dot(a, b, preferred_element_type=F32)


def _dot_nt(a, b):
    return lax.dot_general(a, b, (((1,), (1,)), ((), ())), preferred_element_type=F32)


def _split(a):
    hi = a.astype(BF16)
    lo = (a - hi.astype(F32)).astype(BF16)
    return hi, lo


def _dot3(a, b):
    ah, al = _split(a)
    bh, bl = _split(b)
    return _dot(ah, bh) + _dot(ah, bl) + _dot(al, bh)


def _silu(x):
    return x * (1.0 / (1.0 + jnp.exp(-x)))


def _sigmoid(x):
    return 1.0 / (1.0 + jnp.exp(-x))


def _softplus(x):
    return jnp.maximum(x, 0.0) + jnp.log(1.0 + jnp.exp(-jnp.abs(x)))


def _layer_norm(x, g, b):
    mu = jnp.mean(x, axis=-1, keepdims=True)
    xc = x - mu
    var = jnp.mean(xc * xc, axis=-1, keepdims=True)
    return xc * lax.rsqrt(var + LN_EPS) * g + b


def _pack_bf16_pairs(x):
    w = x.shape[1] // 2
    lo = pltpu.bitcast(x[:, :w].astype(BF16).astype(F32), jnp.uint32) >> 16
    hi = pltpu.bitcast(x[:, w:].astype(BF16).astype(F32), jnp.uint32) & jnp.uint32(0xFFFF0000)
    return lo | hi


def _unpack_bf16_pairs(p):
    lo = pltpu.bitcast(p << 16, F32)
    hi = pltpu.bitcast(p & jnp.uint32(0xFFFF0000), F32)
    return jnp.concatenate([lo, hi], axis=1)


def _shift_rows(x, s):
    if s == 0:
        return x
    n = x.shape[0]
    rows = lax.broadcasted_iota(jnp.int32, x.shape, 0)
    valid = (rows >= s) if s > 0 else (rows < n + s)
    return jnp.where(valid, pltpu.roll(x, s % n, axis=0), 0.0)


def _depthwise_conv(x, w, pad_left):
    acc = None
    for i in range(w.shape[0]):
        term = _shift_rows(x, pad_left - i) * w[i:i + 1, :]
        acc = term if acc is None else acc + term
    return acc


def _mod_kernel(c_ref, w_ref, b_ref, o_ref):
    o_ref[0] = _dot3(_silu(c_ref[...]), w_ref[0]) + b_ref[0]


def _modulation(cond, ada_w, ada_b):
    depth, d, n = ada_w.shape
    rows = cond.shape[0]
    tn = 1536
    return pl.pallas_call(
        _mod_kernel,
        grid=(depth, n // tn),
        in_specs=[
            pl.BlockSpec((rows, d), lambda l, j: (0, 0)),
            pl.BlockSpec((1, d, tn), lambda l, j: (l, 0, j)),
            pl.BlockSpec((1, 1, tn), lambda l, j: (l, 0, j)),
        ],
        out_specs=pl.BlockSpec((1, rows, tn), lambda l, j: (l, 0, j)),
        out_shape=jax.ShapeDtypeStruct((depth, rows, n), F32),
        compiler_params=_params("parallel", "parallel"),
        name="modulation",
    )(cond, ada_w, ada_b.reshape(depth, 1, n))


def _inproj_kernel(*refs, splits, has_pos, has_bias):
    x_ref, sc_ref, sh_ref = refs[:3]
    i = 3
    pos_ref = None
    if has_pos:
        pos_ref = refs[i]
        i += 1
    w_ref = refs[i]
    i += 1
    b_ref = None
    if has_bias:
        b_ref = refs[i]
        i += 1
    o_refs = refs[i:]
    u = x_ref[0] * (1.0 + sc_ref[0]) + sh_ref[0]
    if has_pos:
        u = u + pos_ref[...]
    ub = u.astype(BF16)
    for o_ref, (s, e) in zip(o_refs, splits):
        acc = _dot(ub, w_ref[:, s:e])
        if has_bias:
            acc = acc + b_ref[:, s:e]
        o_ref[0] = acc


def _inproj(x, sc, sh, pos, w, b, widths, tm):
    bsz, length, d = x.shape
    n = w.shape[1]
    splits, s = [], 0
    for wd in widths:
        splits.append((s, s + wd))
        s += wd
    assert s == n
    tm = min(tm, length)
    in_specs = [
        pl.BlockSpec((1, tm, d), lambda bi, i: (bi, i, 0)),
        pl.BlockSpec((1, 1, d), lambda bi, i: (bi, 0, 0)),
        pl.BlockSpec((1, 1, d), lambda bi, i: (bi, 0, 0)),
    ]
    args = [x, sc, sh]
    if pos is not None:
        in_specs.append(pl.BlockSpec((tm, d), lambda bi, i: (i, 0)))
        args.append(pos)
    in_specs.append(pl.BlockSpec((d, n), lambda bi, i: (0, 0)))
    args.append(w)
    if b is not None:
        in_specs.append(pl.BlockSpec((1, n), lambda bi, i: (0, 0)))
        args.append(b)
    return pl.pallas_call(
        functools.partial(_inproj_kernel, splits=tuple(splits), has_pos=pos is not None,
                          has_bias=b is not None),
        grid=(bsz, length // tm),
        in_specs=in_specs,
        out_specs=[pl.BlockSpec((1, tm, wd), lambda bi, i: (bi, i, 0)) for wd in widths],
        out_shape=[jax.ShapeDtypeStruct((bsz, length, wd), F32) for wd in widths],
        compiler_params=_params("parallel", "parallel"),
        name="inproj",
    )(*args)


def _unit_tri_inverses(mats, lower):
    n = mats[0].shape[0]
    nb = 16
    np_ = len(mats)
    ii = lax.broadcasted_iota(jnp.int32, (n, n), 0)
    jj = lax.broadcasted_iota(jnp.int32, (n, n), 1)
    same16 = (ii // nb) == (jj // nb)
    same32 = (ii // (2 * nb)) == (jj // (2 * nb))
    dgs = []
    for a in mats:
        ad = jnp.where(same16, a, 0.0)
        dgs.append(ad[0:nb] + ad[nb:2 * nb] + ad[2 * nb:3 * nb] + ad[3 * nb:4 * nb])
    dg = jnp.concatenate(dgs, axis=0)
    rr = lax.broadcasted_iota(jnp.int32, dg.shape, 0)
    ll = lax.broadcasted_iota(jnp.int32, dg.shape, 1)
    xd = jnp.where(rr % nb == ll % nb, 1.0, 0.0)
    blk0 = (ll // nb) * nb
    for s in (range(nb - 1) if lower else range(nb - 1, 0, -1)):
        col = jnp.take_along_axis(dg, blk0 + s, axis=1)
        row = jnp.concatenate(
            [jnp.broadcast_to(xd[p * nb + s:p * nb + s + 1, :], (nb, n)) for p in range(np_)], axis=0)
        xd = xd - col * row
    ds = [jnp.where(same16, jnp.concatenate([xd[p * nb:(p + 1) * nb]] * (n // nb), axis=0), 0.0)
          for p in range(np_)]
    lvl1 = same32 & jnp.logical_not(same16)
    t1 = [_dot3(d, jnp.where(lvl1, a, 0.0)) for d, a in zip(ds, mats)]
    x1 = [d - _dot3(t, d) for d, t in zip(ds, t1)]
    x1b = [x.astype(BF16) for x in x1]
    t2 = [_dot(xb, jnp.where(same32, 0.0, a).astype(BF16)) for xb, a in zip(x1b, mats)]
    return [x - _dot(t.astype(BF16), xb) for x, t, xb in zip(x1, t2, x1b)]


def _dn_kernel(alog_ref, dtb_ref,
               ql_ref, kl_ref, vl_ref, zl_ref, qc_ref, kc_ref, vc_ref, zc_ref, gt_ref,
               cwq_ref, cwk_ref, cwv_ref, ng_ref,
               yl_ref, yc_ref,
               qn_s, kn_s, vn_s, gc_s, bt_s, nq_s, c_s, gl_s, o_s,
               *, ctx_len, lat_len):
    h = pl.program_id(1)
    csz = DN_CHUNK
    nc_ctx = ctx_len // csz
    nc = (ctx_len + lat_len) // csz

    def prep(src_ref, cw_ref, kind):
        t = _silu(_depthwise_conv(src_ref[0], cw_ref[...], DN_CONV // 2))
        if kind == "v":
            return t
        t = t * lax.rsqrt(jnp.sum(t * t, axis=-1, keepdims=True) + RMS_EPS)
        return t * (DN_DK ** -0.5) if kind == "q" else t

    qn_s[0:ctx_len, :] = prep(qc_ref, cwq_ref, "q")
    qn_s[ctx_len:, :] = prep(ql_ref, cwq_ref, "q")
    kn_s[0:ctx_len, :] = prep(kc_ref, cwk_ref, "k")
    kn_s[ctx_len:, :] = prep(kl_ref, cwk_ref, "k")
    vn_s[0:ctx_len, :] = prep(vc_ref, cwv_ref, "v")
    vn_s[ctx_len:, :] = prep(vl_ref, cwv_ref, "v")

    ii = lax.broadcasted_iota(jnp.int32, (csz, csz), 0)
    jj = lax.broadcasted_iota(jnp.int32, (csz, csz), 1)
    eye = ii == jj
    for d in range(2):
        graw = gt_ref[0, d * 2 * DN_HEADS + h]
        braw = gt_ref[0, d * 2 * DN_HEADS + DN_HEADS + h]
        a_neg = -jnp.exp(jnp.zeros_like(graw) + alog_ref[d, h])
        g = a_neg * _softplus(graw + dtb_ref[d, h])
        tri = jnp.where((ii <= jj) if d == 0 else (ii >= jj), 1.0, 0.0).astype(BF16)
        g1 = g.astype(BF16)
        r1 = g - g1.astype(F32)
        g2 = r1.astype(BF16)
        g3 = (r1 - g2.astype(F32)).astype(BF16)
        gc_s[d] = _dot(g1, tri) + _dot(g2, tri) + _dot(g3, tri)
        bt_s[d] = _sigmoid(braw)

    o_s[...] = jnp.zeros_like(o_s)

    group = 36
    assert nc % group == 0

    def chunk_prep(gi, carry):
        ns = [gi * group + c for c in range(group)]
        r0s = [pl.multiple_of(n * csz, csz) for n in ns]
        qs = [qn_s[pl.ds(r0, csz), :] for r0 in r0s]
        ks = [kn_s[pl.ds(r0, csz), :] for r0 in r0s]
        vs = [vn_s[pl.ds(r0, csz), :] for r0 in r0s]
        kbfs = [k.astype(BF16) for k in ks]
        qks = [_dot_nt(q.astype(BF16), kbf) for q, kbf in zip(qs, kbfs)]
        for d in range(2):
            incl = (ii >= jj) if d == 0 else (ii <= jj)
            strict = (ii > jj) if d == 0 else (ii < jj)
            grs = [gc_s[d, pl.ds(n, 1), :] for n in ns]
            grows = [jnp.broadcast_to(gr, (csz, csz)) for gr in grs]
            gcols = [jnp.sum(jnp.where(eye, grow, 0.0), axis=1, keepdims=True) for grow in grows]
            bcols = [jnp.sum(jnp.where(eye, jnp.broadcast_to(bt_s[d, pl.ds(n, 1), :], (csz, csz)),
                                       0.0), axis=1, keepdims=True) for n in ns]
            decays = [jnp.where(incl, jnp.exp(jnp.where(incl, gcol - grow, 0.0)), 0.0)
                      for gcol, grow in zip(gcols, grows)]
            kbs = [k * bcol for k, bcol in zip(ks, bcols)]
            amats = [jnp.where(strict, _dot_nt(kb.astype(BF16), kbf) * decay, 0.0)
                     for kb, kbf, decay in zip(kbs, kbfs, decays)]
            tbs = [t.astype(BF16) for t in _unit_tri_inverses(amats, lower=(d == 0))]
            cs = range(group)
            egs = [jnp.exp(gcols[c]) for c in cs]
            ubs = [_dot(tbs[c], (vs[c] * bcols[c]).astype(BF16)).astype(BF16) for c in cs]
            wbs = [_dot(tbs[c], (kbs[c] * egs[c]).astype(BF16)).astype(BF16) for c in cs]
            attns = [jnp.where(incl, qks[c] * decays[c], 0.0).astype(BF16) for c in cs]
            glasts = [grs[c][:, csz - 1:csz] if d == 0 else grs[c][:, 0:1] for c in cs]
            kdts = [(ks[c] * jnp.exp(glasts[c] - gcols[c])).T.astype(BF16) for c in cs]
            nmats = [_dot(kdts[c], wbs[c]).astype(BF16) for c in cs]
            qmats = [(qs[c] * egs[c] - _dot(attns[c], wbs[c])).astype(BF16) for c in cs]
            cmats = [_dot(kdts[c], ubs[c]) for c in cs]
            omats = [_dot(attns[c], ubs[c]) for c in cs]
            for c in cs:
                n = ns[c]
                nq_s[d, n, 0:DN_DK, :] = nmats[c]
                nq_s[d, n, DN_DK:DN_DK + csz, :] = qmats[c]
                c_s[d, n] = cmats[c]
                o_s[pl.ds(r0s[c], csz), :] += omats[c]
                gl_s[d, pl.ds(n, 1), :] = jnp.broadcast_to(jnp.exp(glasts[c]), (1, LANES))
        return carry

    lax.fori_loop(0, nc // group, chunk_prep, 0)

    def step(i, states):
        new_states = []
        for d in range(2):
            if d == 0:
                n = i
            else:
                n = jnp.where(i < nc_ctx, nc_ctx - 1 - i, nc + nc_ctx - 1 - i)
            r0 = pl.multiple_of(n * csz, csz)
            s = states[d]
            r = _dot(nq_s[d, n], s.astype(BF16))
            o_s[pl.ds(r0, csz), :] += r[DN_DK:DN_DK + csz]
            new_states.append(s * gl_s[d, pl.ds(n, 1), :] - r[0:DN_DK] + c_s[d, n])
        return tuple(new_states)

    zero = jnp.zeros((DN_DK, DN_DK), F32)
    lax.fori_loop(0, nc, step, (zero, zero))

    def gated_norm(o, z):
        o = o * lax.rsqrt(jnp.mean(o * o, axis=-1, keepdims=True) + RMS_EPS) * ng_ref[...]
        return (o * _silu(z)).astype(yl_ref.dtype)

    yc_ref[0] = gated_norm(o_s[0:ctx_len, :], zc_ref[0])
    yl_ref[0] = gated_norm(o_s[ctx_len:, :], zl_ref[0])


def _deltanet(qkv_l, z_l, qkv_c, z_c, gates, conv_w, a_log, dt_bias, norm_g):
    bsz, lat_len, _ = qkv_l.shape
    ctx_len = qkv_c.shape[1]
    tot = ctx_len + lat_len
    nc = tot // DN_CHUNK
    ncp = gates.shape[2]
    hd = DN_DK
    nh = DN_HEADS

    def col(off):
        return lambda b, h: (b, 0, off + h)

    def wcol(off):
        return lambda b, h: (0, off + h)

    smem = pl.BlockSpec(memory_space=pltpu.SMEM)
    in_specs = [
        smem, smem,
        pl.BlockSpec((1, lat_len, hd), col(0)),
        pl.BlockSpec((1, lat_len, hd), col(nh)),
        pl.BlockSpec((1, lat_len, hd), col(2 * nh)),
        pl.BlockSpec((1, lat_len, hd), col(0)),
        pl.BlockSpec((1, ctx_len, hd), col(0)),
        pl.BlockSpec((1, ctx_len, hd), col(nh)),
        pl.BlockSpec((1, ctx_len, hd), col(2 * nh)),
        pl.BlockSpec((1, ctx_len, hd), col(0)),
        pl.BlockSpec((1, 4 * nh, ncp, DN_CHUNK), lambda b, h: (b, 0, 0, 0)),
        pl.BlockSpec((DN_CONV, hd), wcol(0)),
        pl.BlockSpec((DN_CONV, hd), wcol(nh)),
        pl.BlockSpec((DN_CONV, hd), wcol(2 * nh)),
        pl.BlockSpec((1, hd), lambda b, h: (0, 0)),
    ]
    scratch = [
        pltpu.VMEM((tot, hd), F32), pltpu.VMEM((tot, hd), F32), pltpu.VMEM((tot, hd), F32),
        pltpu.VMEM((2, ncp, DN_CHUNK), F32), pltpu.VMEM((2, ncp, DN_CHUNK), F32),
        pltpu.VMEM((2, nc, hd + DN_CHUNK, hd), BF16),
        pltpu.VMEM((2, nc, hd, hd), F32),
        pltpu.VMEM((2, nc, LANES), F32),
        pltpu.VMEM((tot, hd), F32),
    ]
    return pl.pallas_call(
        functools.partial(_dn_kernel, ctx_len=ctx_len, lat_len=lat_len),
        grid=(bsz, nh),
        in_specs=in_specs,
        out_specs=[pl.BlockSpec((1, lat_len, hd), col(0)),
                   pl.BlockSpec((1, ctx_len, hd), col(0))],
        out_shape=[jax.ShapeDtypeStruct((bsz, lat_len, nh * hd), BF16),
                   jax.ShapeDtypeStruct((bsz, ctx_len, nh * hd), BF16)],
        scratch_shapes=scratch,
        compiler_params=_params("parallel", "parallel"),
        name="deltanet",
    )(a_log, dt_bias, qkv_l, qkv_l, qkv_l, z_l, qkv_c, qkv_c, qkv_c, z_c, gates,
      conv_w, conv_w, conv_w, norm_g.reshape(1, hd))


def _hy_filter_kernel(z_ref, win_ref, bin_ref, wmid_ref, bmid_ref, wout_ref, freq_ref, dec_ref,
                      o_ref):
    freq = freq_ref[...]
    hcur = jnp.sin(freq * (_dot3(z_ref[...], win_ref[...]) + bin_ref[...]))
    for i in range(wmid_ref.shape[0]):
        hcur = jnp.sin(freq * (_dot3(hcur, wmid_ref[i]) + bmid_ref[i]))
    o_ref[...] = _dot3(hcur, wout_ref[...]) * dec_ref[...]


def _hy_filter(z, w_in, b_in, w_mid, b_mid, w_out, freq, dec2):
    length = z.shape[0]
    n_out = w_out.shape[1]
    tl = min(256, length)

    def whole(a):
        return pl.BlockSpec(a.shape, lambda i: (0,) * a.ndim)

    return pl.pallas_call(
        _hy_filter_kernel,
        grid=(length // tl,),
        in_specs=[pl.BlockSpec((tl, z.shape[1]), lambda i: (i, 0)),
                  whole(w_in), whole(b_in), whole(w_mid), whole(b_mid), whole(w_out), whole(freq),
                  pl.BlockSpec((tl, n_out), lambda i: (i, 0))],
        out_specs=pl.BlockSpec((tl, n_out), lambda i: (i, 0)),
        out_shape=jax.ShapeDtypeStruct((length, n_out), F32),
        compiler_params=_params("parallel"),
        name="hyena_filter",
    )(z, w_in, b_in, w_mid, b_mid, w_out, freq, dec2)


def _filt_spec_kernel(fc_ref, fs_ref, h_ref, kc_ref, ks_ref, *, width):
    j = pl.program_id(0)
    hmat = h_ref[...]
    rows = lax.broadcasted_iota(jnp.int32, hmat.shape, 0)
    cols = lax.broadcasted_iota(jnp.int32, hmat.shape, 1)
    hmat = jnp.where((rows == 0) & (cols >= width), 0.0, hmat)
    hh, hl = _split(hmat)
    c = _dot(fc_ref[...], hh) + _dot(fc_ref[...], hl)
    s = _dot(fs_ref[...], hh) + _dot(fs_ref[...], hl)
    kc_ref[...] = c[:, :width] + c[:, width:]
    orow = lax.broadcasted_iota(jnp.int32, (c.shape[0], width), 0)
    sign = jnp.where((orow == 0) & (j == 0), 1.0, -1.0)
    ks_ref[...] = s[:, :width] + sign * s[:, width:]


def _filt_spec(fwd, hfilt, tf):
    length, two_w = hfilt.shape
    width = two_w // 2
    nt = length // tf
    return pl.pallas_call(
        functools.partial(_filt_spec_kernel, width=width),
        grid=(nt,),
        in_specs=[pl.BlockSpec((tf, length), lambda j: (j, 0)),
                  pl.BlockSpec((tf, length), lambda j: (nt + j, 0)),
                  pl.BlockSpec((length, two_w), lambda j: (0, 0))],
        out_specs=[pl.BlockSpec((tf, width), lambda j: (j, 0)),
                   pl.BlockSpec((tf, width), lambda j: (j, 0))],
        out_shape=[jax.ShapeDtypeStruct((length, width), F32)] * 2,
        compiler_params=_params("arbitrary"),
        name="hyena_filter_spectrum",
    )(fwd, fwd, hfilt)


def _hy_prep_kernel(x0_ref, x1_ref, v_ref, w0_ref, w1_ref, w2_ref, b0_ref, b1_ref, b2_ref,
                    x0o_ref, vvo_ref):
    x0 = _depthwise_conv(x0_ref[0], w0_ref[...], HY_CONV // 2) + b0_ref[...]
    x1 = _depthwise_conv(x1_ref[0], w1_ref[...], HY_CONV // 2) + b1_ref[...]
    v = _depthwise_conv(v_ref[0], w2_ref[...], HY_CONV // 2) + b2_ref[...]
    x0o_ref[0] = x0
    vvo_ref[0] = v * x1


def _hy_prep(p_hy, conv_w, conv_b):
    bsz, length, three_w = p_hy.shape
    width = three_w // 3
    nb = width // LANES

    def col(off):
        return lambda b, j: (b, 0, off + j)

    def wcol(off):
        return lambda b, j: (0, off + j)

    k = conv_w.shape[0]
    return pl.pallas_call(
        _hy_prep_kernel,
        grid=(bsz, nb),
        in_specs=[pl.BlockSpec((1, length, LANES), col(0)),
                  pl.BlockSpec((1, length, LANES), col(nb)),
                  pl.BlockSpec((1, length, LANES), col(2 * nb)),
                  pl.BlockSpec((k, LANES), wcol(0)),
                  pl.BlockSpec((k, LANES), wcol(nb)),
                  pl.BlockSpec((k, LANES), wcol(2 * nb)),
                  pl.BlockSpec((1, LANES), wcol(0)),
                  pl.BlockSpec((1, LANES), wcol(nb)),
                  pl.BlockSpec((1, LANES), wcol(2 * nb))],
        out_specs=[pl.BlockSpec((1, length, LANES), col(0)),
                   pl.BlockSpec((1, length, LANES), col(0))],
        out_shape=[jax.ShapeDtypeStruct((bsz, length, width), F32)] * 2,
        compiler_params=_params("parallel", "parallel"),
        name="hyena_prep",
    )(p_hy, p_hy, p_hy, conv_w, conv_w, conv_w, conv_b, conv_b, conv_b)


def _dft_fwd_kernel(fc_ref, fs_ref, v_ref, kc_ref, ks_ref, yc_ref, ys_ref, vb_s):
    j = pl.program_id(1)

    @pl.when(j == 0)
    def _():
        vb_s[...] = v_ref[0].astype(BF16)

    uc = _dot(fc_ref[...], vb_s[...])
    us = _dot(fs_ref[...], vb_s[...])
    kc = kc_ref[...]
    ks = ks_ref[...]
    rows = lax.broadcasted_iota(jnp.int32, uc.shape, 0)
    special = (rows == 0) & (j == 0)
    yc_ref[0] = (uc * kc - jnp.where(special, 0.0, us * ks)).astype(BF16)
    ys_ref[0] = jnp.where(special, us * ks, uc * ks + us * kc).astype(BF16)


def _dft_fwd(fwd, vv, kc, ks, tf):
    bsz, length, width = vv.shape
    nt = length // tf
    return pl.pallas_call(
        _dft_fwd_kernel,
        grid=(bsz, nt),
        in_specs=[pl.BlockSpec((tf, length), lambda b, j: (j, 0)),
                  pl.BlockSpec((tf, length), lambda b, j: (nt + j, 0)),
                  pl.BlockSpec((1, length, width), lambda b, j: (b, 0, 0)),
                  pl.BlockSpec((tf, width), lambda b, j: (j, 0)),
                  pl.BlockSpec((tf, width), lambda b, j: (j, 0))],
        out_specs=[pl.BlockSpec((1, tf, width), lambda b, j: (b, j, 0)),
                   pl.BlockSpec((1, tf, width), lambda b, j: (b, j, 0))],
        out_shape=[jax.ShapeDtypeStruct((bsz, length, width), BF16)] * 2,
        scratch_shapes=[pltpu.VMEM((length, width), BF16)],
        compiler_params=_params("parallel", "arbitrary"),
        name="hyena_dft_fwd",
    )(fwd, fwd, vv, kc, ks)


def _dft_inv_kernel(ic_ref, is_ref, yc_ref, ys_ref, vv_ref, x0_ref, skip_ref, o_ref):
    y = _dot(ic_ref[...], yc_ref[0]) + _dot(is_ref[...], ys_ref[0])
    o_ref[0] = (x0_ref[0] * (y + vv_ref[0] * skip_ref[...])).astype(o_ref.dtype)


def _dft_inv(inv, yc, ys, vv, x0, skip, tt):
    bsz, length, width = vv.shape
    nt = length // tt
    return pl.pallas_call(
        _dft_inv_kernel,
        grid=(bsz, nt),
        in_specs=[pl.BlockSpec((tt, length), lambda b, i: (i, 0)),
                  pl.BlockSpec((tt, length), lambda b, i: (i, 1)),
                  pl.BlockSpec((1, length, width), lambda b, i: (b, 0, 0)),
                  pl.BlockSpec((1, length, width), lambda b, i: (b, 0, 0)),
                  pl.BlockSpec((1, tt, width), lambda b, i: (b, i, 0)),
                  pl.BlockSpec((1, tt, width), lambda b, i: (b, i, 0)),
                  pl.BlockSpec((1, width), lambda b, i: (0, 0))],
        out_specs=pl.BlockSpec((1, tt, width), lambda b, i: (b, i, 0)),
        out_shape=jax.ShapeDtypeStruct((bsz, length, width), BF16),
        compiler_params=_params("parallel", "parallel"),
        name="hyena_dft_inv",
    )(inv, inv, yc, ys, vv, x0, skip)


@functools.lru_cache(maxsize=None)
def _hyena_tables(length):
    n2 = 2 * length
    t = np.linspace(0.0, 1.0, length)[:, None]
    bands = (HY_EMB - 1) // 2
    wpos = 2.0 * np.pi * np.arange(length)[:, None] / length
    fb = np.linspace(1e-4, bands - 1, bands)[None]
    z = np.concatenate([t, np.cos(fb * wpos), -np.sin(fb * wpos)], axis=-1)
    zpad = np.zeros((length, LANES))
    zpad[:, :HY_EMB] = z
    f = np.arange(length)[:, None]
    n = np.arange(length)[None, :]
    ang = 2.0 * np.pi * ((f * n) % n2) / n2
    cos_m = np.cos(ang)
    sin_m = np.sin(ang)
    sin_m[0, :] = np.cos(np.pi * np.arange(length))
    fwd = np.concatenate([cos_m, sin_m], axis=0)
    scale = np.full((1, n2), 2.0 / n2)
    scale[0, 0] = 1.0 / n2
    scale[0, length] = 1.0 / n2
    inv = fwd.T * scale
    return (zpad.astype(np.float32), t.astype(np.float32), fwd.astype(np.float32),
            inv.astype(np.float32))


def _hyena(p_hy, conv_w, conv_b, filt, skip):
    bsz, length, three_w = p_hy.shape
    width = three_w // 3
    w_in, b_in, w_mid, b_mid, w_out, freq = filt
    zpad, t, fwd, inv = _hyena_tables(length)
    deltas = np.abs(np.linspace(HY_MIN_DECAY, HY_MAX_DECAY, width))[None, :]
    dec = np.exp(-t.astype(np.float64) * deltas).astype(np.float32)
    dec2 = jnp.asarray(np.concatenate([dec, dec], axis=1))
    ffn = w_in.shape[1]
    w_in_pad = jnp.zeros((LANES, ffn), F32).at[:HY_EMB].set(w_in)
    hfilt = _hy_filter(jnp.asarray(zpad), w_in_pad, b_in.reshape(1, ffn), w_mid,
                       b_mid.reshape(-1, 1, ffn), w_out, freq.reshape(1, ffn), dec2)
    fwd_b = jnp.asarray(fwd).astype(BF16)
    inv_b = jnp.asarray(inv).astype(BF16)
    tf = min(512, length)
    kc, ks = _filt_spec(fwd_b, hfilt, tf)
    x0, vv = _hy_prep(p_hy, conv_w, conv_b.reshape(1, three_w))
    yc, ys = _dft_fwd(fwd_b, vv, kc, ks, tf)
    return _dft_inv(inv_b, yc, ys, vv, x0, skip.reshape(1, width), tf)


def _gelu_tanh(x):
    return 0.5 * x * (1.0 + jnp.tanh(math.sqrt(2.0 / math.pi) * (x + 0.044715 * x * x * x)))


def _lru_kernel(xl_ref, yl_ref, xc_ref, cw_ref, cb_ref, wa_ref, ba_ref, wx_ref, bx_ref, ap_ref,
                o_ref, xs_s, a_s, b_s, *, ctx_len, lat_len):
    tot = ctx_len + lat_len
    ngrp = tot // 8
    ngrp_ctx = ctx_len // 8
    xs_s[0:ctx_len, :] = _depthwise_conv(xc_ref[0], cw_ref[...], LRU_CONV // 2) + cb_ref[...]
    xs_s[ctx_len:, :] = _depthwise_conv(xl_ref[0], cw_ref[...], LRU_CONV // 2) + cb_ref[...]
    xs = xs_s[...]
    xsb = xs.astype(BF16)
    rows8 = lax.broadcasted_iota(jnp.int32, (ngrp, 8, xs.shape[1]), 1)
    for d in range(2):
        r = _sigmoid(_dot(xsb, wa_ref[d, 0].astype(BF16)) + ba_ref[d])
        gi = _sigmoid(_dot(xsb, wx_ref[d, 0].astype(BF16)) + bx_ref[d])
        log_a = -LRU_C * r * _softplus(ap_ref[d])
        a = jnp.exp(log_a)
        b = jnp.sqrt(1.0 - a * a) * (gi * xs)
        a = a.reshape(ngrp, 8, a.shape[1])
        b = b.reshape(ngrp, 8, b.shape[1])
        for s in (1, 2, 4):
            keep = (rows8 >= s) if d == 0 else (rows8 < 8 - s)
            shift = s if d == 0 else 8 - s
            sa = jnp.where(keep, pltpu.roll(a, shift, axis=1), 1.0)
            sb = jnp.where(keep, pltpu.roll(b, shift, axis=1), 0.0)
            b = a * sb + b
            a = a * sa
        a_s[d] = a.reshape(tot, a.shape[2])
        b_s[d] = b.reshape(tot, b.shape[2])

    def group(i, carry):
        cf, cb = carry
        rf = pl.multiple_of(i * 8, 8)
        gidx = jnp.where(i < ngrp_ctx, ngrp_ctx - 1 - i, ngrp + ngrp_ctx - 1 - i)
        rb = pl.multiple_of(gidx * 8, 8)
        hf = a_s[0, pl.ds(rf, 8), :] * cf + b_s[0, pl.ds(rf, 8), :]
        hb = a_s[1, pl.ds(rb, 8), :] * cb + b_s[1, pl.ds(rb, 8), :]
        b_s[0, pl.ds(rf, 8), :] = hf
        b_s[1, pl.ds(rb, 8), :] = hb
        return (jnp.broadcast_to(hf[7:8, :], hf.shape), jnp.broadcast_to(hb[0:1, :], hb.shape))

    zero = jnp.zeros((8, xs.shape[1]), F32)
    lax.fori_loop(0, ngrp, group, (zero, zero), unroll=4)
    h = b_s[0, ctx_len:, :] + b_s[1, ctx_len:, :]
    o_ref[0] = (h * _gelu_tanh(yl_ref[0])).astype(o_ref.dtype)


def _rglru(xb_l, yb_l, xb_c, conv_w, conv_b, wa, ba, wx, bx, a_param):
    bsz, lat_len, width = xb_l.shape
    ctx_len = xb_c.shape[1]
    tot = ctx_len + lat_len
    blk = width // LRU_HEADS

    def col(b, h):
        return (b, 0, h)

    def wcol(b, h):
        return (0, h)

    def w3(b, h):
        return (0, 0, h)

    return pl.pallas_call(
        functools.partial(_lru_kernel, ctx_len=ctx_len, lat_len=lat_len),
        grid=(bsz, LRU_HEADS),
        in_specs=[pl.BlockSpec((1, lat_len, blk), col),
                  pl.BlockSpec((1, lat_len, blk), col),
                  pl.BlockSpec((1, ctx_len, blk), col),
                  pl.BlockSpec((LRU_CONV, blk), wcol),
                  pl.BlockSpec((1, blk), wcol),
                  pl.BlockSpec((2, 1, blk, blk), lambda b, h: (0, h, 0, 0)),
                  pl.BlockSpec((2, 1, blk), w3),
                  pl.BlockSpec((2, 1, blk, blk), lambda b, h: (0, h, 0, 0)),
                  pl.BlockSpec((2, 1, blk), w3),
                  pl.BlockSpec((2, 1, blk), w3)],
        out_specs=pl.BlockSpec((1, lat_len, blk), col),
        out_shape=jax.ShapeDtypeStruct((bsz, lat_len, width), BF16),
        scratch_shapes=[pltpu.VMEM((tot, blk), F32),
                        pltpu.VMEM((2, tot, blk), F32),
                        pltpu.VMEM((2, tot, blk), F32)],
        compiler_params=_params("parallel", "parallel"),
        name="rglru",
    )(xb_l, yb_l, xb_c, conv_w, conv_b.reshape(1, width), wa, ba.reshape(2, 1, width), wx,
      bx.reshape(2, 1, width), a_param.reshape(2, 1, width))


def _post_mixer_kernel(*refs, n_in, has_bias):
    a_refs = refs[:n_in]
    w_refs = refs[n_in:2 * n_in]
    i = 2 * n_in
    b_ref = None
    if has_bias:
        b_ref = refs[i]
        i += 1
    x_ref, g1_ref, sc_ref, sh_ref, lng_ref, lnb_ref, rw_ref, rb_ref, base_ref = refs[i:i + 9]
    x1_ref, v_ref, route_ref, cnt_ref = refs[i + 9:]

    @pl.when((pl.program_id(0) == 0) & (pl.program_id(1) == 0))
    def _():
        cnt_ref[...] = base_ref[...]

    y = None
    for a_ref, w_ref in zip(a_refs, w_refs):
        t = _dot(a_ref[0], w_ref[...])
        y = t if y is None else y + t
    if has_bias:
        y = y + b_ref[...]
    x1 = _layer_norm(DEEPNORM_ALPHA * x_ref[0] + g1_ref[0] * y, lng_ref[...], lnb_ref[...])
    x1_ref[0] = x1
    v = x1 * (1.0 + sc_ref[0]) + sh_ref[0]
    v_ref[0] = _pack_bf16_pairs(v)
    logits = _dot3(v, rw_ref[...]) + rb_ref[...]
    tm = logits.shape[0]
    lane = lax.broadcasted_iota(jnp.int32, logits.shape, 1).astype(F32)
    work = logits
    picks, firsts = [], []
    m0 = None
    for kk in range(TOP_K):
        m = jnp.max(work, axis=-1, keepdims=True)
        if kk == 0:
            m0 = m
        first = jnp.min(jnp.where(work == m, lane, float(LANES)), axis=-1, keepdims=True)
        pick = lane == first
        picks.append(pick)
        firsts.append(first)
        work = jnp.where(pick, -jnp.inf, work)
    sel = jnp.where(picks[0] | picks[1] | picks[2] | picks[3], 1.0, 0.0)
    e = sel * jnp.exp(logits - m0)
    gate = e / jnp.sum(e, axis=-1, keepdims=True)
    ti = lax.broadcasted_iota(jnp.int32, (tm, tm), 0)
    tj = lax.broadcasted_iota(jnp.int32, (tm, tm), 1)
    before = jnp.where(ti > tj, 1.0, 0.0).astype(BF16)
    slot = _dot(before, sel.astype(BF16)) + cnt_ref[...]
    route = jnp.zeros(logits.shape, F32)
    for kk in range(TOP_K):
        rank = jnp.sum(jnp.where(picks[kk], slot, 0.0), axis=-1, keepdims=True)
        wgt = jnp.sum(jnp.where(picks[kk], gate, 0.0), axis=-1, keepdims=True)
        route = jnp.where(lane == float(kk), firsts[kk], route)
        route = jnp.where(lane == float(TOP_K + kk), rank, route)
        route = jnp.where(lane == float(2 * TOP_K + kk), wgt, route)
    route_ref[0] = route
    cnt_ref[...] += jnp.sum(sel, axis=0, keepdims=True)


def _post_mixer(acts, ws, bias, x, g1, sc2, sh2, ln_g, ln_b, router_w, router_b, base, tm):
    bsz, length, d = x.shape
    tm = min(tm, length)
    n_in = len(acts)

    def row(bi, i):
        return (bi, i, 0)

    def per_b(bi, i):
        return (bi, 0, 0)

    def const(bi, i):
        return (0, 0)

    in_specs = [pl.BlockSpec((1, tm, a.shape[2]), row) for a in acts]
    in_specs += [pl.BlockSpec(w.shape, const) for w in ws]
    args = list(acts) + list(ws)
    if bias is not None:
        in_specs.append(pl.BlockSpec((1, d), const))
        args.append(bias.reshape(1, d))
    in_specs += [pl.BlockSpec((1, tm, d), row),
                 pl.BlockSpec((1, 1, d), per_b), pl.BlockSpec((1, 1, d), per_b),
                 pl.BlockSpec((1, 1, d), per_b),
                 pl.BlockSpec((1, d), const), pl.BlockSpec((1, d), const),
                 pl.BlockSpec((d, LANES), const), pl.BlockSpec((1, LANES), const),
                 pl.BlockSpec((1, LANES), const)]
    rw = jnp.zeros((d, LANES), F32).at[:, :N_EXPERTS].set(router_w)
    rb = jnp.full((1, LANES), -jnp.inf, F32).at[0, :N_EXPERTS].set(router_b)
    args += [x, g1, sc2, sh2, ln_g.reshape(1, d), ln_b.reshape(1, d), rw, rb, base]
    return pl.pallas_call(
        functools.partial(_post_mixer_kernel, n_in=n_in, has_bias=bias is not None),
        grid=(bsz, length // tm),
        in_specs=in_specs,
        out_specs=[pl.BlockSpec((1, tm, d), row), pl.BlockSpec((1, tm, d // 2), row),
                   pl.BlockSpec((1, tm, LANES), row), pl.BlockSpec((1, LANES), const)],
        out_shape=[jax.ShapeDtypeStruct((bsz, length, d), F32),
                   jax.ShapeDtypeStruct((bsz, length, d // 2), jnp.uint32),
                   jax.ShapeDtypeStruct((bsz, length, LANES), F32),
                   jax.ShapeDtypeStruct((1, LANES), F32)],
        compiler_params=_params("arbitrary", "arbitrary"),
        name="post_mixer",
    )(*args)


MOE_W1_CHUNK = 512


def _moe_rows_kernel(te_ref, nv_ref, first_ref, nxt_ref, xs_ref, w1_hbm, b1g_ref, b1l_ref, w2_hbm,
                     b2_ref, ys_ref, w1f_s, w2f_s, w1g_s, w1l_s, w2b_s, sem, *, layer):
    i = pl.program_id(0)
    nv = nv_ref[i]

    def fetch(e):
        return (pltpu.make_async_copy(w1_hbm.at[layer, e], w1f_s, sem.at[0]),
                pltpu.make_async_copy(w2_hbm.at[layer, e], w2f_s, sem.at[1]))

    @pl.when(i == 0)
    def _():
        for cp in fetch(te_ref[0]):
            cp.start()

    @pl.when(first_ref[i] == 1)
    def _():
        for cp in fetch(te_ref[i]):
            cp.wait()
        half = MOE_W1_CHUNK // 2
        for c in range(w1f_s.shape[1] // MOE_W1_CHUNK):
            t = w1f_s[:, c * MOE_W1_CHUNK:(c + 1) * MOE_W1_CHUNK].astype(BF16).T
            words = pltpu.bitcast(t, jnp.uint32)
            w1g_s[c * half:(c + 1) * half, :] = pltpu.bitcast(words << 16, F32).astype(BF16)
            w1l_s[c * half:(c + 1) * half, :] = pltpu.bitcast(
                words & jnp.uint32(0xFFFF0000), F32).astype(BF16)
        w2b_s[...] = w2f_s[...].astype(BF16)

        @pl.when(nxt_ref[i] >= 0)
        def _():
            for cp in fetch(nxt_ref[i]):
                cp.start()

    @pl.when(nv > 0)
    def _():
        rows = lax.broadcasted_iota(jnp.int32, xs_ref.shape, 0)
        x = _unpack_bf16_pairs(jnp.where(rows < nv, xs_ref[...], jnp.uint32(0))).astype(BF16)
        glu = jnp.minimum(_dot_nt(x, w1g_s[...]) + b1g_ref[0, 0], SWIGLU_LIMIT)
        lin = jnp.clip(_dot_nt(x, w1l_s[...]) + b1l_ref[0, 0], -SWIGLU_LIMIT, SWIGLU_LIMIT)
        act = glu * _sigmoid(SWIGLU_ALPHA * glu) * (lin + 1.0)
        ys_ref[...] = _pack_bf16_pairs(_dot(act.astype(BF16), w2b_s[...]) + b2_ref[0, 0])

    @pl.when(nv == 0)
    def _():
        ys_ref[...] = jnp.zeros_like(ys_ref)


def _moe_rows(xs, tile_expert, tile_rows, tile_first, tile_next, layer, w1, b1g, b1l, w2, b2, tm):
    n_rows, dh = xs.shape
    _, _, d, dff2 = w1.shape
    dff = dff2 // 2

    def row(i, te, nv, first, nxt):
        return (i, 0)

    def exp4(i, te, nv, first, nxt):
        return (layer, te[i], 0, 0)

    hbm = pl.BlockSpec(memory_space=pl.ANY)
    return pl.pallas_call(
        functools.partial(_moe_rows_kernel, layer=layer),
        grid_spec=pltpu.PrefetchScalarGridSpec(
            num_scalar_prefetch=4,
            grid=(n_rows // tm,),
            in_specs=[pl.BlockSpec((tm, dh), row), hbm,
                      pl.BlockSpec((1, 1, 1, dff), exp4), pl.BlockSpec((1, 1, 1, dff), exp4),
                      hbm, pl.BlockSpec((1, 1, 1, d), exp4)],
            out_specs=pl.BlockSpec((tm, dh), row),
            scratch_shapes=[pltpu.VMEM((d, dff2), F32), pltpu.VMEM((dff, d), F32),
                            pltpu.VMEM((dff, d), BF16),
                            pltpu.VMEM((dff, d), BF16), pltpu.VMEM((dff, d), BF16),
                            pltpu.SemaphoreType.DMA((2,))]),
        out_shape=jax.ShapeDtypeStruct((n_rows, dh), jnp.uint32),
        compiler_params=_params("arbitrary"),
        name="moe_rows",
    )(tile_expert, tile_rows, tile_first, tile_next, xs, w1, b1g, b1l, w2, b2)


def _moe_combine_kernel(y0_ref, y1_ref, y2_ref, y3_ref, route_ref, x1_ref, g2_ref, lng_ref, lnb_ref,
                        o_ref):
    route = route_ref[...]
    lane = lax.broadcasted_iota(jnp.int32, route.shape, 1)
    f = None
    for kk, y_ref in enumerate((y0_ref, y1_ref, y2_ref, y3_ref)):
        wgt = jnp.sum(jnp.where(lane == 2 * TOP_K + kk, route, 0.0), axis=-1, keepdims=True)
        term = wgt * _unpack_bf16_pairs(y_ref[0])
        f = term if f is None else f + term
    o_ref[0] = _layer_norm(DEEPNORM_ALPHA * x1_ref[0] + g2_ref[0] * f, lng_ref[...], lnb_ref[...])


def _moe_combine(yg, route, row_offset, x1, g2, ln_g, ln_b, tm):
    bsz, length, d = x1.shape
    tm = min(tm, length)
    nt = length // tm
    off = row_offset // tm

    def pick(kk):
        return lambda bi, i: (kk, off + bi * nt + i, 0)

    def const(bi, i):
        return (0, 0)

    return pl.pallas_call(
        _moe_combine_kernel,
        grid=(bsz, nt),
        in_specs=[pl.BlockSpec((1, tm, d // 2), pick(kk)) for kk in range(TOP_K)] + [
            pl.BlockSpec((tm, LANES), lambda bi, i: (off + bi * nt + i, 0)),
            pl.BlockSpec((1, tm, d), lambda bi, i: (bi, i, 0)),
            pl.BlockSpec((1, 1, d), lambda bi, i: (bi, 0, 0)),
            pl.BlockSpec((1, d), const), pl.BlockSpec((1, d), const)],
        out_specs=pl.BlockSpec((1, tm, d), lambda bi, i: (bi, i, 0)),
        out_shape=jax.ShapeDtypeStruct((bsz, length, d), F32),
        compiler_params=_params("parallel", "parallel"),
        name="moe_combine",
    )(yg, yg, yg, yg, route, x1, g2, ln_g.reshape(1, d), ln_b.reshape(1, d))


SC_CORES = 2
SC_SUBCORES = 16
SC_WORKERS = SC_CORES * SC_SUBCORES
SC_WINDOW = 64
SC_TOKEN_WINDOW = 32


def _sc_row_pipeline(nwin, read, write):
    def start(copies):
        for cp in copies:
            cp.start()

    def wait(copies):
        for cp in copies:
            cp.wait()

    start(read(0, 0))

    @pl.loop(0, nwin, step=2)
    def _(w0):
        for b in range(2):
            w = w0 + b

            @pl.when(w + 1 < nwin)
            def _():
                @pl.when(w >= 1)
                def _():
                    wait(write(w - 1, 1 - b))

                start(read(w + 1, 1 - b))

            wait(read(w, b))
            start(write(w, b))

    wait(write(nwin - 2, 0))
    wait(write(nwin - 1, 1))


def _sc_scatter_rows(src, pos, n_out):
    t_rows, d = src.shape
    nw, nwin, picks, win = pos.shape
    assert nw == SC_WORKERS and nwin % 2 == 0 and t_rows == nw * nwin * win
    mesh = plsc.VectorSubcoreMesh(core_axis_name="c", subcore_axis_name="s")

    @functools.partial(
        pl.kernel, mesh=mesh, out_type=jax.ShapeDtypeStruct((n_out, d), src.dtype),
        scratch_types=[pltpu.VMEM((nwin, picks, win), jnp.int32),
                       pltpu.VMEM((2, win, d), src.dtype),
                       pltpu.SemaphoreType.DMA((2,)), pltpu.SemaphoreType.DMA((2,))])
    def scatter(src_hbm, pos_hbm, out_hbm, idx_v, rows_v, rsem, wsem):
        wid = lax.axis_index("s") * SC_CORES + lax.axis_index("c")
        t0 = wid * (nwin * win)
        pltpu.sync_copy(pos_hbm.at[wid], idx_v)

        def read(w, slot):
            return [pltpu.make_async_copy(src_hbm.at[pl.ds(t0 + w * win, win)], rows_v.at[slot],
                                          rsem.at[slot])]

        def write(w, slot):
            return [pltpu.make_async_copy(rows_v.at[slot], out_hbm.at[idx_v.at[w, k]],
                                          wsem.at[slot]) for k in range(picks)]

        _sc_row_pipeline(nwin, read, write)

    return scatter(src, pos)


def _sc_gather_rows(table, pos):
    _, d = table.shape
    nw, nwin, win = pos.shape
    assert nw == SC_WORKERS and nwin % 2 == 0
    per = nwin * win
    mesh = plsc.VectorSubcoreMesh(core_axis_name="c", subcore_axis_name="s")

    @functools.partial(
        pl.kernel, mesh=mesh, out_type=jax.ShapeDtypeStruct((nw * per, d), table.dtype),
        scratch_types=[pltpu.VMEM((nwin, win), jnp.int32), pltpu.VMEM((2, win, d), table.dtype),
                       pltpu.SemaphoreType.DMA((2,)), pltpu.SemaphoreType.DMA((2,))])
    def gather(table_hbm, pos_hbm, out_hbm, idx_v, rows_v, rsem, wsem):
        wid = lax.axis_index("s") * SC_CORES + lax.axis_index("c")
        base = wid * per
        pltpu.sync_copy(pos_hbm.at[wid], idx_v)

        def read(w, slot):
            return [pltpu.make_async_copy(table_hbm.at[idx_v.at[w]], rows_v.at[slot],
                                          rsem.at[slot])]

        def write(w, slot):
            return [pltpu.make_async_copy(rows_v.at[slot], out_hbm.at[pl.ds(base + w * win, win)],
                                          wsem.at[slot])]

        _sc_row_pipeline(nwin, read, write)

    return gather(table, pos)


MOE_TILE = 512


def _moe_sparse(v_all, route, counts, layer, w1, b1g, b1l, w2, b2):
    t_rows, dh = v_all.shape
    n_exp = w1.shape[1]
    pairs = TOP_K * t_rows
    n_tiles = pairs // MOE_TILE + n_exp
    route_t = route.T
    expert = route_t[0:TOP_K].astype(jnp.int32)
    slot = route_t[TOP_K:2 * TOP_K].astype(jnp.int32)
    cnt = counts[0, :n_exp].astype(jnp.int32)
    tiles_per = (cnt + MOE_TILE - 1) // MOE_TILE
    tile_end = jnp.cumsum(tiles_per)
    tile_start = tile_end - tiles_per
    pos = slot
    for e in range(n_exp):
        pos = pos + jnp.where(expert == e, tile_start[e] * MOE_TILE, 0)
    nwin = pairs // (SC_WORKERS * SC_WINDOW)
    pos_km = pos.reshape(SC_WORKERS, nwin, SC_WINDOW)
    nwin_t = t_rows // (SC_WORKERS * SC_TOKEN_WINDOW)
    pos_tm = pos.reshape(TOP_K, SC_WORKERS, nwin_t, SC_TOKEN_WINDOW).transpose(1, 2, 0, 3)
    tile_ids = jnp.arange(n_tiles, dtype=jnp.int32)[:, None]
    owns = (tile_ids >= tile_start[None, :]) & (tile_ids < tile_end[None, :])
    experts = jnp.arange(n_exp, dtype=jnp.int32)[None, :]
    last_used = jnp.max(jnp.where(tiles_per > 0, experts[0], 0))
    used = jnp.any(owns, axis=1)
    te = jnp.where(used, jnp.sum(jnp.where(owns, experts, 0), axis=1), last_used).astype(jnp.int32)
    rows_left = cnt[None, :] - (tile_ids - tile_start[None, :]) * MOE_TILE
    tile_rows = jnp.sum(jnp.where(owns, jnp.clip(rows_left, 0, MOE_TILE), 0), axis=1)
    tile_rows = tile_rows.astype(jnp.int32)
    tile_first = jnp.any(owns & (tile_ids == tile_start[None, :]), axis=1).astype(jnp.int32)
    later = (experts > experts.T) & (tiles_per[None, :] > 0)
    next_e = jnp.min(jnp.where(later, experts, n_exp), axis=1)
    next_e = jnp.where(next_e == n_exp, -1, next_e)
    tile_next = jnp.where(used, jnp.sum(jnp.where(owns, next_e[None, :], 0), axis=1), -1)
    tile_next = tile_next.astype(jnp.int32)
    xs = _sc_scatter_rows(v_all, pos_tm, n_tiles * MOE_TILE)
    ys = _moe_rows(xs, te, tile_rows, tile_first, tile_next, layer, w1, b1g, b1l, w2, b2, MOE_TILE)
    return _sc_gather_rows(ys, pos_km).reshape(TOP_K, t_rows, dh)


@functools.lru_cache(maxsize=None)
def _sincos_2d(rows, cols, dim):
    quarter = dim // 4
    omega = 1.0 / (10000.0 ** (np.arange(quarter, dtype=np.float64) / quarter))

    def emb1d(n):
        ang = np.arange(n, dtype=np.float64)[:, None] * omega
        return np.concatenate([np.sin(ang), np.cos(ang)], axis=-1)

    er = np.broadcast_to(emb1d(rows)[:, None], (rows, cols, dim // 2))
    ec = np.broadcast_to(emb1d(cols)[None], (rows, cols, dim // 2))
    return np.concatenate([er, ec], axis=-1).reshape(rows * cols, dim).astype(np.float32)


def _gate_layout(gates, n_ch):
    bsz, length, _ = gates.shape
    g = gates[..., :n_ch].reshape(bsz, length // DN_CHUNK, DN_CHUNK, n_ch)
    return g.transpose(0, 3, 1, 2)


def kernel(x, c, ctx, c_ctx, ada_w, ada_b, ln_g, ln_b, ev_w_in, ev_w_out, dn_conv_w, dn_a_log, dn_dt_bias, dn_norm_g, hy_conv_w, hy_conv_b, hy_w_in, hy_b_in, hy_w_mid, hy_b_mid, hy_w_out, hy_freq, hy_skip, od_w_in, od_b_in, lru_conv_w, lru_conv_b, lru_wa, lru_ba, lru_wx, lru_bx, lru_a_param, od_w_out, od_b_out, router_w, router_b, moe_w1, moe_b1, moe_w2, moe_b2):
    bsz, length, d = x.shape
    ctx_len = ctx.shape[1]
    pos = jnp.asarray(_sincos_2d(length // GRID_W, GRID_W, d))

    cond = jnp.zeros((16, d), F32).at[:bsz].set(c).at[bsz].set(c_ctx)
    mod = _modulation(cond, ada_w, ada_b).reshape(DEPTH, 16, 6, d)

    def lat_mod(layer, k):
        return mod[layer, :bsz, k][:, None, :]

    def ctx_mod(layer, k):
        return jnp.broadcast_to(mod[layer, bsz, k][None, None, :], (bsz, 1, d))

    hc = ctx
    for layer in range(DEPTH):
        last = layer == DEPTH - 1
        j = layer // 2
        if layer % 2 == 0:
            dn_qk = DN_HEADS * DN_DK
            dn_qkv = 3 * dn_qk
            dn_in = dn_qkv + dn_qk + 4 * DN_HEADS
            w_in = ev_w_in[j]
            gate_w = jnp.zeros((d, LANES), F32).at[:, :4 * DN_HEADS].set(w_in[:, dn_qkv + dn_qk:dn_in])
            w_cat = jnp.concatenate([w_in[:, :dn_qkv + dn_qk], gate_w, w_in[:, dn_in:]],
                                    axis=1).astype(BF16)
            hy_in = w_in.shape[1] - dn_in
            widths = (dn_qkv, dn_qk, LANES, hy_in)
            qkv_l, z_l, gt_l, phy_l = _inproj(x, lat_mod(layer, 1), lat_mod(layer, 0), pos,
                                               w_cat, None, widths, 512)
            qkv_c, z_c, gt_c, phy_c = _inproj(hc, ctx_mod(layer, 1), ctx_mod(layer, 0), None,
                                               w_cat, None, widths, 512)
            gates = jnp.concatenate([_gate_layout(gt_c, 4 * DN_HEADS),
                                     _gate_layout(gt_l, 4 * DN_HEADS)], axis=2)
            n_chunks = gates.shape[2]
            gates = jnp.pad(gates, ((0, 0), (0, 0), (0, -n_chunks % 16), (0, 0)))
            dn_l, dn_c = _deltanet(qkv_l, z_l, qkv_c, z_c, gates, dn_conv_w[j], dn_a_log[j],
                                   dn_dt_bias[j], dn_norm_g[j])
            filt = (hy_w_in[j], hy_b_in[j], hy_w_mid[j], hy_b_mid[j], hy_w_out[j], hy_freq[j])
            hy_l = _hyena(phy_l, hy_conv_w[j], hy_conv_b[j], filt, hy_skip[j])
            w_out = ev_w_out[j].astype(BF16)
            half = dn_l.shape[2]
            acts_l, ws, bias = (dn_l, hy_l), (w_out[:half], w_out[half:]), None
            acts_c = None
            if not last:
                hy_c = _hyena(phy_c, hy_conv_w[j], hy_conv_b[j], filt, hy_skip[j])
                acts_c = (dn_c, hy_c)
        else:
            w_in = od_w_in[j].astype(BF16)
            width = w_in.shape[1] // 2
            b_in = od_b_in[j].reshape(1, 2 * width)
            xb_l, yb_l = _inproj(x, lat_mod(layer, 1), lat_mod(layer, 0), pos, w_in, b_in,
                                 (width, width), 512)
            xb_c, _ = _inproj(hc, ctx_mod(layer, 1), ctx_mod(layer, 0), None, w_in, b_in,
                              (width, width), 512)
            act_l = _rglru(xb_l, yb_l, xb_c, lru_conv_w[j], lru_conv_b[j], lru_wa[j], lru_ba[j],
                           lru_wx[j], lru_bx[j], lru_a_param[j])
            acts_l, ws, bias = (act_l,), (od_w_out[j].astype(BF16),), od_b_out[j]
            acts_c = None
            assert last, "context outputs of the RG-LRU layer are only needed before the last layer"


        x1, v, route, counts = _post_mixer(acts_l, ws, bias, x, lat_mod(layer, 2),
                                           lat_mod(layer, 4), lat_mod(layer, 3), ln_g[layer, 0],
                                           ln_b[layer, 0], router_w[layer], router_b[layer],
                                           jnp.zeros((1, LANES), F32), 512)
        v_all = v.reshape(bsz * length, d // 2)
        route = route.reshape(bsz * length, LANES)
        if not last:
            hc1, vc, route_c, counts = _post_mixer(acts_c, ws, bias, hc, ctx_mod(layer, 2),
                                                   ctx_mod(layer, 4), ctx_mod(layer, 3),
                                                   ln_g[layer, 0], ln_b[layer, 0], router_w[layer],
                                                   router_b[layer], counts, 256)
            v_all = jnp.concatenate([v_all, vc.reshape(bsz * ctx_len, d // 2)], axis=0)
            route = jnp.concatenate([route, route_c.reshape(bsz * ctx_len, LANES)], axis=0)
        yg = _moe_sparse(v_all, route, counts, layer, moe_w1, moe_b1[:, :, None, 0::2],
                         moe_b1[:, :, None, 1::2], moe_w2, moe_b2[:, :, None, :])
        x = _moe_combine(yg, route, 0, x1, lat_mod(layer, 5), ln_g[layer, 1], ln_b[layer, 1], 512)
        if not last:
            hc = _moe_combine(yg, route, bsz * length, hc1, ctx_mod(layer, 5), ln_g[layer, 1],
                              ln_b[layer, 1], 256)
    return x
```

```python
import functools
import math

import numpy as np
import jax
import jax.numpy as jnp
from jax import lax
from jax.experimental import pallas as pl
from jax.experimental.pallas import tpu as pltpu
from jax.experimental.pallas import tpu_sc as plsc

F32 = jnp.float32
BF16 = jnp.bfloat16

VMEM_LIMIT_BYTES = 56 * 1024 * 1024
LANES = 128
ROW_TILE = 512
FILTER_ROW_TILE = 256
MOD_COL_TILE = 1536

DEPTH = 2
GRID_W = 64
DEEPNORM_ALPHA = (2.0 * DEPTH) ** 0.25
LN_EPS = 1e-5
RMS_EPS = 1e-6

DN_HEADS = 4
DN_DK = 128
DN_CHUNK = 64
DN_CONV = 4

HY_EMB = 33
HY_TARGET = 1e-2
HY_MIN_DECAY = math.log(HY_TARGET) / 1.5
HY_MAX_DECAY = math.log(HY_TARGET) / 0.3
HY_CONV = 3

LRU_HEADS = 4
LRU_C = 8.0
LRU_CONV = 4

N_EXPERTS = 32
TOP_K = 4
SWIGLU_ALPHA = 1.702
SWIGLU_LIMIT = 7.0


def _params(*sem):
    return pltpu.CompilerParams(dimension_semantics=sem, vmem_limit_bytes=VMEM_LIMIT_BYTES)


def _dot(a, b):
    return jnp.dot(a, b, preferred_element_type=F32)


def _dot_nt(a, b):
    return lax.dot_general(a, b, (((1,), (1,)), ((), ())), preferred_element_type=F32)


def _split(a):
    hi = a.astype(BF16)
    lo = (a - hi.astype(F32)).astype(BF16)
    return hi, lo


def _dot3(a, b):
    ah, al = _split(a)
    bh, bl = _split(b)
    return _dot(ah, bh) + _dot(ah, bl) + _dot(al, bh)


def _silu(x):
    return x * (1.0 / (1.0 + jnp.exp(-x)))


def _sigmoid(x):
    return 1.0 / (1.0 + jnp.exp(-x))


def _softplus(x):
    return jnp.maximum(x, 0.0) + jnp.log(1.0 + jnp.exp(-jnp.abs(x)))


def _layer_norm(x, g, b):
    mu = jnp.mean(x, axis=-1, keepdims=True)
    xc = x - mu
    var = jnp.mean(xc * xc, axis=-1, keepdims=True)
    return xc * lax.rsqrt(var + LN_EPS) * g + b


def _pack_bf16_pairs(x):
    w = x.shape[1] // 2
    lo = pltpu.bitcast(x[:, :w].astype(BF16).astype(F32), jnp.uint32) >> 16
    hi = pltpu.bitcast(x[:, w:].astype(BF16).astype(F32), jnp.uint32) & jnp.uint32(0xFFFF0000)
    return lo | hi


def _unpack_bf16_pairs(p):
    lo = pltpu.bitcast(p << 16, F32)
    hi = pltpu.bitcast(p & jnp.uint32(0xFFFF0000), F32)
    return jnp.concatenate([lo, hi], axis=1)


def _shift_rows(x, s):
    if s == 0:
        return x
    n = x.shape[0]
    rows = lax.broadcasted_iota(jnp.int32, x.shape, 0)
    valid = (rows >= s) if s > 0 else (rows < n + s)
    return jnp.where(valid, pltpu.roll(x, s % n, axis=0), 0.0)


def _depthwise_conv(x, w, pad_left):
    acc = None
    for i in range(w.shape[0]):
        term = _shift_rows(x, pad_left - i) * w[i:i + 1, :]
        acc = term if acc is None else acc + term
    return acc


def _mod_kernel(c_ref, w_ref, b_ref, o_ref):
    o_ref[0] = _dot3(_silu(c_ref[...]), w_ref[0]) + b_ref[0]


def _modulation(cond, ada_w, ada_b):
    depth, d, n = ada_w.shape
    rows = cond.shape[0]
    tn = MOD_COL_TILE
    return pl.pallas_call(
        _mod_kernel,
        grid=(depth, n // tn),
        in_specs=[
            pl.BlockSpec((rows, d), lambda l, j: (0, 0)),
            pl.BlockSpec((1, d, tn), lambda l, j: (l, 0, j)),
            pl.BlockSpec((1, 1, tn), lambda l, j: (l, 0, j)),
        ],
        out_specs=pl.BlockSpec((1, rows, tn), lambda l, j: (l, 0, j)),
        out_shape=jax.ShapeDtypeStruct((depth, rows, n), F32),
        compiler_params=_params("parallel", "parallel"),
        name="modulation",
    )(cond, ada_w, ada_b.reshape(depth, 1, n))


def _inproj_kernel(*refs, splits, has_pos, has_bias):
    x_ref, sc_ref, sh_ref = refs[:3]
    i = 3
    pos_ref = None
    if has_pos:
        pos_ref = refs[i]
        i += 1
    w_ref = refs[i]
    i += 1
    b_ref = None
    if has_bias:
        b_ref = refs[i]
        i += 1
    o_refs = refs[i:]
    u = x_ref[0] * (1.0 + sc_ref[0]) + sh_ref[0]
    if has_pos:
        u = u + pos_ref[...]
    ub = u.astype(BF16)
    for o_ref, (s, e) in zip(o_refs, splits):
        acc = _dot(ub, w_ref[:, s:e])
        if has_bias:
            acc = acc + b_ref[:, s:e]
        o_ref[0] = acc


def _inproj(x, sc, sh, pos, w, b, widths, tm):
    bsz, length, d = x.shape
    n = w.shape[1]
    splits, s = [], 0
    for wd in widths:
        splits.append((s, s + wd))
        s += wd
    assert s == n
    tm = min(tm, length)
    in_specs = [
        pl.BlockSpec((1, tm, d), lambda bi, i: (bi, i, 0)),
        pl.BlockSpec((1, 1, d), lambda bi, i: (bi, 0, 0)),
        pl.BlockSpec((1, 1, d), lambda bi, i: (bi, 0, 0)),
    ]
    args = [x, sc, sh]
    if pos is not None:
        in_specs.append(pl.BlockSpec((tm, d), lambda bi, i: (i, 0)))
        args.append(pos)
    in_specs.append(pl.BlockSpec((d, n), lambda bi, i: (0, 0)))
    args.append(w)
    if b is not None:
        in_specs.append(pl.BlockSpec((1, n), lambda bi, i: (0, 0)))
        args.append(b)
    return pl.pallas_call(
        functools.partial(_inproj_kernel, splits=tuple(splits), has_pos=pos is not None,
                          has_bias=b is not None),
        grid=(bsz, length // tm),
        in_specs=in_specs,
        out_specs=[pl.BlockSpec((1, tm, wd), lambda bi, i: (bi, i, 0)) for wd in widths],
        out_shape=[jax.ShapeDtypeStruct((bsz, length, wd), F32) for wd in widths],
        compiler_params=_params("parallel", "parallel"),
        name="inproj",
    )(*args)


def _unit_tri_inverses(mats, lower):
    n = mats[0].shape[0]
    nb = 16
    np_ = len(mats)
    ii = lax.broadcasted_iota(jnp.int32, (n, n), 0)
    jj = lax.broadcasted_iota(jnp.int32, (n, n), 1)
    same16 = (ii // nb) == (jj // nb)
    same32 = (ii // (2 * nb)) == (jj // (2 * nb))
    dgs = []
    for a in mats:
        ad = jnp.where(same16, a, 0.0)
        dgs.append(ad[0:nb] + ad[nb:2 * nb] + ad[2 * nb:3 * nb] + ad[3 * nb:4 * nb])
    dg = jnp.concatenate(dgs, axis=0)
    rr = lax.broadcasted_iota(jnp.int32, dg.shape, 0)
    ll = lax.broadcasted_iota(jnp.int32, dg.shape, 1)
    xd = jnp.where(rr % nb == ll % nb, 1.0, 0.0)
    blk0 = (ll // nb) * nb
    for s in (range(nb - 1) if lower else range(nb - 1, 0, -1)):
        col = jnp.take_along_axis(dg, blk0 + s, axis=1)
        row = jnp.concatenate(
            [jnp.broadcast_to(xd[p * nb + s:p * nb + s + 1, :], (nb, n)) for p in range(np_)], axis=0)
        xd = xd - col * row
    ds = [jnp.where(same16, jnp.concatenate([xd[p * nb:(p + 1) * nb]] * (n // nb), axis=0), 0.0)
          for p in range(np_)]
    lvl1 = same32 & jnp.logical_not(same16)
    t1 = [_dot3(d, jnp.where(lvl1, a, 0.0)) for d, a in zip(ds, mats)]
    x1 = [d - _dot3(t, d) for d, t in zip(ds, t1)]
    x1b = [x.astype(BF16) for x in x1]
    t2 = [_dot(xb, jnp.where(same32, 0.0, a).astype(BF16)) for xb, a in zip(x1b, mats)]
    return [x - _dot(t.astype(BF16), xb) for x, t, xb in zip(x1, t2, x1b)]


def _dn_kernel(alog_ref, dtb_ref,
               ql_ref, kl_ref, vl_ref, zl_ref, qc_ref, kc_ref, vc_ref, zc_ref, gt_ref,
               cwq_ref, cwk_ref, cwv_ref, ng_ref,
               yl_ref, yc_ref,
               qn_s, kn_s, vn_s, gc_s, bt_s, nq_s, c_s, gl_s, o_s,
               *, ctx_len, lat_len):
    h = pl.program_id(1)
    csz = DN_CHUNK
    nc_ctx = ctx_len // csz
    nc = (ctx_len + lat_len) // csz

    def prep(src_ref, cw_ref, kind):
        t = _silu(_depthwise_conv(src_ref[0], cw_ref[...], DN_CONV // 2))
        if kind == "v":
            return t
        t = t * lax.rsqrt(jnp.sum(t * t, axis=-1, keepdims=True) + RMS_EPS)
        return t * (DN_DK ** -0.5) if kind == "q" else t

    qn_s[0:ctx_len, :] = prep(qc_ref, cwq_ref, "q")
    qn_s[ctx_len:, :] = prep(ql_ref, cwq_ref, "q")
    kn_s[0:ctx_len, :] = prep(kc_ref, cwk_ref, "k")
    kn_s[ctx_len:, :] = prep(kl_ref, cwk_ref, "k")
    vn_s[0:ctx_len, :] = prep(vc_ref, cwv_ref, "v")
    vn_s[ctx_len:, :] = prep(vl_ref, cwv_ref, "v")

    ii = lax.broadcasted_iota(jnp.int32, (csz, csz), 0)
    jj = lax.broadcasted_iota(jnp.int32, (csz, csz), 1)
    eye = ii == jj
    for d in range(2):
        graw = gt_ref[0, d * 2 * DN_HEADS + h]
        braw = gt_ref[0, d * 2 * DN_HEADS + DN_HEADS + h]
        a_neg = -jnp.exp(jnp.zeros_like(graw) + alog_ref[d, h])
        g = a_neg * _softplus(graw + dtb_ref[d, h])
        tri = jnp.where((ii <= jj) if d == 0 else (ii >= jj), 1.0, 0.0).astype(BF16)
        g1 = g.astype(BF16)
        r1 = g - g1.astype(F32)
        g2 = r1.astype(BF16)
        g3 = (r1 - g2.astype(F32)).astype(BF16)
        gc_s[d] = _dot(g1, tri) + _dot(g2, tri) + _dot(g3, tri)
        bt_s[d] = _sigmoid(braw)

    o_s[...] = jnp.zeros_like(o_s)

    group = nc

    def chunk_prep(gi, carry):
        ns = [gi * group + c for c in range(group)]
        r0s = [pl.multiple_of(n * csz, csz) for n in ns]
        qs = [qn_s[pl.ds(r0, csz), :] for r0 in r0s]
        ks = [kn_s[pl.ds(r0, csz), :] for r0 in r0s]
        vs = [vn_s[pl.ds(r0, csz), :] for r0 in r0s]
        kbfs = [k.astype(BF16) for k in ks]
        qks = [_dot_nt(q.astype(BF16), kbf) for q, kbf in zip(qs, kbfs)]
        for d in range(2):
            incl = (ii >= jj) if d == 0 else (ii <= jj)
            strict = (ii > jj) if d == 0 else (ii < jj)
            grs = [gc_s[d, pl.ds(n, 1), :] for n in ns]
            grows = [jnp.broadcast_to(gr, (csz, csz)) for gr in grs]
            gcols = [jnp.sum(jnp.where(eye, grow, 0.0), axis=1, keepdims=True) for grow in grows]
            bcols = [jnp.sum(jnp.where(eye, jnp.broadcast_to(bt_s[d, pl.ds(n, 1), :], (csz, csz)),
                                       0.0), axis=1, keepdims=True) for n in ns]
            decays = [jnp.where(incl, jnp.exp(jnp.where(incl, gcol - grow, 0.0)), 0.0)
                      for gcol, grow in zip(gcols, grows)]
            kbs = [k * bcol for k, bcol in zip(ks, bcols)]
            amats = [jnp.where(strict, _dot_nt(kb.astype(BF16), kbf) * decay, 0.0)
                     for kb, kbf, decay in zip(kbs, kbfs, decays)]
            tbs = [t.astype(BF16) for t in _unit_tri_inverses(amats, lower=(d == 0))]
            cs = range(group)
            egs = [jnp.exp(gcols[c]) for c in cs]
            ubs = [_dot(tbs[c], (vs[c] * bcols[c]).astype(BF16)).astype(BF16) for c in cs]
            wbs = [_dot(tbs[c], (kbs[c] * egs[c]).astype(BF16)).astype(BF16) for c in cs]
            attns = [jnp.where(incl, qks[c] * decays[c], 0.0).astype(BF16) for c in cs]
            glasts = [grs[c][:, csz - 1:csz] if d == 0 else grs[c][:, 0:1] for c in cs]
            kdts = [(ks[c] * jnp.exp(glasts[c] - gcols[c])).T.astype(BF16) for c in cs]
            nmats = [_dot(kdts[c], wbs[c]).astype(BF16) for c in cs]
            qmats = [(qs[c] * egs[c] - _dot(attns[c], wbs[c])).astype(BF16) for c in cs]
            cmats = [_dot(kdts[c], ubs[c]) for c in cs]
            omats = [_dot(attns[c], ubs[c]) for c in cs]
            for c in cs:
                n = ns[c]
                nq_s[d, n, 0:DN_DK, :] = nmats[c]
                nq_s[d, n, DN_DK:DN_DK + csz, :] = qmats[c]
                c_s[d, n] = cmats[c]
                o_s[pl.ds(r0s[c], csz), :] += omats[c]
                gl_s[d, pl.ds(n, 1), :] = jnp.broadcast_to(jnp.exp(glasts[c]), (1, LANES))
        return carry

    lax.fori_loop(0, nc // group, chunk_prep, 0)

    def step(i, states):
        new_states = []
        for d in range(2):
            if d == 0:
                n = i
            else:
                n = jnp.where(i < nc_ctx, nc_ctx - 1 - i, nc + nc_ctx - 1 - i)
            r0 = pl.multiple_of(n * csz, csz)
            s = states[d]
            r = _dot(nq_s[d, n], s.astype(BF16))
            o_s[pl.ds(r0, csz), :] += r[DN_DK:DN_DK + csz]
            new_states.append(s * gl_s[d, pl.ds(n, 1), :] - r[0:DN_DK] + c_s[d, n])
        return tuple(new_states)

    zero = jnp.zeros((DN_DK, DN_DK), F32)
    lax.fori_loop(0, nc, step, (zero, zero))

    def gated_norm(o, z):
        o = o * lax.rsqrt(jnp.mean(o * o, axis=-1, keepdims=True) + RMS_EPS) * ng_ref[...]
        return (o * _silu(z)).astype(yl_ref.dtype)

    yc_ref[0] = gated_norm(o_s[0:ctx_len, :], zc_ref[0])
    yl_ref[0] = gated_norm(o_s[ctx_len:, :], zl_ref[0])


def _deltanet(qkv_l, z_l, qkv_c, z_c, gates, conv_w, a_log, dt_bias, norm_g):
    bsz, lat_len, _ = qkv_l.shape
    ctx_len = qkv_c.shape[1]
    tot = ctx_len + lat_len
    nc = tot // DN_CHUNK
    ncp = gates.shape[2]
    hd = DN_DK
    nh = DN_HEADS

    def col(off):
        return lambda b, h: (b, 0, off + h)

    def wcol(off):
        return lambda b, h: (0, off + h)

    smem = pl.BlockSpec(memory_space=pltpu.SMEM)
    in_specs = [
        smem, smem,
        pl.BlockSpec((1, lat_len, hd), col(0)),
        pl.BlockSpec((1, lat_len, hd), col(nh)),
        pl.BlockSpec((1, lat_len, hd), col(2 * nh)),
        pl.BlockSpec((1, lat_len, hd), col(0)),
        pl.BlockSpec((1, ctx_len, hd), col(0)),
        pl.BlockSpec((1, ctx_len, hd), col(nh)),
        pl.BlockSpec((1, ctx_len, hd), col(2 * nh)),
        pl.BlockSpec((1, ctx_len, hd), col(0)),
        pl.BlockSpec((1, 4 * nh, ncp, DN_CHUNK), lambda b, h: (b, 0, 0, 0)),
        pl.BlockSpec((DN_CONV, hd), wcol(0)),
        pl.BlockSpec((DN_CONV, hd), wcol(nh)),
        pl.BlockSpec((DN_CONV, hd), wcol(2 * nh)),
        pl.BlockSpec((1, hd), lambda b, h: (0, 0)),
    ]
    scratch = [
        pltpu.VMEM((tot, hd), F32), pltpu.VMEM((tot, hd), F32), pltpu.VMEM((tot, hd), F32),
        pltpu.VMEM((2, ncp, DN_CHUNK), F32), pltpu.VMEM((2, ncp, DN_CHUNK), F32),
        pltpu.VMEM((2, nc, hd + DN_CHUNK, hd), BF16),
        pltpu.VMEM((2, nc, hd, hd), F32),
        pltpu.VMEM((2, nc, LANES), F32),
        pltpu.VMEM((tot, hd), F32),
    ]
    return pl.pallas_call(
        functools.partial(_dn_kernel, ctx_len=ctx_len, lat_len=lat_len),
        grid=(bsz, nh),
        in_specs=in_specs,
        out_specs=[pl.BlockSpec((1, lat_len, hd), col(0)),
                   pl.BlockSpec((1, ctx_len, hd), col(0))],
        out_shape=[jax.ShapeDtypeStruct((bsz, lat_len, nh * hd), BF16),
                   jax.ShapeDtypeStruct((bsz, ctx_len, nh * hd), BF16)],
        scratch_shapes=scratch,
        compiler_params=_params("parallel", "parallel"),
        name="deltanet",
    )(a_log, dt_bias, qkv_l, qkv_l, qkv_l, z_l, qkv_c, qkv_c, qkv_c, z_c, gates,
      conv_w, conv_w, conv_w, norm_g.reshape(1, hd))


def _hy_filter_kernel(z_ref, win_ref, bin_ref, wmid_ref, bmid_ref, wout_ref, freq_ref, dec_ref,
                      o_ref):
    freq = freq_ref[...]
    hcur = jnp.sin(freq * (_dot3(z_ref[...], win_ref[...]) + bin_ref[...]))
    for i in range(wmid_ref.shape[0]):
        hcur = jnp.sin(freq * (_dot3(hcur, wmid_ref[i]) + bmid_ref[i]))
    o_ref[...] = _dot3(hcur, wout_ref[...]) * dec_ref[...]


def _hy_filter(z, w_in, b_in, w_mid, b_mid, w_out, freq, dec2):
    length = z.shape[0]
    n_out = w_out.shape[1]
    tl = min(FILTER_ROW_TILE, length)

    def whole(a):
        return pl.BlockSpec(a.shape, lambda i: (0,) * a.ndim)

    return pl.pallas_call(
        _hy_filter_kernel,
        grid=(length // tl,),
        in_specs=[pl.BlockSpec((tl, z.shape[1]), lambda i: (i, 0)),
                  whole(w_in), whole(b_in), whole(w_mid), whole(b_mid), whole(w_out), whole(freq),
                  pl.BlockSpec((tl, n_out), lambda i: (i, 0))],
        out_specs=pl.BlockSpec((tl, n_out), lambda i: (i, 0)),
        out_shape=jax.ShapeDtypeStruct((length, n_out), F32),
        compiler_params=_params("parallel"),
        name="hyena_filter",
    )(z, w_in, b_in, w_mid, b_mid, w_out, freq, dec2)


def _filt_spec_kernel(fc_ref, fs_ref, h_ref, kc_ref, ks_ref, *, width):
    j = pl.program_id(0)
    hmat = h_ref[...]
    rows = lax.broadcasted_iota(jnp.int32, hmat.shape, 0)
    cols = lax.broadcasted_iota(jnp.int32, hmat.shape, 1)
    hmat = jnp.where((rows == 0) & (cols >= width), 0.0, hmat)
    hh, hl = _split(hmat)
    c = _dot(fc_ref[...], hh) + _dot(fc_ref[...], hl)
    s = _dot(fs_ref[...], hh) + _dot(fs_ref[...], hl)
    kc_ref[...] = c[:, :width] + c[:, width:]
    orow = lax.broadcasted_iota(jnp.int32, (c.shape[0], width), 0)
    sign = jnp.where((orow == 0) & (j == 0), 1.0, -1.0)
    ks_ref[...] = s[:, :width] + sign * s[:, width:]


def _filt_spec(fwd, hfilt, tf):
    length, two_w = hfilt.shape
    width = two_w // 2
    nt = length // tf
    return pl.pallas_call(
        functools.partial(_filt_spec_kernel, width=width),
        grid=(nt,),
        in_specs=[pl.BlockSpec((tf, length), lambda j: (j, 0)),
                  pl.BlockSpec((tf, length), lambda j: (nt + j, 0)),
                  pl.BlockSpec((length, two_w), lambda j: (0, 0))],
        out_specs=[pl.BlockSpec((tf, width), lambda j: (j, 0)),
                   pl.BlockSpec((tf, width), lambda j: (j, 0))],
        out_shape=[jax.ShapeDtypeStruct((length, width), F32)] * 2,
        compiler_params=_params("arbitrary"),
        name="hyena_filter_spectrum",
    )(fwd, fwd, hfilt)


def _hy_prep_kernel(x0_ref, x1_ref, v_ref, w0_ref, w1_ref, w2_ref, b0_ref, b1_ref, b2_ref,
                    x0o_ref, vvo_ref):
    x0 = _depthwise_conv(x0_ref[0], w0_ref[...], HY_CONV // 2) + b0_ref[...]
    x1 = _depthwise_conv(x1_ref[0], w1_ref[...], HY_CONV // 2) + b1_ref[...]
    v = _depthwise_conv(v_ref[0], w2_ref[...], HY_CONV // 2) + b2_ref[...]
    x0o_ref[0] = x0
    vvo_ref[0] = v * x1


def _hy_prep(p_hy, conv_w, conv_b):
    bsz, length, three_w = p_hy.shape
    width = three_w // 3
    nb = width // LANES

    def col(off):
        return lambda b, j: (b, 0, off + j)

    def wcol(off):
        return lambda b, j: (0, off + j)

    k = conv_w.shape[0]
    return pl.pallas_call(
        _hy_prep_kernel,
        grid=(bsz, nb),
        in_specs=[pl.BlockSpec((1, length, LANES), col(0)),
                  pl.BlockSpec((1, length, LANES), col(nb)),
                  pl.BlockSpec((1, length, LANES), col(2 * nb)),
                  pl.BlockSpec((k, LANES), wcol(0)),
                  pl.BlockSpec((k, LANES), wcol(nb)),
                  pl.BlockSpec((k, LANES), wcol(2 * nb)),
                  pl.BlockSpec((1, LANES), wcol(0)),
                  pl.BlockSpec((1, LANES), wcol(nb)),
                  pl.BlockSpec((1, LANES), wcol(2 * nb))],
        out_specs=[pl.BlockSpec((1, length, LANES), col(0)),
                   pl.BlockSpec((1, length, LANES), col(0))],
        out_shape=[jax.ShapeDtypeStruct((bsz, length, width), F32)] * 2,
        compiler_params=_params("parallel", "parallel"),
        name="hyena_prep",
    )(p_hy, p_hy, p_hy, conv_w, conv_w, conv_w, conv_b, conv_b, conv_b)


def _dft_fwd_kernel(fc_ref, fs_ref, v_ref, kc_ref, ks_ref, yc_ref, ys_ref, vb_s):
    j = pl.program_id(1)

    @pl.when(j == 0)
    def _():
        vb_s[...] = v_ref[0].astype(BF16)

    uc = _dot(fc_ref[...], vb_s[...])
    us = _dot(fs_ref[...], vb_s[...])
    kc = kc_ref[...]
    ks = ks_ref[...]
    rows = lax.broadcasted_iota(jnp.int32, uc.shape, 0)
    special = (rows == 0) & (j == 0)
    yc_ref[0] = (uc * kc - jnp.where(special, 0.0, us * ks)).astype(BF16)
    ys_ref[0] = jnp.where(special, us * ks, uc * ks + us * kc).astype(BF16)


def _dft_fwd(fwd, vv, kc, ks, tf):
    bsz, length, width = vv.shape
    nt = length // tf
    return pl.pallas_call(
        _dft_fwd_kernel,
        grid=(bsz, nt),
        in_specs=[pl.BlockSpec((tf, length), lambda b, j: (j, 0)),
                  pl.BlockSpec((tf, length), lambda b, j: (nt + j, 0)),
                  pl.BlockSpec((1, length, width), lambda b, j: (b, 0, 0)),
                  pl.BlockSpec((tf, width), lambda b, j: (j, 0)),
                  pl.BlockSpec((tf, width), lambda b, j: (j, 0))],
        out_specs=[pl.BlockSpec((1, tf, width), lambda b, j: (b, j, 0)),
                   pl.BlockSpec((1, tf, width), lambda b, j: (b, j, 0))],
        out_shape=[jax.ShapeDtypeStruct((bsz, length, width), BF16)] * 2,
        scratch_shapes=[pltpu.VMEM((length, width), BF16)],
        compiler_params=_params("parallel", "arbitrary"),
        name="hyena_dft_fwd",
    )(fwd, fwd, vv, kc, ks)


def _dft_inv_kernel(ic_ref, is_ref, yc_ref, ys_ref, vv_ref, x0_ref, skip_ref, o_ref):
    y = _dot(ic_ref[...], yc_ref[0]) + _dot(is_ref[...], ys_ref[0])
    o_ref[0] = (x0_ref[0] * (y + vv_ref[0] * skip_ref[...])).astype(o_ref.dtype)


def _dft_inv(inv, yc, ys, vv, x0, skip, tt):
    bsz, length, width = vv.shape
    nt = length // tt
    return pl.pallas_call(
        _dft_inv_kernel,
        grid=(bsz, nt),
        in_specs=[pl.BlockSpec((tt, length), lambda b, i: (i, 0)),
                  pl.BlockSpec((tt, length), lambda b, i: (i, 1)),
                  pl.BlockSpec((1, length, width), lambda b, i: (b, 0, 0)),
                  pl.BlockSpec((1, length, width), lambda b, i: (b, 0, 0)),
                  pl.BlockSpec((1, tt, width), lambda b, i: (b, i, 0)),
                  pl.BlockSpec((1, tt, width), lambda b, i: (b, i, 0)),
                  pl.BlockSpec((1, width), lambda b, i: (0, 0))],
        out_specs=pl.BlockSpec((1, tt, width), lambda b, i: (b, i, 0)),
        out_shape=jax.ShapeDtypeStruct((bsz, length, width), BF16),
        compiler_params=_params("parallel", "parallel"),
        name="hyena_dft_inv",
    )(inv, inv, yc, ys, vv, x0, skip)


@functools.lru_cache(maxsize=None)
def _hyena_tables(length):
    n2 = 2 * length
    t = np.linspace(0.0, 1.0, length)[:, None]
    bands = (HY_EMB - 1) // 2
    wpos = 2.0 * np.pi * np.arange(length)[:, None] / length
    fb = np.linspace(1e-4, bands - 1, bands)[None]
    z = np.concatenate([t, np.cos(fb * wpos), -np.sin(fb * wpos)], axis=-1)
    zpad = np.zeros((length, LANES))
    zpad[:, :HY_EMB] = z
    f = np.arange(length)[:, None]
    n = np.arange(length)[None, :]
    ang = 2.0 * np.pi * ((f * n) % n2) / n2
    cos_m = np.cos(ang)
    sin_m = np.sin(ang)
    sin_m[0, :] = np.cos(np.pi * np.arange(length))
    fwd = np.concatenate([cos_m, sin_m], axis=0)
    scale = np.full((1, n2), 2.0 / n2)
    scale[0, 0] = 1.0 / n2
    scale[0, length] = 1.0 / n2
    inv = fwd.T * scale
    return (zpad.astype(np.float32), t.astype(np.float32), fwd.astype(np.float32),
            inv.astype(np.float32))


def _hyena(p_hy, conv_w, conv_b, filt, skip):
    bsz, length, three_w = p_hy.shape
    width = three_w // 3
    w_in, b_in, w_mid, b_mid, w_out, freq = filt
    zpad, t, fwd, inv = _hyena_tables(length)
    deltas = np.abs(np.linspace(HY_MIN_DECAY, HY_MAX_DECAY, width))[None, :]
    dec = np.exp(-t.astype(np.float64) * deltas).astype(np.float32)
    dec2 = jnp.asarray(np.concatenate([dec, dec], axis=1))
    ffn = w_in.shape[1]
    w_in_pad = jnp.zeros((LANES, ffn), F32).at[:HY_EMB].set(w_in)
    hfilt = _hy_filter(jnp.asarray(zpad), w_in_pad, b_in.reshape(1, ffn), w_mid,
                       b_mid.reshape(-1, 1, ffn), w_out, freq.reshape(1, ffn), dec2)
    fwd_b = jnp.asarray(fwd).astype(BF16)
    inv_b = jnp.asarray(inv).astype(BF16)
    tf = min(ROW_TILE, length)
    kc, ks = _filt_spec(fwd_b, hfilt, tf)
    x0, vv = _hy_prep(p_hy, conv_w, conv_b.reshape(1, three_w))
    yc, ys = _dft_fwd(fwd_b, vv, kc, ks, tf)
    return _dft_inv(inv_b, yc, ys, vv, x0, skip.reshape(1, width), tf)


def _gelu_tanh(x):
    return 0.5 * x * (1.0 + jnp.tanh(math.sqrt(2.0 / math.pi) * (x + 0.044715 * x * x * x)))


def _lru_kernel(xl_ref, yl_ref, xc_ref, cw_ref, cb_ref, wa_ref, ba_ref, wx_ref, bx_ref, ap_ref,
                o_ref, xs_s, a_s, b_s, *, ctx_len, lat_len):
    tot = ctx_len + lat_len
    ngrp = tot // 8
    ngrp_ctx = ctx_len // 8
    xs_s[0:ctx_len, :] = _depthwise_conv(xc_ref[0], cw_ref[...], LRU_CONV // 2) + cb_ref[...]
    xs_s[ctx_len:, :] = _depthwise_conv(xl_ref[0], cw_ref[...], LRU_CONV // 2) + cb_ref[...]
    xs = xs_s[...]
    xsb = xs.astype(BF16)
    rows8 = lax.broadcasted_iota(jnp.int32, (ngrp, 8, xs.shape[1]), 1)
    for d in range(2):
        r = _sigmoid(_dot(xsb, wa_ref[d, 0].astype(BF16)) + ba_ref[d])
        gi = _sigmoid(_dot(xsb, wx_ref[d, 0].astype(BF16)) + bx_ref[d])
        log_a = -LRU_C * r * _softplus(ap_ref[d])
        a = jnp.exp(log_a)
        b = jnp.sqrt(1.0 - a * a) * (gi * xs)
        a = a.reshape(ngrp, 8, a.shape[1])
        b = b.reshape(ngrp, 8, b.shape[1])
        for s in (1, 2, 4):
            keep = (rows8 >= s) if d == 0 else (rows8 < 8 - s)
            shift = s if d == 0 else 8 - s
            sa = jnp.where(keep, pltpu.roll(a, shift, axis=1), 1.0)
            sb = jnp.where(keep, pltpu.roll(b, shift, axis=1), 0.0)
            b = a * sb + b
            a = a * sa
        a_s[d] = a.reshape(tot, a.shape[2])
        b_s[d] = b.reshape(tot, b.shape[2])

    def group(i, carry):
        cf, cb = carry
        rf = pl.multiple_of(i * 8, 8)
        gidx = jnp.where(i < ngrp_ctx, ngrp_ctx - 1 - i, ngrp + ngrp_ctx - 1 - i)
        rb = pl.multiple_of(gidx * 8, 8)
        hf = a_s[0, pl.ds(rf, 8), :] * cf + b_s[0, pl.ds(rf, 8), :]
        hb = a_s[1, pl.ds(rb, 8), :] * cb + b_s[1, pl.ds(rb, 8), :]
        b_s[0, pl.ds(rf, 8), :] = hf
        b_s[1, pl.ds(rb, 8), :] = hb
        return (jnp.broadcast_to(hf[7:8, :], hf.shape), jnp.broadcast_to(hb[0:1, :], hb.shape))

    zero = jnp.zeros((8, xs.shape[1]), F32)
    lax.fori_loop(0, ngrp, group, (zero, zero), unroll=4)
    h = b_s[0, ctx_len:, :] + b_s[1, ctx_len:, :]
    o_ref[0] = (h * _gelu_tanh(yl_ref[0])).astype(o_ref.dtype)


def _rglru(xb_l, yb_l, xb_c, conv_w, conv_b, wa, ba, wx, bx, a_param):
    bsz, lat_len, width = xb_l.shape
    ctx_len = xb_c.shape[1]
    tot = ctx_len + lat_len
    blk = width // LRU_HEADS

    def col(b, h):
        return (b, 0, h)

    def wcol(b, h):
        return (0, h)

    def w3(b, h):
        return (0, 0, h)

    return pl.pallas_call(
        functools.partial(_lru_kernel, ctx_len=ctx_len, lat_len=lat_len),
        grid=(bsz, LRU_HEADS),
        in_specs=[pl.BlockSpec((1, lat_len, blk), col),
                  pl.BlockSpec((1, lat_len, blk), col),
                  pl.BlockSpec((1, ctx_len, blk), col),
                  pl.BlockSpec((LRU_CONV, blk), wcol),
                  pl.BlockSpec((1, blk), wcol),
                  pl.BlockSpec((2, 1, blk, blk), lambda b, h: (0, h, 0, 0)),
                  pl.BlockSpec((2, 1, blk), w3),
                  pl.BlockSpec((2, 1, blk, blk), lambda b, h: (0, h, 0, 0)),
                  pl.BlockSpec((2, 1, blk), w3),
                  pl.BlockSpec((2, 1, blk), w3)],
        out_specs=pl.BlockSpec((1, lat_len, blk), col),
        out_shape=jax.ShapeDtypeStruct((bsz, lat_len, width), BF16),
        scratch_shapes=[pltpu.VMEM((tot, blk), F32),
                        pltpu.VMEM((2, tot, blk), F32),
                        pltpu.VMEM((2, tot, blk), F32)],
        compiler_params=_params("parallel", "parallel"),
        name="rglru",
    )(xb_l, yb_l, xb_c, conv_w, conv_b.reshape(1, width), wa, ba.reshape(2, 1, width), wx,
      bx.reshape(2, 1, width), a_param.reshape(2, 1, width))


def _post_mixer_kernel(*refs, n_in, has_bias):
    a_refs = refs[:n_in]
    w_refs = refs[n_in:2 * n_in]
    i = 2 * n_in
    b_ref = None
    if has_bias:
        b_ref = refs[i]
        i += 1
    x_ref, g1_ref, sc_ref, sh_ref, lng_ref, lnb_ref, rw_ref, rb_ref, base_ref = refs[i:i + 9]
    x1_ref, v_ref, route_ref, cnt_ref = refs[i + 9:]

    @pl.when((pl.program_id(0) == 0) & (pl.program_id(1) == 0))
    def _():
        cnt_ref[...] = base_ref[...]

    y = None
    for a_ref, w_ref in zip(a_refs, w_refs):
        t = _dot(a_ref[0], w_ref[...])
        y = t if y is None else y + t
    if has_bias:
        y = y + b_ref[...]
    x1 = _layer_norm(DEEPNORM_ALPHA * x_ref[0] + g1_ref[0] * y, lng_ref[...], lnb_ref[...])
    x1_ref[0] = x1
    v = x1 * (1.0 + sc_ref[0]) + sh_ref[0]
    v_ref[0] = _pack_bf16_pairs(v)
    logits = _dot3(v, rw_ref[...]) + rb_ref[...]
    tm = logits.shape[0]
    lane = lax.broadcasted_iota(jnp.int32, logits.shape, 1).astype(F32)
    work = logits
    picks, firsts = [], []
    m0 = None
    for kk in range(TOP_K):
        m = jnp.max(work, axis=-1, keepdims=True)
        if kk == 0:
            m0 = m
        first = jnp.min(jnp.where(work == m, lane, float(LANES)), axis=-1, keepdims=True)
        pick = lane == first
        picks.append(pick)
        firsts.append(first)
        work = jnp.where(pick, -jnp.inf, work)
    sel = jnp.where(picks[0] | picks[1] | picks[2] | picks[3], 1.0, 0.0)
    e = sel * jnp.exp(logits - m0)
    gate = e / jnp.sum(e, axis=-1, keepdims=True)
    ti = lax.broadcasted_iota(jnp.int32, (tm, tm), 0)
    tj = lax.broadcasted_iota(jnp.int32, (tm, tm), 1)
    before = jnp.where(ti > tj, 1.0, 0.0).astype(BF16)
    slot = _dot(before, sel.astype(BF16)) + cnt_ref[...]
    route = jnp.zeros(logits.shape, F32)
    for kk in range(TOP_K):
        rank = jnp.sum(jnp.where(picks[kk], slot, 0.0), axis=-1, keepdims=True)
        wgt = jnp.sum(jnp.where(picks[kk], gate, 0.0), axis=-1, keepdims=True)
        route = jnp.where(lane == float(kk), firsts[kk], route)
        route = jnp.where(lane == float(TOP_K + kk), rank, route)
        route = jnp.where(lane == float(2 * TOP_K + kk), wgt, route)
    route_ref[0] = route
    cnt_ref[...] += jnp.sum(sel, axis=0, keepdims=True)


def _post_mixer(acts, ws, bias, x, g1, sc2, sh2, ln_g, ln_b, router_w, router_b, base, tm):
    bsz, length, d = x.shape
    tm = min(tm, length)
    n_in = len(acts)

    def row(bi, i):
        return (bi, i, 0)

    def per_b(bi, i):
        return (bi, 0, 0)

    def const(bi, i):
        return (0, 0)

    in_specs = [pl.BlockSpec((1, tm, a.shape[2]), row) for a in acts]
    in_specs += [pl.BlockSpec(w.shape, const) for w in ws]
    args = list(acts) + list(ws)
    if bias is not None:
        in_specs.append(pl.BlockSpec((1, d), const))
        args.append(bias.reshape(1, d))
    in_specs += [pl.BlockSpec((1, tm, d), row),
                 pl.BlockSpec((1, 1, d), per_b), pl.BlockSpec((1, 1, d), per_b),
                 pl.BlockSpec((1, 1, d), per_b),
                 pl.BlockSpec((1, d), const), pl.BlockSpec((1, d), const),
                 pl.BlockSpec((d, LANES), const), pl.BlockSpec((1, LANES), const),
                 pl.BlockSpec((1, LANES), const)]
    rw = jnp.zeros((d, LANES), F32).at[:, :N_EXPERTS].set(router_w)
    rb = jnp.full((1, LANES), -jnp.inf, F32).at[0, :N_EXPERTS].set(router_b)
    args += [x, g1, sc2, sh2, ln_g.reshape(1, d), ln_b.reshape(1, d), rw, rb, base]
    return pl.pallas_call(
        functools.partial(_post_mixer_kernel, n_in=n_in, has_bias=bias is not None),
        grid=(bsz, length // tm),
        in_specs=in_specs,
        out_specs=[pl.BlockSpec((1, tm, d), row), pl.BlockSpec((1, tm, d // 2), row),
                   pl.BlockSpec((1, tm, LANES), row), pl.BlockSpec((1, LANES), const)],
        out_shape=[jax.ShapeDtypeStruct((bsz, length, d), F32),
                   jax.ShapeDtypeStruct((bsz, length, d // 2), jnp.uint32),
                   jax.ShapeDtypeStruct((bsz, length, LANES), F32),
                   jax.ShapeDtypeStruct((1, LANES), F32)],
        compiler_params=_params("arbitrary", "arbitrary"),
        name="post_mixer",
    )(*args)


MOE_W1_CHUNK = 512


def _moe_rows_kernel(te_ref, nv_ref, first_ref, nxt_ref, xs_ref, w1_hbm, b1g_ref, b1l_ref, w2_hbm,
                     b2_ref, ys_ref, w1f_s, w2f_s, w1g_s, w1l_s, w2b_s, sem, *, layer):
    i = pl.program_id(0)
    nv = nv_ref[i]

    def fetch(e):
        return (pltpu.make_async_copy(w1_hbm.at[layer, e], w1f_s, sem.at[0]),
                pltpu.make_async_copy(w2_hbm.at[layer, e], w2f_s, sem.at[1]))

    @pl.when(i == 0)
    def _():
        for cp in fetch(te_ref[0]):
            cp.start()

    @pl.when(first_ref[i] == 1)
    def _():
        for cp in fetch(te_ref[i]):
            cp.wait()
        half = MOE_W1_CHUNK // 2
        for c in range(w1f_s.shape[1] // MOE_W1_CHUNK):
            t = w1f_s[:, c * MOE_W1_CHUNK:(c + 1) * MOE_W1_CHUNK].astype(BF16).T
            words = pltpu.bitcast(t, jnp.uint32)
            w1g_s[c * half:(c + 1) * half, :] = pltpu.bitcast(words << 16, F32).astype(BF16)
            w1l_s[c * half:(c + 1) * half, :] = pltpu.bitcast(
                words & jnp.uint32(0xFFFF0000), F32).astype(BF16)
        w2b_s[...] = w2f_s[...].astype(BF16)

        @pl.when(nxt_ref[i] >= 0)
        def _():
            for cp in fetch(nxt_ref[i]):
                cp.start()

    @pl.when(nv > 0)
    def _():
        rows = lax.broadcasted_iota(jnp.int32, xs_ref.shape, 0)
        x = _unpack_bf16_pairs(jnp.where(rows < nv, xs_ref[...], jnp.uint32(0))).astype(BF16)
        glu = jnp.minimum(_dot_nt(x, w1g_s[...]) + b1g_ref[0, 0], SWIGLU_LIMIT)
        lin = jnp.clip(_dot_nt(x, w1l_s[...]) + b1l_ref[0, 0], -SWIGLU_LIMIT, SWIGLU_LIMIT)
        act = glu * _sigmoid(SWIGLU_ALPHA * glu) * (lin + 1.0)
        ys_ref[...] = _pack_bf16_pairs(_dot(act.astype(BF16), w2b_s[...]) + b2_ref[0, 0])

    @pl.when(nv == 0)
    def _():
        ys_ref[...] = jnp.zeros_like(ys_ref)


def _moe_rows(xs, tile_expert, tile_rows, tile_first, tile_next, layer, w1, b1g, b1l, w2, b2, tm):
    n_rows, dh = xs.shape
    _, _, d, dff2 = w1.shape
    dff = dff2 // 2

    def row(i, te, nv, first, nxt):
        return (i, 0)

    def exp4(i, te, nv, first, nxt):
        return (layer, te[i], 0, 0)

    hbm = pl.BlockSpec(memory_space=pl.ANY)
    return pl.pallas_call(
        functools.partial(_moe_rows_kernel, layer=layer),
        grid_spec=pltpu.PrefetchScalarGridSpec(
            num_scalar_prefetch=4,
            grid=(n_rows // tm,),
            in_specs=[pl.BlockSpec((tm, dh), row), hbm,
                      pl.BlockSpec((1, 1, 1, dff), exp4), pl.BlockSpec((1, 1, 1, dff), exp4),
                      hbm, pl.BlockSpec((1, 1, 1, d), exp4)],
            out_specs=pl.BlockSpec((tm, dh), row),
            scratch_shapes=[pltpu.VMEM((d, dff2), F32), pltpu.VMEM((dff, d), F32),
                            pltpu.VMEM((dff, d), BF16),
                            pltpu.VMEM((dff, d), BF16), pltpu.VMEM((dff, d), BF16),
                            pltpu.SemaphoreType.DMA((2,))]),
        out_shape=jax.ShapeDtypeStruct((n_rows, dh), jnp.uint32),
        compiler_params=_params("arbitrary"),
        name="moe_rows",
    )(tile_expert, tile_rows, tile_first, tile_next, xs, w1, b1g, b1l, w2, b2)


def _moe_combine_kernel(y0_ref, y1_ref, y2_ref, y3_ref, route_ref, x1_ref, g2_ref, lng_ref, lnb_ref,
                        o_ref):
    route = route_ref[...]
    lane = lax.broadcasted_iota(jnp.int32, route.shape, 1)
    f = None
    for kk, y_ref in enumerate((y0_ref, y1_ref, y2_ref, y3_ref)):
        wgt = jnp.sum(jnp.where(lane == 2 * TOP_K + kk, route, 0.0), axis=-1, keepdims=True)
        term = wgt * _unpack_bf16_pairs(y_ref[0])
        f = term if f is None else f + term
    o_ref[0] = _layer_norm(DEEPNORM_ALPHA * x1_ref[0] + g2_ref[0] * f, lng_ref[...], lnb_ref[...])


def _moe_combine(yg, route, row_offset, x1, g2, ln_g, ln_b, tm):
    bsz, length, d = x1.shape
    tm = min(tm, length)
    nt = length // tm
    off = row_offset // tm

    def pick(kk):
        return lambda bi, i: (kk, off + bi * nt + i, 0)

    def const(bi, i):
        return (0, 0)

    return pl.pallas_call(
        _moe_combine_kernel,
        grid=(bsz, nt),
        in_specs=[pl.BlockSpec((1, tm, d // 2), pick(kk)) for kk in range(TOP_K)] + [
            pl.BlockSpec((tm, LANES), lambda bi, i: (off + bi * nt + i, 0)),
            pl.BlockSpec((1, tm, d), lambda bi, i: (bi, i, 0)),
            pl.BlockSpec((1, 1, d), lambda bi, i: (bi, 0, 0)),
            pl.BlockSpec((1, d), const), pl.BlockSpec((1, d), const)],
        out_specs=pl.BlockSpec((1, tm, d), lambda bi, i: (bi, i, 0)),
        out_shape=jax.ShapeDtypeStruct((bsz, length, d), F32),
        compiler_params=_params("parallel", "parallel"),
        name="moe_combine",
    )(yg, yg, yg, yg, route, x1, g2, ln_g.reshape(1, d), ln_b.reshape(1, d))


SC_CORES = 2
SC_SUBCORES = 16
SC_WORKERS = SC_CORES * SC_SUBCORES
SC_WINDOW = 64
SC_TOKEN_WINDOW = 32


def _sc_row_pipeline(nwin, read, write):
    def start(copies):
        for cp in copies:
            cp.start()

    def wait(copies):
        for cp in copies:
            cp.wait()

    start(read(0, 0))

    @pl.loop(0, nwin, step=2)
    def _(w0):
        for b in range(2):
            w = w0 + b

            @pl.when(w + 1 < nwin)
            def _():
                @pl.when(w >= 1)
                def _():
                    wait(write(w - 1, 1 - b))

                start(read(w + 1, 1 - b))

            wait(read(w, b))
            start(write(w, b))

    wait(write(nwin - 2, 0))
    wait(write(nwin - 1, 1))


def _sc_scatter_rows(src, pos, n_out):
    t_rows, d = src.shape
    nw, nwin, picks, win = pos.shape
    assert nw == SC_WORKERS and nwin % 2 == 0 and t_rows == nw * nwin * win
    mesh = plsc.VectorSubcoreMesh(core_axis_name="c", subcore_axis_name="s")

    @functools.partial(
        pl.kernel, mesh=mesh, out_type=jax.ShapeDtypeStruct((n_out, d), src.dtype),
        scratch_types=[pltpu.VMEM((nwin, picks, win), jnp.int32),
                       pltpu.VMEM((2, win, d), src.dtype),
                       pltpu.SemaphoreType.DMA((2,)), pltpu.SemaphoreType.DMA((2,))])
    def scatter(src_hbm, pos_hbm, out_hbm, idx_v, rows_v, rsem, wsem):
        wid = lax.axis_index("s") * SC_CORES + lax.axis_index("c")
        t0 = wid * (nwin * win)
        pltpu.sync_copy(pos_hbm.at[wid], idx_v)

        def read(w, slot):
            return [pltpu.make_async_copy(src_hbm.at[pl.ds(t0 + w * win, win)], rows_v.at[slot],
                                          rsem.at[slot])]

        def write(w, slot):
            return [pltpu.make_async_copy(rows_v.at[slot], out_hbm.at[idx_v.at[w, k]],
                                          wsem.at[slot]) for k in range(picks)]

        _sc_row_pipeline(nwin, read, write)

    return scatter(src, pos)


def _sc_gather_rows(table, pos):
    _, d = table.shape
    nw, nwin, win = pos.shape
    assert nw == SC_WORKERS and nwin % 2 == 0
    per = nwin * win
    mesh = plsc.VectorSubcoreMesh(core_axis_name="c", subcore_axis_name="s")

    @functools.partial(
        pl.kernel, mesh=mesh, out_type=jax.ShapeDtypeStruct((nw * per, d), table.dtype),
        scratch_types=[pltpu.VMEM((nwin, win), jnp.int32), pltpu.VMEM((2, win, d), table.dtype),
                       pltpu.SemaphoreType.DMA((2,)), pltpu.SemaphoreType.DMA((2,))])
    def gather(table_hbm, pos_hbm, out_hbm, idx_v, rows_v, rsem, wsem):
        wid = lax.axis_index("s") * SC_CORES + lax.axis_index("c")
        base = wid * per
        pltpu.sync_copy(pos_hbm.at[wid], idx_v)

        def read(w, slot):
            return [pltpu.make_async_copy(table_hbm.at[idx_v.at[w]], rows_v.at[slot],
                                          rsem.at[slot])]

        def write(w, slot):
            return [pltpu.make_async_copy(rows_v.at[slot], out_hbm.at[pl.ds(base + w * win, win)],
                                          wsem.at[slot])]

        _sc_row_pipeline(nwin, read, write)

    return gather(table, pos)


MOE_TILE = 512


def _moe_sparse(v_all, route, counts, layer, w1, b1g, b1l, w2, b2):
    t_rows, dh = v_all.shape
    n_exp = w1.shape[1]
    pairs = TOP_K * t_rows
    n_tiles = pairs // MOE_TILE + n_exp
    route_t = route.T
    expert = route_t[0:TOP_K].astype(jnp.int32)
    slot = route_t[TOP_K:2 * TOP_K].astype(jnp.int32)
    cnt = counts[0, :n_exp].astype(jnp.int32)
    tiles_per = (cnt + MOE_TILE - 1) // MOE_TILE
    tile_end = jnp.cumsum(tiles_per)
    tile_start = tile_end - tiles_per
    pos = slot
    for e in range(n_exp):
        pos = pos + jnp.where(expert == e, tile_start[e] * MOE_TILE, 0)
    nwin = pairs // (SC_WORKERS * SC_WINDOW)
    pos_km = pos.reshape(SC_WORKERS, nwin, SC_WINDOW)
    nwin_t = t_rows // (SC_WORKERS * SC_TOKEN_WINDOW)
    pos_tm = pos.reshape(TOP_K, SC_WORKERS, nwin_t, SC_TOKEN_WINDOW).transpose(1, 2, 0, 3)
    tile_ids = jnp.arange(n_tiles, dtype=jnp.int32)[:, None]
    owns = (tile_ids >= tile_start[None, :]) & (tile_ids < tile_end[None, :])
    experts = jnp.arange(n_exp, dtype=jnp.int32)[None, :]
    last_used = jnp.max(jnp.where(tiles_per > 0, experts[0], 0))
    used = jnp.any(owns, axis=1)
    te = jnp.where(used, jnp.sum(jnp.where(owns, experts, 0), axis=1), last_used).astype(jnp.int32)
    rows_left = cnt[None, :] - (tile_ids - tile_start[None, :]) * MOE_TILE
    tile_rows = jnp.sum(jnp.where(owns, jnp.clip(rows_left, 0, MOE_TILE), 0), axis=1)
    tile_rows = tile_rows.astype(jnp.int32)
    tile_first = jnp.any(owns & (tile_ids == tile_start[None, :]), axis=1).astype(jnp.int32)
    later = (experts > experts.T) & (tiles_per[None, :] > 0)
    next_e = jnp.min(jnp.where(later, experts, n_exp), axis=1)
    next_e = jnp.where(next_e == n_exp, -1, next_e)
    tile_next = jnp.where(used, jnp.sum(jnp.where(owns, next_e[None, :], 0), axis=1), -1)
    tile_next = tile_next.astype(jnp.int32)
    xs = _sc_scatter_rows(v_all, pos_tm, n_tiles * MOE_TILE)
    ys = _moe_rows(xs, te, tile_rows, tile_first, tile_next, layer, w1, b1g, b1l, w2, b2, MOE_TILE)
    return _sc_gather_rows(ys, pos_km).reshape(TOP_K, t_rows, dh)


@functools.lru_cache(maxsize=None)
def _sincos_2d(rows, cols, dim):
    quarter = dim // 4
    omega = 1.0 / (10000.0 ** (np.arange(quarter, dtype=np.float64) / quarter))

    def emb1d(n):
        ang = np.arange(n, dtype=np.float64)[:, None] * omega
        return np.concatenate([np.sin(ang), np.cos(ang)], axis=-1)

    er = np.broadcast_to(emb1d(rows)[:, None], (rows, cols, dim // 2))
    ec = np.broadcast_to(emb1d(cols)[None], (rows, cols, dim // 2))
    return np.concatenate([er, ec], axis=-1).reshape(rows * cols, dim).astype(np.float32)


def _gate_layout(gates, n_ch):
    bsz, length, _ = gates.shape
    g = gates[..., :n_ch].reshape(bsz, length // DN_CHUNK, DN_CHUNK, n_ch)
    return g.transpose(0, 3, 1, 2)


def kernel(x, c, ctx, c_ctx, ada_w, ada_b, ln_g, ln_b, ev_w_in, ev_w_out, dn_conv_w, dn_a_log, dn_dt_bias, dn_norm_g, hy_conv_w, hy_conv_b, hy_w_in, hy_b_in, hy_w_mid, hy_b_mid, hy_w_out, hy_freq, hy_skip, od_w_in, od_b_in, lru_conv_w, lru_conv_b, lru_wa, lru_ba, lru_wx, lru_bx, lru_a_param, od_w_out, od_b_out, router_w, router_b, moe_w1, moe_b1, moe_w2, moe_b2):
    bsz, length, d = x.shape
    ctx_len = ctx.shape[1]
    pos = jnp.asarray(_sincos_2d(length // GRID_W, GRID_W, d))

    cond = jnp.zeros((16, d), F32).at[:bsz].set(c).at[bsz].set(c_ctx)
    mod = _modulation(cond, ada_w, ada_b).reshape(DEPTH, 16, 6, d)

    def lat_mod(layer, k):
        return mod[layer, :bsz, k][:, None, :]

    def ctx_mod(layer, k):
        return jnp.broadcast_to(mod[layer, bsz, k][None, None, :], (bsz, 1, d))

    hc = ctx
    for layer in range(DEPTH):
        last = layer == DEPTH - 1
        j = layer // 2
        if layer % 2 == 0:
            dn_qk = DN_HEADS * DN_DK
            dn_qkv = 3 * dn_qk
            dn_in = dn_qkv + dn_qk + 4 * DN_HEADS
            w_in = ev_w_in[j]
            gate_w = jnp.zeros((d, LANES), F32).at[:, :4 * DN_HEADS].set(w_in[:, dn_qkv + dn_qk:dn_in])
            w_cat = jnp.concatenate([w_in[:, :dn_qkv + dn_qk], gate_w, w_in[:, dn_in:]],
                                    axis=1).astype(BF16)
            hy_in = w_in.shape[1] - dn_in
            widths = (dn_qkv, dn_qk, LANES, hy_in)
            qkv_l, z_l, gt_l, phy_l = _inproj(x, lat_mod(layer, 1), lat_mod(layer, 0), pos,
                                               w_cat, None, widths, ROW_TILE)
            qkv_c, z_c, gt_c, phy_c = _inproj(hc, ctx_mod(layer, 1), ctx_mod(layer, 0), None,
                                               w_cat, None, widths, ROW_TILE)
            gates = jnp.concatenate([_gate_layout(gt_c, 4 * DN_HEADS),
                                     _gate_layout(gt_l, 4 * DN_HEADS)], axis=2)
            n_chunks = gates.shape[2]
            gates = jnp.pad(gates, ((0, 0), (0, 0), (0, -n_chunks % 16), (0, 0)))
            dn_l, dn_c = _deltanet(qkv_l, z_l, qkv_c, z_c, gates, dn_conv_w[j], dn_a_log[j],
                                   dn_dt_bias[j], dn_norm_g[j])
            filt = (hy_w_in[j], hy_b_in[j], hy_w_mid[j], hy_b_mid[j], hy_w_out[j], hy_freq[j])
            hy_l = _hyena(phy_l, hy_conv_w[j], hy_conv_b[j], filt, hy_skip[j])
            w_out = ev_w_out[j].astype(BF16)
            half = dn_l.shape[2]
            acts_l, ws, bias = (dn_l, hy_l), (w_out[:half], w_out[half:]), None
            acts_c = None
            if not last:
                hy_c = _hyena(phy_c, hy_conv_w[j], hy_conv_b[j], filt, hy_skip[j])
                acts_c = (dn_c, hy_c)
        else:
            w_in = od_w_in[j].astype(BF16)
            width = w_in.shape[1] // 2
            b_in = od_b_in[j].reshape(1, 2 * width)
            xb_l, yb_l = _inproj(x, lat_mod(layer, 1), lat_mod(layer, 0), pos, w_in, b_in,
                                 (width, width), ROW_TILE)
            xb_c, _ = _inproj(hc, ctx_mod(layer, 1), ctx_mod(layer, 0), None, w_in, b_in,
                              (width, width), ROW_TILE)
            act_l = _rglru(xb_l, yb_l, xb_c, lru_conv_w[j], lru_conv_b[j], lru_wa[j], lru_ba[j],
                           lru_wx[j], lru_bx[j], lru_a_param[j])
            acts_l, ws, bias = (act_l,), (od_w_out[j].astype(BF16),), od_b_out[j]
            acts_c = None
            assert last, "context outputs of the RG-LRU layer are only needed before the last layer"


        x1, v, route, counts = _post_mixer(acts_l, ws, bias, x, lat_mod(layer, 2),
                                           lat_mod(layer, 4), lat_mod(layer, 3), ln_g[layer, 0],
                                           ln_b[layer, 0], router_w[layer], router_b[layer],
                                           jnp.zeros((1, LANES), F32), ROW_TILE)
        v_all = v.reshape(bsz * length, d // 2)
        route = route.reshape(bsz * length, LANES)
        if not last:
            hc1, vc, route_c, counts = _post_mixer(acts_c, ws, bias, hc, ctx_mod(layer, 2),
                                                   ctx_mod(layer, 4), ctx_mod(layer, 3),
                                                   ln_g[layer, 0], ln_b[layer, 0], router_w[layer],
                                                   router_b[layer], counts, ROW_TILE)
            v_all = jnp.concatenate([v_all, vc.reshape(bsz * ctx_len, d // 2)], axis=0)
            route = jnp.concatenate([route, route_c.reshape(bsz * ctx_len, LANES)], axis=0)
        yg = _moe_sparse(v_all, route, counts, layer, moe_w1, moe_b1[:, :, None, 0::2],
                         moe_b1[:, :, None, 1::2], moe_w2, moe_b2[:, :, None, :])
        x = _moe_combine(yg, route, 0, x1, lat_mod(layer, 5), ln_g[layer, 1], ln_b[layer, 1],
                         ROW_TILE)
        if not last:
            hc = _moe_combine(yg, route, bsz * length, hc1, ctx_mod(layer, 5), ln_g[layer, 1],
                              ln_b[layer, 1], ROW_TILE)
    return x
```

```python
import functools
import math

import numpy as np
import jax
import jax.numpy as jnp
from jax import lax
from jax.experimental import pallas as pl
from jax.experimental.pallas import tpu as pltpu
from jax.experimental.pallas import tpu_sc as plsc

F32 = jnp.float32
BF16 = jnp.bfloat16

VMEM_LIMIT_BYTES = 56 * 1024 * 1024
LANES = 128
ROW_TILE = 512
FILTER_ROW_TILE = 256
MOD_COL_TILE = 1536

DEPTH = 2
GRID_W = 64
DEEPNORM_ALPHA = (2.0 * DEPTH) ** 0.25
LN_EPS = 1e-5
RMS_EPS = 1e-6

DN_HEADS = 4
DN_DK = 128
DN_CHUNK = 64
DN_CONV = 4

HY_EMB = 33
HY_TARGET = 1e-2
HY_MIN_DECAY = math.log(HY_TARGET) / 1.5
HY_MAX_DECAY = math.log(HY_TARGET) / 0.3
HY_CONV = 3

LRU_HEADS = 4
LRU_C = 8.0
LRU_CONV = 4

N_EXPERTS = 32
TOP_K = 4
SWIGLU_ALPHA = 1.702
SWIGLU_LIMIT = 7.0


def _params(*sem):
    return pltpu.CompilerParams(dimension_semantics=sem, vmem_limit_bytes=VMEM_LIMIT_BYTES)


def _dot(a, b):
    return jnp.dot(a, b, preferred_element_type=F32)


def _dot_nt(a, b):
    return lax.dot_general(a, b, (((1,), (1,)), ((), ())), preferred_element_type=F32)


def _split(a):
    hi = a.astype(BF16)
    lo = (a - hi.astype(F32)).astype(BF16)
    return hi, lo


def _dot3(a, b):
    ah, al = _split(a)
    bh, bl = _split(b)
    return _dot(ah, bh) + _dot(ah, bl) + _dot(al, bh)


def _silu(x):
    return x * (1.0 / (1.0 + jnp.exp(-x)))


def _sigmoid(x):
    return 1.0 / (1.0 + jnp.exp(-x))


def _softplus(x):
    return jnp.maximum(x, 0.0) + jnp.log(1.0 + jnp.exp(-jnp.abs(x)))


def _layer_norm(x, g, b):
    mu = jnp.mean(x, axis=-1, keepdims=True)
    xc = x - mu
    var = jnp.mean(xc * xc, axis=-1, keepdims=True)
    return xc * lax.rsqrt(var + LN_EPS) * g + b


def _pack_bf16_pairs(x):
    w = x.shape[1] // 2
    lo = pltpu.bitcast(x[:, :w].astype(BF16).astype(F32), jnp.uint32) >> 16
    hi = pltpu.bitcast(x[:, w:].astype(BF16).astype(F32), jnp.uint32) & jnp.uint32(0xFFFF0000)
    return lo | hi


def _unpack_bf16_pairs(p):
    lo = pltpu.bitcast(p << 16, F32)
    hi = pltpu.bitcast(p & jnp.uint32(0xFFFF0000), F32)
    return jnp.concatenate([lo, hi], axis=1)


def _shift_rows(x, s):
    if s == 0:
        return x
    n = x.shape[0]
    rows = lax.broadcasted_iota(jnp.int32, x.shape, 0)
    valid = (rows >= s) if s > 0 else (rows < n + s)
    return jnp.where(valid, pltpu.roll(x, s % n, axis=0), 0.0)


def _depthwise_conv(x, w, pad_left):
    acc = None
    for i in range(w.shape[0]):
        term = _shift_rows(x, pad_left - i) * w[i:i + 1, :]
        acc = term if acc is None else acc + term
    return acc


def _mod_kernel(c_ref, w_ref, b_ref, o_ref):
    o_ref[0] = _dot3(_silu(c_ref[...]), w_ref[0]) + b_ref[0]


def _modulation(cond, ada_w, ada_b):
    depth, d, n = ada_w.shape
    rows = cond.shape[0]
    tn = MOD_COL_TILE
    return pl.pallas_call(
        _mod_kernel,
        grid=(depth, n // tn),
        in_specs=[
            pl.BlockSpec((rows, d), lambda l, j: (0, 0)),
            pl.BlockSpec((1, d, tn), lambda l, j: (l, 0, j)),
            pl.BlockSpec((1, 1, tn), lambda l, j: (l, 0, j)),
        ],
        out_specs=pl.BlockSpec((1, rows, tn), lambda l, j: (l, 0, j)),
        out_shape=jax.ShapeDtypeStruct((depth, rows, n), F32),
        compiler_params=_params("parallel", "parallel"),
        name="modulation",
    )(cond, ada_w, ada_b.reshape(depth, 1, n))


def _inproj_kernel(*refs, splits, has_pos, has_bias):
    x_ref, sc_ref, sh_ref = refs[:3]
    i = 3
    pos_ref = None
    if has_pos:
        pos_ref = refs[i]
        i += 1
    w_ref = refs[i]
    i += 1
    b_ref = None
    if has_bias:
        b_ref = refs[i]
        i += 1
    o_refs = refs[i:]
    u = x_ref[0] * (1.0 + sc_ref[0]) + sh_ref[0]
    if has_pos:
        u = u + pos_ref[...]
    ub = u.astype(BF16)
    for o_ref, (s, e) in zip(o_refs, splits):
        acc = _dot(ub, w_ref[:, s:e])
        if has_bias:
            acc = acc + b_ref[:, s:e]
        o_ref[0] = acc


def _inproj(x, sc, sh, pos, w, b, widths, tm):
    bsz, length, d = x.shape
    n = w.shape[1]
    splits, s = [], 0
    for wd in widths:
        splits.append((s, s + wd))
        s += wd
    assert s == n
    tm = min(tm, length)
    in_specs = [
        pl.BlockSpec((1, tm, d), lambda bi, i: (bi, i, 0)),
        pl.BlockSpec((1, 1, d), lambda bi, i: (bi, 0, 0)),
        pl.BlockSpec((1, 1, d), lambda bi, i: (bi, 0, 0)),
    ]
    args = [x, sc, sh]
    if pos is not None:
        in_specs.append(pl.BlockSpec((tm, d), lambda bi, i: (i, 0)))
        args.append(pos)
    in_specs.append(pl.BlockSpec((d, n), lambda bi, i: (0, 0)))
    args.append(w)
    if b is not None:
        in_specs.append(pl.BlockSpec((1, n), lambda bi, i: (0, 0)))
        args.append(b)
    return pl.pallas_call(
        functools.partial(_inproj_kernel, splits=tuple(splits), has_pos=pos is not None,
                          has_bias=b is not None),
        grid=(bsz, length // tm),
        in_specs=in_specs,
        out_specs=[pl.BlockSpec((1, tm, wd), lambda bi, i: (bi, i, 0)) for wd in widths],
        out_shape=[jax.ShapeDtypeStruct((bsz, length, wd), F32) for wd in widths],
        compiler_params=_params("parallel", "parallel"),
        name="inproj",
    )(*args)


def _unit_tri_inverses(mats, lower):
    n = mats[0].shape[0]
    nb = 16
    np_ = len(mats)
    ii = lax.broadcasted_iota(jnp.int32, (n, n), 0)
    jj = lax.broadcasted_iota(jnp.int32, (n, n), 1)
    same16 = (ii // nb) == (jj // nb)
    same32 = (ii // (2 * nb)) == (jj // (2 * nb))
    dgs = []
    for a in mats:
        ad = jnp.where(same16, a, 0.0)
        dgs.append(ad[0:nb] + ad[nb:2 * nb] + ad[2 * nb:3 * nb] + ad[3 * nb:4 * nb])
    dg = jnp.concatenate(dgs, axis=0)
    rr = lax.broadcasted_iota(jnp.int32, dg.shape, 0)
    ll = lax.broadcasted_iota(jnp.int32, dg.shape, 1)
    xd = jnp.where(rr % nb == ll % nb, 1.0, 0.0)
    blk0 = (ll // nb) * nb
    for s in (range(nb - 1) if lower else range(nb - 1, 0, -1)):
        col = jnp.take_along_axis(dg, blk0 + s, axis=1)
        row = jnp.concatenate(
            [jnp.broadcast_to(xd[p * nb + s:p * nb + s + 1, :], (nb, n)) for p in range(np_)], axis=0)
        xd = xd - col * row
    ds = [jnp.where(same16, jnp.concatenate([xd[p * nb:(p + 1) * nb]] * (n // nb), axis=0), 0.0)
          for p in range(np_)]
    lvl1 = same32 & jnp.logical_not(same16)
    t1 = [_dot3(d, jnp.where(lvl1, a, 0.0)) for d, a in zip(ds, mats)]
    x1 = [d - _dot3(t, d) for d, t in zip(ds, t1)]
    x1b = [x.astype(BF16) for x in x1]
    t2 = [_dot(xb, jnp.where(same32, 0.0, a).astype(BF16)) for xb, a in zip(x1b, mats)]
    return [x - _dot(t.astype(BF16), xb) for x, t, xb in zip(x1, t2, x1b)]


def _dn_kernel(alog_ref, dtb_ref,
               ql_ref, kl_ref, vl_ref, zl_ref, qc_ref, kc_ref, vc_ref, zc_ref, gt_ref,
               cwq_ref, cwk_ref, cwv_ref, ng_ref,
               yl_ref, yc_ref,
               qn_s, kn_s, vn_s, gc_s, bt_s, nq_s, c_s, gl_s, o_s,
               *, ctx_len, lat_len):
    h = pl.program_id(1)
    csz = DN_CHUNK
    nc_ctx = ctx_len // csz
    nc = (ctx_len + lat_len) // csz

    def prep(src_ref, cw_ref, kind):
        t = _silu(_depthwise_conv(src_ref[0], cw_ref[...], DN_CONV // 2))
        if kind == "v":
            return t
        t = t * lax.rsqrt(jnp.sum(t * t, axis=-1, keepdims=True) + RMS_EPS)
        return t * (DN_DK ** -0.5) if kind == "q" else t

    qn_s[0:ctx_len, :] = prep(qc_ref, cwq_ref, "q")
    qn_s[ctx_len:, :] = prep(ql_ref, cwq_ref, "q")
    kn_s[0:ctx_len, :] = prep(kc_ref, cwk_ref, "k")
    kn_s[ctx_len:, :] = prep(kl_ref, cwk_ref, "k")
    vn_s[0:ctx_len, :] = prep(vc_ref, cwv_ref, "v")
    vn_s[ctx_len:, :] = prep(vl_ref, cwv_ref, "v")

    ii = lax.broadcasted_iota(jnp.int32, (csz, csz), 0)
    jj = lax.broadcasted_iota(jnp.int32, (csz, csz), 1)
    eye = ii == jj
    for d in range(2):
        graw = gt_ref[0, d * 2 * DN_HEADS + h]
        braw = gt_ref[0, d * 2 * DN_HEADS + DN_HEADS + h]
        a_neg = -jnp.exp(jnp.zeros_like(graw) + alog_ref[d, h])
        g = a_neg * _softplus(graw + dtb_ref[d, h])
        tri = jnp.where((ii <= jj) if d == 0 else (ii >= jj), 1.0, 0.0).astype(BF16)
        g1 = g.astype(BF16)
        r1 = g - g1.astype(F32)
        g2 = r1.astype(BF16)
        g3 = (r1 - g2.astype(F32)).astype(BF16)
        gc_s[d] = _dot(g1, tri) + _dot(g2, tri) + _dot(g3, tri)
        bt_s[d] = _sigmoid(braw)

    o_s[...] = jnp.zeros_like(o_s)

    group = nc

    def chunk_prep(gi, carry):
        ns = [gi * group + c for c in range(group)]
        r0s = [pl.multiple_of(n * csz, csz) for n in ns]
        qs = [qn_s[pl.ds(r0, csz), :] for r0 in r0s]
        ks = [kn_s[pl.ds(r0, csz), :] for r0 in r0s]
        vs = [vn_s[pl.ds(r0, csz), :] for r0 in r0s]
        kbfs = [k.astype(BF16) for k in ks]
        qks = [_dot_nt(q.astype(BF16), kbf) for q, kbf in zip(qs, kbfs)]
        for d in range(2):
            incl = (ii >= jj) if d == 0 else (ii <= jj)
            strict = (ii > jj) if d == 0 else (ii < jj)
            grs = [gc_s[d, pl.ds(n, 1), :] for n in ns]
            grows = [jnp.broadcast_to(gr, (csz, csz)) for gr in grs]
            gcols = [jnp.sum(jnp.where(eye, grow, 0.0), axis=1, keepdims=True) for grow in grows]
            bcols = [jnp.sum(jnp.where(eye, jnp.broadcast_to(bt_s[d, pl.ds(n, 1), :], (csz, csz)),
                                       0.0), axis=1, keepdims=True) for n in ns]
            decays = [jnp.where(incl, jnp.exp(jnp.where(incl, gcol - grow, 0.0)), 0.0)
                      for gcol, grow in zip(gcols, grows)]
            kbs = [k * bcol for k, bcol in zip(ks, bcols)]
            amats = [jnp.where(strict, _dot_nt(kb.astype(BF16), kbf) * decay, 0.0)
                     for kb, kbf, decay in zip(kbs, kbfs, decays)]
            tbs = [t.astype(BF16) for t in _unit_tri_inverses(amats, lower=(d == 0))]
            cs = range(group)
            egs = [jnp.exp(gcols[c]) for c in cs]
            ubs = [_dot(tbs[c], (vs[c] * bcols[c]).astype(BF16)).astype(BF16) for c in cs]
            wbs = [_dot(tbs[c], (kbs[c] * egs[c]).astype(BF16)).astype(BF16) for c in cs]
            attns = [jnp.where(incl, qks[c] * decays[c], 0.0).astype(BF16) for c in cs]
            glasts = [grs[c][:, csz - 1:csz] if d == 0 else grs[c][:, 0:1] for c in cs]
            kdts = [(ks[c] * jnp.exp(glasts[c] - gcols[c])).T.astype(BF16) for c in cs]
            nmats = [_dot(kdts[c], wbs[c]).astype(BF16) for c in cs]
            qmats = [(qs[c] * egs[c] - _dot(attns[c], wbs[c])).astype(BF16) for c in cs]
            cmats = [_dot(kdts[c], ubs[c]) for c in cs]
            omats = [_dot(attns[c], ubs[c]) for c in cs]
            for c in cs:
                n = ns[c]
                nq_s[d, n, 0:DN_DK, :] = nmats[c]
                nq_s[d, n, DN_DK:DN_DK + csz, :] = qmats[c]
                c_s[d, n] = cmats[c]
                o_s[pl.ds(r0s[c], csz), :] += omats[c]
                gl_s[d, pl.ds(n, 1), :] = jnp.broadcast_to(jnp.exp(glasts[c]), (1, LANES))
        return carry

    lax.fori_loop(0, nc // group, chunk_prep, 0)

    def step(i, states):
        new_states = []
        for d in range(2):
            if d == 0:
                n = i
            else:
                n = jnp.where(i < nc_ctx, nc_ctx - 1 - i, nc + nc_ctx - 1 - i)
            r0 = pl.multiple_of(n * csz, csz)
            s = states[d]
            r = _dot(nq_s[d, n], s.astype(BF16))
            o_s[pl.ds(r0, csz), :] += r[DN_DK:DN_DK + csz]
            new_states.append(s * gl_s[d, pl.ds(n, 1), :] - r[0:DN_DK] + c_s[d, n])
        return tuple(new_states)

    zero = jnp.zeros((DN_DK, DN_DK), F32)
    lax.fori_loop(0, nc, step, (zero, zero), unroll=2)

    def gated_norm(o, z):
        o = o * lax.rsqrt(jnp.mean(o * o, axis=-1, keepdims=True) + RMS_EPS) * ng_ref[...]
        return (o * _silu(z)).astype(yl_ref.dtype)

    yc_ref[0] = gated_norm(o_s[0:ctx_len, :], zc_ref[0])
    yl_ref[0] = gated_norm(o_s[ctx_len:, :], zl_ref[0])


def _deltanet(qkv_l, z_l, qkv_c, z_c, gates, conv_w, a_log, dt_bias, norm_g):
    bsz, lat_len, _ = qkv_l.shape
    ctx_len = qkv_c.shape[1]
    tot = ctx_len + lat_len
    nc = tot // DN_CHUNK
    ncp = gates.shape[2]
    hd = DN_DK
    nh = DN_HEADS

    def col(off):
        return lambda b, h: (b, 0, off + h)

    def wcol(off):
        return lambda b, h: (0, off + h)

    smem = pl.BlockSpec(memory_space=pltpu.SMEM)
    in_specs = [
        smem, smem,
        pl.BlockSpec((1, lat_len, hd), col(0)),
        pl.BlockSpec((1, lat_len, hd), col(nh)),
        pl.BlockSpec((1, lat_len, hd), col(2 * nh)),
        pl.BlockSpec((1, lat_len, hd), col(0)),
        pl.BlockSpec((1, ctx_len, hd), col(0)),
        pl.BlockSpec((1, ctx_len, hd), col(nh)),
        pl.BlockSpec((1, ctx_len, hd), col(2 * nh)),
        pl.BlockSpec((1, ctx_len, hd), col(0)),
        pl.BlockSpec((1, 4 * nh, ncp, DN_CHUNK), lambda b, h: (b, 0, 0, 0)),
        pl.BlockSpec((DN_CONV, hd), wcol(0)),
        pl.BlockSpec((DN_CONV, hd), wcol(nh)),
        pl.BlockSpec((DN_CONV, hd), wcol(2 * nh)),
        pl.BlockSpec((1, hd), lambda b, h: (0, 0)),
    ]
    scratch = [
        pltpu.VMEM((tot, hd), F32), pltpu.VMEM((tot, hd), F32), pltpu.VMEM((tot, hd), F32),
        pltpu.VMEM((2, ncp, DN_CHUNK), F32), pltpu.VMEM((2, ncp, DN_CHUNK), F32),
        pltpu.VMEM((2, nc, hd + DN_CHUNK, hd), BF16),
        pltpu.VMEM((2, nc, hd, hd), F32),
        pltpu.VMEM((2, nc, LANES), F32),
        pltpu.VMEM((tot, hd), F32),
    ]
    return pl.pallas_call(
        functools.partial(_dn_kernel, ctx_len=ctx_len, lat_len=lat_len),
        grid=(bsz, nh),
        in_specs=in_specs,
        out_specs=[pl.BlockSpec((1, lat_len, hd), col(0)),
                   pl.BlockSpec((1, ctx_len, hd), col(0))],
        out_shape=[jax.ShapeDtypeStruct((bsz, lat_len, nh * hd), BF16),
                   jax.ShapeDtypeStruct((bsz, ctx_len, nh * hd), BF16)],
        scratch_shapes=scratch,
        compiler_params=_params("parallel", "parallel"),
        name="deltanet",
    )(a_log, dt_bias, qkv_l, qkv_l, qkv_l, z_l, qkv_c, qkv_c, qkv_c, z_c, gates,
      conv_w, conv_w, conv_w, norm_g.reshape(1, hd))


def _hy_filter_kernel(z_ref, win_ref, bin_ref, wmid_ref, bmid_ref, wout_ref, freq_ref, dec_ref,
                      o_ref):
    freq = freq_ref[...]
    hcur = jnp.sin(freq * (_dot3(z_ref[...], win_ref[...]) + bin_ref[...]))
    for i in range(wmid_ref.shape[0]):
        hcur = jnp.sin(freq * (_dot3(hcur, wmid_ref[i]) + bmid_ref[i]))
    o_ref[...] = _dot3(hcur, wout_ref[...]) * dec_ref[...]


def _hy_filter(z, w_in, b_in, w_mid, b_mid, w_out, freq, dec2):
    length = z.shape[0]
    n_out = w_out.shape[1]
    tl = min(FILTER_ROW_TILE, length)

    def whole(a):
        return pl.BlockSpec(a.shape, lambda i: (0,) * a.ndim)

    return pl.pallas_call(
        _hy_filter_kernel,
        grid=(length // tl,),
        in_specs=[pl.BlockSpec((tl, z.shape[1]), lambda i: (i, 0)),
                  whole(w_in), whole(b_in), whole(w_mid), whole(b_mid), whole(w_out), whole(freq),
                  pl.BlockSpec((tl, n_out), lambda i: (i, 0))],
        out_specs=pl.BlockSpec((tl, n_out), lambda i: (i, 0)),
        out_shape=jax.ShapeDtypeStruct((length, n_out), F32),
        compiler_params=_params("parallel"),
        name="hyena_filter",
    )(z, w_in, b_in, w_mid, b_mid, w_out, freq, dec2)


def _filt_spec_kernel(fc_ref, fs_ref, h_ref, kc_ref, ks_ref, *, width):
    j = pl.program_id(0)
    hmat = h_ref[...]
    rows = lax.broadcasted_iota(jnp.int32, hmat.shape, 0)
    cols = lax.broadcasted_iota(jnp.int32, hmat.shape, 1)
    hmat = jnp.where((rows == 0) & (cols >= width), 0.0, hmat)
    hh, hl = _split(hmat)
    c = _dot(fc_ref[...], hh) + _dot(fc_ref[...], hl)
    s = _dot(fs_ref[...], hh) + _dot(fs_ref[...], hl)
    kc_ref[...] = c[:, :width] + c[:, width:]
    orow = lax.broadcasted_iota(jnp.int32, (c.shape[0], width), 0)
    sign = jnp.where((orow == 0) & (j == 0), 1.0, -1.0)
    ks_ref[...] = s[:, :width] + sign * s[:, width:]


def _filt_spec(fwd, hfilt, tf):
    length, two_w = hfilt.shape
    width = two_w // 2
    nt = length // tf
    return pl.pallas_call(
        functools.partial(_filt_spec_kernel, width=width),
        grid=(nt,),
        in_specs=[pl.BlockSpec((tf, length), lambda j: (j, 0)),
                  pl.BlockSpec((tf, length), lambda j: (nt + j, 0)),
                  pl.BlockSpec((length, two_w), lambda j: (0, 0))],
        out_specs=[pl.BlockSpec((tf, width), lambda j: (j, 0)),
                   pl.BlockSpec((tf, width), lambda j: (j, 0))],
        out_shape=[jax.ShapeDtypeStruct((length, width), F32)] * 2,
        compiler_params=_params("arbitrary"),
        name="hyena_filter_spectrum",
    )(fwd, fwd, hfilt)


def _hy_prep_kernel(x0_ref, x1_ref, v_ref, w0_ref, w1_ref, w2_ref, b0_ref, b1_ref, b2_ref,
                    x0o_ref, vvo_ref):
    x0 = _depthwise_conv(x0_ref[0], w0_ref[...], HY_CONV // 2) + b0_ref[...]
    x1 = _depthwise_conv(x1_ref[0], w1_ref[...], HY_CONV // 2) + b1_ref[...]
    v = _depthwise_conv(v_ref[0], w2_ref[...], HY_CONV // 2) + b2_ref[...]
    x0o_ref[0] = x0
    vvo_ref[0] = v * x1


def _hy_prep(p_hy, conv_w, conv_b):
    bsz, length, three_w = p_hy.shape
    width = three_w // 3
    nb = width // LANES

    def col(off):
        return lambda b, j: (b, 0, off + j)

    def wcol(off):
        return lambda b, j: (0, off + j)

    k = conv_w.shape[0]
    return pl.pallas_call(
        _hy_prep_kernel,
        grid=(bsz, nb),
        in_specs=[pl.BlockSpec((1, length, LANES), col(0)),
                  pl.BlockSpec((1, length, LANES), col(nb)),
                  pl.BlockSpec((1, length, LANES), col(2 * nb)),
                  pl.BlockSpec((k, LANES), wcol(0)),
                  pl.BlockSpec((k, LANES), wcol(nb)),
                  pl.BlockSpec((k, LANES), wcol(2 * nb)),
                  pl.BlockSpec((1, LANES), wcol(0)),
                  pl.BlockSpec((1, LANES), wcol(nb)),
                  pl.BlockSpec((1, LANES), wcol(2 * nb))],
        out_specs=[pl.BlockSpec((1, length, LANES), col(0)),
                   pl.BlockSpec((1, length, LANES), col(0))],
        out_shape=[jax.ShapeDtypeStruct((bsz, length, width), F32)] * 2,
        compiler_params=_params("parallel", "parallel"),
        name="hyena_prep",
    )(p_hy, p_hy, p_hy, conv_w, conv_w, conv_w, conv_b, conv_b, conv_b)


def _dft_fwd_kernel(fc_ref, fs_ref, v_ref, kc_ref, ks_ref, yc_ref, ys_ref, vb_s):
    j = pl.program_id(1)

    @pl.when(j == 0)
    def _():
        vb_s[...] = v_ref[0].astype(BF16)

    uc = _dot(fc_ref[...], vb_s[...])
    us = _dot(fs_ref[...], vb_s[...])
    kc = kc_ref[...]
    ks = ks_ref[...]
    rows = lax.broadcasted_iota(jnp.int32, uc.shape, 0)
    special = (rows == 0) & (j == 0)
    yc_ref[0] = (uc * kc - jnp.where(special, 0.0, us * ks)).astype(BF16)
    ys_ref[0] = jnp.where(special, us * ks, uc * ks + us * kc).astype(BF16)


def _dft_fwd(fwd, vv, kc, ks, tf):
    bsz, length, width = vv.shape
    nt = length // tf
    return pl.pallas_call(
        _dft_fwd_kernel,
        grid=(bsz, nt),
        in_specs=[pl.BlockSpec((tf, length), lambda b, j: (j, 0)),
                  pl.BlockSpec((tf, length), lambda b, j: (nt + j, 0)),
                  pl.BlockSpec((1, length, width), lambda b, j: (b, 0, 0)),
                  pl.BlockSpec((tf, width), lambda b, j: (j, 0)),
                  pl.BlockSpec((tf, width), lambda b, j: (j, 0))],
        out_specs=[pl.BlockSpec((1, tf, width), lambda b, j: (b, j, 0)),
                   pl.BlockSpec((1, tf, width), lambda b, j: (b, j, 0))],
        out_shape=[jax.ShapeDtypeStruct((bsz, length, width), BF16)] * 2,
        scratch_shapes=[pltpu.VMEM((length, width), BF16)],
        compiler_params=_params("parallel", "arbitrary"),
        name="hyena_dft_fwd",
    )(fwd, fwd, vv, kc, ks)


def _dft_inv_kernel(ic_ref, is_ref, yc_ref, ys_ref, vv_ref, x0_ref, skip_ref, o_ref):
    y = _dot(ic_ref[...], yc_ref[0]) + _dot(is_ref[...], ys_ref[0])
    o_ref[0] = (x0_ref[0] * (y + vv_ref[0] * skip_ref[...])).astype(o_ref.dtype)


def _dft_inv(inv, yc, ys, vv, x0, skip, tt):
    bsz, length, width = vv.shape
    nt = length // tt
    return pl.pallas_call(
        _dft_inv_kernel,
        grid=(bsz, nt),
        in_specs=[pl.BlockSpec((tt, length), lambda b, i: (i, 0)),
                  pl.BlockSpec((tt, length), lambda b, i: (i, 1)),
                  pl.BlockSpec((1, length, width), lambda b, i: (b, 0, 0)),
                  pl.BlockSpec((1, length, width), lambda b, i: (b, 0, 0)),
                  pl.BlockSpec((1, tt, width), lambda b, i: (b, i, 0)),
                  pl.BlockSpec((1, tt, width), lambda b, i: (b, i, 0)),
                  pl.BlockSpec((1, width), lambda b, i: (0, 0))],
        out_specs=pl.BlockSpec((1, tt, width), lambda b, i: (b, i, 0)),
        out_shape=jax.ShapeDtypeStruct((bsz, length, width), BF16),
        compiler_params=_params("parallel", "parallel"),
        name="hyena_dft_inv",
    )(inv, inv, yc, ys, vv, x0, skip)


@functools.lru_cache(maxsize=None)
def _hyena_tables(length):
    n2 = 2 * length
    t = np.linspace(0.0, 1.0, length)[:, None]
    bands = (HY_EMB - 1) // 2
    wpos = 2.0 * np.pi * np.arange(length)[:, None] / length
    fb = np.linspace(1e-4, bands - 1, bands)[None]
    z = np.concatenate([t, np.cos(fb * wpos), -np.sin(fb * wpos)], axis=-1)
    zpad = np.zeros((length, LANES))
    zpad[:, :HY_EMB] = z
    f = np.arange(length)[:, None]
    n = np.arange(length)[None, :]
    ang = 2.0 * np.pi * ((f * n) % n2) / n2
    cos_m = np.cos(ang)
    sin_m = np.sin(ang)
    sin_m[0, :] = np.cos(np.pi * np.arange(length))
    fwd = np.concatenate([cos_m, sin_m], axis=0)
    scale = np.full((1, n2), 2.0 / n2)
    scale[0, 0] = 1.0 / n2
    scale[0, length] = 1.0 / n2
    inv = fwd.T * scale
    return (zpad.astype(np.float32), t.astype(np.float32), fwd.astype(np.float32),
            inv.astype(np.float32))


def _hyena(p_hy, conv_w, conv_b, filt, skip):
    bsz, length, three_w = p_hy.shape
    width = three_w // 3
    w_in, b_in, w_mid, b_mid, w_out, freq = filt
    zpad, t, fwd, inv = _hyena_tables(length)
    deltas = np.abs(np.linspace(HY_MIN_DECAY, HY_MAX_DECAY, width))[None, :]
    dec = np.exp(-t.astype(np.float64) * deltas).astype(np.float32)
    dec2 = jnp.asarray(np.concatenate([dec, dec], axis=1))
    ffn = w_in.shape[1]
    w_in_pad = jnp.zeros((LANES, ffn), F32).at[:HY_EMB].set(w_in)
    hfilt = _hy_filter(jnp.asarray(zpad), w_in_pad, b_in.reshape(1, ffn), w_mid,
                       b_mid.reshape(-1, 1, ffn), w_out, freq.reshape(1, ffn), dec2)
    fwd_b = jnp.asarray(fwd).astype(BF16)
    inv_b = jnp.asarray(inv).astype(BF16)
    tf = min(ROW_TILE, length)
    kc, ks = _filt_spec(fwd_b, hfilt, tf)
    x0, vv = _hy_prep(p_hy, conv_w, conv_b.reshape(1, three_w))
    yc, ys = _dft_fwd(fwd_b, vv, kc, ks, tf)
    return _dft_inv(inv_b, yc, ys, vv, x0, skip.reshape(1, width), tf)


def _gelu_tanh(x):
    return 0.5 * x * (1.0 + jnp.tanh(math.sqrt(2.0 / math.pi) * (x + 0.044715 * x * x * x)))


def _lru_kernel(xl_ref, yl_ref, xc_ref, cw_ref, cb_ref, wa_ref, ba_ref, wx_ref, bx_ref, ap_ref,
                o_ref, xs_s, a_s, b_s, *, ctx_len, lat_len):
    tot = ctx_len + lat_len
    ngrp = tot // 8
    ngrp_ctx = ctx_len // 8
    xs_s[0:ctx_len, :] = _depthwise_conv(xc_ref[0], cw_ref[...], LRU_CONV // 2) + cb_ref[...]
    xs_s[ctx_len:, :] = _depthwise_conv(xl_ref[0], cw_ref[...], LRU_CONV // 2) + cb_ref[...]
    xs = xs_s[...]
    xsb = xs.astype(BF16)
    rows8 = lax.broadcasted_iota(jnp.int32, (ngrp, 8, xs.shape[1]), 1)
    for d in range(2):
        r = _sigmoid(_dot(xsb, wa_ref[d, 0].astype(BF16)) + ba_ref[d])
        gi = _sigmoid(_dot(xsb, wx_ref[d, 0].astype(BF16)) + bx_ref[d])
        log_a = -LRU_C * r * _softplus(ap_ref[d])
        a = jnp.exp(log_a)
        y = 1.0 - a * a
        b = (y * lax.rsqrt(jnp.maximum(y, 1e-30))) * (gi * xs)
        a = a.reshape(ngrp, 8, a.shape[1])
        b = b.reshape(ngrp, 8, b.shape[1])
        for s in (1, 2, 4):
            keep = (rows8 >= s) if d == 0 else (rows8 < 8 - s)
            shift = s if d == 0 else 8 - s
            sa = jnp.where(keep, pltpu.roll(a, shift, axis=1), 1.0)
            sb = jnp.where(keep, pltpu.roll(b, shift, axis=1), 0.0)
            b = a * sb + b
            a = a * sa
        a_s[d] = a.reshape(tot, a.shape[2])
        b_s[d] = b.reshape(tot, b.shape[2])

    def group(i, carry):
        cf, cb = carry
        rf = pl.multiple_of(i * 8, 8)
        gidx = jnp.where(i < ngrp_ctx, ngrp_ctx - 1 - i, ngrp + ngrp_ctx - 1 - i)
        rb = pl.multiple_of(gidx * 8, 8)
        hf = a_s[0, pl.ds(rf, 8), :] * cf + b_s[0, pl.ds(rf, 8), :]
        hb = a_s[1, pl.ds(rb, 8), :] * cb + b_s[1, pl.ds(rb, 8), :]
        b_s[0, pl.ds(rf, 8), :] = hf
        b_s[1, pl.ds(rb, 8), :] = hb
        return (jnp.broadcast_to(hf[7:8, :], hf.shape), jnp.broadcast_to(hb[0:1, :], hb.shape))

    zero = jnp.zeros((8, xs.shape[1]), F32)
    lax.fori_loop(0, ngrp, group, (zero, zero), unroll=4)
    h = b_s[0, ctx_len:, :] + b_s[1, ctx_len:, :]
    o_ref[0] = (h * _gelu_tanh(yl_ref[0])).astype(o_ref.dtype)


def _rglru(xb_l, yb_l, xb_c, conv_w, conv_b, wa, ba, wx, bx, a_param):
    bsz, lat_len, width = xb_l.shape
    ctx_len = xb_c.shape[1]
    tot = ctx_len + lat_len
    blk = width // LRU_HEADS

    def col(b, h):
        return (b, 0, h)

    def wcol(b, h):
        return (0, h)

    def w3(b, h):
        return (0, 0, h)

    return pl.pallas_call(
        functools.partial(_lru_kernel, ctx_len=ctx_len, lat_len=lat_len),
        grid=(bsz, LRU_HEADS),
        in_specs=[pl.BlockSpec((1, lat_len, blk), col),
                  pl.BlockSpec((1, lat_len, blk), col),
                  pl.BlockSpec((1, ctx_len, blk), col),
                  pl.BlockSpec((LRU_CONV, blk), wcol),
                  pl.BlockSpec((1, blk), wcol),
                  pl.BlockSpec((2, 1, blk, blk), lambda b, h: (0, h, 0, 0)),
                  pl.BlockSpec((2, 1, blk), w3),
                  pl.BlockSpec((2, 1, blk, blk), lambda b, h: (0, h, 0, 0)),
                  pl.BlockSpec((2, 1, blk), w3),
                  pl.BlockSpec((2, 1, blk), w3)],
        out_specs=pl.BlockSpec((1, lat_len, blk), col),
        out_shape=jax.ShapeDtypeStruct((bsz, lat_len, width), BF16),
        scratch_shapes=[pltpu.VMEM((tot, blk), F32),
                        pltpu.VMEM((2, tot, blk), F32),
                        pltpu.VMEM((2, tot, blk), F32)],
        compiler_params=_params("parallel", "parallel"),
        name="rglru",
    )(xb_l, yb_l, xb_c, conv_w, conv_b.reshape(1, width), wa, ba.reshape(2, 1, width), wx,
      bx.reshape(2, 1, width), a_param.reshape(2, 1, width))


def _post_mixer_kernel(*refs, n_in, has_bias):
    a_refs = refs[:n_in]
    w_refs = refs[n_in:2 * n_in]
    i = 2 * n_in
    b_ref = None
    if has_bias:
        b_ref = refs[i]
        i += 1
    x_ref, g1_ref, sc_ref, sh_ref, lng_ref, lnb_ref, rw_ref, rb_ref, base_ref = refs[i:i + 9]
    x1_ref, v_ref, route_ref, cnt_ref = refs[i + 9:]

    @pl.when((pl.program_id(0) == 0) & (pl.program_id(1) == 0))
    def _():
        cnt_ref[...] = base_ref[...]

    y = None
    for a_ref, w_ref in zip(a_refs, w_refs):
        t = _dot(a_ref[0], w_ref[...])
        y = t if y is None else y + t
    if has_bias:
        y = y + b_ref[...]
    x1 = _layer_norm(DEEPNORM_ALPHA * x_ref[0] + g1_ref[0] * y, lng_ref[...], lnb_ref[...])
    x1_ref[0] = x1
    v = x1 * (1.0 + sc_ref[0]) + sh_ref[0]
    v_ref[0] = _pack_bf16_pairs(v)
    logits = _dot3(v, rw_ref[...]) + rb_ref[...]
    tm = logits.shape[0]
    lane = lax.broadcasted_iota(jnp.int32, logits.shape, 1).astype(F32)
    work = logits
    picks, firsts = [], []
    m0 = None
    for kk in range(TOP_K):
        m = jnp.max(work, axis=-1, keepdims=True)
        if kk == 0:
            m0 = m
        first = jnp.min(jnp.where(work == m, lane, float(LANES)), axis=-1, keepdims=True)
        pick = lane == first
        picks.append(pick)
        firsts.append(first)
        work = jnp.where(pick, -jnp.inf, work)
    sel = jnp.where(picks[0] | picks[1] | picks[2] | picks[3], 1.0, 0.0)
    e = sel * jnp.exp(logits - m0)
    gate = e / jnp.sum(e, axis=-1, keepdims=True)
    ti = lax.broadcasted_iota(jnp.int32, (tm, tm), 0)
    tj = lax.broadcasted_iota(jnp.int32, (tm, tm), 1)
    before = jnp.where(ti > tj, 1.0, 0.0).astype(BF16)
    slot = _dot(before, sel.astype(BF16)) + cnt_ref[...]
    route = jnp.zeros(logits.shape, F32)
    for kk in range(TOP_K):
        rank = jnp.sum(jnp.where(picks[kk], slot, 0.0), axis=-1, keepdims=True)
        wgt = jnp.sum(jnp.where(picks[kk], gate, 0.0), axis=-1, keepdims=True)
        route = jnp.where(lane == float(kk), firsts[kk], route)
        route = jnp.where(lane == float(TOP_K + kk), rank, route)
        route = jnp.where(lane == float(2 * TOP_K + kk), wgt, route)
    route_ref[0] = route
    cnt_ref[...] += jnp.sum(sel, axis=0, keepdims=True)


def _post_mixer(acts, ws, bias, x, g1, sc2, sh2, ln_g, ln_b, router_w, router_b, base, tm):
    bsz, length, d = x.shape
    tm = min(tm, length)
    n_in = len(acts)

    def row(bi, i):
        return (bi, i, 0)

    def per_b(bi, i):
        return (bi, 0, 0)

    def const(bi, i):
        return (0, 0)

    in_specs = [pl.BlockSpec((1, tm, a.shape[2]), row) for a in acts]
    in_specs += [pl.BlockSpec(w.shape, const) for w in ws]
    args = list(acts) + list(ws)
    if bias is not None:
        in_specs.append(pl.BlockSpec((1, d), const))
        args.append(bias.reshape(1, d))
    in_specs += [pl.BlockSpec((1, tm, d), row),
                 pl.BlockSpec((1, 1, d), per_b), pl.BlockSpec((1, 1, d), per_b),
                 pl.BlockSpec((1, 1, d), per_b),
                 pl.BlockSpec((1, d), const), pl.BlockSpec((1, d), const),
                 pl.BlockSpec((d, LANES), const), pl.BlockSpec((1, LANES), const),
                 pl.BlockSpec((1, LANES), const)]
    rw = jnp.zeros((d, LANES), F32).at[:, :N_EXPERTS].set(router_w)
    rb = jnp.full((1, LANES), -jnp.inf, F32).at[0, :N_EXPERTS].set(router_b)
    args += [x, g1, sc2, sh2, ln_g.reshape(1, d), ln_b.reshape(1, d), rw, rb, base]
    return pl.pallas_call(
        functools.partial(_post_mixer_kernel, n_in=n_in, has_bias=bias is not None),
        grid=(bsz, length // tm),
        in_specs=in_specs,
        out_specs=[pl.BlockSpec((1, tm, d), row), pl.BlockSpec((1, tm, d // 2), row),
                   pl.BlockSpec((1, tm, LANES), row), pl.BlockSpec((1, LANES), const)],
        out_shape=[jax.ShapeDtypeStruct((bsz, length, d), F32),
                   jax.ShapeDtypeStruct((bsz, length, d // 2), jnp.uint32),
                   jax.ShapeDtypeStruct((bsz, length, LANES), F32),
                   jax.ShapeDtypeStruct((1, LANES), F32)],
        compiler_params=_params("arbitrary", "arbitrary"),
        name="post_mixer",
    )(*args)


MOE_W1_CHUNK = 512


def _moe_rows_kernel(te_ref, nv_ref, first_ref, nxt_ref, xs_ref, w1_hbm, b1g_ref, b1l_ref, w2_hbm,
                     b2_ref, ys_ref, w1f_s, w2f_s, w1g_s, w1l_s, w2b_s, sem, *, layer):
    i = pl.program_id(0)
    nv = nv_ref[i]

    def fetch(e):
        return (pltpu.make_async_copy(w1_hbm.at[layer, e], w1f_s, sem.at[0]),
                pltpu.make_async_copy(w2_hbm.at[layer, e], w2f_s, sem.at[1]))

    @pl.when(i == 0)
    def _():
        for cp in fetch(te_ref[0]):
            cp.start()

    @pl.when(first_ref[i] == 1)
    def _():
        for cp in fetch(te_ref[i]):
            cp.wait()
        half = MOE_W1_CHUNK // 2
        for c in range(w1f_s.shape[1] // MOE_W1_CHUNK):
            t = w1f_s[:, c * MOE_W1_CHUNK:(c + 1) * MOE_W1_CHUNK].astype(BF16).T
            words = pltpu.bitcast(t, jnp.uint32)
            w1g_s[c * half:(c + 1) * half, :] = pltpu.bitcast(words << 16, F32).astype(BF16)
            w1l_s[c * half:(c + 1) * half, :] = pltpu.bitcast(
                words & jnp.uint32(0xFFFF0000), F32).astype(BF16)
        w2b_s[...] = w2f_s[...].astype(BF16)

        @pl.when(nxt_ref[i] >= 0)
        def _():
            for cp in fetch(nxt_ref[i]):
                cp.start()

    @pl.when(nv > 0)
    def _():
        rows = lax.broadcasted_iota(jnp.int32, xs_ref.shape, 0)
        x = _unpack_bf16_pairs(jnp.where(rows < nv, xs_ref[...], jnp.uint32(0))).astype(BF16)
        glu = jnp.minimum(_dot_nt(x, w1g_s[...]) + b1g_ref[0, 0], SWIGLU_LIMIT)
        lin = jnp.clip(_dot_nt(x, w1l_s[...]) + b1l_ref[0, 0], -SWIGLU_LIMIT, SWIGLU_LIMIT)
        act = glu * _sigmoid(SWIGLU_ALPHA * glu) * (lin + 1.0)
        ys_ref[...] = _pack_bf16_pairs(_dot(act.astype(BF16), w2b_s[...]) + b2_ref[0, 0])

    @pl.when(nv == 0)
    def _():
        ys_ref[...] = jnp.zeros_like(ys_ref)


def _moe_rows(xs, tile_expert, tile_rows, tile_first, tile_next, layer, w1, b1g, b1l, w2, b2, tm):
    n_rows, dh = xs.shape
    _, _, d, dff2 = w1.shape
    dff = dff2 // 2

    def row(i, te, nv, first, nxt):
        return (i, 0)

    def exp4(i, te, nv, first, nxt):
        return (layer, te[i], 0, 0)

    hbm = pl.BlockSpec(memory_space=pl.ANY)
    return pl.pallas_call(
        functools.partial(_moe_rows_kernel, layer=layer),
        grid_spec=pltpu.PrefetchScalarGridSpec(
            num_scalar_prefetch=4,
            grid=(n_rows // tm,),
            in_specs=[pl.BlockSpec((tm, dh), row), hbm,
                      pl.BlockSpec((1, 1, 1, dff), exp4), pl.BlockSpec((1, 1, 1, dff), exp4),
                      hbm, pl.BlockSpec((1, 1, 1, d), exp4)],
            out_specs=pl.BlockSpec((tm, dh), row),
            scratch_shapes=[pltpu.VMEM((d, dff2), F32), pltpu.VMEM((dff, d), F32),
                            pltpu.VMEM((dff, d), BF16),
                            pltpu.VMEM((dff, d), BF16), pltpu.VMEM((dff, d), BF16),
                            pltpu.SemaphoreType.DMA((2,))]),
        out_shape=jax.ShapeDtypeStruct((n_rows, dh), jnp.uint32),
        compiler_params=_params("arbitrary"),
        name="moe_rows",
    )(tile_expert, tile_rows, tile_first, tile_next, xs, w1, b1g, b1l, w2, b2)


def _moe_combine_kernel(y0_ref, y1_ref, y2_ref, y3_ref, route_ref, x1_ref, g2_ref, lng_ref, lnb_ref,
                        o_ref):
    route = route_ref[...]
    lane = lax.broadcasted_iota(jnp.int32, route.shape, 1)
    f = None
    for kk, y_ref in enumerate((y0_ref, y1_ref, y2_ref, y3_ref)):
        wgt = jnp.sum(jnp.where(lane == 2 * TOP_K + kk, route, 0.0), axis=-1, keepdims=True)
        term = wgt * _unpack_bf16_pairs(y_ref[0])
        f = term if f is None else f + term
    o_ref[0] = _layer_norm(DEEPNORM_ALPHA * x1_ref[0] + g2_ref[0] * f, lng_ref[...], lnb_ref[...])


def _moe_combine(yg, route, row_offset, x1, g2, ln_g, ln_b, tm):
    bsz, length, d = x1.shape
    tm = min(tm, length)
    nt = length // tm
    off = row_offset // tm

    def pick(kk):
        return lambda bi, i: (kk, off + bi * nt + i, 0)

    def const(bi, i):
        return (0, 0)

    return pl.pallas_call(
        _moe_combine_kernel,
        grid=(bsz, nt),
        in_specs=[pl.BlockSpec((1, tm, d // 2), pick(kk)) for kk in range(TOP_K)] + [
            pl.BlockSpec((tm, LANES), lambda bi, i: (off + bi * nt + i, 0)),
            pl.BlockSpec((1, tm, d), lambda bi, i: (bi, i, 0)),
            pl.BlockSpec((1, 1, d), lambda bi, i: (bi, 0, 0)),
            pl.BlockSpec((1, d), const), pl.BlockSpec((1, d), const)],
        out_specs=pl.BlockSpec((1, tm, d), lambda bi, i: (bi, i, 0)),
        out_shape=jax.ShapeDtypeStruct((bsz, length, d), F32),
        compiler_params=_params("parallel", "parallel"),
        name="moe_combine",
    )(yg, yg, yg, yg, route, x1, g2, ln_g.reshape(1, d), ln_b.reshape(1, d))


SC_CORES = 2
SC_SUBCORES = 16
SC_WORKERS = SC_CORES * SC_SUBCORES
SC_WINDOW = 64
SC_TOKEN_WINDOW = 32


def _sc_row_pipeline(nwin, read, write):
    def start(copies):
        for cp in copies:
            cp.start()

    def wait(copies):
        for cp in copies:
            cp.wait()

    start(read(0, 0))

    @pl.loop(0, nwin, step=2)
    def _(w0):
        for b in range(2):
            w = w0 + b

            @pl.when(w + 1 < nwin)
            def _():
                @pl.when(w >= 1)
                def _():
                    wait(write(w - 1, 1 - b))

                start(read(w + 1, 1 - b))

            wait(read(w, b))
            start(write(w, b))

    wait(write(nwin - 2, 0))
    wait(write(nwin - 1, 1))


def _sc_scatter_rows(src, pos, n_out):
    t_rows, d = src.shape
    nw, nwin, picks, win = pos.shape
    assert nw == SC_WORKERS and nwin % 2 == 0 and t_rows == nw * nwin * win
    mesh = plsc.VectorSubcoreMesh(core_axis_name="c", subcore_axis_name="s")

    @functools.partial(
        pl.kernel, mesh=mesh, out_type=jax.ShapeDtypeStruct((n_out, d), src.dtype),
        scratch_types=[pltpu.VMEM((nwin, picks, win), jnp.int32),
                       pltpu.VMEM((2, win, d), src.dtype),
                       pltpu.SemaphoreType.DMA((2,)), pltpu.SemaphoreType.DMA((2,))])
    def scatter(src_hbm, pos_hbm, out_hbm, idx_v, rows_v, rsem, wsem):
        wid = lax.axis_index("s") * SC_CORES + lax.axis_index("c")
        t0 = wid * (nwin * win)
        pltpu.sync_copy(pos_hbm.at[wid], idx_v)

        def read(w, slot):
            return [pltpu.make_async_copy(src_hbm.at[pl.ds(t0 + w * win, win)], rows_v.at[slot],
                                          rsem.at[slot])]

        def write(w, slot):
            return [pltpu.make_async_copy(rows_v.at[slot], out_hbm.at[idx_v.at[w, k]],
                                          wsem.at[slot]) for k in range(picks)]

        _sc_row_pipeline(nwin, read, write)

    return scatter(src, pos)


def _sc_gather_rows(table, pos):
    _, d = table.shape
    nw, nwin, win = pos.shape
    assert nw == SC_WORKERS and nwin % 2 == 0
    per = nwin * win
    mesh = plsc.VectorSubcoreMesh(core_axis_name="c", subcore_axis_name="s")

    @functools.partial(
        pl.kernel, mesh=mesh, out_type=jax.ShapeDtypeStruct((nw * per, d), table.dtype),
        scratch_types=[pltpu.VMEM((nwin, win), jnp.int32), pltpu.VMEM((2, win, d), table.dtype),
                       pltpu.SemaphoreType.DMA((2,)), pltpu.SemaphoreType.DMA((2,))])
    def gather(table_hbm, pos_hbm, out_hbm, idx_v, rows_v, rsem, wsem):
        wid = lax.axis_index("s") * SC_CORES + lax.axis_index("c")
        base = wid * per
        pltpu.sync_copy(pos_hbm.at[wid], idx_v)

        def read(w, slot):
            return [pltpu.make_async_copy(table_hbm.at[idx_v.at[w]], rows_v.at[slot],
                                          rsem.at[slot])]

        def write(w, slot):
            return [pltpu.make_async_copy(rows_v.at[slot], out_hbm.at[pl.ds(base + w * win, win)],
                                          wsem.at[slot])]

        _sc_row_pipeline(nwin, read, write)

    return gather(table, pos)


MOE_TILE = 512


def _moe_sparse(v_all, route, counts, layer, w1, b1g, b1l, w2, b2):
    t_rows, dh = v_all.shape
    n_exp = w1.shape[1]
    pairs = TOP_K * t_rows
    n_tiles = pairs // MOE_TILE + n_exp
    route_t = route.T
    expert = route_t[0:TOP_K].astype(jnp.int32)
    slot = route_t[TOP_K:2 * TOP_K].astype(jnp.int32)
    cnt = counts[0, :n_exp].astype(jnp.int32)
    tiles_per = (cnt + MOE_TILE - 1) // MOE_TILE
    tile_end = jnp.cumsum(tiles_per)
    tile_start = tile_end - tiles_per
    pos = slot
    for e in range(n_exp):
        pos = pos + jnp.where(expert == e, tile_start[e] * MOE_TILE, 0)
    nwin = pairs // (SC_WORKERS * SC_WINDOW)
    pos_km = pos.reshape(SC_WORKERS, nwin, SC_WINDOW)
    nwin_t = t_rows // (SC_WORKERS * SC_TOKEN_WINDOW)
    pos_tm = pos.reshape(TOP_K, SC_WORKERS, nwin_t, SC_TOKEN_WINDOW).transpose(1, 2, 0, 3)
    tile_ids = jnp.arange(n_tiles, dtype=jnp.int32)[:, None]
    owns = (tile_ids >= tile_start[None, :]) & (tile_ids < tile_end[None, :])
    experts = jnp.arange(n_exp, dtype=jnp.int32)[None, :]
    last_used = jnp.max(jnp.where(tiles_per > 0, experts[0], 0))
    used = jnp.any(owns, axis=1)
    te = jnp.where(used, jnp.sum(jnp.where(owns, experts, 0), axis=1), last_used).astype(jnp.int32)
    rows_left = cnt[None, :] - (tile_ids - tile_start[None, :]) * MOE_TILE
    tile_rows = jnp.sum(jnp.where(owns, jnp.clip(rows_left, 0, MOE_TILE), 0), axis=1)
    tile_rows = tile_rows.astype(jnp.int32)
    tile_first = jnp.any(owns & (tile_ids == tile_start[None, :]), axis=1).astype(jnp.int32)
    later = (experts > experts.T) & (tiles_per[None, :] > 0)
    next_e = jnp.min(jnp.where(later, experts, n_exp), axis=1)
    next_e = jnp.where(next_e == n_exp, -1, next_e)
    tile_next = jnp.where(used, jnp.sum(jnp.where(owns, next_e[None, :], 0), axis=1), -1)
    tile_next = tile_next.astype(jnp.int32)
    xs = _sc_scatter_rows(v_all, pos_tm, n_tiles * MOE_TILE)
    ys = _moe_rows(xs, te, tile_rows, tile_first, tile_next, layer, w1, b1g, b1l, w2, b2, MOE_TILE)
    return _sc_gather_rows(ys, pos_km).reshape(TOP_K, t_rows, dh)


@functools.lru_cache(maxsize=None)
def _sincos_2d(rows, cols, dim):
    quarter = dim // 4
    omega = 1.0 / (10000.0 ** (np.arange(quarter, dtype=np.float64) / quarter))

    def emb1d(n):
        ang = np.arange(n, dtype=np.float64)[:, None] * omega
        return np.concatenate([np.sin(ang), np.cos(ang)], axis=-1)

    er = np.broadcast_to(emb1d(rows)[:, None], (rows, cols, dim // 2))
    ec = np.broadcast_to(emb1d(cols)[None], (rows, cols, dim // 2))
    return np.concatenate([er, ec], axis=-1).reshape(rows * cols, dim).astype(np.float32)


def _gate_layout(gates, n_ch):
    bsz, length, _ = gates.shape
    g = gates[..., :n_ch].reshape(bsz, length // DN_CHUNK, DN_CHUNK, n_ch)
    return g.transpose(0, 3, 1, 2)


def kernel(x, c, ctx, c_ctx, ada_w, ada_b, ln_g, ln_b, ev_w_in, ev_w_out, dn_conv_w, dn_a_log, dn_dt_bias, dn_norm_g, hy_conv_w, hy_conv_b, hy_w_in, hy_b_in, hy_w_mid, hy_b_mid, hy_w_out, hy_freq, hy_skip, od_w_in, od_b_in, lru_conv_w, lru_conv_b, lru_wa, lru_ba, lru_wx, lru_bx, lru_a_param, od_w_out, od_b_out, router_w, router_b, moe_w1, moe_b1, moe_w2, moe_b2):
    bsz, length, d = x.shape
    ctx_len = ctx.shape[1]
    pos = jnp.asarray(_sincos_2d(length // GRID_W, GRID_W, d))

    cond = jnp.zeros((16, d), F32).at[:bsz].set(c).at[bsz].set(c_ctx)
    mod = _modulation(cond, ada_w, ada_b).reshape(DEPTH, 16, 6, d)

    def lat_mod(layer, k):
        return mod[layer, :bsz, k][:, None, :]

    def ctx_mod(layer, k):
        return jnp.broadcast_to(mod[layer, bsz, k][None, None, :], (bsz, 1, d))

    hc = ctx
    for layer in range(DEPTH):
        last = layer == DEPTH - 1
        j = layer // 2
        if layer % 2 == 0:
            dn_qk = DN_HEADS * DN_DK
            dn_qkv = 3 * dn_qk
            dn_in = dn_qkv + dn_qk + 4 * DN_HEADS
            w_in = ev_w_in[j]
            gate_w = jnp.zeros((d, LANES), F32).at[:, :4 * DN_HEADS].set(w_in[:, dn_qkv + dn_qk:dn_in])
            w_cat = jnp.concatenate([w_in[:, :dn_qkv + dn_qk], gate_w, w_in[:, dn_in:]],
                                    axis=1).astype(BF16)
            hy_in = w_in.shape[1] - dn_in
            widths = (dn_qkv, dn_qk, LANES, hy_in)
            qkv_l, z_l, gt_l, phy_l = _inproj(x, lat_mod(layer, 1), lat_mod(layer, 0), pos,
                                               w_cat, None, widths, ROW_TILE)
            qkv_c, z_c, gt_c, phy_c = _inproj(hc, ctx_mod(layer, 1), ctx_mod(layer, 0), None,
                                               w_cat, None, widths, ROW_TILE)
            gates = jnp.concatenate([_gate_layout(gt_c, 4 * DN_HEADS),
                                     _gate_layout(gt_l, 4 * DN_HEADS)], axis=2)
            n_chunks = gates.shape[2]
            gates = jnp.pad(gates, ((0, 0), (0, 0), (0, -n_chunks % 16), (0, 0)))
            dn_l, dn_c = _deltanet(qkv_l, z_l, qkv_c, z_c, gates, dn_conv_w[j], dn_a_log[j],
                                   dn_dt_bias[j], dn_norm_g[j])
            filt = (hy_w_in[j], hy_b_in[j], hy_w_mid[j], hy_b_mid[j], hy_w_out[j], hy_freq[j])
            hy_l = _hyena(phy_l, hy_conv_w[j], hy_conv_b[j], filt, hy_skip[j])
            w_out = ev_w_out[j].astype(BF16)
            half = dn_l.shape[2]
            acts_l, ws, bias = (dn_l, hy_l), (w_out[:half], w_out[half:]), None
            acts_c = None
            if not last:
                hy_c = _hyena(phy_c, hy_conv_w[j], hy_conv_b[j], filt, hy_skip[j])
                acts_c = (dn_c, hy_c)
        else:
            w_in = od_w_in[j].astype(BF16)
            width = w_in.shape[1] // 2
            b_in = od_b_in[j].reshape(1, 2 * width)
            xb_l, yb_l = _inproj(x, lat_mod(layer, 1), lat_mod(layer, 0), pos, w_in, b_in,
                                 (width, width), ROW_TILE)
            xb_c, _ = _inproj(hc, ctx_mod(layer, 1), ctx_mod(layer, 0), None, w_in, b_in,
                              (width, width), ROW_TILE)
            act_l = _rglru(xb_l, yb_l, xb_c, lru_conv_w[j], lru_conv_b[j], lru_wa[j], lru_ba[j],
                           lru_wx[j], lru_bx[j], lru_a_param[j])
            acts_l, ws, bias = (act_l,), (od_w_out[j].astype(BF16),), od_b_out[j]
            acts_c = None
            assert last, "context outputs of the RG-LRU layer are only needed before the last layer"


        x1, v, route, counts = _post_mixer(acts_l, ws, bias, x, lat_mod(layer, 2),
                                           lat_mod(layer, 4), lat_mod(layer, 3), ln_g[layer, 0],
                                           ln_b[layer, 0], router_w[layer], router_b[layer],
                                           jnp.zeros((1, LANES), F32), ROW_TILE)
        v_all = v.reshape(bsz * length, d // 2)
        route = route.reshape(bsz * length, LANES)
        if not last:
            hc1, vc, route_c, counts = _post_mixer(acts_c, ws, bias, hc, ctx_mod(layer, 2),
                                                   ctx_mod(layer, 4), ctx_mod(layer, 3),
                                                   ln_g[layer, 0], ln_b[layer, 0], router_w[layer],
                                                   router_b[layer], counts, ROW_TILE)
            v_all = jnp.concatenate([v_all, vc.reshape(bsz * ctx_len, d // 2)], axis=0)
            route = jnp.concatenate([route, route_c.reshape(bsz * ctx_len, LANES)], axis=0)
        yg = _moe_sparse(v_all, route, counts, layer, moe_w1, moe_b1[:, :, None, 0::2],
                         moe_b1[:, :, None, 1::2], moe_w2, moe_b2[:, :, None, :])
        x = _moe_combine(yg, route, 0, x1, lat_mod(layer, 5), ln_g[layer, 1], ln_b[layer, 1],
                         ROW_TILE)
        if not last:
            hc = _moe_combine(yg, route, bsz * length, hc1, ctx_mod(layer, 5), ln_g[layer, 1],
                              ln_b[layer, 1], ROW_TILE)
    return x
```
